```python
import jax, jax.numpy as jnp
from jax import lax
import numpy as np

D_MODEL = 1024
BATCH = 2
SEQ = 16384
DEPTH = 2

D_FF = 2816
EPS = 1e-6
NSA_HEADS = 8
NSA_KV_GROUPS = 2
HEAD_DIM = 64
NSA_WIDTH = NSA_HEADS * HEAD_DIM
KV_WIDTH = NSA_KV_GROUPS * HEAD_DIM
CMP_BLOCK = 32
CMP_STRIDE = 16
CMP_HIDDEN = 128
SLC_BLOCK = 64
RATIO = SLC_BLOCK // CMP_STRIDE
SLC_TOP_N = 16
WINDOW = 512
Q_BLOCK = 128
FORCE_BONUS = 1e4
CONV_WIDTH = 512
CONV_K = 3
IN_SIZES = (NSA_WIDTH, 3 * NSA_HEADS, KV_WIDTH, KV_WIDTH, KV_WIDTH, KV_WIDTH,
            KV_WIDTH, KV_WIDTH, CONV_WIDTH, CONV_WIDTH, CONV_WIDTH)
IN_COLS = sum(IN_SIZES)
IN_SPLITS = tuple(int(s) for s in np.cumsum(IN_SIZES)[:-1])
MIX_OUT = NSA_WIDTH + CONV_WIDTH
RWKV_HEAD = 64
RWKV_HEADS = D_MODEL // RWKV_HEAD
DECAY_LORA = 64
AAA_LORA = 64
GATE_LORA = 128
LNX_EPS = 64e-5

kernel_name = "hybrid_nsa_shortconv_rwkv7_macaron"


def rmsnorm(x, g):
    xf = x.astype(jnp.float32)
    y = xf * lax.rsqrt(jnp.mean(xf * xf, -1, keepdims=True) + EPS)
    return (y * g.astype(jnp.float32)).astype(x.dtype)


def macaron_half_ffn(x, pre_g, post_g, w_gate, w_up, w_down):
    h = rmsnorm(x, pre_g)
    h = (jax.nn.silu(h @ w_gate) * (h @ w_up)) @ w_down
    return x + 0.5 * rmsnorm(h, post_g)


def alibi_slopes(n):
    return 2.0 ** (-8.0 * jnp.arange(1, n + 1, dtype=jnp.float32) / n)


def masked_softmax(s, mask):
    s = jnp.where(mask, s.astype(jnp.float32), -jnp.inf)
    m = jnp.max(s, -1, keepdims=True)
    m = jnp.where(jnp.isfinite(m), m, 0.0)
    p = jnp.exp(s - m)
    return p / jnp.maximum(jnp.sum(p, -1, keepdims=True), 1e-30)


def compress_blocks(kv, pe, w1, w2):
    B, S, G, Dh = kv.shape
    halves = kv.reshape(B, S // CMP_STRIDE, CMP_STRIDE, G, Dh)
    blocks = jnp.concatenate([halves[:, :-1], halves[:, 1:]], axis=2)
    blocks = blocks + pe[None, None, :, None, :]
    nc = blocks.shape[1]
    flat = blocks.transpose(0, 1, 3, 2, 4).reshape(B, nc, G, CMP_BLOCK * Dh)
    return jax.nn.gelu(flat @ w1) @ w2


def nsa_attention(q, gates, kc, vc, k_slc, v_slc, k_win, v_win):
    B, S, H, Dh = q.shape
    G = kc.shape[2]
    R = H // G
    n_cmp = kc.shape[1]
    n_slc = S // SLC_BLOCK
    top_n = min(SLC_TOP_N, n_slc)
    slopes = alibi_slopes(H).reshape(G, R)
    cmp_end = jnp.arange(n_cmp) * CMP_STRIDE + (CMP_BLOCK - 1)
    qg = q.reshape(B, S, G, R, Dh) * (Dh ** -0.5)
    gg = gates.reshape(B, S, G, R, 3)
    kb = k_slc.reshape(B, n_slc, SLC_BLOCK, G, Dh).transpose(0, 3, 1, 2, 4)
    vb = v_slc.reshape(B, n_slc, SLC_BLOCK, G, Dh).transpose(0, 3, 1, 2, 4)
    gather = jax.vmap(jax.vmap(lambda blocks, ix: blocks[ix]))
    kw = jnp.pad(k_win, ((0, 0), (WINDOW, 0), (0, 0), (0, 0)))
    vw = jnp.pad(v_win, ((0, 0), (WINDOW, 0), (0, 0), (0, 0)))
    j = jnp.arange(n_slc)
    offs = jnp.arange(SLC_BLOCK)
    woffs = jnp.arange(Q_BLOCK + WINDOW) - WINDOW

    def query_block(qi):
        q0 = qi * Q_BLOCK
        t = q0 + jnp.arange(Q_BLOCK)
        qb = lax.dynamic_slice_in_dim(qg, q0, Q_BLOCK, axis=1)
        gb = lax.dynamic_slice_in_dim(gg, q0, Q_BLOCK, axis=1)
        d_cmp = t[:, None] - cmp_end[None, :]
        s = jnp.einsum("btgrd,bngd->bgrtn", qb, kc) - slopes[:, :, None, None] * d_cmp
        p_cmp = masked_softmax(s, d_cmp >= 0)
        o_cmp = jnp.einsum("bgrtn,bngd->btgrd", p_cmp.astype(vc.dtype), vc)
        imp = jnp.pad(p_cmp.sum(2), ((0, 0), (0, 0), (0, 0), (0, 1)))
        imp = imp.reshape(B, G, Q_BLOCK, n_slc, RATIO)
        last = imp[..., RATIO - 1]
        prev = jnp.pad(last, ((0, 0), (0, 0), (0, 0), (1, 0)))[..., :-1]
        imp = imp.sum(-1) - 0.5 * last + 0.5 * prev
        blk_t = (t // SLC_BLOCK)[:, None]
        forced = (j == 0) | (j == blk_t) | (j == blk_t - 1)
        imp = jnp.where(forced, imp + FORCE_BONUS, imp)
        imp = jnp.where(j <= blk_t, imp, -jnp.inf)
        _, idx = lax.top_k(imp, top_n)
        ks = gather(kb, idx)
        vs = gather(vb, idx)
        d_slc = t[:, None, None] - (idx[..., None] * SLC_BLOCK + offs)
        s = jnp.einsum("btgrd,bgtnld->bgrtnl", qb, ks) - slopes[:, :, None, None, None] * d_slc[:, :, None]
        mask = jnp.broadcast_to((d_slc >= 0)[:, :, None], s.shape)
        p = masked_softmax(s.reshape(B, G, R, Q_BLOCK, -1), mask.reshape(B, G, R, Q_BLOCK, -1))
        o_slc = jnp.einsum("bgrtm,bgtmd->btgrd", p.astype(vs.dtype),
                           vs.reshape(B, G, Q_BLOCK, top_n * SLC_BLOCK, Dh))
        kwb = lax.dynamic_slice_in_dim(kw, q0, Q_BLOCK + WINDOW, axis=1)
        vwb = lax.dynamic_slice_in_dim(vw, q0, Q_BLOCK + WINDOW, axis=1)
        kpos = q0 + woffs
        d_win = t[:, None] - kpos[None, :]
        mask = (d_win >= 0) & (d_win < WINDOW) & (kpos[None, :] >= 0)
        s = jnp.einsum("btgrd,bsgd->bgrts", qb, kwb) - slopes[:, :, None, None] * d_win
        p = masked_softmax(s, mask)
        o_win = jnp.einsum("bgrts,bsgd->btgrd", p.astype(vwb.dtype), vwb)
        return gb[..., 0:1] * o_cmp + gb[..., 1:2] * o_slc + gb[..., 2:3] * o_win

    out = lax.map(query_block, jnp.arange(S // Q_BLOCK))
    return out.transpose(1, 0, 2, 3, 4, 5).reshape(B, S, H * Dh)


def short_conv(b_gate, c_gate, u, conv_w, conv_b):
    z = c_gate * u
    y = lax.conv_general_dilated(z, conv_w[:, None, :].astype(z.dtype), window_strides=(1,),
                                 padding=[(CONV_K - 1, 0)],
                                 dimension_numbers=("NWC", "WIO", "NWC"),
                                 feature_group_count=z.shape[-1])
    return b_gate * (y + conv_b)


def nsa_shortconv_mixer(h, w_in, cmp_pe_k, cmp_w1_k, cmp_w2_k, cmp_pe_v, cmp_w1_v, cmp_w2_v,
                        conv_w, conv_b, w_out):
    B, S, _ = h.shape
    q, g, kc, vc, ks, vs, kw, vw, cb, cc, cu = jnp.split(h @ w_in, IN_SPLITS, axis=-1)
    kv = lambda t: t.reshape(B, S, NSA_KV_GROUPS, HEAD_DIM)
    kc = compress_blocks(kv(kc), cmp_pe_k, cmp_w1_k, cmp_w2_k)
    vc = compress_blocks(kv(vc), cmp_pe_v, cmp_w1_v, cmp_w2_v)
    o_nsa = nsa_attention(q.reshape(B, S, NSA_HEADS, HEAD_DIM),
                          jax.nn.sigmoid(g).reshape(B, S, NSA_HEADS, 3),
                          kc, vc, kv(ks), kv(vs), kv(kw), kv(vw))
    o_conv = short_conv(cb, cc, cu, conv_w, conv_b)
    return jnp.concatenate([o_nsa.astype(h.dtype), o_conv], axis=-1) @ w_out


def rwkv7_mixer(h, mu, w_r, w_k, w_v, w_o, w0, w_dec1, w_dec2, a0, w_a1, w_a2,
                w_g1, w_g2, k_k, k_a, r_k, lnx_g, lnx_b):
    B, S, D = h.shape
    H, N = RWKV_HEADS, RWKV_HEAD
    xx = jnp.pad(h, ((0, 0), (1, 0), (0, 0)))[:, :-1] - h
    xr, xw, xk, xv, xa, xg = [h + xx * mu[i] for i in range(6)]
    r = xr @ w_r
    k = xk @ w_k
    v = xv @ w_v
    w_log = -jax.nn.softplus(-(w0 + jnp.tanh(xw @ w_dec1) @ w_dec2)) - 0.5
    a = jax.nn.sigmoid(a0 + (xa @ w_a1) @ w_a2)
    g = jax.nn.sigmoid(xg @ w_g1) @ w_g2
    kk = (k * k_k).reshape(B, S, H, N).astype(jnp.float32)
    kk = kk * lax.rsqrt(jnp.maximum(jnp.sum(kk * kk, -1, keepdims=True), 1e-12))
    k = k * (1.0 + (a - 1.0) * k_a)
    decay = jnp.exp(-jnp.exp(w_log.astype(jnp.float32)))
    heads = lambda t: t.reshape(B, S, H, N).astype(jnp.float32).transpose(1, 0, 2, 3)
    xs = (heads(r), heads(decay), heads(k), heads(v), kk.transpose(1, 0, 2, 3), heads(a))

    def step(state, inp):
        r_t, w_t, k_t, v_t, kk_t, a_t = inp
        sa = jnp.einsum("bhij,bhj->bhi", state, -kk_t)
        state = (state * w_t[:, :, None, :] + sa[..., None] * (kk_t * a_t)[:, :, None, :]
                 + v_t[..., None] * k_t[:, :, None, :])
        return state, jnp.einsum("bhij,bhj->bhi", state, r_t)

    state0 = jnp.zeros((B, H, N, N), jnp.float32)
    _, y = lax.scan(step, state0, xs)
    y = y.transpose(1, 0, 2, 3)
    mean = jnp.mean(y, -1, keepdims=True)
    var = jnp.mean(jnp.square(y - mean), -1, keepdims=True)
    y = ((y - mean) * lax.rsqrt(var + LNX_EPS)).reshape(B, S, D) * lnx_g + lnx_b
    rh, kh, vh = (t.reshape(B, S, H, N).astype(jnp.float32) for t in (r, k, v))
    bonus = (jnp.sum(rh * kh * r_k, -1, keepdims=True) * vh).reshape(B, S, D)
    return ((y + bonus).astype(h.dtype) * g) @ w_o


def setup_inputs(seed: int = 0) -> dict:
    key = jax.random.key(seed)
    keys = iter(jax.random.split(key, 80))

    def nrm(shape, scale):
        return jax.random.normal(next(keys), shape, jnp.float32) * scale

    def unif(shape, lo, hi):
        return jax.random.uniform(next(keys), shape, jnp.float32, lo, hi)

    def gain(n=D_MODEL):
        return 1.0 + nrm((n,), 0.05)

    inp = {"x": nrm((BATCH, SEQ, D_MODEL), 1.0)}

    def add_ffn(p):
        inp[p + "_pre_g"] = gain()
        inp[p + "_post_g"] = gain()
        inp[p + "_w_gate"] = nrm((D_MODEL, D_FF), D_MODEL ** -0.5)
        inp[p + "_w_up"] = nrm((D_MODEL, D_FF), D_MODEL ** -0.5)
        inp[p + "_w_down"] = nrm((D_FF, D_MODEL), D_FF ** -0.5)

    add_ffn("l0_ffn1")
    inp["l0_mix_pre_g"] = gain()
    inp["l0_mix_post_g"] = gain()
    inp["l0_w_in"] = nrm((D_MODEL, IN_COLS), D_MODEL ** -0.5)
    for kvn in ("k", "v"):
        inp["l0_cmp_pe_" + kvn] = nrm((CMP_BLOCK, HEAD_DIM), 0.1)
        inp["l0_cmp_w1_" + kvn] = nrm((CMP_BLOCK * HEAD_DIM, CMP_HIDDEN), (CMP_BLOCK * HEAD_DIM) ** -0.5)
        inp["l0_cmp_w2_" + kvn] = nrm((CMP_HIDDEN, HEAD_DIM), CMP_HIDDEN ** -0.5)
    inp["l0_conv_w"] = nrm((CONV_K, CONV_WIDTH), 0.5)
    inp["l0_conv_b"] = nrm((CONV_WIDTH,), 0.02)
    inp["l0_w_out"] = nrm((MIX_OUT, D_MODEL), MIX_OUT ** -0.5)
    add_ffn("l0_ffn2")
    add_ffn("l1_ffn1")
    inp["l1_mix_pre_g"] = gain()
    inp["l1_mix_post_g"] = gain()
    inp["l1_mu"] = unif((6, D_MODEL), 0.0, 1.0)
    inp["l1_w_r"] = nrm((D_MODEL, D_MODEL), D_MODEL ** -0.5)
    inp["l1_w_k"] = nrm((D_MODEL, D_MODEL), D_MODEL ** -0.5)
    inp["l1_w_v"] = nrm((D_MODEL, D_MODEL), D_MODEL ** -0.5)
    inp["l1_w_o"] = nrm((D_MODEL, D_MODEL), D_MODEL ** -0.5)
    inp["l1_w0"] = unif((D_MODEL,), -6.0, 1.0)
    inp["l1_w_dec1"] = nrm((D_MODEL, DECAY_LORA), D_MODEL ** -0.5)
    inp["l1_w_dec2"] = nrm((DECAY_LORA, D_MODEL), 0.1 * DECAY_LORA ** -0.5)
    inp["l1_a0"] = nrm((D_MODEL,), 0.5)
    inp["l1_w_a1"] = nrm((D_MODEL, AAA_LORA), D_MODEL ** -0.5)
    inp["l1_w_a2"] = nrm((AAA_LORA, D_MODEL), 0.5 * AAA_LORA ** -0.5)
    inp["l1_w_g1"] = nrm((D_MODEL, GATE_LORA), D_MODEL ** -0.5)
    inp["l1_w_g2"] = nrm((GATE_LORA, D_MODEL), GATE_LORA ** -0.5)
    inp["l1_k_k"] = 0.85 + nrm((D_MODEL,), 0.05)
    inp["l1_k_a"] = 1.0 + nrm((D_MODEL,), 0.05)
    inp["l1_r_k"] = nrm((RWKV_HEADS, RWKV_HEAD), 0.1)
    inp["l1_lnx_g"] = gain()
    inp["l1_lnx_b"] = nrm((D_MODEL,), 0.02)
    add_ffn("l1_ffn2")
    return inp


def reference(x,
              l0_ffn1_pre_g, l0_ffn1_post_g, l0_ffn1_w_gate, l0_ffn1_w_up, l0_ffn1_w_down,
              l0_mix_pre_g, l0_mix_post_g, l0_w_in,
              l0_cmp_pe_k, l0_cmp_w1_k, l0_cmp_w2_k, l0_cmp_pe_v, l0_cmp_w1_v, l0_cmp_w2_v,
              l0_conv_w, l0_conv_b, l0_w_out,
              l0_ffn2_pre_g, l0_ffn2_post_g, l0_ffn2_w_gate, l0_ffn2_w_up, l0_ffn2_w_down,
              l1_ffn1_pre_g, l1_ffn1_post_g, l1_ffn1_w_gate, l1_ffn1_w_up, l1_ffn1_w_down,
              l1_mix_pre_g, l1_mix_post_g, l1_mu, l1_w_r, l1_w_k, l1_w_v, l1_w_o,
              l1_w0, l1_w_dec1, l1_w_dec2, l1_a0, l1_w_a1, l1_w_a2, l1_w_g1, l1_w_g2,
              l1_k_k, l1_k_a, l1_r_k, l1_lnx_g, l1_lnx_b,
              l1_ffn2_pre_g, l1_ffn2_post_g, l1_ffn2_w_gate, l1_ffn2_w_up, l1_ffn2_w_down):
    layers = [
        dict(ffn1=(l0_ffn1_pre_g, l0_ffn1_post_g, l0_ffn1_w_gate, l0_ffn1_w_up, l0_ffn1_w_down),
             norms=(l0_mix_pre_g, l0_mix_post_g),
             mixer=nsa_shortconv_mixer,
             mix=(l0_w_in, l0_cmp_pe_k, l0_cmp_w1_k, l0_cmp_w2_k, l0_cmp_pe_v, l0_cmp_w1_v,
                  l0_cmp_w2_v, l0_conv_w, l0_conv_b, l0_w_out),
             ffn2=(l0_ffn2_pre_g, l0_ffn2_post_g, l0_ffn2_w_gate, l0_ffn2_w_up, l0_ffn2_w_down)),
        dict(ffn1=(l1_ffn1_pre_g, l1_ffn1_post_g, l1_ffn1_w_gate, l1_ffn1_w_up, l1_ffn1_w_down),
             norms=(l1_mix_pre_g, l1_mix_post_g),
             mixer=rwkv7_mixer,
             mix=(l1_mu, l1_w_r, l1_w_k, l1_w_v, l1_w_o, l1_w0, l1_w_dec1, l1_w_dec2, l1_a0,
                  l1_w_a1, l1_w_a2, l1_w_g1, l1_w_g2, l1_k_k, l1_k_a, l1_r_k, l1_lnx_g, l1_lnx_b),
             ffn2=(l1_ffn2_pre_g, l1_ffn2_post_g, l1_ffn2_w_gate, l1_ffn2_w_up, l1_ffn2_w_down)),
    ]
    for i in range(DEPTH):
        p = layers[i]
        x = macaron_half_ffn(x, *p["ffn1"])
        pre_g, post_g = p["norms"]
        x = x + rmsnorm(p["mixer"](rmsnorm(x, pre_g), *p["mix"]), post_g)
        x = macaron_half_ffn(x, *p["ffn2"])
    return x
```

```python
import functools

import jax
import jax.numpy as jnp
from jax import lax
from jax.experimental import pallas as pl
from jax.experimental.pallas import tpu as pltpu

BF = jnp.bfloat16
F32 = jnp.float32

EPS = 1e-6
LNX_EPS = 64e-5
HEAD_DIM = 64
NSA_HEADS = 8
NSA_GROUPS = 2
HEADS_PER_GROUP = NSA_HEADS // NSA_GROUPS
CMP_STRIDE = 16
CMP_BLOCK = 32
CMP_HIDDEN = 128
SLC_BLOCK = 64
SLC_TOP_N = 16
WINDOW = 512
Q_BLOCK = 128
FORCE_BONUS = 1e4
CONV_WIDTH = 512
RWKV_HEAD = 64
RWKV_CHUNK = 64
LANES = 128
POS_SHIFT = 7
POS_SPLIT = 1 << POS_SHIFT
NEG = -1e30
VMEM_LIMIT = 56 * 1024 * 1024

NT_DIMS = (((1,), (1,)), ((), ()))
TN_DIMS = (((0,), (0,)), ((), ()))


def _dot(a, b):
    return jnp.dot(a, b, preferred_element_type=F32)


def _dot_nt(a, b):
    return lax.dot_general(a, b, NT_DIMS, preferred_element_type=F32)


def _dot_tn(a, b):
    return lax.dot_general(a, b, TN_DIMS, preferred_element_type=F32)


def _split_dot(x, w):
    hi = x.astype(BF)
    lo = (x - hi.astype(F32)).astype(BF)
    return _dot(hi, w) + _dot(lo, w)


def _rmsnorm(x, g):
    ms = jnp.mean(x * x, axis=-1, keepdims=True)
    return x * lax.rsqrt(ms + EPS) * g


def _sigmoid(x):
    return 1.0 / (1.0 + jnp.exp(-x))


def _const_spec(shape):
    zeros = (0,) * len(shape)
    return pl.BlockSpec(shape, lambda *_: zeros)


def _params(n_grid):
    return pltpu.CompilerParams(dimension_semantics=("arbitrary",) * n_grid,
                                vmem_limit_bytes=VMEM_LIMIT)


def _ffn_kernel(x_ref, pre_ref, post_ref, wg_ref, wu_ref, wd_ref, o_ref, *, ff_chunk):
    x = x_ref[...]
    h = _rmsnorm(x, pre_ref[...]).astype(BF)
    acc = jnp.zeros(x.shape, F32)
    for c0 in range(0, wg_ref.shape[1], ff_chunk):
        g = _dot(h, wg_ref[:, c0:c0 + ff_chunk])
        u = _dot(h, wu_ref[:, c0:c0 + ff_chunk])
        a = (g * _sigmoid(g) * u).astype(BF)
        acc = acc + _dot(a, wd_ref[c0:c0 + ff_chunk, :])
    o_ref[...] = x + 0.5 * _rmsnorm(acc, post_ref[...])


def _ffn(x2, pre_g, post_g, w_gate, w_up, w_down, *, tm=512):
    t, d = x2.shape
    dff = w_gate.shape[1]
    ff_chunk = dff // 2 if (dff // 2) % LANES == 0 else dff
    row = pl.BlockSpec((tm, d), lambda i: (i, 0))
    return pl.pallas_call(
        functools.partial(_ffn_kernel, ff_chunk=ff_chunk),
        grid=(t // tm,),
        in_specs=[row, _const_spec((1, d)), _const_spec((1, d)),
                  _const_spec((d, dff)), _const_spec((d, dff)), _const_spec((dff, d))],
        out_specs=row,
        out_shape=jax.ShapeDtypeStruct((t, d), F32),
        compiler_params=_params(1),
        name="ffn",
    )(x2, pre_g.reshape(1, d), post_g.reshape(1, d),
      w_gate.astype(BF), w_up.astype(BF), w_down.astype(BF))


_C_Q = 0
_C_G = _C_Q + NSA_HEADS * LANES
_C_KC = _C_G + NSA_GROUPS * LANES
_C_VC = _C_KC + LANES
_C_KS = _C_VC + LANES
_C_CONV = _C_KS + 4 * NSA_GROUPS * LANES
_C_END = _C_CONV + 3 * CONV_WIDTH


def _l0_in_weight(w_in):
    d = w_in.shape[0]
    nsa_w = NSA_HEADS * HEAD_DIM
    kv_w = NSA_GROUPS * HEAD_DIM
    o = 0
    q = w_in[:, o:o + nsa_w]; o += nsa_w
    g = w_in[:, o:o + 3 * NSA_HEADS]; o += 3 * NSA_HEADS
    kvs = []
    for _ in range(6):
        kvs.append(w_in[:, o:o + kv_w]); o += kv_w
    conv = w_in[:, o:]
    zpad = lambda w, n: jnp.pad(w, ((0, 0), (0, n - w.shape[1])))
    cols = [zpad(q[:, h * HEAD_DIM:(h + 1) * HEAD_DIM], LANES) for h in range(NSA_HEADS)]
    gpg = 3 * HEADS_PER_GROUP
    cols += [zpad(g[:, i * gpg:(i + 1) * gpg], LANES) for i in range(NSA_GROUPS)]
    cols += [kvs[0], kvs[1]]
    for w in kvs[2:]:
        cols += [zpad(w[:, i * HEAD_DIM:(i + 1) * HEAD_DIM], LANES) for i in range(NSA_GROUPS)]
    cols.append(conv)
    w = jnp.concatenate(cols, axis=1)
    assert w.shape == (d, _C_END)
    return w.astype(BF)


def _l0_in_kernel(x_ref, pre_ref, w_ref, cw_ref, cb_ref,
                  q_ref, g_ref, kcr_ref, vcr_ref, ks_ref, vs_ref, kw_ref, vw_ref, oc_ref,
                  zprev_ref):
    si = pl.program_id(1)
    tm = x_ref.shape[1]
    h = _rmsnorm(x_ref[0], pre_ref[...]).astype(BF)
    lane = lax.broadcasted_iota(jnp.int32, (1, LANES), 1)
    pos = si * tm + lax.broadcasted_iota(jnp.int32, (tm, 1), 0)
    pos_hi = (pos >> POS_SHIFT).astype(F32)
    pos_lo = (pos & (POS_SPLIT - 1)).astype(F32)

    for hd in range(NSA_HEADS):
        slope = 2.0 ** (-(hd + 1))
        qh = _dot(h, w_ref[:, _C_Q + hd * LANES:_C_Q + (hd + 1) * LANES]) * (HEAD_DIM ** -0.5)
        qh = jnp.where(lane == HEAD_DIM, POS_SPLIT * slope, jnp.where(lane == HEAD_DIM + 1, slope, qh))
        q_ref[0, :, hd * LANES:(hd + 1) * LANES] = qh.astype(BF)

    g_ref[0] = _sigmoid(_dot(h, w_ref[:, _C_G:_C_G + NSA_GROUPS * LANES]))
    kcr_ref[0] = _dot(h, w_ref[:, _C_KC:_C_KC + LANES])
    vcr_ref[0] = _dot(h, w_ref[:, _C_VC:_C_VC + LANES])

    for n, (ref, is_key) in enumerate(((ks_ref, True), (vs_ref, False), (kw_ref, True), (vw_ref, False))):
        for gi in range(NSA_GROUPS):
            c0 = _C_KS + (n * NSA_GROUPS + gi) * LANES
            t = _dot(h, w_ref[:, c0:c0 + LANES])
            if is_key:
                t = jnp.where(lane == HEAD_DIM, pos_hi, jnp.where(lane == HEAD_DIM + 1, pos_lo, t))
            else:
                t = jnp.where(lane == HEAD_DIM, 1.0, t)
            ref[0, gi] = t.astype(BF)

    @pl.when(si == 0)
    def _():
        zprev_ref[...] = jnp.zeros(zprev_ref.shape, F32)

    cb = _dot(h, w_ref[:, _C_CONV:_C_CONV + CONV_WIDTH])
    cc = _dot(h, w_ref[:, _C_CONV + CONV_WIDTH:_C_CONV + 2 * CONV_WIDTH])
    cu = _dot(h, w_ref[:, _C_CONV + 2 * CONV_WIDTH:_C_CONV + 3 * CONV_WIDTH])
    z = cc * cu
    prev = zprev_ref[...]
    rows = lax.broadcasted_iota(jnp.int32, (tm, 1), 0)
    zm1 = jnp.where(rows == 0, prev[7:8], pltpu.roll(z, 1, axis=0))
    zm2 = jnp.where(rows == 0, prev[6:7], jnp.where(rows == 1, prev[7:8], pltpu.roll(z, 2, axis=0)))
    y = cw_ref[0:1] * zm2 + cw_ref[1:2] * zm1 + cw_ref[2:3] * z
    oc_ref[0] = (cb * (y + cb_ref[...])).astype(BF)
    zprev_ref[...] = z[tm - 8:tm]


def _l0_in(x, pre_g, w_all, conv_w, conv_b, *, tm=512):
    b, s, d = x.shape
    row = lambda w: pl.BlockSpec((1, tm, w), lambda bi, si: (bi, si, 0))
    grp = pl.BlockSpec((1, NSA_GROUPS, tm, LANES), lambda bi, si: (bi, 0, si, 0))
    sds = jax.ShapeDtypeStruct
    kv_shape = sds((b, NSA_GROUPS, s, LANES), BF)
    return pl.pallas_call(
        _l0_in_kernel,
        grid=(b, s // tm),
        in_specs=[row(d), _const_spec((1, d)), _const_spec(w_all.shape),
                  _const_spec(conv_w.shape), _const_spec((1, CONV_WIDTH))],
        out_specs=[row(NSA_HEADS * LANES), row(NSA_GROUPS * LANES), row(LANES), row(LANES),
                   grp, grp, grp, grp, row(CONV_WIDTH)],
        out_shape=[sds((b, s, NSA_HEADS * LANES), BF), sds((b, s, NSA_GROUPS * LANES), F32),
                   sds((b, s, LANES), F32), sds((b, s, LANES), F32),
                   kv_shape, kv_shape, kv_shape, kv_shape, sds((b, s, CONV_WIDTH), BF)],
        scratch_shapes=[pltpu.VMEM((8, CONV_WIDTH), F32)],
        compiler_params=_params(2),
        name="l0_in",
    )(x, pre_g.reshape(1, d), w_all, conv_w, conv_b.reshape(1, CONV_WIDTH))


def _cmp_weights(pe, w1, w2):
    half = CMP_STRIDE * HEAD_DIM
    def expand(w):
        w = w.reshape(CMP_STRIDE, HEAD_DIM, CMP_HIDDEN)
        out = jnp.zeros((CMP_STRIDE, NSA_GROUPS, HEAD_DIM, NSA_GROUPS, CMP_HIDDEN), F32)
        for gi in range(NSA_GROUPS):
            out = out.at[:, gi, :, gi, :].set(w)
        return out.reshape(CMP_STRIDE * NSA_GROUPS * HEAD_DIM, NSA_GROUPS * CMP_HIDDEN).astype(BF)
    pe_row = lambda p: jnp.tile(p, (1, NSA_GROUPS)).reshape(1, CMP_STRIDE * NSA_GROUPS * HEAD_DIM)
    w2p = jnp.pad(w2, ((0, 0), (0, LANES - HEAD_DIM))).astype(BF)
    return (pe_row(pe[:CMP_STRIDE]), pe_row(pe[CMP_STRIDE:]), expand(w1[:half]), expand(w1[half:]), w2p)


def _gelu_tanh(x):
    return 0.5 * x * (1.0 + jnp.tanh(0.7978845608028654 * (x + 0.044715 * (x * x * x))))


def _cmp_kernel(x_ref, pet_ref, peb_ref, wt_ref, wb_ref, w2_ref, o_ref, *, is_key):
    nb = x_ref.shape[1]
    hw = x_ref.shape[2] // 4
    top, bot = [], []
    for c in range(4):
        xc = x_ref[0, :, c * hw:(c + 1) * hw]
        top.append(_dot((xc + pet_ref[...]).astype(BF), wt_ref[...]))
        bot.append(_dot((xc + peb_ref[...]).astype(BF), wb_ref[...]))
    lane = lax.broadcasted_iota(jnp.int32, (1, LANES), 1)
    j = lax.broadcasted_iota(jnp.int32, (nb, 1), 0)
    for c in range(4):
        nxt = bot[c + 1] if c < 3 else pltpu.roll(bot[0], nb - 1, axis=0)
        hid = _gelu_tanh(top[c] + nxt)
        n = 4 * j + c
        end = n * CMP_STRIDE + (CMP_BLOCK - 1)
        exists = n < 4 * nb - 1
        for gi in range(NSA_GROUPS):
            t = _dot(hid[:, gi * CMP_HIDDEN:(gi + 1) * CMP_HIDDEN].astype(BF), w2_ref[...])
            t = jnp.where(exists, t, 0.0)
            if is_key:
                t = jnp.where(lane == HEAD_DIM, (end >> POS_SHIFT).astype(F32),
                              jnp.where(lane == HEAD_DIM + 1, (end & (POS_SPLIT - 1)).astype(F32), t))
            else:
                t = jnp.where(lane == HEAD_DIM, 1.0, t)
            o_ref[0, gi, c * nb:(c + 1) * nb, :] = t.astype(BF)


def _compress(raw, weights, *, is_key):
    b, s, _ = raw.shape
    nb = s // (4 * CMP_STRIDE)
    x = raw.reshape(b, nb, 4 * CMP_STRIDE * LANES)
    pet, peb, wt, wb, w2p = weights
    return pl.pallas_call(
        functools.partial(_cmp_kernel, is_key=is_key),
        grid=(b,),
        in_specs=[pl.BlockSpec((1, nb, x.shape[2]), lambda bi: (bi, 0, 0)),
                  _const_spec(pet.shape), _const_spec(peb.shape), _const_spec(wt.shape),
                  _const_spec(wb.shape), _const_spec(w2p.shape)],
        out_specs=pl.BlockSpec((1, NSA_GROUPS, 4 * nb, LANES), lambda bi: (bi, 0, 0, 0)),
        out_shape=jax.ShapeDtypeStruct((b, NSA_GROUPS, 4 * nb, LANES), BF),
        compiler_params=_params(1),
        name="compress",
    )(x, pet, peb, wt, wb, w2p)


def _softmax_update(carry, s, v):
    m, acc = carry
    m_new = jnp.maximum(m, jnp.max(s, axis=-1, keepdims=True))
    p = jnp.exp(s - m_new)
    acc = jnp.exp(m - m_new) * acc + _dot(p.astype(BF), v)
    return m_new, acc


def _normalize(acc):
    return acc[:, :HEAD_DIM] / jnp.maximum(acc[:, HEAD_DIM:HEAD_DIM + 1], 1e-30)


def _nsa_kernel(q_ref, g_ref, kc_ref, vc_ref, ks_ref, vs_ref, *rest):
    nwin = WINDOW // Q_BLOCK + 1
    kw_refs, vw_refs, o_ref = rest[:nwin], rest[nwin:2 * nwin], rest[2 * nwin]
    qi = pl.program_id(2)
    q0 = qi * Q_BLOCK
    rq = HEADS_PER_GROUP * Q_BLOCK
    ncp = kc_ref.shape[2]
    nb = ncp // 4
    nb_shift = nb.bit_length() - 1
    assert nb == 1 << nb_shift

    qs = jnp.concatenate([q_ref[0, :, r * LANES:(r + 1) * LANES] for r in range(HEADS_PER_GROUP)], axis=0)
    t1 = q0 + lax.broadcasted_iota(jnp.int32, (Q_BLOCK, 1), 0)
    t4 = q0 + (lax.broadcasted_iota(jnp.int32, (rq, 1), 0) & (Q_BLOCK - 1))

    s = _dot_nt(qs, kc_ref[0, 0])
    col = lax.broadcasted_iota(jnp.int32, (1, ncp), 1)
    cmp_end = (4 * (col & (nb - 1)) + (col >> nb_shift)) * CMP_STRIDE + (CMP_BLOCK - 1)
    valid = cmp_end <= t4
    s = jnp.where(valid, s, NEG)
    m = jnp.max(s, axis=-1, keepdims=True)
    p = jnp.where(valid, jnp.exp(s - m), 0.0)
    p = p / jnp.maximum(jnp.sum(p, axis=-1, keepdims=True), 1e-30)
    o_cmp = _dot(p.astype(BF), vc_ref[0, 0])[:, :HEAD_DIM]

    ps = p[0:Q_BLOCK]
    for r in range(1, HEADS_PER_GROUP):
        ps = ps + p[r * Q_BLOCK:(r + 1) * Q_BLOCK]
    parts = [ps[:, c * nb:(c + 1) * nb] for c in range(4)]
    jl = lax.broadcasted_iota(jnp.int32, (1, nb), 1)
    prev = jnp.where(jl == 0, 0.0, pltpu.roll(parts[3], 1, axis=1))
    imp = (parts[0] + parts[1] + parts[2] + parts[3]) - 0.5 * parts[3] + 0.5 * prev
    blk_t = t1 >> 6
    forced = (jl == 0) | (jl == blk_t) | (jl == blk_t - 1)
    imp = jnp.where(forced, imp + FORCE_BONUS, imp)
    eligible = jl <= blk_t
    work = jnp.where(eligible, imp, NEG)
    jf = jl.astype(F32)
    sel = jnp.zeros((Q_BLOCK, nb), F32)
    for _ in range(min(SLC_TOP_N, nb)):
        mx = jnp.max(work, axis=-1, keepdims=True)
        first = jnp.min(jnp.where(work == mx, jf, float(nb)), axis=-1, keepdims=True)
        hit = jf == first
        sel = jnp.where(hit, 1.0, sel)
        work = jnp.where(hit, 3.0 * NEG, work)
    selb = jnp.where(eligible, sel, 0.0).astype(BF)

    tk = Q_BLOCK
    jrow = lax.broadcasted_iota(jnp.int32, (nb, tk), 0)
    kblk = lax.broadcasted_iota(jnp.int32, (nb, tk), 1) >> 6
    kcol = lax.broadcasted_iota(jnp.int32, (1, tk), 1)

    def slc_tile(kt, carry, causal):
        k0 = pl.multiple_of(kt * tk, tk)
        kt_ = ks_ref[0, 0, pl.ds(k0, tk), :]
        vt_ = vs_ref[0, 0, pl.ds(k0, tk), :]
        expand = jnp.where(jrow == kblk + kt * (tk // SLC_BLOCK), 1.0, 0.0).astype(BF)
        mk = _dot(selb, expand)
        if causal:
            mk = jnp.where(k0 + kcol <= t1, mk, 0.0)
        mk4 = jnp.concatenate([mk] * HEADS_PER_GROUP, axis=0)
        s_ = jnp.where(mk4 > 0.5, _dot_nt(qs, kt_), NEG)
        return _softmax_update(carry, s_, vt_)

    init = (jnp.full((rq, 1), NEG, F32), jnp.zeros((rq, LANES), F32))
    carry = lax.fori_loop(0, qi, lambda kt, c: slc_tile(kt, c, False), init)
    o_slc = _normalize(slc_tile(qi, carry, True)[1])

    carry = init
    for i in range(nwin):
        kti = qi - (nwin - 1) + i
        k0 = jnp.where(kti >= 0, kti * Q_BLOCK, 1 << 24)
        d = t4 - (k0 + kcol)
        ok = (d >= 0) & (d < WINDOW)
        s_ = jnp.where(ok, _dot_nt(qs, kw_refs[i][0, 0]), NEG)
        carry = _softmax_update(carry, s_, vw_refs[i][0, 0])
    o_win = _normalize(carry[1])

    for r in range(HEADS_PER_GROUP):
        rows = slice(r * Q_BLOCK, (r + 1) * Q_BLOCK)
        gate = lambda c: g_ref[0, :, 3 * r + c:3 * r + c + 1]
        o = gate(0) * o_cmp[rows] + gate(1) * o_slc[rows] + gate(2) * o_win[rows]
        o_ref[0, :, r * HEAD_DIM:(r + 1) * HEAD_DIM] = o.astype(BF)


def _nsa(q, gates, kc, vc, ks, vs, kw, vw):
    b, s, _ = q.shape
    ncp = kc.shape[2]
    nwin = WINDOW // Q_BLOCK + 1
    gw = HEADS_PER_GROUP * LANES
    full = lambda n: pl.BlockSpec((1, 1, n, LANES), lambda bi, gi, qi: (bi, gi, 0, 0))
    win = lambda i: pl.BlockSpec((1, 1, Q_BLOCK, LANES),
                                 lambda bi, gi, qi: (bi, gi, jnp.maximum(qi - (nwin - 1) + i, 0), 0))
    return pl.pallas_call(
        _nsa_kernel,
        grid=(b, NSA_GROUPS, s // Q_BLOCK),
        in_specs=[pl.BlockSpec((1, Q_BLOCK, gw), lambda bi, gi, qi: (bi, qi, gi)),
                  pl.BlockSpec((1, Q_BLOCK, LANES), lambda bi, gi, qi: (bi, qi, gi)),
                  full(ncp), full(ncp), full(s), full(s)]
                 + [win(i) for i in range(nwin)] * 2,
        out_specs=pl.BlockSpec((1, Q_BLOCK, HEADS_PER_GROUP * HEAD_DIM), lambda bi, gi, qi: (bi, qi, gi)),
        out_shape=jax.ShapeDtypeStruct((b, s, NSA_HEADS * HEAD_DIM), BF),
        compiler_params=_params(3),
        name="nsa",
    )(q, gates, kc, vc, ks, vs, *([kw] * nwin), *([vw] * nwin))


def _l0_out_kernel(x_ref, a_ref, c_ref, wa_ref, wc_ref, post_ref, o_ref):
    y = _dot(a_ref[...], wa_ref[...]) + _dot(c_ref[...], wc_ref[...])
    o_ref[...] = x_ref[...] + _rmsnorm(y, post_ref[...])


def _l0_out(x2, o_nsa, o_conv, w_out, post_g, *, tm=512):
    t, d = x2.shape
    na, nc = o_nsa.shape[1], o_conv.shape[1]
    row = lambda w: pl.BlockSpec((tm, w), lambda i: (i, 0))
    return pl.pallas_call(
        _l0_out_kernel,
        grid=(t // tm,),
        in_specs=[row(d), row(na), row(nc), _const_spec((na, d)), _const_spec((nc, d)), _const_spec((1, d))],
        out_specs=row(d),
        out_shape=jax.ShapeDtypeStruct((t, d), F32),
        compiler_params=_params(1),
        name="l0_out",
    )(x2, o_nsa, o_conv, w_out[:na].astype(BF), w_out[na:].astype(BF), post_g.reshape(1, d))


def _head_indicators(d):
    ch = jnp.arange(d)[:, None] // RWKV_HEAD
    ind = (ch == jnp.arange(LANES)[None, :]).astype(BF)
    return ind, ind.T


def _rwkv_pre_kernel(x_ref, pre_ref, mu_ref, wr_ref, wk_ref, wv_ref, w0_ref, wd1_ref, wd2_ref,
                     a0_ref, wa1_ref, wa2_ref, wg1_ref, wg2_ref, kk_ref, ka_ref, ind_ref, indt_ref,
                     r_out, lw_out, k_out, v_out, kk_out, kka_out, g_out, hprev_ref):
    si = pl.program_id(1)
    tm = x_ref.shape[1]

    @pl.when(si == 0)
    def _():
        hprev_ref[...] = jnp.zeros(hprev_ref.shape, F32)

    h = _rmsnorm(x_ref[0], pre_ref[...])
    rows = lax.broadcasted_iota(jnp.int32, (tm, 1), 0)
    xx = jnp.where(rows == 0, hprev_ref[7:8], pltpu.roll(h, 1, axis=0)) - h
    hprev_ref[...] = h[tm - 8:tm]
    mix = lambda i: (h + xx * mu_ref[i:i + 1]).astype(BF)
    xr, xw, xk, xv, xa, xg = [mix(i) for i in range(6)]

    r = _dot(xr, wr_ref[...])
    k = _dot(xk, wk_ref[...])
    v = _dot(xv, wv_ref[...])
    z = w0_ref[...] + _dot(jnp.tanh(_dot(xw, wd1_ref[...])).astype(BF), wd2_ref[...])
    softplus = jnp.maximum(-z, 0.0) + jnp.log(1.0 + jnp.exp(-jnp.abs(z)))
    lw = -jnp.exp(-softplus - 0.5)
    a = _sigmoid(a0_ref[...] + _dot(_dot(xa, wa1_ref[...]).astype(BF), wa2_ref[...]))
    g = _dot(_sigmoid(_dot(xg, wg1_ref[...])).astype(BF), wg2_ref[...])

    kraw = k * kk_ref[...]
    ss = _split_dot(kraw * kraw, ind_ref[...])
    inv = lax.rsqrt(jnp.maximum(ss, 1e-12))
    kk = kraw * _split_dot(inv, indt_ref[...])

    r_out[0] = r
    lw_out[0] = lw
    k_out[0] = k * (1.0 + (a - 1.0) * ka_ref[...])
    v_out[0] = v
    kk_out[0] = kk
    kka_out[0] = kk * a
    g_out[0] = g


def _rwkv_pre(x, pre_g, mu, w_r, w_k, w_v, w0, w_dec1, w_dec2, a0, w_a1, w_a2, w_g1, w_g2,
              k_k, k_a, ind, indt, *, tm=256):
    b, s, d = x.shape
    row = pl.BlockSpec((1, tm, d), lambda bi, si: (bi, si, 0))
    vec = lambda a: a.reshape(1, d)
    ops = [x, vec(pre_g), mu, w_r.astype(BF), w_k.astype(BF), w_v.astype(BF), vec(w0),
           w_dec1.astype(BF), w_dec2.astype(BF), vec(a0), w_a1.astype(BF), w_a2.astype(BF),
           w_g1.astype(BF), w_g2.astype(BF), vec(k_k), vec(k_a), ind, indt]
    return pl.pallas_call(
        _rwkv_pre_kernel,
        grid=(b, s // tm),
        in_specs=[row] + [_const_spec(o.shape) for o in ops[1:]],
        out_specs=[row] * 7,
        out_shape=[jax.ShapeDtypeStruct((b, s, d), F32)] * 7,
        scratch_shapes=[pltpu.VMEM((8, d), F32)],
        compiler_params=_params(2),
        name="rwkv_pre",
    )(*ops)


def _rwkv_scan_kernel(r_ref, lw_ref, k_ref, v_ref, kk_ref, kka_ref, y_ref, state_ref):
    c = r_ref.shape[1]
    slab = 4 * RWKV_HEAD
    n_slab = r_ref.shape[2] // slab
    rows4 = 4 * c

    @pl.when(pl.program_id(1) == 0)
    def _():
        state_ref[...] = jnp.zeros(state_ref.shape, F32)

    ri = lax.broadcasted_iota(jnp.int32, (rows4, 1), 0)
    ci = lax.broadcasted_iota(jnp.int32, (1, rows4), 1)
    li = lax.broadcasted_iota(jnp.int32, (1, slab), 1)
    c_shift = c.bit_length() - 1
    h_shift = RWKV_HEAD.bit_length() - 1
    assert c == 1 << c_shift and RWKV_HEAD == 1 << h_shift
    head_match = (ri >> c_shift) == (li >> h_shift)
    t_r, t_c = ri & (c - 1), ci & (c - 1)
    strict = t_c < t_r
    incl = t_c <= t_r
    eye = jnp.where(ri == ci, 1.0, 0.0)
    tri = jnp.where(lax.broadcasted_iota(jnp.int32, (c, c), 1) <= lax.broadcasted_iota(jnp.int32, (c, c), 0),
                    1.0, 0.0).astype(BF)
    state_match = (lax.broadcasted_iota(jnp.int32, (slab, 1), 0) >> h_shift) == (li >> h_shift)

    def block_diag(x):
        return jnp.where(head_match, jnp.concatenate([x] * 4, axis=0), 0.0)

    def fold(x):
        return x[0:c] + x[c:2 * c] + x[2 * c:3 * c] + x[3 * c:4 * c]

    bf = lambda x: x.astype(BF)

    for hs in range(n_slab):
        sl = slice(hs * slab, (hs + 1) * slab)
        lw = lw_ref[0, :, sl]
        hi = bf(lw)
        rem = lw - hi.astype(F32)
        mid = bf(rem)
        lo = bf(rem - mid.astype(F32))
        cw = _dot(tri, hi) + _dot(tri, mid) + _dot(tri, lo)
        w_in = jnp.exp(cw)
        w_inv = jnp.exp(-cw)
        w_prev = jnp.exp(cw - lw)
        w_end = w_in[c - 1:c]

        kk = kk_ref[0, :, sl]
        a_t = -kk * w_prev
        b_t = kka_ref[0, :, sl] * w_inv
        k_t = k_ref[0, :, sl] * w_inv
        r_t = r_ref[0, :, sl] * w_in
        v = v_ref[0, :, sl]

        xa, xr = bf(block_diag(a_t)), bf(block_diag(r_t))
        yb, yk = bf(block_diag(b_t)), bf(block_diag(k_t))
        a_ab = jnp.where(strict, _dot_nt(xa, yb), 0.0)
        a_ak = jnp.where(strict, _dot_nt(xa, yk), 0.0)
        a_rb = jnp.where(incl, _dot_nt(xr, yb), 0.0)
        a_rk = jnp.where(incl, _dot_nt(xr, yk), 0.0)

        inv = eye + a_ab
        pw = a_ab
        for _ in range(max(c.bit_length() - 2, 0)):
            pw = _dot(bf(pw), bf(pw))
            inv = inv + _dot(bf(inv), bf(pw))

        state = state_ref[hs]
        sb = bf(state)
        vbd = bf(block_diag(v))
        u_bd = _dot(bf(inv), bf(block_diag(_dot_nt(bf(a_t), sb)) + _dot(bf(a_ak), vbd)))
        y_bd = block_diag(_dot_nt(bf(r_t), sb)) + _dot(bf(a_rb), bf(u_bd)) + _dot(bf(a_rk), vbd)
        y_ref[0, :, sl] = fold(y_bd)

        uv = jnp.concatenate([bf(fold(u_bd)), bf(v)], axis=0)
        bk = jnp.concatenate([bf(b_t * w_end), bf(k_t * w_end)], axis=0)
        state_ref[hs] = jnp.where(state_match, state * w_end + _dot_tn(uv, bk), 0.0)


def _rwkv_scan(r, lw, k, v, kk, kka):
    b, s, d = r.shape
    c = RWKV_CHUNK
    slab = 4 * RWKV_HEAD
    blk = pl.BlockSpec((1, c, d), lambda bi, ci: (bi, ci, 0))
    return pl.pallas_call(
        _rwkv_scan_kernel,
        grid=(b, s // c),
        in_specs=[blk] * 6,
        out_specs=blk,
        out_shape=jax.ShapeDtypeStruct((b, s, d), F32),
        scratch_shapes=[pltpu.VMEM((d // slab, slab, slab), F32)],
        compiler_params=_params(2),
        name="rwkv_scan",
    )(r, lw, k, v, kk, kka)


def _rwkv_post_kernel(x_ref, y_ref, r_ref, k_ref, v_ref, g_ref, wo_ref, lng_ref, lnb_ref, rk_ref,
                      post_ref, ind_ref, indt_ref, o_ref):
    ind, indt = ind_ref[...], indt_ref[...]
    y = y_ref[...]
    inv_n = 1.0 / RWKV_HEAD
    mean = _split_dot(_split_dot(y, ind) * inv_n, indt)
    yc = y - mean
    var = _split_dot(yc * yc, ind) * inv_n
    rstd = _split_dot(lax.rsqrt(var + LNX_EPS), indt)
    yn = yc * rstd * lng_ref[...] + lnb_ref[...]
    bonus = _split_dot(_split_dot(r_ref[...] * k_ref[...] * rk_ref[...], ind), indt) * v_ref[...]
    out = _dot(((yn + bonus) * g_ref[...]).astype(BF), wo_ref[...])
    o_ref[...] = x_ref[...] + _rmsnorm(out, post_ref[...])


def _rwkv_post(x2, y, r, k, v, g, w_o, lnx_g, lnx_b, r_k, post_g, ind, indt, *, tm=256):
    t, d = x2.shape
    row = pl.BlockSpec((tm, d), lambda i: (i, 0))
    vec = lambda a: a.reshape(1, d)
    ops = [x2, y, r, k, v, g, w_o.astype(BF), vec(lnx_g), vec(lnx_b), vec(r_k), vec(post_g), ind, indt]
    return pl.pallas_call(
        _rwkv_post_kernel,
        grid=(t // tm,),
        in_specs=[row] * 6 + [_const_spec(o.shape) for o in ops[6:]],
        out_specs=row,
        out_shape=jax.ShapeDtypeStruct((t, d), F32),
        compiler_params=_params(1),
        name="rwkv_post",
    )(*ops)


def _layer0_mixer(x, pre_g, post_g, w_in, pe_k, w1_k, w2_k, pe_v, w1_v, w2_v, conv_w, conv_b, w_out):
    b, s, d = x.shape
    q, gates, kcr, vcr, ks, vs, kw, vw, o_conv = _l0_in(x, pre_g, _l0_in_weight(w_in), conv_w, conv_b)
    kc = _compress(kcr, _cmp_weights(pe_k, w1_k, w2_k), is_key=True)
    vc = _compress(vcr, _cmp_weights(pe_v, w1_v, w2_v), is_key=False)
    o_nsa = _nsa(q, gates, kc, vc, ks, vs, kw, vw)
    t = b * s
    return _l0_out(x.reshape(t, d), o_nsa.reshape(t, -1), o_conv.reshape(t, -1), w_out, post_g).reshape(b, s, d)


def _layer1_mixer(x, pre_g, post_g, mu, w_r, w_k, w_v, w_o, w0, w_dec1, w_dec2, a0, w_a1, w_a2,
                  w_g1, w_g2, k_k, k_a, r_k, lnx_g, lnx_b):
    b, s, d = x.shape
    ind, indt = _head_indicators(d)
    r, lw, k, v, kk, kka, g = _rwkv_pre(x, pre_g, mu, w_r, w_k, w_v, w0, w_dec1, w_dec2, a0,
                                        w_a1, w_a2, w_g1, w_g2, k_k, k_a, ind, indt)
    y = _rwkv_scan(r, lw, k, v, kk, kka)
    f = lambda a: a.reshape(b * s, d)
    return _rwkv_post(f(x), f(y), f(r), f(k), f(v), f(g), w_o, lnx_g, lnx_b, r_k, post_g,
                      ind, indt).reshape(b, s, d)


def kernel(x, l0_ffn1_pre_g, l0_ffn1_post_g, l0_ffn1_w_gate, l0_ffn1_w_up, l0_ffn1_w_down, l0_mix_pre_g, l0_mix_post_g, l0_w_in, l0_cmp_pe_k, l0_cmp_w1_k, l0_cmp_w2_k, l0_cmp_pe_v, l0_cmp_w1_v, l0_cmp_w2_v, l0_conv_w, l0_conv_b, l0_w_out, l0_ffn2_pre_g, l0_ffn2_post_g, l0_ffn2_w_gate, l0_ffn2_w_up, l0_ffn2_w_down, l1_ffn1_pre_g, l1_ffn1_post_g, l1_ffn1_w_gate, l1_ffn1_w_up, l1_ffn1_w_down, l1_mix_pre_g, l1_mix_post_g, l1_mu, l1_w_r, l1_w_k, l1_w_v, l1_w_o, l1_w0, l1_w_dec1, l1_w_dec2, l1_a0, l1_w_a1, l1_w_a2, l1_w_g1, l1_w_g2, l1_k_k, l1_k_a, l1_r_k, l1_lnx_g, l1_lnx_b, l1_ffn2_pre_g, l1_ffn2_post_g, l1_ffn2_w_gate, l1_ffn2_w_up, l1_ffn2_w_down):
    b, s, d = x.shape
    ffn = lambda a, *w: _ffn(a.reshape(b * s, d), *w).reshape(b, s, d)
    x = ffn(x, l0_ffn1_pre_g, l0_ffn1_post_g, l0_ffn1_w_gate, l0_ffn1_w_up, l0_ffn1_w_down)
    x = _layer0_mixer(x, l0_mix_pre_g, l0_mix_post_g, l0_w_in, l0_cmp_pe_k, l0_cmp_w1_k, l0_cmp_w2_k,
                      l0_cmp_pe_v, l0_cmp_w1_v, l0_cmp_w2_v, l0_conv_w, l0_conv_b, l0_w_out)
    x = ffn(x, l0_ffn2_pre_g, l0_ffn2_post_g, l0_ffn2_w_gate, l0_ffn2_w_up, l0_ffn2_w_down)
    x = ffn(x, l1_ffn1_pre_g, l1_ffn1_post_g, l1_ffn1_w_gate, l1_ffn1_w_up, l1_ffn1_w_down)
    x = _layer1_mixer(x, l1_mix_pre_g, l1_mix_post_g, l1_mu, l1_w_r, l1_w_k, l1_w_v, l1_w_o, l1_w0,
                      l1_w_dec1, l1_w_dec2, l1_a0, l1_w_a1, l1_w_a2, l1_w_g1, l1_w_g2, l1_k_k, l1_k_a,
                      l1_r_k, l1_lnx_g, l1_lnx_b)
    x = ffn(x, l1_ffn2_pre_g, l1_ffn2_post_g, l1_ffn2_w_gate, l1_ffn2_w_up, l1_ffn2_w_down)
    return x
```

```python
import functools

import jax
import jax.numpy as jnp
from jax import lax
from jax.experimental import pallas as pl
from jax.experimental.pallas import tpu as pltpu

BF = jnp.bfloat16
F32 = jnp.float32

EPS = 1e-6
LNX_EPS = 64e-5
HEAD_DIM = 64
NSA_HEADS = 8
NSA_GROUPS = 2
HEADS_PER_GROUP = NSA_HEADS // NSA_GROUPS
CMP_STRIDE = 16
CMP_BLOCK = 32
CMP_HIDDEN = 128
SLC_BLOCK = 64
SLC_TOP_N = 16
WINDOW = 512
Q_BLOCK = 128
FORCE_BONUS = 1e4
CONV_WIDTH = 512
RWKV_HEAD = 64
RWKV_CHUNK = 64
LANES = 128
POS_SHIFT = 7
POS_SPLIT = 1 << POS_SHIFT
NEG = -1e30
VMEM_LIMIT = 56 * 1024 * 1024

NT_DIMS = (((1,), (1,)), ((), ()))
TN_DIMS = (((0,), (0,)), ((), ()))


def _dot(a, b):
    return jnp.dot(a, b, preferred_element_type=F32)


def _dot_nt(a, b):
    return lax.dot_general(a, b, NT_DIMS, preferred_element_type=F32)


def _dot_tn(a, b):
    return lax.dot_general(a, b, TN_DIMS, preferred_element_type=F32)


def _split_dot(x, w):
    hi = x.astype(BF)
    lo = (x - hi.astype(F32)).astype(BF)
    return _dot(hi, w) + _dot(lo, w)


def _rmsnorm(x, g):
    ms = jnp.mean(x * x, axis=-1, keepdims=True)
    return x * lax.rsqrt(ms + EPS) * g


def _sigmoid(x):
    return 1.0 / (1.0 + jnp.exp(-x))


def _const_spec(shape):
    zeros = (0,) * len(shape)
    return pl.BlockSpec(shape, lambda *_: zeros)


def _params(n_grid):
    return pltpu.CompilerParams(dimension_semantics=("arbitrary",) * n_grid,
                                vmem_limit_bytes=VMEM_LIMIT)


def _ffn_kernel(x_ref, pre_ref, post_ref, wg_ref, wu_ref, wd_ref, o_ref, *, ff_chunk):
    x = x_ref[...]
    h = _rmsnorm(x, pre_ref[...]).astype(BF)
    acc = jnp.zeros(x.shape, F32)
    for c0 in range(0, wg_ref.shape[1], ff_chunk):
        g = _dot(h, wg_ref[:, c0:c0 + ff_chunk])
        u = _dot(h, wu_ref[:, c0:c0 + ff_chunk])
        a = (g * _sigmoid(g) * u).astype(BF)
        acc = acc + _dot(a, wd_ref[c0:c0 + ff_chunk, :])
    o_ref[...] = x + 0.5 * _rmsnorm(acc, post_ref[...])


def _ffn(x2, pre_g, post_g, w_gate, w_up, w_down, *, tm=512):
    t, d = x2.shape
    dff = w_gate.shape[1]
    ff_chunk = dff // 2 if (dff // 2) % LANES == 0 else dff
    row = pl.BlockSpec((tm, d), lambda i: (i, 0))
    return pl.pallas_call(
        functools.partial(_ffn_kernel, ff_chunk=ff_chunk),
        grid=(t // tm,),
        in_specs=[row, _const_spec((1, d)), _const_spec((1, d)),
                  _const_spec((d, dff)), _const_spec((d, dff)), _const_spec((dff, d))],
        out_specs=row,
        out_shape=jax.ShapeDtypeStruct((t, d), F32),
        compiler_params=_params(1),
        name="ffn",
    )(x2, pre_g.reshape(1, d), post_g.reshape(1, d),
      w_gate.astype(BF), w_up.astype(BF), w_down.astype(BF))


_C_Q = 0
_C_G = _C_Q + NSA_HEADS * LANES
_C_KC = _C_G + NSA_GROUPS * LANES
_C_VC = _C_KC + LANES
_C_KS = _C_VC + LANES
_C_CONV = _C_KS + 4 * NSA_GROUPS * LANES
_C_END = _C_CONV + 3 * CONV_WIDTH


def _l0_in_weight(w_in):
    d = w_in.shape[0]
    nsa_w = NSA_HEADS * HEAD_DIM
    kv_w = NSA_GROUPS * HEAD_DIM
    o = 0
    q = w_in[:, o:o + nsa_w]; o += nsa_w
    g = w_in[:, o:o + 3 * NSA_HEADS]; o += 3 * NSA_HEADS
    kvs = []
    for _ in range(6):
        kvs.append(w_in[:, o:o + kv_w]); o += kv_w
    conv = w_in[:, o:]
    zpad = lambda w, n: jnp.pad(w, ((0, 0), (0, n - w.shape[1])))
    cols = [zpad(q[:, h * HEAD_DIM:(h + 1) * HEAD_DIM], LANES) for h in range(NSA_HEADS)]
    gpg = 3 * HEADS_PER_GROUP
    cols += [zpad(g[:, i * gpg:(i + 1) * gpg], LANES) for i in range(NSA_GROUPS)]
    cols += [kvs[0], kvs[1]]
    for w in kvs[2:]:
        cols += [zpad(w[:, i * HEAD_DIM:(i + 1) * HEAD_DIM], LANES) for i in range(NSA_GROUPS)]
    cols.append(conv)
    w = jnp.concatenate(cols, axis=1)
    assert w.shape == (d, _C_END)
    return w.astype(BF)


def _l0_in_kernel(x_ref, pre_ref, w_ref, cw_ref, cb_ref,
                  q_ref, g_ref, kcr_ref, vcr_ref, ks_ref, vs_ref, kw_ref, vw_ref, oc_ref,
                  zprev_ref):
    si = pl.program_id(1)
    tm = x_ref.shape[1]
    h = _rmsnorm(x_ref[0], pre_ref[...]).astype(BF)
    lane = lax.broadcasted_iota(jnp.int32, (1, LANES), 1)
    pos = si * tm + lax.broadcasted_iota(jnp.int32, (tm, 1), 0)
    pos_hi = (pos >> POS_SHIFT).astype(F32)
    pos_lo = (pos & (POS_SPLIT - 1)).astype(F32)

    for hd in range(NSA_HEADS):
        slope = 2.0 ** (-(hd + 1))
        qh = _dot(h, w_ref[:, _C_Q + hd * LANES:_C_Q + (hd + 1) * LANES]) * (HEAD_DIM ** -0.5)
        qh = jnp.where(lane == HEAD_DIM, POS_SPLIT * slope, jnp.where(lane == HEAD_DIM + 1, slope, qh))
        q_ref[0, :, hd * LANES:(hd + 1) * LANES] = qh.astype(BF)

    g_ref[0] = _sigmoid(_dot(h, w_ref[:, _C_G:_C_G + NSA_GROUPS * LANES]))
    kcr_ref[0] = _dot(h, w_ref[:, _C_KC:_C_KC + LANES])
    vcr_ref[0] = _dot(h, w_ref[:, _C_VC:_C_VC + LANES])

    for n, (ref, is_key) in enumerate(((ks_ref, True), (vs_ref, False), (kw_ref, True), (vw_ref, False))):
        for gi in range(NSA_GROUPS):
            c0 = _C_KS + (n * NSA_GROUPS + gi) * LANES
            t = _dot(h, w_ref[:, c0:c0 + LANES])
            if is_key:
                t = jnp.where(lane == HEAD_DIM, pos_hi, jnp.where(lane == HEAD_DIM + 1, pos_lo, t))
            else:
                t = jnp.where(lane == HEAD_DIM, 1.0, t)
            ref[0, gi] = t.astype(BF)

    @pl.when(si == 0)
    def _():
        zprev_ref[...] = jnp.zeros(zprev_ref.shape, F32)

    cb = _dot(h, w_ref[:, _C_CONV:_C_CONV + CONV_WIDTH])
    cc = _dot(h, w_ref[:, _C_CONV + CONV_WIDTH:_C_CONV + 2 * CONV_WIDTH])
    cu = _dot(h, w_ref[:, _C_CONV + 2 * CONV_WIDTH:_C_CONV + 3 * CONV_WIDTH])
    z = cc * cu
    prev = zprev_ref[...]
    rows = lax.broadcasted_iota(jnp.int32, (tm, 1), 0)
    zm1 = jnp.where(rows == 0, prev[7:8], pltpu.roll(z, 1, axis=0))
    zm2 = jnp.where(rows == 0, prev[6:7], jnp.where(rows == 1, prev[7:8], pltpu.roll(z, 2, axis=0)))
    y = cw_ref[0:1] * zm2 + cw_ref[1:2] * zm1 + cw_ref[2:3] * z
    oc_ref[0] = (cb * (y + cb_ref[...])).astype(BF)
    zprev_ref[...] = z[tm - 8:tm]


def _l0_in(x, pre_g, w_all, conv_w, conv_b, *, tm=512):
    b, s, d = x.shape
    row = lambda w: pl.BlockSpec((1, tm, w), lambda bi, si: (bi, si, 0))
    grp = pl.BlockSpec((1, NSA_GROUPS, tm, LANES), lambda bi, si: (bi, 0, si, 0))
    sds = jax.ShapeDtypeStruct
    kv_shape = sds((b, NSA_GROUPS, s, LANES), BF)
    return pl.pallas_call(
        _l0_in_kernel,
        grid=(b, s // tm),
        in_specs=[row(d), _const_spec((1, d)), _const_spec(w_all.shape),
                  _const_spec(conv_w.shape), _const_spec((1, CONV_WIDTH))],
        out_specs=[row(NSA_HEADS * LANES), row(NSA_GROUPS * LANES), row(LANES), row(LANES),
                   grp, grp, grp, grp, row(CONV_WIDTH)],
        out_shape=[sds((b, s, NSA_HEADS * LANES), BF), sds((b, s, NSA_GROUPS * LANES), F32),
                   sds((b, s, LANES), F32), sds((b, s, LANES), F32),
                   kv_shape, kv_shape, kv_shape, kv_shape, sds((b, s, CONV_WIDTH), BF)],
        scratch_shapes=[pltpu.VMEM((8, CONV_WIDTH), F32)],
        compiler_params=_params(2),
        name="l0_in",
    )(x, pre_g.reshape(1, d), w_all, conv_w, conv_b.reshape(1, CONV_WIDTH))


def _cmp_weights(pe, w1, w2):
    half = CMP_STRIDE * HEAD_DIM
    def expand(w):
        w = w.reshape(CMP_STRIDE, HEAD_DIM, CMP_HIDDEN)
        out = jnp.zeros((CMP_STRIDE, NSA_GROUPS, HEAD_DIM, NSA_GROUPS, CMP_HIDDEN), F32)
        for gi in range(NSA_GROUPS):
            out = out.at[:, gi, :, gi, :].set(w)
        return out.reshape(CMP_STRIDE * NSA_GROUPS * HEAD_DIM, NSA_GROUPS * CMP_HIDDEN).astype(BF)
    pe_row = lambda p: jnp.tile(p, (1, NSA_GROUPS)).reshape(1, CMP_STRIDE * NSA_GROUPS * HEAD_DIM)
    w2p = jnp.pad(w2, ((0, 0), (0, LANES - HEAD_DIM))).astype(BF)
    return (pe_row(pe[:CMP_STRIDE]), pe_row(pe[CMP_STRIDE:]), expand(w1[:half]), expand(w1[half:]), w2p)


def _gelu_tanh(x):
    return 0.5 * x * (1.0 + jnp.tanh(0.7978845608028654 * (x + 0.044715 * (x * x * x))))


def _cmp_kernel(x_ref, pet_ref, peb_ref, wt_ref, wb_ref, w2_ref, o_ref, *, is_key):
    nb = x_ref.shape[1]
    hw = x_ref.shape[2] // 4
    top, bot = [], []
    for c in range(4):
        xc = x_ref[0, :, c * hw:(c + 1) * hw]
        top.append(_dot((xc + pet_ref[...]).astype(BF), wt_ref[...]))
        bot.append(_dot((xc + peb_ref[...]).astype(BF), wb_ref[...]))
    lane = lax.broadcasted_iota(jnp.int32, (1, LANES), 1)
    j = lax.broadcasted_iota(jnp.int32, (nb, 1), 0)
    for c in range(4):
        nxt = bot[c + 1] if c < 3 else pltpu.roll(bot[0], nb - 1, axis=0)
        hid = _gelu_tanh(top[c] + nxt)
        n = 4 * j + c
        end = n * CMP_STRIDE + (CMP_BLOCK - 1)
        exists = n < 4 * nb - 1
        for gi in range(NSA_GROUPS):
            t = _dot(hid[:, gi * CMP_HIDDEN:(gi + 1) * CMP_HIDDEN].astype(BF), w2_ref[...])
            t = jnp.where(exists, t, 0.0)
            if is_key:
                t = jnp.where(lane == HEAD_DIM, (end >> POS_SHIFT).astype(F32),
                              jnp.where(lane == HEAD_DIM + 1, (end & (POS_SPLIT - 1)).astype(F32), t))
            else:
                t = jnp.where(lane == HEAD_DIM, 1.0, t)
            o_ref[0, gi, c * nb:(c + 1) * nb, :] = t.astype(BF)


def _compress(raw, weights, *, is_key):
    b, s, _ = raw.shape
    nb = s // (4 * CMP_STRIDE)
    x = raw.reshape(b, nb, 4 * CMP_STRIDE * LANES)
    pet, peb, wt, wb, w2p = weights
    return pl.pallas_call(
        functools.partial(_cmp_kernel, is_key=is_key),
        grid=(b,),
        in_specs=[pl.BlockSpec((1, nb, x.shape[2]), lambda bi: (bi, 0, 0)),
                  _const_spec(pet.shape), _const_spec(peb.shape), _const_spec(wt.shape),
                  _const_spec(wb.shape), _const_spec(w2p.shape)],
        out_specs=pl.BlockSpec((1, NSA_GROUPS, 4 * nb, LANES), lambda bi: (bi, 0, 0, 0)),
        out_shape=jax.ShapeDtypeStruct((b, NSA_GROUPS, 4 * nb, LANES), BF),
        compiler_params=_params(1),
        name="compress",
    )(x, pet, peb, wt, wb, w2p)


def _softmax_update(carry, s, v):
    m, acc = carry
    m_new = jnp.maximum(m, jnp.max(s, axis=-1, keepdims=True))
    p = jnp.exp(s - m_new)
    acc = jnp.exp(m - m_new) * acc + _dot(p.astype(BF), v)
    return m_new, acc


def _normalize(acc):
    return acc[:, :HEAD_DIM] / jnp.maximum(acc[:, HEAD_DIM:HEAD_DIM + 1], 1e-30)


FLAG_BITS = 16
TILE_BLOCKS = Q_BLOCK // SLC_BLOCK
GROUP_TILES = 4


def _stack_heads(q_ref, g):
    base = g * HEADS_PER_GROUP
    return jnp.concatenate([q_ref[0, :, (base + r) * LANES:(base + r + 1) * LANES]
                            for r in range(HEADS_PER_GROUP)], axis=0)


def _flag_weights(nb):
    j = jnp.arange(nb)[:, None]
    w = jnp.arange(LANES)[None, :]
    return jnp.where(j // FLAG_BITS == w, 2.0 ** (j % FLAG_BITS), 0.0).astype(BF)


def _select_kernel(q_ref, kc_ref, vc_ref, pw_ref, oc_ref, sel_ref, flag_ref):
    qi = pl.program_id(1)
    q0 = qi * Q_BLOCK
    rq = HEADS_PER_GROUP * Q_BLOCK
    ncp = kc_ref.shape[2]
    nb = ncp // 4
    nb_shift = nb.bit_length() - 1
    assert nb == 1 << nb_shift
    t1 = q0 + lax.broadcasted_iota(jnp.int32, (Q_BLOCK, 1), 0)
    t4 = q0 + (lax.broadcasted_iota(jnp.int32, (rq, 1), 0) & (Q_BLOCK - 1))
    col = lax.broadcasted_iota(jnp.int32, (1, ncp), 1)
    cmp_end = (4 * (col & (nb - 1)) + (col >> nb_shift)) * CMP_STRIDE + (CMP_BLOCK - 1)
    valid = cmp_end <= t4
    jl = lax.broadcasted_iota(jnp.int32, (1, nb), 1)
    jf = jl.astype(F32)
    blk_t = t1 >> 6
    forced = (jl == 0) | (jl == blk_t) | (jl == blk_t - 1)
    eligible = jl <= blk_t

    for g in range(NSA_GROUPS):
        s = jnp.where(valid, _dot_nt(_stack_heads(q_ref, g), kc_ref[0, g]), NEG)
        m = jnp.max(s, axis=-1, keepdims=True)
        p = jnp.where(valid, jnp.exp(s - m), 0.0)
        p = p / jnp.maximum(jnp.sum(p, axis=-1, keepdims=True), 1e-30)
        o_cmp = _dot(p.astype(BF), vc_ref[0, g])
        for r in range(HEADS_PER_GROUP):
            h0 = (g * HEADS_PER_GROUP + r) * HEAD_DIM
            oc_ref[0, :, h0:h0 + HEAD_DIM] = o_cmp[r * Q_BLOCK:(r + 1) * Q_BLOCK, :HEAD_DIM]

        ps = p[0:Q_BLOCK]
        for r in range(1, HEADS_PER_GROUP):
            ps = ps + p[r * Q_BLOCK:(r + 1) * Q_BLOCK]
        parts = [ps[:, c * nb:(c + 1) * nb] for c in range(4)]
        prev = jnp.where(jl == 0, 0.0, pltpu.roll(parts[3], 1, axis=1))
        imp = (parts[0] + parts[1] + parts[2] + parts[3]) - 0.5 * parts[3] + 0.5 * prev
        imp = jnp.where(forced, imp + FORCE_BONUS, imp)
        work = jnp.where(eligible, imp, NEG)
        sel = jnp.zeros((Q_BLOCK, nb), F32)
        for _ in range(min(SLC_TOP_N, nb)):
            mx = jnp.max(work, axis=-1, keepdims=True)
            first = jnp.min(jnp.where(work == mx, jf, float(nb)), axis=-1, keepdims=True)
            hit = jf == first
            sel = jnp.where(hit, 1.0, sel)
            work = jnp.where(hit, 3.0 * NEG, work)
        sel = jnp.where(eligible, sel, 0.0)
        sel_ref[0, g] = sel.astype(BF)
        any_sel = jnp.broadcast_to(jnp.max(sel, axis=0, keepdims=True), (8, nb)).astype(BF)
        flag_ref[0, g, 0] = _dot(any_sel, pw_ref[...]).astype(jnp.int32)


def _select(q, kc, vc):
    b, s, _ = q.shape
    ncp = kc.shape[2]
    nb = ncp // 4
    nq = s // Q_BLOCK
    pw = _flag_weights(nb)
    sds = jax.ShapeDtypeStruct
    cmp_spec = pl.BlockSpec((1, NSA_GROUPS, ncp, LANES), lambda bi, qi: (bi, 0, 0, 0))
    return pl.pallas_call(
        _select_kernel,
        grid=(b, nq),
        in_specs=[pl.BlockSpec((1, Q_BLOCK, NSA_HEADS * LANES), lambda bi, qi: (bi, qi, 0)),
                  cmp_spec, cmp_spec, _const_spec(pw.shape)],
        out_specs=[pl.BlockSpec((1, Q_BLOCK, NSA_HEADS * HEAD_DIM), lambda bi, qi: (bi, qi, 0)),
                   pl.BlockSpec((1, NSA_GROUPS, Q_BLOCK, nb), lambda bi, qi: (bi, 0, qi, 0)),
                   pl.BlockSpec((1, NSA_GROUPS, 1, 8, LANES), lambda bi, qi: (bi, 0, qi, 0, 0))],
        out_shape=[sds((b, s, NSA_HEADS * HEAD_DIM), F32), sds((b, NSA_GROUPS, s, nb), BF),
                   sds((b, NSA_GROUPS, nq, 8, LANES), jnp.int32)],
        compiler_params=_params(2),
        name="nsa_select",
    )(q, kc, vc, pw)


def _attend_kernel(flag_ref, q_ref, g_ref, oc_ref, sel_ref, ks_ref, vs_ref, *rest, n_words):
    nwin = WINDOW // Q_BLOCK + 1
    kw_refs, vw_refs = rest[:nwin], rest[nwin:2 * nwin]
    o_ref, list_ref = rest[2 * nwin], rest[2 * nwin + 1]
    bi, gi, qi = pl.program_id(0), pl.program_id(1), pl.program_id(2)
    q0 = qi * Q_BLOCK
    rq = HEADS_PER_GROUP * Q_BLOCK
    nb = sel_ref.shape[3]
    tiles_per_word = FLAG_BITS // TILE_BLOCKS
    tile_bits = (1 << TILE_BLOCKS) - 1

    qs = _stack_heads(q_ref, 0)
    selb = sel_ref[0, 0]
    t1 = q0 + lax.broadcasted_iota(jnp.int32, (Q_BLOCK, 1), 0)
    t4 = q0 + (lax.broadcasted_iota(jnp.int32, (rq, 1), 0) & (Q_BLOCK - 1))

    base = ((bi * NSA_GROUPS + gi) * pl.num_programs(2) + qi) * n_words

    def word_body(wi, n):
        word = flag_ref[base + wi]

        def scan_word(n):
            def tile_body(u, n):
                kt = wi * tiles_per_word + u
                active = (((word >> (u * TILE_BLOCKS)) & tile_bits) != 0) & (kt < qi)

                @pl.when(active)
                def _():
                    list_ref[n] = kt
                return n + active.astype(jnp.int32)
            return lax.fori_loop(0, tiles_per_word, tile_body, n)
        return lax.cond(word != 0, scan_word, lambda n: n, n)

    n_active = lax.fori_loop(0, (qi + tiles_per_word - 1) // tiles_per_word, word_body, 0)
    for u in range(GROUP_TILES):
        list_ref[n_active + u] = 0

    jrow = lax.broadcasted_iota(jnp.int32, (nb, Q_BLOCK), 0)
    kblk = lax.broadcasted_iota(jnp.int32, (nb, Q_BLOCK), 1) >> 6
    kcol = lax.broadcasted_iota(jnp.int32, (1, Q_BLOCK), 1)

    def masked_scores(k_all, member):
        s_ = _dot_nt(qs, k_all)
        return jnp.concatenate([jnp.where(member, s_[r * Q_BLOCK:(r + 1) * Q_BLOCK], NEG)
                                for r in range(HEADS_PER_GROUP)], axis=0)

    def tile_rows(ref, kt):
        return ref[0, 0, pl.ds(pl.multiple_of(kt * Q_BLOCK, Q_BLOCK), Q_BLOCK), :]

    def group_body(it, carry):
        ks_t, vs_t, ex_t = [], [], []
        for u in range(GROUP_TILES):
            pos = it * GROUP_TILES + u
            kt = list_ref[pos]
            first_blk = jnp.where(pos < n_active, kt * TILE_BLOCKS, -TILE_BLOCKS - nb)
            ks_t.append(tile_rows(ks_ref, kt))
            vs_t.append(tile_rows(vs_ref, kt))
            ex_t.append(jnp.where(jrow == kblk + first_blk, 1.0, 0.0).astype(BF))
        member = _dot(selb, jnp.concatenate(ex_t, axis=1)) > 0.5
        s_ = masked_scores(jnp.concatenate(ks_t, axis=0), member)
        return _softmax_update(carry, s_, jnp.concatenate(vs_t, axis=0))

    init = (jnp.full((rq, 1), NEG, F32), jnp.zeros((rq, LANES), F32))
    carry = lax.fori_loop(0, (n_active + GROUP_TILES - 1) // GROUP_TILES, group_body, init)
    expand = jnp.where(jrow == kblk + qi * TILE_BLOCKS, 1.0, 0.0).astype(BF)
    member = (_dot(selb, expand) > 0.5) & (q0 + kcol <= t1)
    carry = _softmax_update(carry, masked_scores(tile_rows(ks_ref, qi), member), tile_rows(vs_ref, qi))
    o_slc = _normalize(carry[1])

    kpos = []
    for i in range(nwin):
        kti = qi - (nwin - 1) + i
        kpos.append(jnp.where(kti >= 0, kti * Q_BLOCK, 1 << 24) + kcol)
    d = t4 - jnp.concatenate(kpos, axis=1)
    s_ = _dot_nt(qs, jnp.concatenate([r[0, 0] for r in kw_refs], axis=0))
    s_ = jnp.where((d >= 0) & (d < WINDOW), s_, NEG)
    o_win = _normalize(_softmax_update(init, s_, jnp.concatenate([r[0, 0] for r in vw_refs], axis=0))[1])

    for r in range(HEADS_PER_GROUP):
        rows = slice(r * Q_BLOCK, (r + 1) * Q_BLOCK)
        gate = lambda c: g_ref[0, :, 3 * r + c:3 * r + c + 1]
        o = (gate(0) * oc_ref[0, :, r * HEAD_DIM:(r + 1) * HEAD_DIM]
             + gate(1) * o_slc[rows] + gate(2) * o_win[rows])
        o_ref[0, :, r * HEAD_DIM:(r + 1) * HEAD_DIM] = o.astype(BF)


def _attend(flags, q, gates, o_cmp, sel, ks, vs, kw, vw):
    b, s, _ = q.shape
    nb = sel.shape[3]
    nq = s // Q_BLOCK
    n_words = nb // FLAG_BITS
    nwin = WINDOW // Q_BLOCK + 1
    gw = HEADS_PER_GROUP * LANES
    flat = flags[:, :, :, 0, :n_words].reshape(-1)
    full = pl.BlockSpec((1, 1, s, LANES), lambda bi, gi, qi, f: (bi, gi, 0, 0))
    win = lambda i: pl.BlockSpec((1, 1, Q_BLOCK, LANES),
                                 lambda bi, gi, qi, f: (bi, gi, jnp.maximum(qi - (nwin - 1) + i, 0), 0))
    grid_spec = pltpu.PrefetchScalarGridSpec(
        num_scalar_prefetch=1,
        grid=(b, NSA_GROUPS, nq),
        in_specs=[pl.BlockSpec((1, Q_BLOCK, gw), lambda bi, gi, qi, f: (bi, qi, gi)),
                  pl.BlockSpec((1, Q_BLOCK, LANES), lambda bi, gi, qi, f: (bi, qi, gi)),
                  pl.BlockSpec((1, Q_BLOCK, HEADS_PER_GROUP * HEAD_DIM), lambda bi, gi, qi, f: (bi, qi, gi)),
                  pl.BlockSpec((1, 1, Q_BLOCK, nb), lambda bi, gi, qi, f: (bi, gi, qi, 0)),
                  full, full] + [win(i) for i in range(nwin)] * 2,
        out_specs=pl.BlockSpec((1, Q_BLOCK, HEADS_PER_GROUP * HEAD_DIM), lambda bi, gi, qi, f: (bi, qi, gi)),
        scratch_shapes=[pltpu.SMEM((nq + GROUP_TILES,), jnp.int32)],
    )
    return pl.pallas_call(
        functools.partial(_attend_kernel, n_words=n_words),
        grid_spec=grid_spec,
        out_shape=jax.ShapeDtypeStruct((b, s, NSA_HEADS * HEAD_DIM), BF),
        compiler_params=_params(3),
        name="nsa_attend",
    )(flat, q, gates, o_cmp, sel, ks, vs, *([kw] * nwin), *([vw] * nwin))


def _nsa_kernel(q_ref, g_ref, kc_ref, vc_ref, ks_ref, vs_ref, *rest):
    nwin = WINDOW // Q_BLOCK + 1
    kw_refs, vw_refs, o_ref = rest[:nwin], rest[nwin:2 * nwin], rest[2 * nwin]
    qi = pl.program_id(2)
    q0 = qi * Q_BLOCK
    rq = HEADS_PER_GROUP * Q_BLOCK
    ncp = kc_ref.shape[2]
    nb = ncp // 4
    nb_shift = nb.bit_length() - 1
    assert nb == 1 << nb_shift

    qs = jnp.concatenate([q_ref[0, :, r * LANES:(r + 1) * LANES] for r in range(HEADS_PER_GROUP)], axis=0)
    t1 = q0 + lax.broadcasted_iota(jnp.int32, (Q_BLOCK, 1), 0)
    t4 = q0 + (lax.broadcasted_iota(jnp.int32, (rq, 1), 0) & (Q_BLOCK - 1))

    s = _dot_nt(qs, kc_ref[0, 0])
    col = lax.broadcasted_iota(jnp.int32, (1, ncp), 1)
    cmp_end = (4 * (col & (nb - 1)) + (col >> nb_shift)) * CMP_STRIDE + (CMP_BLOCK - 1)
    valid = cmp_end <= t4
    s = jnp.where(valid, s, NEG)
    m = jnp.max(s, axis=-1, keepdims=True)
    p = jnp.where(valid, jnp.exp(s - m), 0.0)
    p = p / jnp.maximum(jnp.sum(p, axis=-1, keepdims=True), 1e-30)
    o_cmp = _dot(p.astype(BF), vc_ref[0, 0])[:, :HEAD_DIM]

    ps = p[0:Q_BLOCK]
    for r in range(1, HEADS_PER_GROUP):
        ps = ps + p[r * Q_BLOCK:(r + 1) * Q_BLOCK]
    parts = [ps[:, c * nb:(c + 1) * nb] for c in range(4)]
    jl = lax.broadcasted_iota(jnp.int32, (1, nb), 1)
    prev = jnp.where(jl == 0, 0.0, pltpu.roll(parts[3], 1, axis=1))
    imp = (parts[0] + parts[1] + parts[2] + parts[3]) - 0.5 * parts[3] + 0.5 * prev
    blk_t = t1 >> 6
    forced = (jl == 0) | (jl == blk_t) | (jl == blk_t - 1)
    imp = jnp.where(forced, imp + FORCE_BONUS, imp)
    eligible = jl <= blk_t
    work = jnp.where(eligible, imp, NEG)
    jf = jl.astype(F32)
    sel = jnp.zeros((Q_BLOCK, nb), F32)
    for _ in range(min(SLC_TOP_N, nb)):
        mx = jnp.max(work, axis=-1, keepdims=True)
        first = jnp.min(jnp.where(work == mx, jf, float(nb)), axis=-1, keepdims=True)
        hit = jf == first
        sel = jnp.where(hit, 1.0, sel)
        work = jnp.where(hit, 3.0 * NEG, work)
    selb = jnp.where(eligible, sel, 0.0).astype(BF)

    tk = Q_BLOCK
    jrow = lax.broadcasted_iota(jnp.int32, (nb, tk), 0)
    kblk = lax.broadcasted_iota(jnp.int32, (nb, tk), 1) >> 6
    kcol = lax.broadcasted_iota(jnp.int32, (1, tk), 1)

    def slc_tile(kt, carry, causal):
        k0 = pl.multiple_of(kt * tk, tk)
        kt_ = ks_ref[0, 0, pl.ds(k0, tk), :]
        vt_ = vs_ref[0, 0, pl.ds(k0, tk), :]
        expand = jnp.where(jrow == kblk + kt * (tk // SLC_BLOCK), 1.0, 0.0).astype(BF)
        mk = _dot(selb, expand)
        if causal:
            mk = jnp.where(k0 + kcol <= t1, mk, 0.0)
        mk4 = jnp.concatenate([mk] * HEADS_PER_GROUP, axis=0)
        s_ = jnp.where(mk4 > 0.5, _dot_nt(qs, kt_), NEG)
        return _softmax_update(carry, s_, vt_)

    init = (jnp.full((rq, 1), NEG, F32), jnp.zeros((rq, LANES), F32))
    carry = lax.fori_loop(0, qi, lambda kt, c: slc_tile(kt, c, False), init)
    o_slc = _normalize(slc_tile(qi, carry, True)[1])

    carry = init
    for i in range(nwin):
        kti = qi - (nwin - 1) + i
        k0 = jnp.where(kti >= 0, kti * Q_BLOCK, 1 << 24)
        d = t4 - (k0 + kcol)
        ok = (d >= 0) & (d < WINDOW)
        s_ = jnp.where(ok, _dot_nt(qs, kw_refs[i][0, 0]), NEG)
        carry = _softmax_update(carry, s_, vw_refs[i][0, 0])
    o_win = _normalize(carry[1])

    for r in range(HEADS_PER_GROUP):
        rows = slice(r * Q_BLOCK, (r + 1) * Q_BLOCK)
        gate = lambda c: g_ref[0, :, 3 * r + c:3 * r + c + 1]
        o = gate(0) * o_cmp[rows] + gate(1) * o_slc[rows] + gate(2) * o_win[rows]
        o_ref[0, :, r * HEAD_DIM:(r + 1) * HEAD_DIM] = o.astype(BF)


def _nsa(q, gates, kc, vc, ks, vs, kw, vw):
    b, s, _ = q.shape
    ncp = kc.shape[2]
    nwin = WINDOW // Q_BLOCK + 1
    gw = HEADS_PER_GROUP * LANES
    full = lambda n: pl.BlockSpec((1, 1, n, LANES), lambda bi, gi, qi: (bi, gi, 0, 0))
    win = lambda i: pl.BlockSpec((1, 1, Q_BLOCK, LANES),
                                 lambda bi, gi, qi: (bi, gi, jnp.maximum(qi - (nwin - 1) + i, 0), 0))
    return pl.pallas_call(
        _nsa_kernel,
        grid=(b, NSA_GROUPS, s // Q_BLOCK),
        in_specs=[pl.BlockSpec((1, Q_BLOCK, gw), lambda bi, gi, qi: (bi, qi, gi)),
                  pl.BlockSpec((1, Q_BLOCK, LANES), lambda bi, gi, qi: (bi, qi, gi)),
                  full(ncp), full(ncp), full(s), full(s)]
                 + [win(i) for i in range(nwin)] * 2,
        out_specs=pl.BlockSpec((1, Q_BLOCK, HEADS_PER_GROUP * HEAD_DIM), lambda bi, gi, qi: (bi, qi, gi)),
        out_shape=jax.ShapeDtypeStruct((b, s, NSA_HEADS * HEAD_DIM), BF),
        compiler_params=_params(3),
        name="nsa",
    )(q, gates, kc, vc, ks, vs, *([kw] * nwin), *([vw] * nwin))


def _l0_out_kernel(x_ref, a_ref, c_ref, wa_ref, wc_ref, post_ref, o_ref):
    y = _dot(a_ref[...], wa_ref[...]) + _dot(c_ref[...], wc_ref[...])
    o_ref[...] = x_ref[...] + _rmsnorm(y, post_ref[...])


def _l0_out(x2, o_nsa, o_conv, w_out, post_g, *, tm=512):
    t, d = x2.shape
    na, nc = o_nsa.shape[1], o_conv.shape[1]
    row = lambda w: pl.BlockSpec((tm, w), lambda i: (i, 0))
    return pl.pallas_call(
        _l0_out_kernel,
        grid=(t // tm,),
        in_specs=[row(d), row(na), row(nc), _const_spec((na, d)), _const_spec((nc, d)), _const_spec((1, d))],
        out_specs=row(d),
        out_shape=jax.ShapeDtypeStruct((t, d), F32),
        compiler_params=_params(1),
        name="l0_out",
    )(x2, o_nsa, o_conv, w_out[:na].astype(BF), w_out[na:].astype(BF), post_g.reshape(1, d))


def _head_indicators(d):
    ch = jnp.arange(d)[:, None] // RWKV_HEAD
    ind = (ch == jnp.arange(LANES)[None, :]).astype(BF)
    return ind, ind.T


def _rwkv_pre_kernel(x_ref, pre_ref, mu_ref, wr_ref, wk_ref, wv_ref, w0_ref, wd1_ref, wd2_ref,
                     a0_ref, wa1_ref, wa2_ref, wg1_ref, wg2_ref, kk_ref, ka_ref, ind_ref, indt_ref,
                     r_out, lw_out, k_out, v_out, kk_out, kka_out, g_out, hprev_ref):
    si = pl.program_id(1)
    tm = x_ref.shape[1]

    @pl.when(si == 0)
    def _():
        hprev_ref[...] = jnp.zeros(hprev_ref.shape, F32)

    h = _rmsnorm(x_ref[0], pre_ref[...])
    rows = lax.broadcasted_iota(jnp.int32, (tm, 1), 0)
    xx = jnp.where(rows == 0, hprev_ref[7:8], pltpu.roll(h, 1, axis=0)) - h
    hprev_ref[...] = h[tm - 8:tm]
    mix = lambda i: (h + xx * mu_ref[i:i + 1]).astype(BF)
    xr, xw, xk, xv, xa, xg = [mix(i) for i in range(6)]

    r = _dot(xr, wr_ref[...])
    k = _dot(xk, wk_ref[...])
    v = _dot(xv, wv_ref[...])
    z = w0_ref[...] + _dot(jnp.tanh(_dot(xw, wd1_ref[...])).astype(BF), wd2_ref[...])
    softplus = jnp.maximum(-z, 0.0) + jnp.log(1.0 + jnp.exp(-jnp.abs(z)))
    lw = -jnp.exp(-softplus - 0.5)
    a = _sigmoid(a0_ref[...] + _dot(_dot(xa, wa1_ref[...]).astype(BF), wa2_ref[...]))
    g = _dot(_sigmoid(_dot(xg, wg1_ref[...])).astype(BF), wg2_ref[...])

    kraw = k * kk_ref[...]
    ss = _split_dot(kraw * kraw, ind_ref[...])
    inv = lax.rsqrt(jnp.maximum(ss, 1e-12))
    kk = kraw * _split_dot(inv, indt_ref[...])

    r_out[0] = r
    lw_out[0] = lw
    k_out[0] = k * (1.0 + (a - 1.0) * ka_ref[...])
    v_out[0] = v
    kk_out[0] = kk
    kka_out[0] = kk * a
    g_out[0] = g


def _rwkv_pre(x, pre_g, mu, w_r, w_k, w_v, w0, w_dec1, w_dec2, a0, w_a1, w_a2, w_g1, w_g2,
              k_k, k_a, ind, indt, *, tm=256):
    b, s, d = x.shape
    row = pl.BlockSpec((1, tm, d), lambda bi, si: (bi, si, 0))
    vec = lambda a: a.reshape(1, d)
    ops = [x, vec(pre_g), mu, w_r.astype(BF), w_k.astype(BF), w_v.astype(BF), vec(w0),
           w_dec1.astype(BF), w_dec2.astype(BF), vec(a0), w_a1.astype(BF), w_a2.astype(BF),
           w_g1.astype(BF), w_g2.astype(BF), vec(k_k), vec(k_a), ind, indt]
    return pl.pallas_call(
        _rwkv_pre_kernel,
        grid=(b, s // tm),
        in_specs=[row] + [_const_spec(o.shape) for o in ops[1:]],
        out_specs=[row] * 7,
        out_shape=[jax.ShapeDtypeStruct((b, s, d), F32)] * 7,
        scratch_shapes=[pltpu.VMEM((8, d), F32)],
        compiler_params=_params(2),
        name="rwkv_pre",
    )(*ops)


def _rwkv_scan_kernel(r_ref, lw_ref, k_ref, v_ref, kk_ref, kka_ref, y_ref, state_ref):
    c = r_ref.shape[1]
    slab = 4 * RWKV_HEAD
    n_slab = r_ref.shape[2] // slab
    rows4 = 4 * c

    @pl.when(pl.program_id(1) == 0)
    def _():
        state_ref[...] = jnp.zeros(state_ref.shape, F32)

    ri = lax.broadcasted_iota(jnp.int32, (rows4, 1), 0)
    ci = lax.broadcasted_iota(jnp.int32, (1, rows4), 1)
    li = lax.broadcasted_iota(jnp.int32, (1, slab), 1)
    c_shift = c.bit_length() - 1
    h_shift = RWKV_HEAD.bit_length() - 1
    assert c == 1 << c_shift and RWKV_HEAD == 1 << h_shift
    head_match = (ri >> c_shift) == (li >> h_shift)
    t_r, t_c = ri & (c - 1), ci & (c - 1)
    strict = t_c < t_r
    incl = t_c <= t_r
    eye = jnp.where(ri == ci, 1.0, 0.0)
    tri = jnp.where(lax.broadcasted_iota(jnp.int32, (c, c), 1) <= lax.broadcasted_iota(jnp.int32, (c, c), 0),
                    1.0, 0.0).astype(BF)
    state_match = (lax.broadcasted_iota(jnp.int32, (slab, 1), 0) >> h_shift) == (li >> h_shift)

    def block_diag(x):
        return jnp.where(head_match, jnp.concatenate([x] * 4, axis=0), 0.0)

    def fold(x):
        return x[0:c] + x[c:2 * c] + x[2 * c:3 * c] + x[3 * c:4 * c]

    bf = lambda x: x.astype(BF)

    for hs in range(n_slab):
        sl = slice(hs * slab, (hs + 1) * slab)
        lw = lw_ref[0, :, sl]
        hi = bf(lw)
        rem = lw - hi.astype(F32)
        mid = bf(rem)
        lo = bf(rem - mid.astype(F32))
        cw = _dot(tri, hi) + _dot(tri, mid) + _dot(tri, lo)
        w_in = jnp.exp(cw)
        w_inv = jnp.exp(-cw)
        w_prev = jnp.exp(cw - lw)
        w_end = w_in[c - 1:c]

        kk = kk_ref[0, :, sl]
        a_t = -kk * w_prev
        b_t = kka_ref[0, :, sl] * w_inv
        k_t = k_ref[0, :, sl] * w_inv
        r_t = r_ref[0, :, sl] * w_in
        v = v_ref[0, :, sl]

        xa, xr = bf(block_diag(a_t)), bf(block_diag(r_t))
        yb, yk = bf(block_diag(b_t)), bf(block_diag(k_t))
        a_ab = jnp.where(strict, _dot_nt(xa, yb), 0.0)
        a_ak = jnp.where(strict, _dot_nt(xa, yk), 0.0)
        a_rb = jnp.where(incl, _dot_nt(xr, yb), 0.0)
        a_rk = jnp.where(incl, _dot_nt(xr, yk), 0.0)

        inv = eye + a_ab
        pw = a_ab
        for _ in range(max(c.bit_length() - 2, 0)):
            pw = _dot(bf(pw), bf(pw))
            inv = inv + _dot(bf(inv), bf(pw))

        state = state_ref[hs]
        sb = bf(state)
        vbd = bf(block_diag(v))
        u_bd = _dot(bf(inv), bf(block_diag(_dot_nt(bf(a_t), sb)) + _dot(bf(a_ak), vbd)))
        y_bd = block_diag(_dot_nt(bf(r_t), sb)) + _dot(bf(a_rb), bf(u_bd)) + _dot(bf(a_rk), vbd)
        y_ref[0, :, sl] = fold(y_bd)

        uv = jnp.concatenate([bf(fold(u_bd)), bf(v)], axis=0)
        bk = jnp.concatenate([bf(b_t * w_end), bf(k_t * w_end)], axis=0)
        state_ref[hs] = jnp.where(state_match, state * w_end + _dot_tn(uv, bk), 0.0)


def _rwkv_scan(r, lw, k, v, kk, kka):
    b, s, d = r.shape
    c = RWKV_CHUNK
    slab = 4 * RWKV_HEAD
    blk = pl.BlockSpec((1, c, d), lambda bi, ci: (bi, ci, 0))
    return pl.pallas_call(
        _rwkv_scan_kernel,
        grid=(b, s // c),
        in_specs=[blk] * 6,
        out_specs=blk,
        out_shape=jax.ShapeDtypeStruct((b, s, d), F32),
        scratch_shapes=[pltpu.VMEM((d // slab, slab, slab), F32)],
        compiler_params=_params(2),
        name="rwkv_scan",
    )(r, lw, k, v, kk, kka)


def _rwkv_post_kernel(x_ref, y_ref, r_ref, k_ref, v_ref, g_ref, wo_ref, lng_ref, lnb_ref, rk_ref,
                      post_ref, ind_ref, indt_ref, o_ref):
    ind, indt = ind_ref[...], indt_ref[...]
    y = y_ref[...]
    inv_n = 1.0 / RWKV_HEAD
    mean = _split_dot(_split_dot(y, ind) * inv_n, indt)
    yc = y - mean
    var = _split_dot(yc * yc, ind) * inv_n
    rstd = _split_dot(lax.rsqrt(var + LNX_EPS), indt)
    yn = yc * rstd * lng_ref[...] + lnb_ref[...]
    bonus = _split_dot(_split_dot(r_ref[...] * k_ref[...] * rk_ref[...], ind), indt) * v_ref[...]
    out = _dot(((yn + bonus) * g_ref[...]).astype(BF), wo_ref[...])
    o_ref[...] = x_ref[...] + _rmsnorm(out, post_ref[...])


def _rwkv_post(x2, y, r, k, v, g, w_o, lnx_g, lnx_b, r_k, post_g, ind, indt, *, tm=256):
    t, d = x2.shape
    row = pl.BlockSpec((tm, d), lambda i: (i, 0))
    vec = lambda a: a.reshape(1, d)
    ops = [x2, y, r, k, v, g, w_o.astype(BF), vec(lnx_g), vec(lnx_b), vec(r_k), vec(post_g), ind, indt]
    return pl.pallas_call(
        _rwkv_post_kernel,
        grid=(t // tm,),
        in_specs=[row] * 6 + [_const_spec(o.shape) for o in ops[6:]],
        out_specs=row,
        out_shape=jax.ShapeDtypeStruct((t, d), F32),
        compiler_params=_params(1),
        name="rwkv_post",
    )(*ops)


def _layer0_mixer(x, pre_g, post_g, w_in, pe_k, w1_k, w2_k, pe_v, w1_v, w2_v, conv_w, conv_b, w_out):
    b, s, d = x.shape
    q, gates, kcr, vcr, ks, vs, kw, vw, o_conv = _l0_in(x, pre_g, _l0_in_weight(w_in), conv_w, conv_b)
    kc = _compress(kcr, _cmp_weights(pe_k, w1_k, w2_k), is_key=True)
    vc = _compress(vcr, _cmp_weights(pe_v, w1_v, w2_v), is_key=False)
    o_cmp, sel, flags = _select(q, kc, vc)
    o_nsa = _attend(flags, q, gates, o_cmp, sel, ks, vs, kw, vw)
    t = b * s
    return _l0_out(x.reshape(t, d), o_nsa.reshape(t, -1), o_conv.reshape(t, -1), w_out, post_g).reshape(b, s, d)


def _layer1_mixer(x, pre_g, post_g, mu, w_r, w_k, w_v, w_o, w0, w_dec1, w_dec2, a0, w_a1, w_a2,
                  w_g1, w_g2, k_k, k_a, r_k, lnx_g, lnx_b):
    b, s, d = x.shape
    ind, indt = _head_indicators(d)
    r, lw, k, v, kk, kka, g = _rwkv_pre(x, pre_g, mu, w_r, w_k, w_v, w0, w_dec1, w_dec2, a0,
                                        w_a1, w_a2, w_g1, w_g2, k_k, k_a, ind, indt)
    y = _rwkv_scan(r, lw, k, v, kk, kka)
    f = lambda a: a.reshape(b * s, d)
    return _rwkv_post(f(x), f(y), f(r), f(k), f(v), f(g), w_o, lnx_g, lnx_b, r_k, post_g,
                      ind, indt).reshape(b, s, d)


def kernel(x, l0_ffn1_pre_g, l0_ffn1_post_g, l0_ffn1_w_gate, l0_ffn1_w_up, l0_ffn1_w_down, l0_mix_pre_g, l0_mix_post_g, l0_w_in, l0_cmp_pe_k, l0_cmp_w1_k, l0_cmp_w2_k, l0_cmp_pe_v, l0_cmp_w1_v, l0_cmp_w2_v, l0_conv_w, l0_conv_b, l0_w_out, l0_ffn2_pre_g, l0_ffn2_post_g, l0_ffn2_w_gate, l0_ffn2_w_up, l0_ffn2_w_down, l1_ffn1_pre_g, l1_ffn1_post_g, l1_ffn1_w_gate, l1_ffn1_w_up, l1_ffn1_w_down, l1_mix_pre_g, l1_mix_post_g, l1_mu, l1_w_r, l1_w_k, l1_w_v, l1_w_o, l1_w0, l1_w_dec1, l1_w_dec2, l1_a0, l1_w_a1, l1_w_a2, l1_w_g1, l1_w_g2, l1_k_k, l1_k_a, l1_r_k, l1_lnx_g, l1_lnx_b, l1_ffn2_pre_g, l1_ffn2_post_g, l1_ffn2_w_gate, l1_ffn2_w_up, l1_ffn2_w_down):
    b, s, d = x.shape
    ffn = lambda a, *w: _ffn(a.reshape(b * s, d), *w).reshape(b, s, d)
    x = ffn(x, l0_ffn1_pre_g, l0_ffn1_post_g, l0_ffn1_w_gate, l0_ffn1_w_up, l0_ffn1_w_down)
    x = _layer0_mixer(x, l0_mix_pre_g, l0_mix_post_g, l0_w_in, l0_cmp_pe_k, l0_cmp_w1_k, l0_cmp_w2_k,
                      l0_cmp_pe_v, l0_cmp_w1_v, l0_cmp_w2_v, l0_conv_w, l0_conv_b, l0_w_out)
    x = ffn(x, l0_ffn2_pre_g, l0_ffn2_post_g, l0_ffn2_w_gate, l0_ffn2_w_up, l0_ffn2_w_down)
    x = ffn(x, l1_ffn1_pre_g, l1_ffn1_post_g, l1_ffn1_w_gate, l1_ffn1_w_up, l1_ffn1_w_down)
    x = _layer1_mixer(x, l1_mix_pre_g, l1_mix_post_g, l1_mu, l1_w_r, l1_w_k, l1_w_v, l1_w_o, l1_w0,
                      l1_w_dec1, l1_w_dec2, l1_a0, l1_w_a1, l1_w_a2, l1_w_g1, l1_w_g2, l1_k_k, l1_k_a,
                      l1_r_k, l1_lnx_g, l1_lnx_b)
    x = ffn(x, l1_ffn2_pre_g, l1_ffn2_post_g, l1_ffn2_w_gate, l1_ffn2_w_up, l1_ffn2_w_down)
    return x
```

```python
import functools

import jax
import jax.numpy as jnp
from jax import lax
from jax.experimental import pallas as pl
from jax.experimental.pallas import tpu as pltpu

BF = jnp.bfloat16
F32 = jnp.float32

EPS = 1e-6
LNX_EPS = 64e-5
HEAD_DIM = 64
NSA_HEADS = 8
NSA_GROUPS = 2
HEADS_PER_GROUP = NSA_HEADS // NSA_GROUPS
CMP_STRIDE = 16
CMP_BLOCK = 32
CMP_HIDDEN = 128
SLC_BLOCK = 64
SLC_TOP_N = 16
WINDOW = 512
Q_BLOCK = 128
FORCE_BONUS = 1e4
CONV_WIDTH = 512
RWKV_HEAD = 64
RWKV_CHUNK = 64
SLAB_HEADS = 2
LANES = 128
POS_SHIFT = 7
POS_SPLIT = 1 << POS_SHIFT
NEG = -1e30
VMEM_LIMIT = 56 * 1024 * 1024

NT_DIMS = (((1,), (1,)), ((), ()))
TN_DIMS = (((0,), (0,)), ((), ()))


def _dot(a, b):
    return jnp.dot(a, b, preferred_element_type=F32)


def _dot_nt(a, b):
    return lax.dot_general(a, b, NT_DIMS, preferred_element_type=F32)


def _dot_tn(a, b):
    return lax.dot_general(a, b, TN_DIMS, preferred_element_type=F32)


def _split_dot(x, w):
    hi = x.astype(BF)
    lo = (x - hi.astype(F32)).astype(BF)
    return _dot(hi, w) + _dot(lo, w)


def _rmsnorm(x, g):
    ms = jnp.mean(x * x, axis=-1, keepdims=True)
    return x * lax.rsqrt(ms + EPS) * g


def _sigmoid(x):
    return 1.0 / (1.0 + jnp.exp(-x))


def _const_spec(shape):
    zeros = (0,) * len(shape)
    return pl.BlockSpec(shape, lambda *_: zeros)


def _params(n_grid):
    return pltpu.CompilerParams(dimension_semantics=("arbitrary",) * n_grid,
                                vmem_limit_bytes=VMEM_LIMIT)


def _ffn_kernel(x_ref, pre_ref, post_ref, wg_ref, wu_ref, wd_ref, o_ref, *, ff_chunk):
    x = x_ref[...]
    h = _rmsnorm(x, pre_ref[...]).astype(BF)
    acc = jnp.zeros(x.shape, F32)
    for c0 in range(0, wg_ref.shape[1], ff_chunk):
        g = _dot(h, wg_ref[:, c0:c0 + ff_chunk])
        u = _dot(h, wu_ref[:, c0:c0 + ff_chunk])
        a = (g * _sigmoid(g) * u).astype(BF)
        acc = acc + _dot(a, wd_ref[c0:c0 + ff_chunk, :])
    o_ref[...] = x + 0.5 * _rmsnorm(acc, post_ref[...])


def _ffn(x2, pre_g, post_g, w_gate, w_up, w_down, *, tm=512):
    t, d = x2.shape
    dff = w_gate.shape[1]
    ff_chunk = dff // 2 if (dff // 2) % LANES == 0 else dff
    row = pl.BlockSpec((tm, d), lambda i: (i, 0))
    return pl.pallas_call(
        functools.partial(_ffn_kernel, ff_chunk=ff_chunk),
        grid=(t // tm,),
        in_specs=[row, _const_spec((1, d)), _const_spec((1, d)),
                  _const_spec((d, dff)), _const_spec((d, dff)), _const_spec((dff, d))],
        out_specs=row,
        out_shape=jax.ShapeDtypeStruct((t, d), F32),
        compiler_params=_params(1),
        name="ffn",
    )(x2, pre_g.reshape(1, d), post_g.reshape(1, d),
      w_gate.astype(BF), w_up.astype(BF), w_down.astype(BF))


_C_Q = 0
_C_G = _C_Q + NSA_HEADS * LANES
_C_KC = _C_G + NSA_GROUPS * LANES
_C_VC = _C_KC + LANES
_C_KS = _C_VC + LANES
_C_CONV = _C_KS + 4 * NSA_GROUPS * LANES
_C_END = _C_CONV + 3 * CONV_WIDTH


def _l0_in_weight(w_in):
    d = w_in.shape[0]
    nsa_w = NSA_HEADS * HEAD_DIM
    kv_w = NSA_GROUPS * HEAD_DIM
    o = 0
    q = w_in[:, o:o + nsa_w]; o += nsa_w
    g = w_in[:, o:o + 3 * NSA_HEADS]; o += 3 * NSA_HEADS
    kvs = []
    for _ in range(6):
        kvs.append(w_in[:, o:o + kv_w]); o += kv_w
    conv = w_in[:, o:]
    zpad = lambda w, n: jnp.pad(w, ((0, 0), (0, n - w.shape[1])))
    cols = [zpad(q[:, h * HEAD_DIM:(h + 1) * HEAD_DIM], LANES) for h in range(NSA_HEADS)]
    gpg = 3 * HEADS_PER_GROUP
    cols += [zpad(g[:, i * gpg:(i + 1) * gpg], LANES) for i in range(NSA_GROUPS)]
    cols += [kvs[0], kvs[1]]
    for w in kvs[2:]:
        cols += [zpad(w[:, i * HEAD_DIM:(i + 1) * HEAD_DIM], LANES) for i in range(NSA_GROUPS)]
    cols.append(conv)
    w = jnp.concatenate(cols, axis=1)
    assert w.shape == (d, _C_END)
    return w.astype(BF)


def _l0_in_kernel(x_ref, pre_ref, w_ref, cw_ref, cb_ref,
                  q_ref, g_ref, kcr_ref, vcr_ref, ks_ref, vs_ref, kw_ref, vw_ref, oc_ref,
                  zprev_ref):
    si = pl.program_id(1)
    tm = x_ref.shape[1]
    h = _rmsnorm(x_ref[0], pre_ref[...]).astype(BF)
    lane = lax.broadcasted_iota(jnp.int32, (1, LANES), 1)
    pos = si * tm + lax.broadcasted_iota(jnp.int32, (tm, 1), 0)
    pos_hi = (pos >> POS_SHIFT).astype(F32)
    pos_lo = (pos & (POS_SPLIT - 1)).astype(F32)

    for hd in range(NSA_HEADS):
        slope = 2.0 ** (-(hd + 1))
        qh = _dot(h, w_ref[:, _C_Q + hd * LANES:_C_Q + (hd + 1) * LANES]) * (HEAD_DIM ** -0.5)
        qh = jnp.where(lane == HEAD_DIM, POS_SPLIT * slope, jnp.where(lane == HEAD_DIM + 1, slope, qh))
        q_ref[0, :, hd * LANES:(hd + 1) * LANES] = qh.astype(BF)

    g_ref[0] = _sigmoid(_dot(h, w_ref[:, _C_G:_C_G + NSA_GROUPS * LANES]))
    kcr_ref[0] = _dot(h, w_ref[:, _C_KC:_C_KC + LANES])
    vcr_ref[0] = _dot(h, w_ref[:, _C_VC:_C_VC + LANES])

    for n, (ref, is_key) in enumerate(((ks_ref, True), (vs_ref, False), (kw_ref, True), (vw_ref, False))):
        for gi in range(NSA_GROUPS):
            c0 = _C_KS + (n * NSA_GROUPS + gi) * LANES
            t = _dot(h, w_ref[:, c0:c0 + LANES])
            if is_key:
                t = jnp.where(lane == HEAD_DIM, pos_hi, jnp.where(lane == HEAD_DIM + 1, pos_lo, t))
            else:
                t = jnp.where(lane == HEAD_DIM, 1.0, t)
            ref[0, gi] = t.astype(BF)

    @pl.when(si == 0)
    def _():
        zprev_ref[...] = jnp.zeros(zprev_ref.shape, F32)

    cb = _dot(h, w_ref[:, _C_CONV:_C_CONV + CONV_WIDTH])
    cc = _dot(h, w_ref[:, _C_CONV + CONV_WIDTH:_C_CONV + 2 * CONV_WIDTH])
    cu = _dot(h, w_ref[:, _C_CONV + 2 * CONV_WIDTH:_C_CONV + 3 * CONV_WIDTH])
    z = cc * cu
    prev = zprev_ref[...]
    rows = lax.broadcasted_iota(jnp.int32, (tm, 1), 0)
    zm1 = jnp.where(rows == 0, prev[7:8], pltpu.roll(z, 1, axis=0))
    zm2 = jnp.where(rows == 0, prev[6:7], jnp.where(rows == 1, prev[7:8], pltpu.roll(z, 2, axis=0)))
    y = cw_ref[0:1] * zm2 + cw_ref[1:2] * zm1 + cw_ref[2:3] * z
    oc_ref[0] = (cb * (y + cb_ref[...])).astype(BF)
    zprev_ref[...] = z[tm - 8:tm]


def _l0_in(x, pre_g, w_all, conv_w, conv_b, *, tm=512):
    b, s, d = x.shape
    row = lambda w: pl.BlockSpec((1, tm, w), lambda bi, si: (bi, si, 0))
    grp = pl.BlockSpec((1, NSA_GROUPS, tm, LANES), lambda bi, si: (bi, 0, si, 0))
    sds = jax.ShapeDtypeStruct
    kv_shape = sds((b, NSA_GROUPS, s, LANES), BF)
    return pl.pallas_call(
        _l0_in_kernel,
        grid=(b, s // tm),
        in_specs=[row(d), _const_spec((1, d)), _const_spec(w_all.shape),
                  _const_spec(conv_w.shape), _const_spec((1, CONV_WIDTH))],
        out_specs=[row(NSA_HEADS * LANES), row(NSA_GROUPS * LANES), row(LANES), row(LANES),
                   grp, grp, grp, grp, row(CONV_WIDTH)],
        out_shape=[sds((b, s, NSA_HEADS * LANES), BF), sds((b, s, NSA_GROUPS * LANES), F32),
                   sds((b, s, LANES), F32), sds((b, s, LANES), F32),
                   kv_shape, kv_shape, kv_shape, kv_shape, sds((b, s, CONV_WIDTH), BF)],
        scratch_shapes=[pltpu.VMEM((8, CONV_WIDTH), F32)],
        compiler_params=_params(2),
        name="l0_in",
    )(x, pre_g.reshape(1, d), w_all, conv_w, conv_b.reshape(1, CONV_WIDTH))


def _cmp_weights(pe, w1, w2):
    half = CMP_STRIDE * HEAD_DIM
    def expand(w):
        w = w.reshape(CMP_STRIDE, HEAD_DIM, CMP_HIDDEN)
        out = jnp.zeros((CMP_STRIDE, NSA_GROUPS, HEAD_DIM, NSA_GROUPS, CMP_HIDDEN), F32)
        for gi in range(NSA_GROUPS):
            out = out.at[:, gi, :, gi, :].set(w)
        return out.reshape(CMP_STRIDE * NSA_GROUPS * HEAD_DIM, NSA_GROUPS * CMP_HIDDEN).astype(BF)
    pe_row = lambda p: jnp.tile(p, (1, NSA_GROUPS)).reshape(1, CMP_STRIDE * NSA_GROUPS * HEAD_DIM)
    w2p = jnp.pad(w2, ((0, 0), (0, LANES - HEAD_DIM))).astype(BF)
    return (pe_row(pe[:CMP_STRIDE]), pe_row(pe[CMP_STRIDE:]), expand(w1[:half]), expand(w1[half:]), w2p)


def _gelu_tanh(x):
    return 0.5 * x * (1.0 + jnp.tanh(0.7978845608028654 * (x + 0.044715 * (x * x * x))))


def _cmp_kernel(x_ref, pet_ref, peb_ref, wt_ref, wb_ref, w2_ref, o_ref, *, is_key):
    nb = x_ref.shape[1]
    hw = x_ref.shape[2] // 4
    top, bot = [], []
    for c in range(4):
        xc = x_ref[0, :, c * hw:(c + 1) * hw]
        top.append(_dot((xc + pet_ref[...]).astype(BF), wt_ref[...]))
        bot.append(_dot((xc + peb_ref[...]).astype(BF), wb_ref[...]))
    lane = lax.broadcasted_iota(jnp.int32, (1, LANES), 1)
    j = lax.broadcasted_iota(jnp.int32, (nb, 1), 0)
    for c in range(4):
        nxt = bot[c + 1] if c < 3 else pltpu.roll(bot[0], nb - 1, axis=0)
        hid = _gelu_tanh(top[c] + nxt)
        n = 4 * j + c
        end = n * CMP_STRIDE + (CMP_BLOCK - 1)
        exists = n < 4 * nb - 1
        for gi in range(NSA_GROUPS):
            t = _dot(hid[:, gi * CMP_HIDDEN:(gi + 1) * CMP_HIDDEN].astype(BF), w2_ref[...])
            t = jnp.where(exists, t, 0.0)
            if is_key:
                t = jnp.where(lane == HEAD_DIM, (end >> POS_SHIFT).astype(F32),
                              jnp.where(lane == HEAD_DIM + 1, (end & (POS_SPLIT - 1)).astype(F32), t))
            else:
                t = jnp.where(lane == HEAD_DIM, 1.0, t)
            o_ref[0, gi, c * nb:(c + 1) * nb, :] = t.astype(BF)


def _compress(raw, weights, *, is_key):
    b, s, _ = raw.shape
    nb = s // (4 * CMP_STRIDE)
    x = raw.reshape(b, nb, 4 * CMP_STRIDE * LANES)
    pet, peb, wt, wb, w2p = weights
    return pl.pallas_call(
        functools.partial(_cmp_kernel, is_key=is_key),
        grid=(b,),
        in_specs=[pl.BlockSpec((1, nb, x.shape[2]), lambda bi: (bi, 0, 0)),
                  _const_spec(pet.shape), _const_spec(peb.shape), _const_spec(wt.shape),
                  _const_spec(wb.shape), _const_spec(w2p.shape)],
        out_specs=pl.BlockSpec((1, NSA_GROUPS, 4 * nb, LANES), lambda bi: (bi, 0, 0, 0)),
        out_shape=jax.ShapeDtypeStruct((b, NSA_GROUPS, 4 * nb, LANES), BF),
        compiler_params=_params(1),
        name="compress",
    )(x, pet, peb, wt, wb, w2p)


def _softmax_update(carry, s, v):
    m, acc = carry
    m_new = jnp.maximum(m, jnp.max(s, axis=-1, keepdims=True))
    p = jnp.exp(s - m_new)
    acc = jnp.exp(m - m_new) * acc + _dot(p.astype(BF), v)
    return m_new, acc


def _normalize(acc):
    return acc[:, :HEAD_DIM] / jnp.maximum(acc[:, HEAD_DIM:HEAD_DIM + 1], 1e-30)


N_FORCED = 3
FLAG_BITS = 16
TILE_BLOCKS = Q_BLOCK // SLC_BLOCK
GROUP_TILES = 4


def _stack_heads(q_ref, g):
    base = g * HEADS_PER_GROUP
    return jnp.concatenate([q_ref[0, :, (base + r) * LANES:(base + r + 1) * LANES]
                            for r in range(HEADS_PER_GROUP)], axis=0)


def _flag_weights(nb):
    j = jnp.arange(nb)[:, None]
    w = jnp.arange(LANES)[None, :]
    return jnp.where(j // FLAG_BITS == w, 2.0 ** (j % FLAG_BITS), 0.0).astype(BF)


def _select_kernel(q_ref, kc_ref, vc_ref, pw_ref, oc_ref, sel_ref, flag_ref):
    qi = pl.program_id(1)
    q0 = qi * Q_BLOCK
    rq = HEADS_PER_GROUP * Q_BLOCK
    ncp = kc_ref.shape[2]
    nb = ncp // 4
    nb_shift = nb.bit_length() - 1
    assert nb == 1 << nb_shift
    t1 = q0 + lax.broadcasted_iota(jnp.int32, (Q_BLOCK, 1), 0)
    t4 = q0 + (lax.broadcasted_iota(jnp.int32, (rq, 1), 0) & (Q_BLOCK - 1))
    col = lax.broadcasted_iota(jnp.int32, (1, ncp), 1)
    cmp_end = (4 * (col & (nb - 1)) + (col >> nb_shift)) * CMP_STRIDE + (CMP_BLOCK - 1)
    valid = cmp_end <= t4
    jl = lax.broadcasted_iota(jnp.int32, (1, nb), 1)
    blk_t = t1 >> 6
    forced = (jl == 0) | (jl == blk_t) | (jl == blk_t - 1)
    eligible = jl <= blk_t

    groups = range(NSA_GROUPS)

    def importance(p):
        ps = p[0:Q_BLOCK]
        for r in range(1, HEADS_PER_GROUP):
            ps = ps + p[r * Q_BLOCK:(r + 1) * Q_BLOCK]
        parts = [ps[:, c * nb:(c + 1) * nb] for c in range(4)]
        prev = jnp.where(jl == 0, 0.0, pltpu.roll(parts[3], 1, axis=1))
        return (parts[0] + parts[1] + parts[2] + parts[3]) - 0.5 * parts[3] + 0.5 * prev

    s = [jnp.where(valid, _dot_nt(_stack_heads(q_ref, g), kc_ref[0, g]), NEG) for g in groups]
    m = [jnp.max(s[g], axis=-1, keepdims=True) for g in groups]
    p = [jnp.where(valid, jnp.exp(s[g] - m[g]), 0.0) for g in groups]
    p = [p[g] / jnp.maximum(jnp.sum(p[g], axis=-1, keepdims=True), 1e-30) for g in groups]
    o_cmp = [_dot(p[g].astype(BF), vc_ref[0, g]) for g in groups]
    pickable = eligible & ~forced
    work = [jnp.where(pickable, importance(p[g]), NEG) for g in groups]
    sel = [jnp.where(forced, 1.0, 0.0) for g in groups]
    for _ in range(min(SLC_TOP_N, nb) - N_FORCED):
        hit = [jl == jnp.argmax(work[g], axis=-1, keepdims=True).astype(jnp.int32) for g in groups]
        sel = [jnp.where(hit[g], 1.0, sel[g]) for g in groups]
        work = [jnp.where(hit[g], 3.0 * NEG, work[g]) for g in groups]
    for g in groups:
        for r in range(HEADS_PER_GROUP):
            h0 = (g * HEADS_PER_GROUP + r) * HEAD_DIM
            oc_ref[0, :, h0:h0 + HEAD_DIM] = o_cmp[g][r * Q_BLOCK:(r + 1) * Q_BLOCK, :HEAD_DIM]
        sel_g = jnp.where(eligible, sel[g], 0.0)
        sel_ref[0, g] = sel_g.astype(BF)
        any_sel = jnp.broadcast_to(jnp.max(sel_g, axis=0, keepdims=True), (8, nb)).astype(BF)
        flag_ref[0, g, 0] = _dot(any_sel, pw_ref[...]).astype(jnp.int32)


def _select(q, kc, vc):
    b, s, _ = q.shape
    ncp = kc.shape[2]
    nb = ncp // 4
    nq = s // Q_BLOCK
    pw = _flag_weights(nb)
    sds = jax.ShapeDtypeStruct
    cmp_spec = pl.BlockSpec((1, NSA_GROUPS, ncp, LANES), lambda bi, qi: (bi, 0, 0, 0))
    return pl.pallas_call(
        _select_kernel,
        grid=(b, nq),
        in_specs=[pl.BlockSpec((1, Q_BLOCK, NSA_HEADS * LANES), lambda bi, qi: (bi, qi, 0)),
                  cmp_spec, cmp_spec, _const_spec(pw.shape)],
        out_specs=[pl.BlockSpec((1, Q_BLOCK, NSA_HEADS * HEAD_DIM), lambda bi, qi: (bi, qi, 0)),
                   pl.BlockSpec((1, NSA_GROUPS, Q_BLOCK, nb), lambda bi, qi: (bi, 0, qi, 0)),
                   pl.BlockSpec((1, NSA_GROUPS, 1, 8, LANES), lambda bi, qi: (bi, 0, qi, 0, 0))],
        out_shape=[sds((b, s, NSA_HEADS * HEAD_DIM), F32), sds((b, NSA_GROUPS, s, nb), BF),
                   sds((b, NSA_GROUPS, nq, 8, LANES), jnp.int32)],
        compiler_params=_params(2),
        name="nsa_select",
    )(q, kc, vc, pw)


def _attend_kernel(flag_ref, q_ref, g_ref, oc_ref, sel_ref, ks_ref, vs_ref, *rest, n_words):
    nwin = WINDOW // Q_BLOCK + 1
    kw_refs, vw_refs = rest[:nwin], rest[nwin:2 * nwin]
    o_ref, list_ref = rest[2 * nwin], rest[2 * nwin + 1]
    bi, gi, qi = pl.program_id(0), pl.program_id(1), pl.program_id(2)
    q0 = qi * Q_BLOCK
    rq = HEADS_PER_GROUP * Q_BLOCK
    nb = sel_ref.shape[3]
    tiles_per_word = FLAG_BITS // TILE_BLOCKS
    tile_bits = (1 << TILE_BLOCKS) - 1

    qs = _stack_heads(q_ref, 0)
    selb = sel_ref[0, 0]
    t1 = q0 + lax.broadcasted_iota(jnp.int32, (Q_BLOCK, 1), 0)
    t4 = q0 + (lax.broadcasted_iota(jnp.int32, (rq, 1), 0) & (Q_BLOCK - 1))

    base = ((bi * NSA_GROUPS + gi) * pl.num_programs(2) + qi) * n_words

    def word_body(wi, n):
        word = flag_ref[base + wi]

        def scan_word(n):
            def tile_body(u, n):
                kt = wi * tiles_per_word + u
                active = (((word >> (u * TILE_BLOCKS)) & tile_bits) != 0) & (kt < qi)

                @pl.when(active)
                def _():
                    list_ref[n] = kt
                return n + active.astype(jnp.int32)
            return lax.fori_loop(0, tiles_per_word, tile_body, n)
        return lax.cond(word != 0, scan_word, lambda n: n, n)

    n_active = lax.fori_loop(0, (qi + tiles_per_word - 1) // tiles_per_word, word_body, 0)
    for u in range(GROUP_TILES):
        list_ref[n_active + u] = 0

    jrow = lax.broadcasted_iota(jnp.int32, (nb, Q_BLOCK), 0)
    kblk = lax.broadcasted_iota(jnp.int32, (nb, Q_BLOCK), 1) >> 6
    kcol = lax.broadcasted_iota(jnp.int32, (1, Q_BLOCK), 1)

    def masked_scores(k_all, member):
        s_ = _dot_nt(qs, k_all)
        return jnp.concatenate([jnp.where(member, s_[r * Q_BLOCK:(r + 1) * Q_BLOCK], NEG)
                                for r in range(HEADS_PER_GROUP)], axis=0)

    def tile_rows(ref, kt):
        return ref[0, 0, pl.ds(pl.multiple_of(kt * Q_BLOCK, Q_BLOCK), Q_BLOCK), :]

    def group_body(it, carry):
        ks_t, vs_t, ex_t = [], [], []
        for u in range(GROUP_TILES):
            pos = it * GROUP_TILES + u
            kt = list_ref[pos]
            first_blk = jnp.where(pos < n_active, kt * TILE_BLOCKS, -TILE_BLOCKS - nb)
            ks_t.append(tile_rows(ks_ref, kt))
            vs_t.append(tile_rows(vs_ref, kt))
            ex_t.append(jnp.where(jrow == kblk + first_blk, 1.0, 0.0).astype(BF))
        member = _dot(selb, jnp.concatenate(ex_t, axis=1)) > 0.5
        s_ = masked_scores(jnp.concatenate(ks_t, axis=0), member)
        return _softmax_update(carry, s_, jnp.concatenate(vs_t, axis=0))

    init = (jnp.full((rq, 1), NEG, F32), jnp.zeros((rq, LANES), F32))
    carry = lax.fori_loop(0, (n_active + GROUP_TILES - 1) // GROUP_TILES, group_body, init)
    expand = jnp.where(jrow == kblk + qi * TILE_BLOCKS, 1.0, 0.0).astype(BF)
    member = (_dot(selb, expand) > 0.5) & (q0 + kcol <= t1)
    kpos = []
    for i in range(nwin):
        kti = qi - (nwin - 1) + i
        kpos.append(jnp.where(kti >= 0, kti * Q_BLOCK, 1 << 24) + kcol)
    d = t4 - jnp.concatenate(kpos, axis=1)
    s_d = masked_scores(tile_rows(ks_ref, qi), member)
    s_w = _dot_nt(qs, jnp.concatenate([r[0, 0] for r in kw_refs], axis=0))
    s_w = jnp.where((d >= 0) & (d < WINDOW), s_w, NEG)
    m_d = jnp.maximum(carry[0], jnp.max(s_d, axis=-1, keepdims=True))
    m_w = jnp.max(s_w, axis=-1, keepdims=True)
    p_d = jnp.exp(s_d - m_d).astype(BF)
    p_w = jnp.exp(s_w - m_w).astype(BF)
    acc_d = jnp.exp(carry[0] - m_d) * carry[1] + _dot(p_d, tile_rows(vs_ref, qi))
    acc_w = _dot(p_w, jnp.concatenate([r[0, 0] for r in vw_refs], axis=0))
    o_slc = _normalize(acc_d)
    o_win = _normalize(acc_w)

    for r in range(HEADS_PER_GROUP):
        rows = slice(r * Q_BLOCK, (r + 1) * Q_BLOCK)
        gate = lambda c: g_ref[0, :, 3 * r + c:3 * r + c + 1]
        o = (gate(0) * oc_ref[0, :, r * HEAD_DIM:(r + 1) * HEAD_DIM]
             + gate(1) * o_slc[rows] + gate(2) * o_win[rows])
        o_ref[0, :, r * HEAD_DIM:(r + 1) * HEAD_DIM] = o.astype(BF)


def _attend(flags, q, gates, o_cmp, sel, ks, vs, kw, vw):
    b, s, _ = q.shape
    nb = sel.shape[3]
    nq = s // Q_BLOCK
    n_words = nb // FLAG_BITS
    nwin = WINDOW // Q_BLOCK + 1
    gw = HEADS_PER_GROUP * LANES
    flat = flags[:, :, :, 0, :n_words].reshape(-1)
    full = pl.BlockSpec((1, 1, s, LANES), lambda bi, gi, qi, f: (bi, gi, 0, 0))
    win = lambda i: pl.BlockSpec((1, 1, Q_BLOCK, LANES),
                                 lambda bi, gi, qi, f: (bi, gi, jnp.maximum(qi - (nwin - 1) + i, 0), 0))
    grid_spec = pltpu.PrefetchScalarGridSpec(
        num_scalar_prefetch=1,
        grid=(b, NSA_GROUPS, nq),
        in_specs=[pl.BlockSpec((1, Q_BLOCK, gw), lambda bi, gi, qi, f: (bi, qi, gi)),
                  pl.BlockSpec((1, Q_BLOCK, LANES), lambda bi, gi, qi, f: (bi, qi, gi)),
                  pl.BlockSpec((1, Q_BLOCK, HEADS_PER_GROUP * HEAD_DIM), lambda bi, gi, qi, f: (bi, qi, gi)),
                  pl.BlockSpec((1, 1, Q_BLOCK, nb), lambda bi, gi, qi, f: (bi, gi, qi, 0)),
                  full, full] + [win(i) for i in range(nwin)] * 2,
        out_specs=pl.BlockSpec((1, Q_BLOCK, HEADS_PER_GROUP * HEAD_DIM), lambda bi, gi, qi, f: (bi, qi, gi)),
        scratch_shapes=[pltpu.SMEM((nq + GROUP_TILES,), jnp.int32)],
    )
    return pl.pallas_call(
        functools.partial(_attend_kernel, n_words=n_words),
        grid_spec=grid_spec,
        out_shape=jax.ShapeDtypeStruct((b, s, NSA_HEADS * HEAD_DIM), BF),
        compiler_params=_params(3),
        name="nsa_attend",
    )(flat, q, gates, o_cmp, sel, ks, vs, *([kw] * nwin), *([vw] * nwin))


def _nsa_kernel(q_ref, g_ref, kc_ref, vc_ref, ks_ref, vs_ref, *rest):
    nwin = WINDOW // Q_BLOCK + 1
    kw_refs, vw_refs, o_ref = rest[:nwin], rest[nwin:2 * nwin], rest[2 * nwin]
    qi = pl.program_id(2)
    q0 = qi * Q_BLOCK
    rq = HEADS_PER_GROUP * Q_BLOCK
    ncp = kc_ref.shape[2]
    nb = ncp // 4
    nb_shift = nb.bit_length() - 1
    assert nb == 1 << nb_shift

    qs = jnp.concatenate([q_ref[0, :, r * LANES:(r + 1) * LANES] for r in range(HEADS_PER_GROUP)], axis=0)
    t1 = q0 + lax.broadcasted_iota(jnp.int32, (Q_BLOCK, 1), 0)
    t4 = q0 + (lax.broadcasted_iota(jnp.int32, (rq, 1), 0) & (Q_BLOCK - 1))

    s = _dot_nt(qs, kc_ref[0, 0])
    col = lax.broadcasted_iota(jnp.int32, (1, ncp), 1)
    cmp_end = (4 * (col & (nb - 1)) + (col >> nb_shift)) * CMP_STRIDE + (CMP_BLOCK - 1)
    valid = cmp_end <= t4
    s = jnp.where(valid, s, NEG)
    m = jnp.max(s, axis=-1, keepdims=True)
    p = jnp.where(valid, jnp.exp(s - m), 0.0)
    p = p / jnp.maximum(jnp.sum(p, axis=-1, keepdims=True), 1e-30)
    o_cmp = _dot(p.astype(BF), vc_ref[0, 0])[:, :HEAD_DIM]

    ps = p[0:Q_BLOCK]
    for r in range(1, HEADS_PER_GROUP):
        ps = ps + p[r * Q_BLOCK:(r + 1) * Q_BLOCK]
    parts = [ps[:, c * nb:(c + 1) * nb] for c in range(4)]
    jl = lax.broadcasted_iota(jnp.int32, (1, nb), 1)
    prev = jnp.where(jl == 0, 0.0, pltpu.roll(parts[3], 1, axis=1))
    imp = (parts[0] + parts[1] + parts[2] + parts[3]) - 0.5 * parts[3] + 0.5 * prev
    blk_t = t1 >> 6
    forced = (jl == 0) | (jl == blk_t) | (jl == blk_t - 1)
    imp = jnp.where(forced, imp + FORCE_BONUS, imp)
    eligible = jl <= blk_t
    work = jnp.where(eligible, imp, NEG)
    jf = jl.astype(F32)
    sel = jnp.zeros((Q_BLOCK, nb), F32)
    for _ in range(min(SLC_TOP_N, nb)):
        mx = jnp.max(work, axis=-1, keepdims=True)
        first = jnp.min(jnp.where(work == mx, jf, float(nb)), axis=-1, keepdims=True)
        hit = jf == first
        sel = jnp.where(hit, 1.0, sel)
        work = jnp.where(hit, 3.0 * NEG, work)
    selb = jnp.where(eligible, sel, 0.0).astype(BF)

    tk = Q_BLOCK
    jrow = lax.broadcasted_iota(jnp.int32, (nb, tk), 0)
    kblk = lax.broadcasted_iota(jnp.int32, (nb, tk), 1) >> 6
    kcol = lax.broadcasted_iota(jnp.int32, (1, tk), 1)

    def slc_tile(kt, carry, causal):
        k0 = pl.multiple_of(kt * tk, tk)
        kt_ = ks_ref[0, 0, pl.ds(k0, tk), :]
        vt_ = vs_ref[0, 0, pl.ds(k0, tk), :]
        expand = jnp.where(jrow == kblk + kt * (tk // SLC_BLOCK), 1.0, 0.0).astype(BF)
        mk = _dot(selb, expand)
        if causal:
            mk = jnp.where(k0 + kcol <= t1, mk, 0.0)
        mk4 = jnp.concatenate([mk] * HEADS_PER_GROUP, axis=0)
        s_ = jnp.where(mk4 > 0.5, _dot_nt(qs, kt_), NEG)
        return _softmax_update(carry, s_, vt_)

    init = (jnp.full((rq, 1), NEG, F32), jnp.zeros((rq, LANES), F32))
    carry = lax.fori_loop(0, qi, lambda kt, c: slc_tile(kt, c, False), init)
    o_slc = _normalize(slc_tile(qi, carry, True)[1])

    carry = init
    for i in range(nwin):
        kti = qi - (nwin - 1) + i
        k0 = jnp.where(kti >= 0, kti * Q_BLOCK, 1 << 24)
        d = t4 - (k0 + kcol)
        ok = (d >= 0) & (d < WINDOW)
        s_ = jnp.where(ok, _dot_nt(qs, kw_refs[i][0, 0]), NEG)
        carry = _softmax_update(carry, s_, vw_refs[i][0, 0])
    o_win = _normalize(carry[1])

    for r in range(HEADS_PER_GROUP):
        rows = slice(r * Q_BLOCK, (r + 1) * Q_BLOCK)
        gate = lambda c: g_ref[0, :, 3 * r + c:3 * r + c + 1]
        o = gate(0) * o_cmp[rows] + gate(1) * o_slc[rows] + gate(2) * o_win[rows]
        o_ref[0, :, r * HEAD_DIM:(r + 1) * HEAD_DIM] = o.astype(BF)


def _nsa(q, gates, kc, vc, ks, vs, kw, vw):
    b, s, _ = q.shape
    ncp = kc.shape[2]
    nwin = WINDOW // Q_BLOCK + 1
    gw = HEADS_PER_GROUP * LANES
    full = lambda n: pl.BlockSpec((1, 1, n, LANES), lambda bi, gi, qi: (bi, gi, 0, 0))
    win = lambda i: pl.BlockSpec((1, 1, Q_BLOCK, LANES),
                                 lambda bi, gi, qi: (bi, gi, jnp.maximum(qi - (nwin - 1) + i, 0), 0))
    return pl.pallas_call(
        _nsa_kernel,
        grid=(b, NSA_GROUPS, s // Q_BLOCK),
        in_specs=[pl.BlockSpec((1, Q_BLOCK, gw), lambda bi, gi, qi: (bi, qi, gi)),
                  pl.BlockSpec((1, Q_BLOCK, LANES), lambda bi, gi, qi: (bi, qi, gi)),
                  full(ncp), full(ncp), full(s), full(s)]
                 + [win(i) for i in range(nwin)] * 2,
        out_specs=pl.BlockSpec((1, Q_BLOCK, HEADS_PER_GROUP * HEAD_DIM), lambda bi, gi, qi: (bi, qi, gi)),
        out_shape=jax.ShapeDtypeStruct((b, s, NSA_HEADS * HEAD_DIM), BF),
        compiler_params=_params(3),
        name="nsa",
    )(q, gates, kc, vc, ks, vs, *([kw] * nwin), *([vw] * nwin))


def _l0_out_kernel(x_ref, a_ref, c_ref, wa_ref, wc_ref, post_ref, o_ref):
    y = _dot(a_ref[...], wa_ref[...]) + _dot(c_ref[...], wc_ref[...])
    o_ref[...] = x_ref[...] + _rmsnorm(y, post_ref[...])


def _l0_out(x2, o_nsa, o_conv, w_out, post_g, *, tm=512):
    t, d = x2.shape
    na, nc = o_nsa.shape[1], o_conv.shape[1]
    row = lambda w: pl.BlockSpec((tm, w), lambda i: (i, 0))
    return pl.pallas_call(
        _l0_out_kernel,
        grid=(t // tm,),
        in_specs=[row(d), row(na), row(nc), _const_spec((na, d)), _const_spec((nc, d)), _const_spec((1, d))],
        out_specs=row(d),
        out_shape=jax.ShapeDtypeStruct((t, d), F32),
        compiler_params=_params(1),
        name="l0_out",
    )(x2, o_nsa, o_conv, w_out[:na].astype(BF), w_out[na:].astype(BF), post_g.reshape(1, d))


def _head_indicators(d):
    ch = jnp.arange(d)[:, None] // RWKV_HEAD
    ind = (ch == jnp.arange(LANES)[None, :]).astype(BF)
    return ind, ind.T


def _rwkv_pre_kernel(x_ref, pre_ref, mu_ref, wr_ref, wk_ref, wv_ref, w0_ref, wd1_ref, wd2_ref,
                     a0_ref, wa1_ref, wa2_ref, wg1_ref, wg2_ref, kk_ref, ka_ref, ind_ref, indt_ref,
                     r_out, lw_out, k_out, v_out, kk_out, kka_out, g_out, hprev_ref):
    si = pl.program_id(1)
    tm = x_ref.shape[1]

    @pl.when(si == 0)
    def _():
        hprev_ref[...] = jnp.zeros(hprev_ref.shape, F32)

    h = _rmsnorm(x_ref[0], pre_ref[...])
    rows = lax.broadcasted_iota(jnp.int32, (tm, 1), 0)
    xx = jnp.where(rows == 0, hprev_ref[7:8], pltpu.roll(h, 1, axis=0)) - h
    hprev_ref[...] = h[tm - 8:tm]
    mix = lambda i: (h + xx * mu_ref[i:i + 1]).astype(BF)
    xr, xw, xk, xv, xa, xg = [mix(i) for i in range(6)]

    r = _dot(xr, wr_ref[...])
    k = _dot(xk, wk_ref[...])
    v = _dot(xv, wv_ref[...])
    z = w0_ref[...] + _dot(jnp.tanh(_dot(xw, wd1_ref[...])).astype(BF), wd2_ref[...])
    softplus = jnp.maximum(-z, 0.0) + jnp.log(1.0 + jnp.exp(-jnp.abs(z)))
    lw = -jnp.exp(-softplus - 0.5)
    a = _sigmoid(a0_ref[...] + _dot(_dot(xa, wa1_ref[...]).astype(BF), wa2_ref[...]))
    g = _dot(_sigmoid(_dot(xg, wg1_ref[...])).astype(BF), wg2_ref[...])

    kraw = k * kk_ref[...]
    ss = _split_dot(kraw * kraw, ind_ref[...])
    inv = lax.rsqrt(jnp.maximum(ss, 1e-12))
    kk = kraw * _split_dot(inv, indt_ref[...])

    r_out[0] = r
    lw_out[0] = lw
    k_out[0] = k * (1.0 + (a - 1.0) * ka_ref[...])
    v_out[0] = v
    kk_out[0] = kk
    kka_out[0] = kk * a
    g_out[0] = g


def _rwkv_pre(x, pre_g, mu, w_r, w_k, w_v, w0, w_dec1, w_dec2, a0, w_a1, w_a2, w_g1, w_g2,
              k_k, k_a, ind, indt, *, tm=256):
    b, s, d = x.shape
    row = pl.BlockSpec((1, tm, d), lambda bi, si: (bi, si, 0))
    vec = lambda a: a.reshape(1, d)
    ops = [x, vec(pre_g), mu, w_r.astype(BF), w_k.astype(BF), w_v.astype(BF), vec(w0),
           w_dec1.astype(BF), w_dec2.astype(BF), vec(a0), w_a1.astype(BF), w_a2.astype(BF),
           w_g1.astype(BF), w_g2.astype(BF), vec(k_k), vec(k_a), ind, indt]
    return pl.pallas_call(
        _rwkv_pre_kernel,
        grid=(b, s // tm),
        in_specs=[row] + [_const_spec(o.shape) for o in ops[1:]],
        out_specs=[row] * 7,
        out_shape=[jax.ShapeDtypeStruct((b, s, d), F32)] * 7,
        scratch_shapes=[pltpu.VMEM((8, d), F32)],
        compiler_params=_params(2),
        name="rwkv_pre",
    )(*ops)


def _rwkv_scan_kernel(r_ref, lw_ref, k_ref, v_ref, kk_ref, kka_ref, y_ref, state_ref):
    c = r_ref.shape[1]
    slab = SLAB_HEADS * RWKV_HEAD
    n_slab = r_ref.shape[2] // slab
    rows = SLAB_HEADS * c

    @pl.when(pl.program_id(1) == 0)
    def _():
        state_ref[...] = jnp.zeros(state_ref.shape, F32)

    ri = lax.broadcasted_iota(jnp.int32, (rows, 1), 0)
    ci = lax.broadcasted_iota(jnp.int32, (1, rows), 1)
    li = lax.broadcasted_iota(jnp.int32, (1, slab), 1)
    c_shift = c.bit_length() - 1
    h_shift = RWKV_HEAD.bit_length() - 1
    assert c == 1 << c_shift and RWKV_HEAD == 1 << h_shift
    head_match = (ri >> c_shift) == (li >> h_shift)
    t_r, t_c = ri & (c - 1), ci & (c - 1)
    strict = t_c < t_r
    incl = t_c <= t_r
    eye = jnp.where(ri == ci, 1.0, 0.0)
    tri = jnp.where(lax.broadcasted_iota(jnp.int32, (c, c), 1) <= lax.broadcasted_iota(jnp.int32, (c, c), 0),
                    1.0, 0.0).astype(BF)
    state_match = (lax.broadcasted_iota(jnp.int32, (slab, 1), 0) >> h_shift) == (li >> h_shift)

    incl2 = jnp.concatenate([incl, incl], axis=1)

    def block_diag(x):
        return jnp.where(head_match, jnp.concatenate([x] * SLAB_HEADS, axis=0), 0.0)

    def fold(x):
        out = x[0:c]
        for h in range(1, SLAB_HEADS):
            out = out + x[h * c:(h + 1) * c]
        return out

    bf = lambda x: x.astype(BF)
    slabs = [slice(hs * slab, (hs + 1) * slab) for hs in range(n_slab)]

    def prepare(sl):
        lw = lw_ref[0, :, sl]
        hi = bf(lw)
        rem = lw - hi.astype(F32)
        mid = bf(rem)
        lo = bf(rem - mid.astype(F32))
        cw = _dot(tri, hi) + _dot(tri, mid) + _dot(tri, lo)
        w_in = jnp.exp(cw)
        w_inv = jnp.exp(-cw)
        w_prev = jnp.exp(cw - lw)
        w_end = w_in[c - 1:c]

        kk = kk_ref[0, :, sl]
        a_t = -kk * w_prev
        b_t = kka_ref[0, :, sl] * w_inv
        k_t = k_ref[0, :, sl] * w_inv
        r_t = r_ref[0, :, sl] * w_in
        return a_t, b_t, k_t, r_t, w_end

    def interactions(a_t, b_t, k_t, r_t):
        lhs = bf(jnp.concatenate([block_diag(a_t), block_diag(r_t)], axis=0))
        rhs = bf(jnp.concatenate([block_diag(b_t), block_diag(k_t)], axis=0))
        big = _dot_nt(lhs, rhs)
        a_ab = jnp.where(strict, big[:rows, :rows], 0.0)
        a_ak = jnp.where(strict, big[:rows, rows:], 0.0)
        a_r = jnp.where(incl2, big[rows:], 0.0)
        return a_ab, bf(a_ak), bf(a_r)

    every = range(n_slab)
    prep = [prepare(sl) for sl in slabs]
    inter = [interactions(*p[:4]) for p in prep]

    states = [state_ref[hs] for hs in every]
    ar = [_dot_nt(bf(jnp.concatenate([prep[hs][0], prep[hs][3]], axis=0)), bf(states[hs])) for hs in every]
    vs = [v_ref[0, :, sl] for sl in slabs]
    vbd = [bf(block_diag(v)) for v in vs]
    akv = [_dot(inter[hs][1], vbd[hs]) for hs in every]

    levels = c.bit_length() - 2
    assert levels >= 1
    invs = [eye + inter[hs][0] for hs in every]
    pws = [_dot(bf(inter[hs][0]), bf(inter[hs][0])) for hs in every]
    for _ in range(levels - 1):
        both = [_dot(bf(pws[hs]), bf(jnp.concatenate([pws[hs], invs[hs]], axis=1))) for hs in every]
        pws = [x[:, :rows] for x in both]
        invs = [invs[hs] + both[hs][:, rows:] for hs in every]
    invs = [bf(invs[hs] + _dot(bf(pws[hs]), bf(invs[hs]))) for hs in every]

    u_bd = [_dot(invs[hs], bf(block_diag(ar[hs][:c]) + akv[hs])) for hs in every]
    y_bd = [block_diag(ar[hs][c:]) + _dot(inter[hs][2], jnp.concatenate([bf(u_bd[hs]), vbd[hs]], axis=0))
            for hs in every]
    for hs, sl in enumerate(slabs):
        y_ref[0, :, sl] = fold(y_bd[hs])
    for hs in every:
        _, b_t, k_t, _, w_end = prep[hs]
        uv = jnp.concatenate([bf(fold(u_bd[hs])), bf(vs[hs])], axis=0)
        bk = jnp.concatenate([bf(b_t * w_end), bf(k_t * w_end)], axis=0)
        state_ref[hs] = jnp.where(state_match, states[hs] * w_end + _dot_tn(uv, bk), 0.0)


def _rwkv_scan(r, lw, k, v, kk, kka):
    b, s, d = r.shape
    c = RWKV_CHUNK
    slab = SLAB_HEADS * RWKV_HEAD
    blk = pl.BlockSpec((1, c, d), lambda bi, ci: (bi, ci, 0))
    return pl.pallas_call(
        _rwkv_scan_kernel,
        grid=(b, s // c),
        in_specs=[blk] * 6,
        out_specs=blk,
        out_shape=jax.ShapeDtypeStruct((b, s, d), F32),
        scratch_shapes=[pltpu.VMEM((d // slab, slab, slab), F32)],
        compiler_params=_params(2),
        name="rwkv_scan",
    )(r, lw, k, v, kk, kka)


def _rwkv_post_kernel(x_ref, y_ref, r_ref, k_ref, v_ref, g_ref, wo_ref, lng_ref, lnb_ref, rk_ref,
                      post_ref, ind_ref, indt_ref, o_ref):
    ind, indt = ind_ref[...], indt_ref[...]
    y = y_ref[...]
    inv_n = 1.0 / RWKV_HEAD
    mean = _split_dot(_split_dot(y, ind) * inv_n, indt)
    yc = y - mean
    var = _split_dot(yc * yc, ind) * inv_n
    rstd = _split_dot(lax.rsqrt(var + LNX_EPS), indt)
    yn = yc * rstd * lng_ref[...] + lnb_ref[...]
    bonus = _split_dot(_split_dot(r_ref[...] * k_ref[...] * rk_ref[...], ind), indt) * v_ref[...]
    out = _dot(((yn + bonus) * g_ref[...]).astype(BF), wo_ref[...])
    o_ref[...] = x_ref[...] + _rmsnorm(out, post_ref[...])


def _rwkv_post(x2, y, r, k, v, g, w_o, lnx_g, lnx_b, r_k, post_g, ind, indt, *, tm=256):
    t, d = x2.shape
    row = pl.BlockSpec((tm, d), lambda i: (i, 0))
    vec = lambda a: a.reshape(1, d)
    ops = [x2, y, r, k, v, g, w_o.astype(BF), vec(lnx_g), vec(lnx_b), vec(r_k), vec(post_g), ind, indt]
    return pl.pallas_call(
        _rwkv_post_kernel,
        grid=(t // tm,),
        in_specs=[row] * 6 + [_const_spec(o.shape) for o in ops[6:]],
        out_specs=row,
        out_shape=jax.ShapeDtypeStruct((t, d), F32),
        compiler_params=_params(1),
        name="rwkv_post",
    )(*ops)


def _layer0_mixer(x, pre_g, post_g, w_in, pe_k, w1_k, w2_k, pe_v, w1_v, w2_v, conv_w, conv_b, w_out):
    b, s, d = x.shape
    q, gates, kcr, vcr, ks, vs, kw, vw, o_conv = _l0_in(x, pre_g, _l0_in_weight(w_in), conv_w, conv_b)
    kc = _compress(kcr, _cmp_weights(pe_k, w1_k, w2_k), is_key=True)
    vc = _compress(vcr, _cmp_weights(pe_v, w1_v, w2_v), is_key=False)
    o_cmp, sel, flags = _select(q, kc, vc)
    o_nsa = _attend(flags, q, gates, o_cmp, sel, ks, vs, kw, vw)
    t = b * s
    return _l0_out(x.reshape(t, d), o_nsa.reshape(t, -1), o_conv.reshape(t, -1), w_out, post_g).reshape(b, s, d)


def _layer1_mixer(x, pre_g, post_g, mu, w_r, w_k, w_v, w_o, w0, w_dec1, w_dec2, a0, w_a1, w_a2,
                  w_g1, w_g2, k_k, k_a, r_k, lnx_g, lnx_b):
    b, s, d = x.shape
    ind, indt = _head_indicators(d)
    r, lw, k, v, kk, kka, g = _rwkv_pre(x, pre_g, mu, w_r, w_k, w_v, w0, w_dec1, w_dec2, a0,
                                        w_a1, w_a2, w_g1, w_g2, k_k, k_a, ind, indt)
    y = _rwkv_scan(r, lw, k, v, kk, kka)
    f = lambda a: a.reshape(b * s, d)
    return _rwkv_post(f(x), f(y), f(r), f(k), f(v), f(g), w_o, lnx_g, lnx_b, r_k, post_g,
                      ind, indt).reshape(b, s, d)


def kernel(x, l0_ffn1_pre_g, l0_ffn1_post_g, l0_ffn1_w_gate, l0_ffn1_w_up, l0_ffn1_w_down, l0_mix_pre_g, l0_mix_post_g, l0_w_in, l0_cmp_pe_k, l0_cmp_w1_k, l0_cmp_w2_k, l0_cmp_pe_v, l0_cmp_w1_v, l0_cmp_w2_v, l0_conv_w, l0_conv_b, l0_w_out, l0_ffn2_pre_g, l0_ffn2_post_g, l0_ffn2_w_gate, l0_ffn2_w_up, l0_ffn2_w_down, l1_ffn1_pre_g, l1_ffn1_post_g, l1_ffn1_w_gate, l1_ffn1_w_up, l1_ffn1_w_down, l1_mix_pre_g, l1_mix_post_g, l1_mu, l1_w_r, l1_w_k, l1_w_v, l1_w_o, l1_w0, l1_w_dec1, l1_w_dec2, l1_a0, l1_w_a1, l1_w_a2, l1_w_g1, l1_w_g2, l1_k_k, l1_k_a, l1_r_k, l1_lnx_g, l1_lnx_b, l1_ffn2_pre_g, l1_ffn2_post_g, l1_ffn2_w_gate, l1_ffn2_w_up, l1_ffn2_w_down):
    b, s, d = x.shape
    ffn = lambda a, *w: _ffn(a.reshape(b * s, d), *w).reshape(b, s, d)
    x = ffn(x, l0_ffn1_pre_g, l0_ffn1_post_g, l0_ffn1_w_gate, l0_ffn1_w_up, l0_ffn1_w_down)
    x = _layer0_mixer(x, l0_mix_pre_g, l0_mix_post_g, l0_w_in, l0_cmp_pe_k, l0_cmp_w1_k, l0_cmp_w2_k,
                      l0_cmp_pe_v, l0_cmp_w1_v, l0_cmp_w2_v, l0_conv_w, l0_conv_b, l0_w_out)
    x = ffn(x, l0_ffn2_pre_g, l0_ffn2_post_g, l0_ffn2_w_gate, l0_ffn2_w_up, l0_ffn2_w_down)
    x = ffn(x, l1_ffn1_pre_g, l1_ffn1_post_g, l1_ffn1_w_gate, l1_ffn1_w_up, l1_ffn1_w_down)
    x = _layer1_mixer(x, l1_mix_pre_g, l1_mix_post_g, l1_mu, l1_w_r, l1_w_k, l1_w_v, l1_w_o, l1_w0,
                      l1_w_dec1, l1_w_dec2, l1_a0, l1_w_a1, l1_w_a2, l1_w_g1, l1_w_g2, l1_k_k, l1_k_a,
                      l1_r_k, l1_lnx_g, l1_lnx_b)
    x = ffn(x, l1_ffn2_pre_g, l1_ffn2_post_g, l1_ffn2_w_gate, l1_ffn2_w_up, l1_ffn2_w_down)
    return x
```

```python
import functools

import jax
import jax.numpy as jnp
from jax import lax
from jax.experimental import pallas as pl
from jax.experimental.pallas import tpu as pltpu

BF = jnp.bfloat16
F32 = jnp.float32

EPS = 1e-6
LNX_EPS = 64e-5
HEAD_DIM = 64
NSA_HEADS = 8
NSA_GROUPS = 2
HEADS_PER_GROUP = NSA_HEADS // NSA_GROUPS
CMP_STRIDE = 16
CMP_BLOCK = 32
CMP_HIDDEN = 128
SLC_BLOCK = 64
SLC_TOP_N = 16
WINDOW = 512
Q_BLOCK = 128
FORCE_BONUS = 1e4
CONV_WIDTH = 512
RWKV_HEAD = 64
RWKV_CHUNK = 64
SLAB_HEADS = 2
LANES = 128
MXU_DIM = 256
POS_SHIFT = 7
POS_SPLIT = 1 << POS_SHIFT
NEG = -1e30
VMEM_LIMIT = 56 * 1024 * 1024

NT_DIMS = (((1,), (1,)), ((), ()))
TN_DIMS = (((0,), (0,)), ((), ()))


def _dot(a, b):
    return jnp.dot(a, b, preferred_element_type=F32)


def _dot_nt(a, b):
    return lax.dot_general(a, b, NT_DIMS, preferred_element_type=F32)


def _dot_tn(a, b):
    return lax.dot_general(a, b, TN_DIMS, preferred_element_type=F32)


def _split_dot(x, w):
    hi = x.astype(BF)
    lo = (x - hi.astype(F32)).astype(BF)
    return _dot(hi, w) + _dot(lo, w)


def _rmsnorm(x, g):
    ms = jnp.mean(x * x, axis=-1, keepdims=True)
    return x * lax.rsqrt(ms + EPS) * g


def _sigmoid(x):
    return 1.0 / (1.0 + jnp.exp(-x))


def _const_spec(shape):
    zeros = (0,) * len(shape)
    return pl.BlockSpec(shape, lambda *_: zeros)


def _params(n_grid):
    return pltpu.CompilerParams(dimension_semantics=("arbitrary",) * n_grid,
                                vmem_limit_bytes=VMEM_LIMIT)


def _ffn_kernel(x_ref, pre_ref, post_ref, wg_ref, wu_ref, wd_ref, o_ref, *, ff_chunk):
    x = x_ref[...]
    h = _rmsnorm(x, pre_ref[...]).astype(BF)
    acc = jnp.zeros(x.shape, F32)
    dff = wg_ref.shape[1]
    for c0 in range(0, dff, ff_chunk):
        c1 = min(c0 + ff_chunk, dff)
        g = _dot(h, wg_ref[:, c0:c1])
        u = _dot(h, wu_ref[:, c0:c1])
        a = (g * _sigmoid(g) * u).astype(BF)
        acc = acc + _dot(a, wd_ref[c0:c1, :])
    o_ref[...] = x + 0.5 * _rmsnorm(acc, post_ref[...])


def _ffn(x2, pre_g, post_g, w_gate, w_up, w_down, *, tm=512):
    t, d = x2.shape
    dff = w_gate.shape[1]
    ff_chunk = MXU_DIM * pl.cdiv(pl.cdiv(dff, MXU_DIM), 2)
    row = pl.BlockSpec((tm, d), lambda i: (i, 0))
    return pl.pallas_call(
        functools.partial(_ffn_kernel, ff_chunk=ff_chunk),
        grid=(t // tm,),
        in_specs=[row, _const_spec((1, d)), _const_spec((1, d)),
                  _const_spec((d, dff)), _const_spec((d, dff)), _const_spec((dff, d))],
        out_specs=row,
        out_shape=jax.ShapeDtypeStruct((t, d), F32),
        compiler_params=_params(1),
        name="ffn",
    )(x2, pre_g.reshape(1, d), post_g.reshape(1, d),
      w_gate.astype(BF), w_up.astype(BF), w_down.astype(BF))


_C_Q = 0
_C_G = _C_Q + NSA_HEADS * LANES
_C_KC = _C_G + NSA_GROUPS * LANES
_C_VC = _C_KC + LANES
_C_KS = _C_VC + LANES
_C_CONV = _C_KS + 4 * NSA_GROUPS * LANES
_C_END = _C_CONV + 3 * CONV_WIDTH


def _l0_in_weight(w_in):
    d = w_in.shape[0]
    nsa_w = NSA_HEADS * HEAD_DIM
    kv_w = NSA_GROUPS * HEAD_DIM
    o = 0
    q = w_in[:, o:o + nsa_w]; o += nsa_w
    g = w_in[:, o:o + 3 * NSA_HEADS]; o += 3 * NSA_HEADS
    kvs = []
    for _ in range(6):
        kvs.append(w_in[:, o:o + kv_w]); o += kv_w
    conv = w_in[:, o:]
    zpad = lambda w, n: jnp.pad(w, ((0, 0), (0, n - w.shape[1])))
    cols = [zpad(q[:, h * HEAD_DIM:(h + 1) * HEAD_DIM], LANES) for h in range(NSA_HEADS)]
    gpg = 3 * HEADS_PER_GROUP
    cols += [zpad(g[:, i * gpg:(i + 1) * gpg], LANES) for i in range(NSA_GROUPS)]
    cols += [kvs[0], kvs[1]]
    for w in kvs[2:]:
        cols += [zpad(w[:, i * HEAD_DIM:(i + 1) * HEAD_DIM], LANES) for i in range(NSA_GROUPS)]
    cols.append(conv)
    w = jnp.concatenate(cols, axis=1)
    assert w.shape == (d, _C_END)
    return w.astype(BF)


def _l0_in_kernel(x_ref, pre_ref, w_ref, cw_ref, cb_ref,
                  q_ref, g_ref, kcr_ref, vcr_ref, ks_ref, vs_ref, kw_ref, vw_ref, oc_ref,
                  zprev_ref):
    si = pl.program_id(1)
    tm = x_ref.shape[1]
    h = _rmsnorm(x_ref[0], pre_ref[...]).astype(BF)
    lane = lax.broadcasted_iota(jnp.int32, (1, LANES), 1)
    pos = si * tm + lax.broadcasted_iota(jnp.int32, (tm, 1), 0)
    pos_hi = (pos >> POS_SHIFT).astype(F32)
    pos_lo = (pos & (POS_SPLIT - 1)).astype(F32)

    wide = 4 * LANES

    def project(c0):
        res = _dot(h, w_ref[:, c0:c0 + wide])
        return [res[:, j * LANES:(j + 1) * LANES] for j in range(4)]

    for hd0 in range(0, NSA_HEADS, 4):
        for j, qh in enumerate(project(_C_Q + hd0 * LANES)):
            hd = hd0 + j
            slope = 2.0 ** (-(hd + 1))
            qh = qh * (HEAD_DIM ** -0.5)
            qh = jnp.where(lane == HEAD_DIM, POS_SPLIT * slope, jnp.where(lane == HEAD_DIM + 1, slope, qh))
            q_ref[0, :, hd * LANES:(hd + 1) * LANES] = qh.astype(BF)

    assert _C_KC == _C_G + 2 * LANES and _C_VC == _C_KC + LANES and NSA_GROUPS == 2
    g0, g1, kcr, vcr = project(_C_G)
    g_ref[0] = _sigmoid(jnp.concatenate([g0, g1], axis=1))
    kcr_ref[0] = kcr
    vcr_ref[0] = vcr

    slabs = project(_C_KS) + project(_C_KS + wide)
    for n, (ref, is_key) in enumerate(((ks_ref, True), (vs_ref, False), (kw_ref, True), (vw_ref, False))):
        for gi in range(NSA_GROUPS):
            t = slabs[n * NSA_GROUPS + gi]
            if is_key:
                t = jnp.where(lane == HEAD_DIM, pos_hi, jnp.where(lane == HEAD_DIM + 1, pos_lo, t))
            else:
                t = jnp.where(lane == HEAD_DIM, 1.0, t)
            ref[0, gi] = t.astype(BF)

    @pl.when(si == 0)
    def _():
        zprev_ref[...] = jnp.zeros(zprev_ref.shape, F32)

    cb = _dot(h, w_ref[:, _C_CONV:_C_CONV + CONV_WIDTH])
    cc = _dot(h, w_ref[:, _C_CONV + CONV_WIDTH:_C_CONV + 2 * CONV_WIDTH])
    cu = _dot(h, w_ref[:, _C_CONV + 2 * CONV_WIDTH:_C_CONV + 3 * CONV_WIDTH])
    z = cc * cu
    prev = zprev_ref[...]
    rows = lax.broadcasted_iota(jnp.int32, (tm, 1), 0)
    zm1 = jnp.where(rows == 0, prev[7:8], pltpu.roll(z, 1, axis=0))
    zm2 = jnp.where(rows == 0, prev[6:7], jnp.where(rows == 1, prev[7:8], pltpu.roll(z, 2, axis=0)))
    y = cw_ref[0:1] * zm2 + cw_ref[1:2] * zm1 + cw_ref[2:3] * z
    oc_ref[0] = (cb * (y + cb_ref[...])).astype(BF)
    zprev_ref[...] = z[tm - 8:tm]


def _l0_in(x, pre_g, w_all, conv_w, conv_b, *, tm=512):
    b, s, d = x.shape
    row = lambda w: pl.BlockSpec((1, tm, w), lambda bi, si: (bi, si, 0))
    grp = pl.BlockSpec((1, NSA_GROUPS, tm, LANES), lambda bi, si: (bi, 0, si, 0))
    sds = jax.ShapeDtypeStruct
    kv_shape = sds((b, NSA_GROUPS, s, LANES), BF)
    return pl.pallas_call(
        _l0_in_kernel,
        grid=(b, s // tm),
        in_specs=[row(d), _const_spec((1, d)), _const_spec(w_all.shape),
                  _const_spec(conv_w.shape), _const_spec((1, CONV_WIDTH))],
        out_specs=[row(NSA_HEADS * LANES), row(NSA_GROUPS * LANES), row(LANES), row(LANES),
                   grp, grp, grp, grp, row(CONV_WIDTH)],
        out_shape=[sds((b, s, NSA_HEADS * LANES), BF), sds((b, s, NSA_GROUPS * LANES), F32),
                   sds((b, s, LANES), F32), sds((b, s, LANES), F32),
                   kv_shape, kv_shape, kv_shape, kv_shape, sds((b, s, CONV_WIDTH), BF)],
        scratch_shapes=[pltpu.VMEM((8, CONV_WIDTH), F32)],
        compiler_params=_params(2),
        name="l0_in",
    )(x, pre_g.reshape(1, d), w_all, conv_w, conv_b.reshape(1, CONV_WIDTH))


def _cmp_weights(pe, w1, w2):
    half = CMP_STRIDE * HEAD_DIM
    def expand(w):
        w = w.reshape(CMP_STRIDE, HEAD_DIM, CMP_HIDDEN)
        out = jnp.zeros((CMP_STRIDE, NSA_GROUPS, HEAD_DIM, NSA_GROUPS, CMP_HIDDEN), F32)
        for gi in range(NSA_GROUPS):
            out = out.at[:, gi, :, gi, :].set(w)
        return out.reshape(CMP_STRIDE * NSA_GROUPS * HEAD_DIM, NSA_GROUPS * CMP_HIDDEN).astype(BF)
    pe_row = lambda p: jnp.tile(p, (1, NSA_GROUPS)).reshape(1, CMP_STRIDE * NSA_GROUPS * HEAD_DIM)
    w2p = jnp.pad(w2, ((0, 0), (0, LANES - HEAD_DIM))).astype(BF)
    return (pe_row(pe[:CMP_STRIDE]), pe_row(pe[CMP_STRIDE:]), expand(w1[:half]), expand(w1[half:]), w2p)


def _gelu_tanh(x):
    return 0.5 * x * (1.0 + jnp.tanh(0.7978845608028654 * (x + 0.044715 * (x * x * x))))


def _cmp_kernel(x_ref, pet_ref, peb_ref, wt_ref, wb_ref, w2_ref, o_ref, *, is_key):
    nb = x_ref.shape[1]
    hw = x_ref.shape[2] // 4
    top, bot = [], []
    for c in range(4):
        xc = x_ref[0, :, c * hw:(c + 1) * hw]
        top.append(_dot((xc + pet_ref[...]).astype(BF), wt_ref[...]))
        bot.append(_dot((xc + peb_ref[...]).astype(BF), wb_ref[...]))
    lane = lax.broadcasted_iota(jnp.int32, (1, LANES), 1)
    j = lax.broadcasted_iota(jnp.int32, (nb, 1), 0)
    for c in range(4):
        nxt = bot[c + 1] if c < 3 else pltpu.roll(bot[0], nb - 1, axis=0)
        hid = _gelu_tanh(top[c] + nxt)
        n = 4 * j + c
        end = n * CMP_STRIDE + (CMP_BLOCK - 1)
        exists = n < 4 * nb - 1
        for gi in range(NSA_GROUPS):
            t = _dot(hid[:, gi * CMP_HIDDEN:(gi + 1) * CMP_HIDDEN].astype(BF), w2_ref[...])
            t = jnp.where(exists, t, 0.0)
            if is_key:
                t = jnp.where(lane == HEAD_DIM, (end >> POS_SHIFT).astype(F32),
                              jnp.where(lane == HEAD_DIM + 1, (end & (POS_SPLIT - 1)).astype(F32), t))
            else:
                t = jnp.where(lane == HEAD_DIM, 1.0, t)
            o_ref[0, gi, c * nb:(c + 1) * nb, :] = t.astype(BF)


def _compress(raw, weights, *, is_key):
    b, s, _ = raw.shape
    nb = s // (4 * CMP_STRIDE)
    x = raw.reshape(b, nb, 4 * CMP_STRIDE * LANES)
    pet, peb, wt, wb, w2p = weights
    return pl.pallas_call(
        functools.partial(_cmp_kernel, is_key=is_key),
        grid=(b,),
        in_specs=[pl.BlockSpec((1, nb, x.shape[2]), lambda bi: (bi, 0, 0)),
                  _const_spec(pet.shape), _const_spec(peb.shape), _const_spec(wt.shape),
                  _const_spec(wb.shape), _const_spec(w2p.shape)],
        out_specs=pl.BlockSpec((1, NSA_GROUPS, 4 * nb, LANES), lambda bi: (bi, 0, 0, 0)),
        out_shape=jax.ShapeDtypeStruct((b, NSA_GROUPS, 4 * nb, LANES), BF),
        compiler_params=_params(1),
        name="compress",
    )(x, pet, peb, wt, wb, w2p)


def _softmax_update(carry, s, v):
    m, acc = carry
    m_new = jnp.maximum(m, jnp.max(s, axis=-1, keepdims=True))
    p = jnp.exp(s - m_new)
    acc = jnp.exp(m - m_new) * acc + _dot(p.astype(BF), v)
    return m_new, acc


def _normalize(acc):
    return acc[:, :HEAD_DIM] / jnp.maximum(acc[:, HEAD_DIM:HEAD_DIM + 1], 1e-30)


N_FORCED = 3
FLAG_BITS = 16
ATT_Q = 256
KEY_TILE = 128
TILE_BLOCKS = KEY_TILE // SLC_BLOCK
GROUP_TILES = 4


def _stack_heads(q_ref, g):
    base = g * HEADS_PER_GROUP
    return jnp.concatenate([q_ref[0, :, (base + r) * LANES:(base + r + 1) * LANES]
                            for r in range(HEADS_PER_GROUP)], axis=0)


def _flag_weights(nb):
    j = jnp.arange(nb)[:, None]
    w = jnp.arange(LANES)[None, :]
    return jnp.where(j // FLAG_BITS == w, 2.0 ** (j % FLAG_BITS), 0.0).astype(BF)


def _prefix_weights(nb):
    return (jnp.arange(nb)[:, None] <= jnp.arange(nb)[None, :]).astype(BF)


def _select_kernel(q_ref, kc_ref, vc_ref, pw_ref, tri_ref, oc_ref, sel_ref, flag_ref):
    qi = pl.program_id(1)
    q0 = qi * Q_BLOCK
    rq = HEADS_PER_GROUP * Q_BLOCK
    ncp = kc_ref.shape[2]
    nb = ncp // 4
    nb_shift = nb.bit_length() - 1
    assert nb == 1 << nb_shift
    t1 = q0 + lax.broadcasted_iota(jnp.int32, (Q_BLOCK, 1), 0)
    t4 = q0 + (lax.broadcasted_iota(jnp.int32, (rq, 1), 0) & (Q_BLOCK - 1))
    col = lax.broadcasted_iota(jnp.int32, (1, ncp), 1)
    cmp_end = (4 * (col & (nb - 1)) + (col >> nb_shift)) * CMP_STRIDE + (CMP_BLOCK - 1)
    valid = cmp_end <= t4
    jl = lax.broadcasted_iota(jnp.int32, (1, nb), 1)
    blk_t = t1 >> 6
    forced = (jl == 0) | (jl == blk_t) | (jl == blk_t - 1)
    eligible = jl <= blk_t

    groups = range(NSA_GROUPS)

    def importance(p):
        ps = p[0:Q_BLOCK]
        for r in range(1, HEADS_PER_GROUP):
            ps = ps + p[r * Q_BLOCK:(r + 1) * Q_BLOCK]
        parts = [ps[:, c * nb:(c + 1) * nb] for c in range(4)]
        prev = jnp.where(jl == 0, 0.0, pltpu.roll(parts[3], 1, axis=1))
        return (parts[0] + parts[1] + parts[2] + parts[3]) - 0.5 * parts[3] + 0.5 * prev

    s = [jnp.where(valid, _dot_nt(_stack_heads(q_ref, g), kc_ref[0, g]), NEG) for g in groups]
    m = [jnp.max(s[g], axis=-1, keepdims=True) for g in groups]
    p = [jnp.where(valid, jnp.exp(s[g] - m[g]), 0.0) for g in groups]
    p = [p[g] / jnp.maximum(jnp.sum(p[g], axis=-1, keepdims=True), 1e-30) for g in groups]
    o_cmp = [_dot(p[g].astype(BF), vc_ref[0, g]) for g in groups]
    pickable = eligible & ~forced
    n_pick = min(SLC_TOP_N, nb) - N_FORCED
    score = [jnp.where(pickable, importance(p[g]), NEG) for g in groups]
    work = score
    for _ in range(n_pick - 1):
        hit = [jl == jnp.argmax(work[g], axis=-1, keepdims=True).astype(jnp.int32) for g in groups]
        work = [jnp.where(hit[g], 3.0 * NEG, work[g]) for g in groups]
    thr = [jnp.max(work[g], axis=-1, keepdims=True) for g in groups]
    above = [score[g] > thr[g] for g in groups]
    tie = [score[g] == thr[g] for g in groups]
    n_above = [jnp.sum(jnp.where(above[g], 1.0, 0.0), axis=-1, keepdims=True) for g in groups]
    tie_rank = [_dot(jnp.where(tie[g], 1.0, 0.0).astype(BF), tri_ref[...]) for g in groups]
    sel = [forced | above[g] | (tie[g] & (tie_rank[g] <= n_pick - n_above[g])) for g in groups]
    for g in groups:
        for r in range(HEADS_PER_GROUP):
            h0 = (g * HEADS_PER_GROUP + r) * HEAD_DIM
            oc_ref[0, :, h0:h0 + HEAD_DIM] = o_cmp[g][r * Q_BLOCK:(r + 1) * Q_BLOCK, :HEAD_DIM]
        sel_g = jnp.where(eligible & sel[g], 1.0, 0.0)
        sel_ref[0, g] = sel_g.astype(BF)
        any_sel = jnp.broadcast_to(jnp.max(sel_g, axis=0, keepdims=True), (8, nb)).astype(BF)
        flag_ref[0, g, 0] = _dot(any_sel, pw_ref[...]).astype(jnp.int32)


def _select(q, kc, vc):
    b, s, _ = q.shape
    ncp = kc.shape[2]
    nb = ncp // 4
    nq = s // Q_BLOCK
    pw = _flag_weights(nb)
    tri = _prefix_weights(nb)
    sds = jax.ShapeDtypeStruct
    cmp_spec = pl.BlockSpec((1, NSA_GROUPS, ncp, LANES), lambda bi, qi: (bi, 0, 0, 0))
    return pl.pallas_call(
        _select_kernel,
        grid=(b, nq),
        in_specs=[pl.BlockSpec((1, Q_BLOCK, NSA_HEADS * LANES), lambda bi, qi: (bi, qi, 0)),
                  cmp_spec, cmp_spec, _const_spec(pw.shape), _const_spec(tri.shape)],
        out_specs=[pl.BlockSpec((1, Q_BLOCK, NSA_HEADS * HEAD_DIM), lambda bi, qi: (bi, qi, 0)),
                   pl.BlockSpec((1, NSA_GROUPS, Q_BLOCK, nb), lambda bi, qi: (bi, 0, qi, 0)),
                   pl.BlockSpec((1, NSA_GROUPS, 1, 8, LANES), lambda bi, qi: (bi, 0, qi, 0, 0))],
        out_shape=[sds((b, s, NSA_HEADS * HEAD_DIM), F32), sds((b, NSA_GROUPS, s, nb), BF),
                   sds((b, NSA_GROUPS, nq, 8, LANES), jnp.int32)],
        compiler_params=_params(2),
        name="nsa_select",
    )(q, kc, vc, pw, tri)


def _attend_kernel(flag_ref, q_ref, g_ref, oc_ref, sel_ref, ks_ref, vs_ref, *rest, n_words):
    nwin = (WINDOW + ATT_Q) // KEY_TILE
    kw_refs, vw_refs = rest[:nwin], rest[nwin:2 * nwin]
    o_ref, list_ref = rest[2 * nwin], rest[2 * nwin + 1]
    bi, gi, qi = pl.program_id(0), pl.program_id(1), pl.program_id(2)
    q0 = qi * ATT_Q
    diag0 = qi * (ATT_Q // KEY_TILE)
    rq = HEADS_PER_GROUP * ATT_Q
    nb = sel_ref.shape[3]
    tiles_per_word = FLAG_BITS // TILE_BLOCKS
    tile_bits = (1 << TILE_BLOCKS) - 1
    sel_per_step = ATT_Q // Q_BLOCK

    qs = _stack_heads(q_ref, 0)
    selb = sel_ref[0, 0]
    t1 = q0 + lax.broadcasted_iota(jnp.int32, (ATT_Q, 1), 0)
    t4 = q0 + (lax.broadcasted_iota(jnp.int32, (rq, 1), 0) & (ATT_Q - 1))

    base = ((bi * NSA_GROUPS + gi) * pl.num_programs(2) + qi) * sel_per_step * n_words

    def word_body(wi, n):
        word = flag_ref[base + wi]
        for extra in range(1, sel_per_step):
            word = word | flag_ref[base + extra * n_words + wi]

        def scan_word(n):
            def tile_body(u, n):
                kt = wi * tiles_per_word + u
                active = (((word >> (u * TILE_BLOCKS)) & tile_bits) != 0) & (kt < diag0)

                @pl.when(active)
                def _():
                    list_ref[n] = kt
                return n + active.astype(jnp.int32)
            return lax.fori_loop(0, tiles_per_word, tile_body, n)
        return lax.cond(word != 0, scan_word, lambda n: n, n)

    n_active = lax.fori_loop(0, (diag0 + tiles_per_word - 1) // tiles_per_word, word_body, 0)
    for u in range(GROUP_TILES):
        list_ref[n_active + u] = 0

    jrow = lax.broadcasted_iota(jnp.int32, (nb, KEY_TILE), 0)
    kblk = lax.broadcasted_iota(jnp.int32, (nb, KEY_TILE), 1) >> 6
    kcol = lax.broadcasted_iota(jnp.int32, (1, KEY_TILE), 1)

    def masked_scores(k_all, member):
        s_ = _dot_nt(qs, k_all)
        return jnp.concatenate([jnp.where(member, s_[r * ATT_Q:(r + 1) * ATT_Q], NEG)
                                for r in range(HEADS_PER_GROUP)], axis=0)

    def tile_rows(ref, kt, n=KEY_TILE):
        return ref[0, 0, pl.ds(pl.multiple_of(kt * KEY_TILE, KEY_TILE), n), :]

    def group_body(it, carry):
        ks_t, vs_t, ex_t = [], [], []
        for u in range(GROUP_TILES):
            pos = it * GROUP_TILES + u
            kt = list_ref[pos]
            first_blk = jnp.where(pos < n_active, kt * TILE_BLOCKS, -TILE_BLOCKS - nb)
            ks_t.append(tile_rows(ks_ref, kt))
            vs_t.append(tile_rows(vs_ref, kt))
            ex_t.append(jnp.where(jrow == kblk + first_blk, 1.0, 0.0).astype(BF))
        member = _dot(selb, jnp.concatenate(ex_t, axis=1)) > 0.5
        s_ = masked_scores(jnp.concatenate(ks_t, axis=0), member)
        return _softmax_update(carry, s_, jnp.concatenate(vs_t, axis=0))

    init = (jnp.full((rq, 1), NEG, F32), jnp.zeros((rq, LANES), F32))
    carry = lax.fori_loop(0, (n_active + GROUP_TILES - 1) // GROUP_TILES, group_body, init)
    drow = lax.broadcasted_iota(jnp.int32, (nb, ATT_Q), 0)
    dcol = lax.broadcasted_iota(jnp.int32, (1, ATT_Q), 1)
    expand = jnp.where(drow == (dcol >> 6) + diag0 * TILE_BLOCKS, 1.0, 0.0).astype(BF)
    member = (_dot(selb, expand) > 0.5) & (q0 + dcol <= t1)
    kpos = []
    for i in range(nwin):
        kti = diag0 - WINDOW // KEY_TILE + i
        kpos.append(jnp.where(kti >= 0, kti * KEY_TILE, 1 << 24) + kcol)
    d = t4 - jnp.concatenate(kpos, axis=1)
    s_d = masked_scores(tile_rows(ks_ref, diag0, ATT_Q), member)
    s_w = _dot_nt(qs, jnp.concatenate([r[0, 0] for r in kw_refs], axis=0))
    s_w = jnp.where((d >= 0) & (d < WINDOW), s_w, NEG)
    m_d = jnp.maximum(carry[0], jnp.max(s_d, axis=-1, keepdims=True))
    m_w = jnp.max(s_w, axis=-1, keepdims=True)
    p_d = jnp.exp(s_d - m_d).astype(BF)
    p_w = jnp.exp(s_w - m_w).astype(BF)
    acc_d = jnp.exp(carry[0] - m_d) * carry[1] + _dot(p_d, tile_rows(vs_ref, diag0, ATT_Q))
    acc_w = _dot(p_w, jnp.concatenate([r[0, 0] for r in vw_refs], axis=0))
    o_slc = _normalize(acc_d)
    o_win = _normalize(acc_w)

    for r in range(HEADS_PER_GROUP):
        rows = slice(r * ATT_Q, (r + 1) * ATT_Q)
        gate = lambda c: g_ref[0, :, 3 * r + c:3 * r + c + 1]
        o = (gate(0) * oc_ref[0, :, r * HEAD_DIM:(r + 1) * HEAD_DIM]
             + gate(1) * o_slc[rows] + gate(2) * o_win[rows])
        o_ref[0, :, r * HEAD_DIM:(r + 1) * HEAD_DIM] = o.astype(BF)


def _attend(flags, q, gates, o_cmp, sel, ks, vs, kw, vw):
    b, s, _ = q.shape
    nb = sel.shape[3]
    n_words = nb // FLAG_BITS
    nwin = (WINDOW + ATT_Q) // KEY_TILE
    per_q = ATT_Q // KEY_TILE
    gw = HEADS_PER_GROUP * LANES
    flat = flags[:, :, :, 0, :n_words].reshape(-1)
    full = pl.BlockSpec((1, 1, s, LANES), lambda bi, gi, qi, f: (bi, gi, 0, 0))
    win = lambda i: pl.BlockSpec(
        (1, 1, KEY_TILE, LANES),
        lambda bi, gi, qi, f: (bi, gi, jnp.maximum(qi * per_q - WINDOW // KEY_TILE + i, 0), 0))
    grid_spec = pltpu.PrefetchScalarGridSpec(
        num_scalar_prefetch=1,
        grid=(b, NSA_GROUPS, s // ATT_Q),
        in_specs=[pl.BlockSpec((1, ATT_Q, gw), lambda bi, gi, qi, f: (bi, qi, gi)),
                  pl.BlockSpec((1, ATT_Q, LANES), lambda bi, gi, qi, f: (bi, qi, gi)),
                  pl.BlockSpec((1, ATT_Q, HEADS_PER_GROUP * HEAD_DIM), lambda bi, gi, qi, f: (bi, qi, gi)),
                  pl.BlockSpec((1, 1, ATT_Q, nb), lambda bi, gi, qi, f: (bi, gi, qi, 0)),
                  full, full] + [win(i) for i in range(nwin)] * 2,
        out_specs=pl.BlockSpec((1, ATT_Q, HEADS_PER_GROUP * HEAD_DIM), lambda bi, gi, qi, f: (bi, qi, gi)),
        scratch_shapes=[pltpu.SMEM((s // KEY_TILE + GROUP_TILES,), jnp.int32)],
    )
    return pl.pallas_call(
        functools.partial(_attend_kernel, n_words=n_words),
        grid_spec=grid_spec,
        out_shape=jax.ShapeDtypeStruct((b, s, NSA_HEADS * HEAD_DIM), BF),
        compiler_params=_params(3),
        name="nsa_attend",
    )(flat, q, gates, o_cmp, sel, ks, vs, *([kw] * nwin), *([vw] * nwin))


def _nsa_kernel(q_ref, g_ref, kc_ref, vc_ref, ks_ref, vs_ref, *rest):
    nwin = WINDOW // Q_BLOCK + 1
    kw_refs, vw_refs, o_ref = rest[:nwin], rest[nwin:2 * nwin], rest[2 * nwin]
    qi = pl.program_id(2)
    q0 = qi * Q_BLOCK
    rq = HEADS_PER_GROUP * Q_BLOCK
    ncp = kc_ref.shape[2]
    nb = ncp // 4
    nb_shift = nb.bit_length() - 1
    assert nb == 1 << nb_shift

    qs = jnp.concatenate([q_ref[0, :, r * LANES:(r + 1) * LANES] for r in range(HEADS_PER_GROUP)], axis=0)
    t1 = q0 + lax.broadcasted_iota(jnp.int32, (Q_BLOCK, 1), 0)
    t4 = q0 + (lax.broadcasted_iota(jnp.int32, (rq, 1), 0) & (Q_BLOCK - 1))

    s = _dot_nt(qs, kc_ref[0, 0])
    col = lax.broadcasted_iota(jnp.int32, (1, ncp), 1)
    cmp_end = (4 * (col & (nb - 1)) + (col >> nb_shift)) * CMP_STRIDE + (CMP_BLOCK - 1)
    valid = cmp_end <= t4
    s = jnp.where(valid, s, NEG)
    m = jnp.max(s, axis=-1, keepdims=True)
    p = jnp.where(valid, jnp.exp(s - m), 0.0)
    p = p / jnp.maximum(jnp.sum(p, axis=-1, keepdims=True), 1e-30)
    o_cmp = _dot(p.astype(BF), vc_ref[0, 0])[:, :HEAD_DIM]

    ps = p[0:Q_BLOCK]
    for r in range(1, HEADS_PER_GROUP):
        ps = ps + p[r * Q_BLOCK:(r + 1) * Q_BLOCK]
    parts = [ps[:, c * nb:(c + 1) * nb] for c in range(4)]
    jl = lax.broadcasted_iota(jnp.int32, (1, nb), 1)
    prev = jnp.where(jl == 0, 0.0, pltpu.roll(parts[3], 1, axis=1))
    imp = (parts[0] + parts[1] + parts[2] + parts[3]) - 0.5 * parts[3] + 0.5 * prev
    blk_t = t1 >> 6
    forced = (jl == 0) | (jl == blk_t) | (jl == blk_t - 1)
    imp = jnp.where(forced, imp + FORCE_BONUS, imp)
    eligible = jl <= blk_t
    work = jnp.where(eligible, imp, NEG)
    jf = jl.astype(F32)
    sel = jnp.zeros((Q_BLOCK, nb), F32)
    for _ in range(min(SLC_TOP_N, nb)):
        mx = jnp.max(work, axis=-1, keepdims=True)
        first = jnp.min(jnp.where(work == mx, jf, float(nb)), axis=-1, keepdims=True)
        hit = jf == first
        sel = jnp.where(hit, 1.0, sel)
        work = jnp.where(hit, 3.0 * NEG, work)
    selb = jnp.where(eligible, sel, 0.0).astype(BF)

    tk = Q_BLOCK
    jrow = lax.broadcasted_iota(jnp.int32, (nb, tk), 0)
    kblk = lax.broadcasted_iota(jnp.int32, (nb, tk), 1) >> 6
    kcol = lax.broadcasted_iota(jnp.int32, (1, tk), 1)

    def slc_tile(kt, carry, causal):
        k0 = pl.multiple_of(kt * tk, tk)
        kt_ = ks_ref[0, 0, pl.ds(k0, tk), :]
        vt_ = vs_ref[0, 0, pl.ds(k0, tk), :]
        expand = jnp.where(jrow == kblk + kt * (tk // SLC_BLOCK), 1.0, 0.0).astype(BF)
        mk = _dot(selb, expand)
        if causal:
            mk = jnp.where(k0 + kcol <= t1, mk, 0.0)
        mk4 = jnp.concatenate([mk] * HEADS_PER_GROUP, axis=0)
        s_ = jnp.where(mk4 > 0.5, _dot_nt(qs, kt_), NEG)
        return _softmax_update(carry, s_, vt_)

    init = (jnp.full((rq, 1), NEG, F32), jnp.zeros((rq, LANES), F32))
    carry = lax.fori_loop(0, qi, lambda kt, c: slc_tile(kt, c, False), init)
    o_slc = _normalize(slc_tile(qi, carry, True)[1])

    carry = init
    for i in range(nwin):
        kti = qi - (nwin - 1) + i
        k0 = jnp.where(kti >= 0, kti * Q_BLOCK, 1 << 24)
        d = t4 - (k0 + kcol)
        ok = (d >= 0) & (d < WINDOW)
        s_ = jnp.where(ok, _dot_nt(qs, kw_refs[i][0, 0]), NEG)
        carry = _softmax_update(carry, s_, vw_refs[i][0, 0])
    o_win = _normalize(carry[1])

    for r in range(HEADS_PER_GROUP):
        rows = slice(r * Q_BLOCK, (r + 1) * Q_BLOCK)
        gate = lambda c: g_ref[0, :, 3 * r + c:3 * r + c + 1]
        o = gate(0) * o_cmp[rows] + gate(1) * o_slc[rows] + gate(2) * o_win[rows]
        o_ref[0, :, r * HEAD_DIM:(r + 1) * HEAD_DIM] = o.astype(BF)


def _nsa(q, gates, kc, vc, ks, vs, kw, vw):
    b, s, _ = q.shape
    ncp = kc.shape[2]
    nwin = WINDOW // Q_BLOCK + 1
    gw = HEADS_PER_GROUP * LANES
    full = lambda n: pl.BlockSpec((1, 1, n, LANES), lambda bi, gi, qi: (bi, gi, 0, 0))
    win = lambda i: pl.BlockSpec((1, 1, Q_BLOCK, LANES),
                                 lambda bi, gi, qi: (bi, gi, jnp.maximum(qi - (nwin - 1) + i, 0), 0))
    return pl.pallas_call(
        _nsa_kernel,
        grid=(b, NSA_GROUPS, s // Q_BLOCK),
        in_specs=[pl.BlockSpec((1, Q_BLOCK, gw), lambda bi, gi, qi: (bi, qi, gi)),
                  pl.BlockSpec((1, Q_BLOCK, LANES), lambda bi, gi, qi: (bi, qi, gi)),
                  full(ncp), full(ncp), full(s), full(s)]
                 + [win(i) for i in range(nwin)] * 2,
        out_specs=pl.BlockSpec((1, Q_BLOCK, HEADS_PER_GROUP * HEAD_DIM), lambda bi, gi, qi: (bi, qi, gi)),
        out_shape=jax.ShapeDtypeStruct((b, s, NSA_HEADS * HEAD_DIM), BF),
        compiler_params=_params(3),
        name="nsa",
    )(q, gates, kc, vc, ks, vs, *([kw] * nwin), *([vw] * nwin))


def _l0_out_kernel(x_ref, a_ref, c_ref, wa_ref, wc_ref, post_ref, o_ref):
    y = _dot(a_ref[...], wa_ref[...]) + _dot(c_ref[...], wc_ref[...])
    o_ref[...] = x_ref[...] + _rmsnorm(y, post_ref[...])


def _l0_out(x2, o_nsa, o_conv, w_out, post_g, *, tm=512):
    t, d = x2.shape
    na, nc = o_nsa.shape[1], o_conv.shape[1]
    row = lambda w: pl.BlockSpec((tm, w), lambda i: (i, 0))
    return pl.pallas_call(
        _l0_out_kernel,
        grid=(t // tm,),
        in_specs=[row(d), row(na), row(nc), _const_spec((na, d)), _const_spec((nc, d)), _const_spec((1, d))],
        out_specs=row(d),
        out_shape=jax.ShapeDtypeStruct((t, d), F32),
        compiler_params=_params(1),
        name="l0_out",
    )(x2, o_nsa, o_conv, w_out[:na].astype(BF), w_out[na:].astype(BF), post_g.reshape(1, d))


def _head_indicators(d):
    ch = jnp.arange(d)[:, None] // RWKV_HEAD
    ind = (ch == jnp.arange(LANES)[None, :]).astype(BF)
    return ind, ind.T


def _rwkv_pre_kernel(x_ref, pre_ref, mu_ref, wr_ref, wk_ref, wv_ref, w0_ref, wd1_ref, wd2_ref,
                     a0_ref, wa1_ref, wa2_ref, wg1_ref, wg2_ref, kk_ref, ka_ref, ind_ref, indt_ref,
                     r_out, lw_out, k_out, v_out, kk_out, kka_out, g_out, hprev_ref):
    si = pl.program_id(1)
    tm = x_ref.shape[1]

    @pl.when(si == 0)
    def _():
        hprev_ref[...] = jnp.zeros(hprev_ref.shape, F32)

    h = _rmsnorm(x_ref[0], pre_ref[...])
    rows = lax.broadcasted_iota(jnp.int32, (tm, 1), 0)
    xx = jnp.where(rows == 0, hprev_ref[7:8], pltpu.roll(h, 1, axis=0)) - h
    hprev_ref[...] = h[tm - 8:tm]
    mix = lambda i: (h + xx * mu_ref[i:i + 1]).astype(BF)
    xr, xw, xk, xv, xa, xg = [mix(i) for i in range(6)]

    r = _dot(xr, wr_ref[...])
    k = _dot(xk, wk_ref[...])
    v = _dot(xv, wv_ref[...])
    z = w0_ref[...] + _dot(jnp.tanh(_dot(xw, wd1_ref[...])).astype(BF), wd2_ref[...])
    softplus = jnp.maximum(-z, 0.0) + jnp.log(1.0 + jnp.exp(-jnp.abs(z)))
    lw = -jnp.exp(-softplus - 0.5)
    a = _sigmoid(a0_ref[...] + _dot(_dot(xa, wa1_ref[...]).astype(BF), wa2_ref[...]))
    g = _dot(_sigmoid(_dot(xg, wg1_ref[...])).astype(BF), wg2_ref[...])

    kraw = k * kk_ref[...]
    ss = _split_dot(kraw * kraw, ind_ref[...])
    inv = lax.rsqrt(jnp.maximum(ss, 1e-12))
    kk = kraw * _split_dot(inv, indt_ref[...])

    r_out[0] = r
    lw_out[0] = lw
    k_out[0] = k * (1.0 + (a - 1.0) * ka_ref[...])
    v_out[0] = v
    kk_out[0] = kk
    kka_out[0] = kk * a
    g_out[0] = g


def _rwkv_pre(x, pre_g, mu, w_r, w_k, w_v, w0, w_dec1, w_dec2, a0, w_a1, w_a2, w_g1, w_g2,
              k_k, k_a, ind, indt, *, tm=256):
    b, s, d = x.shape
    row = pl.BlockSpec((1, tm, d), lambda bi, si: (bi, si, 0))
    vec = lambda a: a.reshape(1, d)
    ops = [x, vec(pre_g), mu, w_r.astype(BF), w_k.astype(BF), w_v.astype(BF), vec(w0),
           w_dec1.astype(BF), w_dec2.astype(BF), vec(a0), w_a1.astype(BF), w_a2.astype(BF),
           w_g1.astype(BF), w_g2.astype(BF), vec(k_k), vec(k_a), ind, indt]
    return pl.pallas_call(
        _rwkv_pre_kernel,
        grid=(b, s // tm),
        in_specs=[row] + [_const_spec(o.shape) for o in ops[1:]],
        out_specs=[row] * 7,
        out_shape=[jax.ShapeDtypeStruct((b, s, d), F32)] * 7,
        scratch_shapes=[pltpu.VMEM((8, d), F32)],
        compiler_params=_params(2),
        name="rwkv_pre",
    )(*ops)


def _rwkv_scan_kernel(r_ref, lw_ref, k_ref, v_ref, kk_ref, kka_ref, y_ref, state_ref):
    c = r_ref.shape[1]
    slab = SLAB_HEADS * RWKV_HEAD
    n_slab = r_ref.shape[2] // slab
    rows = SLAB_HEADS * c

    @pl.when(pl.program_id(1) == 0)
    def _():
        state_ref[...] = jnp.zeros(state_ref.shape, F32)

    ri = lax.broadcasted_iota(jnp.int32, (rows, 1), 0)
    ci = lax.broadcasted_iota(jnp.int32, (1, rows), 1)
    li = lax.broadcasted_iota(jnp.int32, (1, slab), 1)
    c_shift = c.bit_length() - 1
    h_shift = RWKV_HEAD.bit_length() - 1
    assert c == 1 << c_shift and RWKV_HEAD == 1 << h_shift
    head_match = (ri >> c_shift) == (li >> h_shift)
    t_r, t_c = ri & (c - 1), ci & (c - 1)
    strict = t_c < t_r
    incl = t_c <= t_r
    eye = jnp.where(ri == ci, 1.0, 0.0)
    tri = jnp.where(lax.broadcasted_iota(jnp.int32, (c, c), 1) <= lax.broadcasted_iota(jnp.int32, (c, c), 0),
                    1.0, 0.0).astype(BF)
    state_match = (lax.broadcasted_iota(jnp.int32, (slab, 1), 0) >> h_shift) == (li >> h_shift)

    incl2 = jnp.concatenate([incl, incl], axis=1)

    def block_diag(x):
        return jnp.where(head_match, jnp.concatenate([x] * SLAB_HEADS, axis=0), 0.0)

    def fold(x):
        out = x[0:c]
        for h in range(1, SLAB_HEADS):
            out = out + x[h * c:(h + 1) * c]
        return out

    bf = lambda x: x.astype(BF)
    slabs = [slice(hs * slab, (hs + 1) * slab) for hs in range(n_slab)]

    def prepare(sl):
        lw = lw_ref[0, :, sl]
        hi = bf(lw)
        rem = lw - hi.astype(F32)
        mid = bf(rem)
        lo = bf(rem - mid.astype(F32))
        cw = _dot(tri, hi) + _dot(tri, mid) + _dot(tri, lo)
        w_in = jnp.exp(cw)
        w_inv = jnp.exp(-cw)
        w_prev = jnp.exp(cw - lw)
        w_end = w_in[c - 1:c]

        kk = kk_ref[0, :, sl]
        a_t = -kk * w_prev
        b_t = kka_ref[0, :, sl] * w_inv
        k_t = k_ref[0, :, sl] * w_inv
        r_t = r_ref[0, :, sl] * w_in
        return a_t, b_t, k_t, r_t, w_end

    def interactions(a_t, b_t, k_t, r_t):
        lhs = bf(jnp.concatenate([block_diag(a_t), block_diag(r_t)], axis=0))
        rhs = bf(jnp.concatenate([block_diag(b_t), block_diag(k_t)], axis=0))
        big = _dot_nt(lhs, rhs)
        a_ab = jnp.where(strict, big[:rows, :rows], 0.0)
        a_ak = jnp.where(strict, big[:rows, rows:], 0.0)
        a_r = jnp.where(incl2, big[rows:], 0.0)
        return a_ab, bf(a_ak), bf(a_r)

    every = range(n_slab)
    prep = [prepare(sl) for sl in slabs]
    inter = [interactions(*p[:4]) for p in prep]

    states = [state_ref[hs] for hs in every]
    ar = [_dot_nt(bf(jnp.concatenate([prep[hs][0], prep[hs][3]], axis=0)), bf(states[hs])) for hs in every]
    vs = [v_ref[0, :, sl] for sl in slabs]
    vbd = [bf(block_diag(v)) for v in vs]
    akv = [_dot(inter[hs][1], vbd[hs]) for hs in every]

    levels = c.bit_length() - 2
    assert levels >= 1
    invs = [eye + inter[hs][0] for hs in every]
    pws = [_dot(bf(inter[hs][0]), bf(inter[hs][0])) for hs in every]
    for _ in range(levels - 1):
        both = [_dot(bf(pws[hs]), bf(jnp.concatenate([pws[hs], invs[hs]], axis=1))) for hs in every]
        pws = [x[:, :rows] for x in both]
        invs = [invs[hs] + both[hs][:, rows:] for hs in every]
    invs = [bf(invs[hs] + _dot(bf(pws[hs]), bf(invs[hs]))) for hs in every]

    u_bd = [_dot(invs[hs], bf(block_diag(ar[hs][:c]) + akv[hs])) for hs in every]
    y_bd = [block_diag(ar[hs][c:]) + _dot(inter[hs][2], jnp.concatenate([bf(u_bd[hs]), vbd[hs]], axis=0))
            for hs in every]
    for hs, sl in enumerate(slabs):
        y_ref[0, :, sl] = fold(y_bd[hs])
    for hs in every:
        _, b_t, k_t, _, w_end = prep[hs]
        uv = jnp.concatenate([bf(fold(u_bd[hs])), bf(vs[hs])], axis=0)
        bk = jnp.concatenate([bf(b_t * w_end), bf(k_t * w_end)], axis=0)
        state_ref[hs] = jnp.where(state_match, states[hs] * w_end + _dot_tn(uv, bk), 0.0)


def _rwkv_scan(r, lw, k, v, kk, kka):
    b, s, d = r.shape
    c = RWKV_CHUNK
    slab = SLAB_HEADS * RWKV_HEAD
    blk = pl.BlockSpec((1, c, d), lambda bi, ci: (bi, ci, 0))
    return pl.pallas_call(
        _rwkv_scan_kernel,
        grid=(b, s // c),
        in_specs=[blk] * 6,
        out_specs=blk,
        out_shape=jax.ShapeDtypeStruct((b, s, d), F32),
        scratch_shapes=[pltpu.VMEM((d // slab, slab, slab), F32)],
        compiler_params=_params(2),
        name="rwkv_scan",
    )(r, lw, k, v, kk, kka)


def _rwkv_post_kernel(x_ref, y_ref, r_ref, k_ref, v_ref, g_ref, wo_ref, lng_ref, lnb_ref, rk_ref,
                      post_ref, ind_ref, indt_ref, o_ref):
    ind, indt = ind_ref[...], indt_ref[...]
    y = y_ref[...]
    inv_n = 1.0 / RWKV_HEAD
    mean = _split_dot(_split_dot(y, ind) * inv_n, indt)
    yc = y - mean
    var = _split_dot(yc * yc, ind) * inv_n
    rstd = _split_dot(lax.rsqrt(var + LNX_EPS), indt)
    yn = yc * rstd * lng_ref[...] + lnb_ref[...]
    bonus = _split_dot(_split_dot(r_ref[...] * k_ref[...] * rk_ref[...], ind), indt) * v_ref[...]
    out = _dot(((yn + bonus) * g_ref[...]).astype(BF), wo_ref[...])
    o_ref[...] = x_ref[...] + _rmsnorm(out, post_ref[...])


def _rwkv_post(x2, y, r, k, v, g, w_o, lnx_g, lnx_b, r_k, post_g, ind, indt, *, tm=256):
    t, d = x2.shape
    row = pl.BlockSpec((tm, d), lambda i: (i, 0))
    vec = lambda a: a.reshape(1, d)
    ops = [x2, y, r, k, v, g, w_o.astype(BF), vec(lnx_g), vec(lnx_b), vec(r_k), vec(post_g), ind, indt]
    return pl.pallas_call(
        _rwkv_post_kernel,
        grid=(t // tm,),
        in_specs=[row] * 6 + [_const_spec(o.shape) for o in ops[6:]],
        out_specs=row,
        out_shape=jax.ShapeDtypeStruct((t, d), F32),
        compiler_params=_params(1),
        name="rwkv_post",
    )(*ops)


def _layer0_mixer(x, pre_g, post_g, w_in, pe_k, w1_k, w2_k, pe_v, w1_v, w2_v, conv_w, conv_b, w_out):
    b, s, d = x.shape
    q, gates, kcr, vcr, ks, vs, kw, vw, o_conv = _l0_in(x, pre_g, _l0_in_weight(w_in), conv_w, conv_b)
    kc = _compress(kcr, _cmp_weights(pe_k, w1_k, w2_k), is_key=True)
    vc = _compress(vcr, _cmp_weights(pe_v, w1_v, w2_v), is_key=False)
    o_cmp, sel, flags = _select(q, kc, vc)
    o_nsa = _attend(flags, q, gates, o_cmp, sel, ks, vs, kw, vw)
    t = b * s
    return _l0_out(x.reshape(t, d), o_nsa.reshape(t, -1), o_conv.reshape(t, -1), w_out, post_g).reshape(b, s, d)


def _layer1_mixer(x, pre_g, post_g, mu, w_r, w_k, w_v, w_o, w0, w_dec1, w_dec2, a0, w_a1, w_a2,
                  w_g1, w_g2, k_k, k_a, r_k, lnx_g, lnx_b):
    b, s, d = x.shape
    ind, indt = _head_indicators(d)
    r, lw, k, v, kk, kka, g = _rwkv_pre(x, pre_g, mu, w_r, w_k, w_v, w0, w_dec1, w_dec2, a0,
                                        w_a1, w_a2, w_g1, w_g2, k_k, k_a, ind, indt)
    y = _rwkv_scan(r, lw, k, v, kk, kka)
    f = lambda a: a.reshape(b * s, d)
    return _rwkv_post(f(x), f(y), f(r), f(k), f(v), f(g), w_o, lnx_g, lnx_b, r_k, post_g,
                      ind, indt).reshape(b, s, d)


def kernel(x, l0_ffn1_pre_g, l0_ffn1_post_g, l0_ffn1_w_gate, l0_ffn1_w_up, l0_ffn1_w_down, l0_mix_pre_g, l0_mix_post_g, l0_w_in, l0_cmp_pe_k, l0_cmp_w1_k, l0_cmp_w2_k, l0_cmp_pe_v, l0_cmp_w1_v, l0_cmp_w2_v, l0_conv_w, l0_conv_b, l0_w_out, l0_ffn2_pre_g, l0_ffn2_post_g, l0_ffn2_w_gate, l0_ffn2_w_up, l0_ffn2_w_down, l1_ffn1_pre_g, l1_ffn1_post_g, l1_ffn1_w_gate, l1_ffn1_w_up, l1_ffn1_w_down, l1_mix_pre_g, l1_mix_post_g, l1_mu, l1_w_r, l1_w_k, l1_w_v, l1_w_o, l1_w0, l1_w_dec1, l1_w_dec2, l1_a0, l1_w_a1, l1_w_a2, l1_w_g1, l1_w_g2, l1_k_k, l1_k_a, l1_r_k, l1_lnx_g, l1_lnx_b, l1_ffn2_pre_g, l1_ffn2_post_g, l1_ffn2_w_gate, l1_ffn2_w_up, l1_ffn2_w_down):
    b, s, d = x.shape
    ffn = lambda a, *w: _ffn(a.reshape(b * s, d), *w).reshape(b, s, d)
    x = ffn(x, l0_ffn1_pre_g, l0_ffn1_post_g, l0_ffn1_w_gate, l0_ffn1_w_up, l0_ffn1_w_down)
    x = _layer0_mixer(x, l0_mix_pre_g, l0_mix_post_g, l0_w_in, l0_cmp_pe_k, l0_cmp_w1_k, l0_cmp_w2_k,
                      l0_cmp_pe_v, l0_cmp_w1_v, l0_cmp_w2_v, l0_conv_w, l0_conv_b, l0_w_out)
    x = ffn(x, l0_ffn2_pre_g, l0_ffn2_post_g, l0_ffn2_w_gate, l0_ffn2_w_up, l0_ffn2_w_down)
    x = ffn(x, l1_ffn1_pre_g, l1_ffn1_post_g, l1_ffn1_w_gate, l1_ffn1_w_up, l1_ffn1_w_down)
    x = _layer1_mixer(x, l1_mix_pre_g, l1_mix_post_g, l1_mu, l1_w_r, l1_w_k, l1_w_v, l1_w_o, l1_w0,
                      l1_w_dec1, l1_w_dec2, l1_a0, l1_w_a1, l1_w_a2, l1_w_g1, l1_w_g2, l1_k_k, l1_k_a,
                      l1_r_k, l1_lnx_g, l1_lnx_b)
    x = ffn(x, l1_ffn2_pre_g, l1_ffn2_post_g, l1_ffn2_w_gate, l1_ffn2_w_up, l1_ffn2_w_down)
    return x
```

```python
import functools

import jax
import jax.numpy as jnp
from jax import lax
from jax.experimental import pallas as pl
from jax.experimental.pallas import tpu as pltpu

BF = jnp.bfloat16
F32 = jnp.float32

EPS = 1e-6
LNX_EPS = 64e-5
HEAD_DIM = 64
NSA_HEADS = 8
NSA_GROUPS = 2
HEADS_PER_GROUP = NSA_HEADS // NSA_GROUPS
CMP_STRIDE = 16
CMP_BLOCK = 32
CMP_HIDDEN = 128
SLC_BLOCK = 64
SLC_TOP_N = 16
WINDOW = 512
Q_BLOCK = 128
FORCE_BONUS = 1e4
CONV_WIDTH = 512
RWKV_HEAD = 64
RWKV_CHUNK = 64
SLAB_HEADS = 2
SCAN_SUB = 2
LANES = 128
MXU_DIM = 256
POS_SHIFT = 7
POS_SPLIT = 1 << POS_SHIFT
NEG = -1e30
VMEM_LIMIT = 56 * 1024 * 1024

NT_DIMS = (((1,), (1,)), ((), ()))
TN_DIMS = (((0,), (0,)), ((), ()))


def _dot(a, b):
    return jnp.dot(a, b, preferred_element_type=F32)


def _dot_nt(a, b):
    return lax.dot_general(a, b, NT_DIMS, preferred_element_type=F32)


def _dot_tn(a, b):
    return lax.dot_general(a, b, TN_DIMS, preferred_element_type=F32)


def _split_dot(x, w):
    hi = x.astype(BF)
    lo = (x - hi.astype(F32)).astype(BF)
    return _dot(hi, w) + _dot(lo, w)


def _rmsnorm(x, g):
    ms = jnp.mean(x * x, axis=-1, keepdims=True)
    return x * lax.rsqrt(ms + EPS) * g


def _sigmoid(x):
    return 1.0 / (1.0 + jnp.exp(-x))


def _const_spec(shape):
    zeros = (0,) * len(shape)
    return pl.BlockSpec(shape, lambda *_: zeros)


def _params(n_grid):
    return pltpu.CompilerParams(dimension_semantics=("arbitrary",) * n_grid,
                                vmem_limit_bytes=VMEM_LIMIT)


def _ffn_kernel(x_ref, pre_ref, post_ref, wg_ref, wu_ref, wd_ref, o_ref, *, ff_chunk):
    x = x_ref[...]
    h = _rmsnorm(x, pre_ref[...]).astype(BF)
    acc = jnp.zeros(x.shape, F32)
    dff = wg_ref.shape[1]
    for c0 in range(0, dff, ff_chunk):
        c1 = min(c0 + ff_chunk, dff)
        g = _dot(h, wg_ref[:, c0:c1])
        u = _dot(h, wu_ref[:, c0:c1])
        a = (g * _sigmoid(g) * u).astype(BF)
        acc = acc + _dot(a, wd_ref[c0:c1, :])
    o_ref[...] = x + 0.5 * _rmsnorm(acc, post_ref[...])


def _ffn(x2, pre_g, post_g, w_gate, w_up, w_down, *, tm=512):
    t, d = x2.shape
    dff = w_gate.shape[1]
    ff_chunk = MXU_DIM * pl.cdiv(pl.cdiv(dff, MXU_DIM), 2)
    row = pl.BlockSpec((tm, d), lambda i: (i, 0))
    return pl.pallas_call(
        functools.partial(_ffn_kernel, ff_chunk=ff_chunk),
        grid=(t // tm,),
        in_specs=[row, _const_spec((1, d)), _const_spec((1, d)),
                  _const_spec((d, dff)), _const_spec((d, dff)), _const_spec((dff, d))],
        out_specs=row,
        out_shape=jax.ShapeDtypeStruct((t, d), F32),
        compiler_params=_params(1),
        name="ffn",
    )(x2, pre_g.reshape(1, d), post_g.reshape(1, d),
      w_gate.astype(BF), w_up.astype(BF), w_down.astype(BF))


_C_Q = 0
_C_G = _C_Q + NSA_HEADS * LANES
_C_KC = _C_G + NSA_GROUPS * LANES
_C_VC = _C_KC + LANES
_C_KS = _C_VC + LANES
_C_CONV = _C_KS + 4 * NSA_GROUPS * LANES
_C_END = _C_CONV + 3 * CONV_WIDTH


def _l0_in_weight(w_in):
    d = w_in.shape[0]
    nsa_w = NSA_HEADS * HEAD_DIM
    kv_w = NSA_GROUPS * HEAD_DIM
    o = 0
    q = w_in[:, o:o + nsa_w]; o += nsa_w
    g = w_in[:, o:o + 3 * NSA_HEADS]; o += 3 * NSA_HEADS
    kvs = []
    for _ in range(6):
        kvs.append(w_in[:, o:o + kv_w]); o += kv_w
    conv = w_in[:, o:]
    zpad = lambda w, n: jnp.pad(w, ((0, 0), (0, n - w.shape[1])))
    cols = [zpad(q[:, h * HEAD_DIM:(h + 1) * HEAD_DIM], LANES) for h in range(NSA_HEADS)]
    gpg = 3 * HEADS_PER_GROUP
    cols += [zpad(g[:, i * gpg:(i + 1) * gpg], LANES) for i in range(NSA_GROUPS)]
    cols += [kvs[0], kvs[1]]
    for w in kvs[2:]:
        cols += [zpad(w[:, i * HEAD_DIM:(i + 1) * HEAD_DIM], LANES) for i in range(NSA_GROUPS)]
    cols.append(conv)
    w = jnp.concatenate(cols, axis=1)
    assert w.shape == (d, _C_END)
    return w.astype(BF)


def _l0_in_kernel(x_ref, pre_ref, w_ref, cw_ref, cb_ref,
                  q_ref, g_ref, kcr_ref, vcr_ref, ks_ref, vs_ref, kw_ref, vw_ref, oc_ref,
                  zprev_ref):
    si = pl.program_id(1)
    tm = x_ref.shape[1]
    h = _rmsnorm(x_ref[0], pre_ref[...]).astype(BF)
    lane = lax.broadcasted_iota(jnp.int32, (1, LANES), 1)
    pos = si * tm + lax.broadcasted_iota(jnp.int32, (tm, 1), 0)
    pos_hi = (pos >> POS_SHIFT).astype(F32)
    pos_lo = (pos & (POS_SPLIT - 1)).astype(F32)

    wide = 4 * LANES

    def project(c0):
        res = _dot(h, w_ref[:, c0:c0 + wide])
        return [res[:, j * LANES:(j + 1) * LANES] for j in range(4)]

    for hd0 in range(0, NSA_HEADS, 4):
        for j, qh in enumerate(project(_C_Q + hd0 * LANES)):
            hd = hd0 + j
            slope = 2.0 ** (-(hd + 1))
            qh = qh * (HEAD_DIM ** -0.5)
            qh = jnp.where(lane == HEAD_DIM, POS_SPLIT * slope, jnp.where(lane == HEAD_DIM + 1, slope, qh))
            q_ref[0, :, hd * LANES:(hd + 1) * LANES] = qh.astype(BF)

    assert _C_KC == _C_G + 2 * LANES and _C_VC == _C_KC + LANES and NSA_GROUPS == 2
    g0, g1, kcr, vcr = project(_C_G)
    g_ref[0] = _sigmoid(jnp.concatenate([g0, g1], axis=1))
    kcr_ref[0] = kcr
    vcr_ref[0] = vcr

    slabs = project(_C_KS) + project(_C_KS + wide)
    for n, (ref, is_key) in enumerate(((ks_ref, True), (vs_ref, False), (kw_ref, True), (vw_ref, False))):
        for gi in range(NSA_GROUPS):
            t = slabs[n * NSA_GROUPS + gi]
            if is_key:
                t = jnp.where(lane == HEAD_DIM, pos_hi, jnp.where(lane == HEAD_DIM + 1, pos_lo, t))
            else:
                t = jnp.where(lane == HEAD_DIM, 1.0, t)
            ref[0, gi] = t.astype(BF)

    @pl.when(si == 0)
    def _():
        zprev_ref[...] = jnp.zeros(zprev_ref.shape, F32)

    cb = _dot(h, w_ref[:, _C_CONV:_C_CONV + CONV_WIDTH])
    cc = _dot(h, w_ref[:, _C_CONV + CONV_WIDTH:_C_CONV + 2 * CONV_WIDTH])
    cu = _dot(h, w_ref[:, _C_CONV + 2 * CONV_WIDTH:_C_CONV + 3 * CONV_WIDTH])
    z = cc * cu
    prev = zprev_ref[...]
    rows = lax.broadcasted_iota(jnp.int32, (tm, 1), 0)
    zm1 = jnp.where(rows == 0, prev[7:8], pltpu.roll(z, 1, axis=0))
    zm2 = jnp.where(rows == 0, prev[6:7], jnp.where(rows == 1, prev[7:8], pltpu.roll(z, 2, axis=0)))
    y = cw_ref[0:1] * zm2 + cw_ref[1:2] * zm1 + cw_ref[2:3] * z
    oc_ref[0] = (cb * (y + cb_ref[...])).astype(BF)
    zprev_ref[...] = z[tm - 8:tm]


def _l0_in(x, pre_g, w_all, conv_w, conv_b, *, tm=512):
    b, s, d = x.shape
    row = lambda w: pl.BlockSpec((1, tm, w), lambda bi, si: (bi, si, 0))
    grp = pl.BlockSpec((1, NSA_GROUPS, tm, LANES), lambda bi, si: (bi, 0, si, 0))
    sds = jax.ShapeDtypeStruct
    kv_shape = sds((b, NSA_GROUPS, s, LANES), BF)
    return pl.pallas_call(
        _l0_in_kernel,
        grid=(b, s // tm),
        in_specs=[row(d), _const_spec((1, d)), _const_spec(w_all.shape),
                  _const_spec(conv_w.shape), _const_spec((1, CONV_WIDTH))],
        out_specs=[row(NSA_HEADS * LANES), row(NSA_GROUPS * LANES), row(LANES), row(LANES),
                   grp, grp, grp, grp, row(CONV_WIDTH)],
        out_shape=[sds((b, s, NSA_HEADS * LANES), BF), sds((b, s, NSA_GROUPS * LANES), F32),
                   sds((b, s, LANES), F32), sds((b, s, LANES), F32),
                   kv_shape, kv_shape, kv_shape, kv_shape, sds((b, s, CONV_WIDTH), BF)],
        scratch_shapes=[pltpu.VMEM((8, CONV_WIDTH), F32)],
        compiler_params=_params(2),
        name="l0_in",
    )(x, pre_g.reshape(1, d), w_all, conv_w, conv_b.reshape(1, CONV_WIDTH))


def _cmp_weights(pe, w1, w2):
    half = CMP_STRIDE * HEAD_DIM
    def expand(w):
        w = w.reshape(CMP_STRIDE, HEAD_DIM, CMP_HIDDEN)
        out = jnp.zeros((CMP_STRIDE, NSA_GROUPS, HEAD_DIM, NSA_GROUPS, CMP_HIDDEN), F32)
        for gi in range(NSA_GROUPS):
            out = out.at[:, gi, :, gi, :].set(w)
        return out.reshape(CMP_STRIDE * NSA_GROUPS * HEAD_DIM, NSA_GROUPS * CMP_HIDDEN).astype(BF)
    pe_row = lambda p: jnp.tile(p, (1, NSA_GROUPS)).reshape(1, CMP_STRIDE * NSA_GROUPS * HEAD_DIM)
    w2p = jnp.pad(w2, ((0, 0), (0, LANES - HEAD_DIM))).astype(BF)
    return (pe_row(pe[:CMP_STRIDE]), pe_row(pe[CMP_STRIDE:]), expand(w1[:half]), expand(w1[half:]), w2p)


def _gelu_tanh(x):
    return 0.5 * x * (1.0 + jnp.tanh(0.7978845608028654 * (x + 0.044715 * (x * x * x))))


def _cmp_kernel(x_ref, pet_ref, peb_ref, wt_ref, wb_ref, w2_ref, o_ref, *, is_key):
    nb = x_ref.shape[1]
    hw = x_ref.shape[2] // 4
    top, bot = [], []
    for c in range(4):
        xc = x_ref[0, :, c * hw:(c + 1) * hw]
        top.append(_dot((xc + pet_ref[...]).astype(BF), wt_ref[...]))
        bot.append(_dot((xc + peb_ref[...]).astype(BF), wb_ref[...]))
    lane = lax.broadcasted_iota(jnp.int32, (1, LANES), 1)
    j = lax.broadcasted_iota(jnp.int32, (nb, 1), 0)
    for c in range(4):
        nxt = bot[c + 1] if c < 3 else pltpu.roll(bot[0], nb - 1, axis=0)
        hid = _gelu_tanh(top[c] + nxt)
        n = 4 * j + c
        end = n * CMP_STRIDE + (CMP_BLOCK - 1)
        exists = n < 4 * nb - 1
        for gi in range(NSA_GROUPS):
            t = _dot(hid[:, gi * CMP_HIDDEN:(gi + 1) * CMP_HIDDEN].astype(BF), w2_ref[...])
            t = jnp.where(exists, t, 0.0)
            if is_key:
                t = jnp.where(lane == HEAD_DIM, (end >> POS_SHIFT).astype(F32),
                              jnp.where(lane == HEAD_DIM + 1, (end & (POS_SPLIT - 1)).astype(F32), t))
            else:
                t = jnp.where(lane == HEAD_DIM, 1.0, t)
            o_ref[0, gi, c * nb:(c + 1) * nb, :] = t.astype(BF)


def _compress(raw, weights, *, is_key):
    b, s, _ = raw.shape
    nb = s // (4 * CMP_STRIDE)
    x = raw.reshape(b, nb, 4 * CMP_STRIDE * LANES)
    pet, peb, wt, wb, w2p = weights
    return pl.pallas_call(
        functools.partial(_cmp_kernel, is_key=is_key),
        grid=(b,),
        in_specs=[pl.BlockSpec((1, nb, x.shape[2]), lambda bi: (bi, 0, 0)),
                  _const_spec(pet.shape), _const_spec(peb.shape), _const_spec(wt.shape),
                  _const_spec(wb.shape), _const_spec(w2p.shape)],
        out_specs=pl.BlockSpec((1, NSA_GROUPS, 4 * nb, LANES), lambda bi: (bi, 0, 0, 0)),
        out_shape=jax.ShapeDtypeStruct((b, NSA_GROUPS, 4 * nb, LANES), BF),
        compiler_params=_params(1),
        name="compress",
    )(x, pet, peb, wt, wb, w2p)


def _softmax_update(carry, s, v):
    m, acc = carry
    m_new = jnp.maximum(m, jnp.max(s, axis=-1, keepdims=True))
    p = jnp.exp(s - m_new)
    acc = jnp.exp(m - m_new) * acc + _dot(p.astype(BF), v)
    return m_new, acc


def _normalize(acc):
    return acc[:, :HEAD_DIM] / jnp.maximum(acc[:, HEAD_DIM:HEAD_DIM + 1], 1e-30)


N_FORCED = 3
FLAG_BITS = 16
ATT_Q = 256
KEY_TILE = 128
TILE_BLOCKS = KEY_TILE // SLC_BLOCK
GROUP_TILES = 4


def _stack_heads(q_ref, g):
    base = g * HEADS_PER_GROUP
    return jnp.concatenate([q_ref[0, :, (base + r) * LANES:(base + r + 1) * LANES]
                            for r in range(HEADS_PER_GROUP)], axis=0)


def _flag_weights(nb):
    j = jnp.arange(nb)[:, None]
    w = jnp.arange(LANES)[None, :]
    return jnp.where(j // FLAG_BITS == w, 2.0 ** (j % FLAG_BITS), 0.0).astype(BF)


def _prefix_weights(nb):
    return (jnp.arange(nb)[:, None] <= jnp.arange(nb)[None, :]).astype(BF)


def _select_kernel(q_ref, kc_ref, vc_ref, pw_ref, tri_ref, oc_ref, sel_ref, flag_ref):
    qi = pl.program_id(1)
    q0 = qi * Q_BLOCK
    rq = HEADS_PER_GROUP * Q_BLOCK
    nb = kc_ref.shape[2] // 4
    t1 = q0 + lax.broadcasted_iota(jnp.int32, (Q_BLOCK, 1), 0)
    t4 = q0 + (lax.broadcasted_iota(jnp.int32, (rq, 1), 0) & (Q_BLOCK - 1))
    blk_t = t1 >> 6
    groups = range(NSA_GROUPS)

    def run(w):
        w_shift = w.bit_length() - 1
        assert w == 1 << w_shift and w % LANES == 0
        col = lax.broadcasted_iota(jnp.int32, (1, 4 * w), 1)
        cmp_end = (4 * (col & (w - 1)) + (col >> w_shift)) * CMP_STRIDE + (CMP_BLOCK - 1)
        valid = cmp_end <= t4
        jl = lax.broadcasted_iota(jnp.int32, (1, w), 1)
        forced = (jl == 0) | (jl == blk_t) | (jl == blk_t - 1)
        eligible = jl <= blk_t

        def columns(ref, g):
            if w == nb:
                return ref[0, g]
            return jnp.concatenate([ref[0, g, c * nb:c * nb + w] for c in range(4)], axis=0)

        def importance(p):
            ps = p[0:Q_BLOCK]
            for r in range(1, HEADS_PER_GROUP):
                ps = ps + p[r * Q_BLOCK:(r + 1) * Q_BLOCK]
            parts = [ps[:, c * w:(c + 1) * w] for c in range(4)]
            prev = jnp.where(jl == 0, 0.0, pltpu.roll(parts[3], 1, axis=1))
            return (parts[0] + parts[1] + parts[2] + parts[3]) - 0.5 * parts[3] + 0.5 * prev

        s = [jnp.where(valid, _dot_nt(_stack_heads(q_ref, g), columns(kc_ref, g)), NEG) for g in groups]
        m = [jnp.max(s[g], axis=-1, keepdims=True) for g in groups]
        p = [jnp.where(valid, jnp.exp(s[g] - m[g]), 0.0) for g in groups]
        p = [p[g] / jnp.maximum(jnp.sum(p[g], axis=-1, keepdims=True), 1e-30) for g in groups]
        o_cmp = [_dot(p[g].astype(BF), columns(vc_ref, g)) for g in groups]
        pickable = eligible & ~forced
        n_pick = min(SLC_TOP_N, nb) - N_FORCED
        score = [jnp.where(pickable, importance(p[g]), NEG) for g in groups]
        work = score
        for _ in range(n_pick - 1):
            hit = [jl == jnp.argmax(work[g], axis=-1, keepdims=True).astype(jnp.int32) for g in groups]
            work = [jnp.where(hit[g], 3.0 * NEG, work[g]) for g in groups]
        thr = [jnp.max(work[g], axis=-1, keepdims=True) for g in groups]
        above = [score[g] > thr[g] for g in groups]
        tie = [score[g] == thr[g] for g in groups]
        n_above = [jnp.sum(jnp.where(above[g], 1.0, 0.0), axis=-1, keepdims=True) for g in groups]
        tie_rank = [_dot(jnp.where(tie[g], 1.0, 0.0).astype(BF), tri_ref[0:w, 0:w]) for g in groups]
        sel = [forced | above[g] | (tie[g] & (tie_rank[g] <= n_pick - n_above[g])) for g in groups]
        for g in groups:
            for r in range(HEADS_PER_GROUP):
                h0 = (g * HEADS_PER_GROUP + r) * HEAD_DIM
                oc_ref[0, :, h0:h0 + HEAD_DIM] = o_cmp[g][r * Q_BLOCK:(r + 1) * Q_BLOCK, :HEAD_DIM]
            sel_g = jnp.where(eligible & sel[g], 1.0, 0.0)
            sel_ref[0, g, :, 0:w] = sel_g.astype(BF)
            if w < nb:
                sel_ref[0, g, :, w:nb] = jnp.zeros((Q_BLOCK, nb - w), BF)
            any_sel = jnp.broadcast_to(jnp.max(sel_g, axis=0, keepdims=True), (8, w)).astype(BF)
            flag_ref[0, g, 0] = _dot(any_sel, pw_ref[0:w]).astype(jnp.int32)

    half = nb // 2
    if half % LANES == 0:
        in_first_half = q0 + Q_BLOCK <= half * SLC_BLOCK
        pl.when(in_first_half)(lambda: run(half))
        pl.when(jnp.logical_not(in_first_half))(lambda: run(nb))
    else:
        run(nb)


def _select(q, kc, vc):
    b, s, _ = q.shape
    ncp = kc.shape[2]
    nb = ncp // 4
    nq = s // Q_BLOCK
    pw = _flag_weights(nb)
    tri = _prefix_weights(nb)
    sds = jax.ShapeDtypeStruct
    cmp_spec = pl.BlockSpec((1, NSA_GROUPS, ncp, LANES), lambda bi, qi: (bi, 0, 0, 0))
    return pl.pallas_call(
        _select_kernel,
        grid=(b, nq),
        in_specs=[pl.BlockSpec((1, Q_BLOCK, NSA_HEADS * LANES), lambda bi, qi: (bi, qi, 0)),
                  cmp_spec, cmp_spec, _const_spec(pw.shape), _const_spec(tri.shape)],
        out_specs=[pl.BlockSpec((1, Q_BLOCK, NSA_HEADS * HEAD_DIM), lambda bi, qi: (bi, qi, 0)),
                   pl.BlockSpec((1, NSA_GROUPS, Q_BLOCK, nb), lambda bi, qi: (bi, 0, qi, 0)),
                   pl.BlockSpec((1, NSA_GROUPS, 1, 8, LANES), lambda bi, qi: (bi, 0, qi, 0, 0))],
        out_shape=[sds((b, s, NSA_HEADS * HEAD_DIM), F32), sds((b, NSA_GROUPS, s, nb), BF),
                   sds((b, NSA_GROUPS, nq, 8, LANES), jnp.int32)],
        compiler_params=_params(2),
        name="nsa_select",
    )(q, kc, vc, pw, tri)


def _attend_kernel(flag_ref, q_ref, g_ref, oc_ref, sel_ref, ks_ref, vs_ref, *rest, n_words):
    nwin = (WINDOW + ATT_Q) // KEY_TILE
    kw_refs, vw_refs = rest[:nwin], rest[nwin:2 * nwin]
    o_ref, list_ref = rest[2 * nwin], rest[2 * nwin + 1]
    bi, gi, qi = pl.program_id(0), pl.program_id(1), pl.program_id(2)
    q0 = qi * ATT_Q
    diag0 = qi * (ATT_Q // KEY_TILE)
    rq = HEADS_PER_GROUP * ATT_Q
    nb = sel_ref.shape[3]
    tiles_per_word = FLAG_BITS // TILE_BLOCKS
    tile_bits = (1 << TILE_BLOCKS) - 1
    sel_per_step = ATT_Q // Q_BLOCK

    qs = _stack_heads(q_ref, 0)
    selb = sel_ref[0, 0]
    t1 = q0 + lax.broadcasted_iota(jnp.int32, (ATT_Q, 1), 0)
    t4 = q0 + (lax.broadcasted_iota(jnp.int32, (rq, 1), 0) & (ATT_Q - 1))

    base = ((bi * NSA_GROUPS + gi) * pl.num_programs(2) + qi) * sel_per_step * n_words

    def word_body(wi, n):
        word = flag_ref[base + wi]
        for extra in range(1, sel_per_step):
            word = word | flag_ref[base + extra * n_words + wi]

        def scan_word(n):
            def tile_body(u, n):
                kt = wi * tiles_per_word + u
                active = (((word >> (u * TILE_BLOCKS)) & tile_bits) != 0) & (kt < diag0)

                @pl.when(active)
                def _():
                    list_ref[n] = kt
                return n + active.astype(jnp.int32)
            return lax.fori_loop(0, tiles_per_word, tile_body, n)
        return lax.cond(word != 0, scan_word, lambda n: n, n)

    n_active = lax.fori_loop(0, (diag0 + tiles_per_word - 1) // tiles_per_word, word_body, 0)
    for u in range(GROUP_TILES):
        list_ref[n_active + u] = 0

    jrow = lax.broadcasted_iota(jnp.int32, (nb, KEY_TILE), 0)
    kblk = lax.broadcasted_iota(jnp.int32, (nb, KEY_TILE), 1) >> 6
    kcol = lax.broadcasted_iota(jnp.int32, (1, KEY_TILE), 1)

    def masked_scores(k_all, member):
        s_ = _dot_nt(qs, k_all)
        return jnp.concatenate([jnp.where(member, s_[r * ATT_Q:(r + 1) * ATT_Q], NEG)
                                for r in range(HEADS_PER_GROUP)], axis=0)

    def tile_rows(ref, kt, n=KEY_TILE):
        return ref[0, 0, pl.ds(pl.multiple_of(kt * KEY_TILE, KEY_TILE), n), :]

    def group_body(it, carry):
        ks_t, vs_t, ex_t = [], [], []
        for u in range(GROUP_TILES):
            pos = it * GROUP_TILES + u
            kt = list_ref[pos]
            first_blk = jnp.where(pos < n_active, kt * TILE_BLOCKS, -TILE_BLOCKS - nb)
            ks_t.append(tile_rows(ks_ref, kt))
            vs_t.append(tile_rows(vs_ref, kt))
            ex_t.append(jnp.where(jrow == kblk + first_blk, 1.0, 0.0).astype(BF))
        member = _dot(selb, jnp.concatenate(ex_t, axis=1)) > 0.5
        s_ = masked_scores(jnp.concatenate(ks_t, axis=0), member)
        return _softmax_update(carry, s_, jnp.concatenate(vs_t, axis=0))

    init = (jnp.full((rq, 1), NEG, F32), jnp.zeros((rq, LANES), F32))
    carry = lax.fori_loop(0, (n_active + GROUP_TILES - 1) // GROUP_TILES, group_body, init)
    drow = lax.broadcasted_iota(jnp.int32, (nb, ATT_Q), 0)
    dcol = lax.broadcasted_iota(jnp.int32, (1, ATT_Q), 1)
    expand = jnp.where(drow == (dcol >> 6) + diag0 * TILE_BLOCKS, 1.0, 0.0).astype(BF)
    member = (_dot(selb, expand) > 0.5) & (q0 + dcol <= t1)
    kpos = []
    for i in range(nwin):
        kti = diag0 - WINDOW // KEY_TILE + i
        kpos.append(jnp.where(kti >= 0, kti * KEY_TILE, 1 << 24) + kcol)
    d = t4 - jnp.concatenate(kpos, axis=1)
    s_d = masked_scores(tile_rows(ks_ref, diag0, ATT_Q), member)
    s_w = _dot_nt(qs, jnp.concatenate([r[0, 0] for r in kw_refs], axis=0))
    s_w = jnp.where((d >= 0) & (d < WINDOW), s_w, NEG)
    m_d = jnp.maximum(carry[0], jnp.max(s_d, axis=-1, keepdims=True))
    m_w = jnp.max(s_w, axis=-1, keepdims=True)
    p_d = jnp.exp(s_d - m_d).astype(BF)
    p_w = jnp.exp(s_w - m_w).astype(BF)
    acc_d = jnp.exp(carry[0] - m_d) * carry[1] + _dot(p_d, tile_rows(vs_ref, diag0, ATT_Q))
    acc_w = _dot(p_w, jnp.concatenate([r[0, 0] for r in vw_refs], axis=0))
    o_slc = _normalize(acc_d)
    o_win = _normalize(acc_w)

    for r in range(HEADS_PER_GROUP):
        rows = slice(r * ATT_Q, (r + 1) * ATT_Q)
        gate = lambda c: g_ref[0, :, 3 * r + c:3 * r + c + 1]
        o = (gate(0) * oc_ref[0, :, r * HEAD_DIM:(r + 1) * HEAD_DIM]
             + gate(1) * o_slc[rows] + gate(2) * o_win[rows])
        o_ref[0, :, r * HEAD_DIM:(r + 1) * HEAD_DIM] = o.astype(BF)


def _attend(flags, q, gates, o_cmp, sel, ks, vs, kw, vw):
    b, s, _ = q.shape
    nb = sel.shape[3]
    n_words = nb // FLAG_BITS
    nwin = (WINDOW + ATT_Q) // KEY_TILE
    per_q = ATT_Q // KEY_TILE
    gw = HEADS_PER_GROUP * LANES
    flat = flags[:, :, :, 0, :n_words].reshape(-1)
    full = pl.BlockSpec((1, 1, s, LANES), lambda bi, gi, qi, f: (bi, gi, 0, 0))
    win = lambda i: pl.BlockSpec(
        (1, 1, KEY_TILE, LANES),
        lambda bi, gi, qi, f: (bi, gi, jnp.maximum(qi * per_q - WINDOW // KEY_TILE + i, 0), 0))
    grid_spec = pltpu.PrefetchScalarGridSpec(
        num_scalar_prefetch=1,
        grid=(b, NSA_GROUPS, s // ATT_Q),
        in_specs=[pl.BlockSpec((1, ATT_Q, gw), lambda bi, gi, qi, f: (bi, qi, gi)),
                  pl.BlockSpec((1, ATT_Q, LANES), lambda bi, gi, qi, f: (bi, qi, gi)),
                  pl.BlockSpec((1, ATT_Q, HEADS_PER_GROUP * HEAD_DIM), lambda bi, gi, qi, f: (bi, qi, gi)),
                  pl.BlockSpec((1, 1, ATT_Q, nb), lambda bi, gi, qi, f: (bi, gi, qi, 0)),
                  full, full] + [win(i) for i in range(nwin)] * 2,
        out_specs=pl.BlockSpec((1, ATT_Q, HEADS_PER_GROUP * HEAD_DIM), lambda bi, gi, qi, f: (bi, qi, gi)),
        scratch_shapes=[pltpu.SMEM((s // KEY_TILE + GROUP_TILES,), jnp.int32)],
    )
    return pl.pallas_call(
        functools.partial(_attend_kernel, n_words=n_words),
        grid_spec=grid_spec,
        out_shape=jax.ShapeDtypeStruct((b, s, NSA_HEADS * HEAD_DIM), BF),
        compiler_params=_params(3),
        name="nsa_attend",
    )(flat, q, gates, o_cmp, sel, ks, vs, *([kw] * nwin), *([vw] * nwin))


def _nsa_kernel(q_ref, g_ref, kc_ref, vc_ref, ks_ref, vs_ref, *rest):
    nwin = WINDOW // Q_BLOCK + 1
    kw_refs, vw_refs, o_ref = rest[:nwin], rest[nwin:2 * nwin], rest[2 * nwin]
    qi = pl.program_id(2)
    q0 = qi * Q_BLOCK
    rq = HEADS_PER_GROUP * Q_BLOCK
    ncp = kc_ref.shape[2]
    nb = ncp // 4
    nb_shift = nb.bit_length() - 1
    assert nb == 1 << nb_shift

    qs = jnp.concatenate([q_ref[0, :, r * LANES:(r + 1) * LANES] for r in range(HEADS_PER_GROUP)], axis=0)
    t1 = q0 + lax.broadcasted_iota(jnp.int32, (Q_BLOCK, 1), 0)
    t4 = q0 + (lax.broadcasted_iota(jnp.int32, (rq, 1), 0) & (Q_BLOCK - 1))

    s = _dot_nt(qs, kc_ref[0, 0])
    col = lax.broadcasted_iota(jnp.int32, (1, ncp), 1)
    cmp_end = (4 * (col & (nb - 1)) + (col >> nb_shift)) * CMP_STRIDE + (CMP_BLOCK - 1)
    valid = cmp_end <= t4
    s = jnp.where(valid, s, NEG)
    m = jnp.max(s, axis=-1, keepdims=True)
    p = jnp.where(valid, jnp.exp(s - m), 0.0)
    p = p / jnp.maximum(jnp.sum(p, axis=-1, keepdims=True), 1e-30)
    o_cmp = _dot(p.astype(BF), vc_ref[0, 0])[:, :HEAD_DIM]

    ps = p[0:Q_BLOCK]
    for r in range(1, HEADS_PER_GROUP):
        ps = ps + p[r * Q_BLOCK:(r + 1) * Q_BLOCK]
    parts = [ps[:, c * nb:(c + 1) * nb] for c in range(4)]
    jl = lax.broadcasted_iota(jnp.int32, (1, nb), 1)
    prev = jnp.where(jl == 0, 0.0, pltpu.roll(parts[3], 1, axis=1))
    imp = (parts[0] + parts[1] + parts[2] + parts[3]) - 0.5 * parts[3] + 0.5 * prev
    blk_t = t1 >> 6
    forced = (jl == 0) | (jl == blk_t) | (jl == blk_t - 1)
    imp = jnp.where(forced, imp + FORCE_BONUS, imp)
    eligible = jl <= blk_t
    work = jnp.where(eligible, imp, NEG)
    jf = jl.astype(F32)
    sel = jnp.zeros((Q_BLOCK, nb), F32)
    for _ in range(min(SLC_TOP_N, nb)):
        mx = jnp.max(work, axis=-1, keepdims=True)
        first = jnp.min(jnp.where(work == mx, jf, float(nb)), axis=-1, keepdims=True)
        hit = jf == first
        sel = jnp.where(hit, 1.0, sel)
        work = jnp.where(hit, 3.0 * NEG, work)
    selb = jnp.where(eligible, sel, 0.0).astype(BF)

    tk = Q_BLOCK
    jrow = lax.broadcasted_iota(jnp.int32, (nb, tk), 0)
    kblk = lax.broadcasted_iota(jnp.int32, (nb, tk), 1) >> 6
    kcol = lax.broadcasted_iota(jnp.int32, (1, tk), 1)

    def slc_tile(kt, carry, causal):
        k0 = pl.multiple_of(kt * tk, tk)
        kt_ = ks_ref[0, 0, pl.ds(k0, tk), :]
        vt_ = vs_ref[0, 0, pl.ds(k0, tk), :]
        expand = jnp.where(jrow == kblk + kt * (tk // SLC_BLOCK), 1.0, 0.0).astype(BF)
        mk = _dot(selb, expand)
        if causal:
            mk = jnp.where(k0 + kcol <= t1, mk, 0.0)
        mk4 = jnp.concatenate([mk] * HEADS_PER_GROUP, axis=0)
        s_ = jnp.where(mk4 > 0.5, _dot_nt(qs, kt_), NEG)
        return _softmax_update(carry, s_, vt_)

    init = (jnp.full((rq, 1), NEG, F32), jnp.zeros((rq, LANES), F32))
    carry = lax.fori_loop(0, qi, lambda kt, c: slc_tile(kt, c, False), init)
    o_slc = _normalize(slc_tile(qi, carry, True)[1])

    carry = init
    for i in range(nwin):
        kti = qi - (nwin - 1) + i
        k0 = jnp.where(kti >= 0, kti * Q_BLOCK, 1 << 24)
        d = t4 - (k0 + kcol)
        ok = (d >= 0) & (d < WINDOW)
        s_ = jnp.where(ok, _dot_nt(qs, kw_refs[i][0, 0]), NEG)
        carry = _softmax_update(carry, s_, vw_refs[i][0, 0])
    o_win = _normalize(carry[1])

    for r in range(HEADS_PER_GROUP):
        rows = slice(r * Q_BLOCK, (r + 1) * Q_BLOCK)
        gate = lambda c: g_ref[0, :, 3 * r + c:3 * r + c + 1]
        o = gate(0) * o_cmp[rows] + gate(1) * o_slc[rows] + gate(2) * o_win[rows]
        o_ref[0, :, r * HEAD_DIM:(r + 1) * HEAD_DIM] = o.astype(BF)


def _nsa(q, gates, kc, vc, ks, vs, kw, vw):
    b, s, _ = q.shape
    ncp = kc.shape[2]
    nwin = WINDOW // Q_BLOCK + 1
    gw = HEADS_PER_GROUP * LANES
    full = lambda n: pl.BlockSpec((1, 1, n, LANES), lambda bi, gi, qi: (bi, gi, 0, 0))
    win = lambda i: pl.BlockSpec((1, 1, Q_BLOCK, LANES),
                                 lambda bi, gi, qi: (bi, gi, jnp.maximum(qi - (nwin - 1) + i, 0), 0))
    return pl.pallas_call(
        _nsa_kernel,
        grid=(b, NSA_GROUPS, s // Q_BLOCK),
        in_specs=[pl.BlockSpec((1, Q_BLOCK, gw), lambda bi, gi, qi: (bi, qi, gi)),
                  pl.BlockSpec((1, Q_BLOCK, LANES), lambda bi, gi, qi: (bi, qi, gi)),
                  full(ncp), full(ncp), full(s), full(s)]
                 + [win(i) for i in range(nwin)] * 2,
        out_specs=pl.BlockSpec((1, Q_BLOCK, HEADS_PER_GROUP * HEAD_DIM), lambda bi, gi, qi: (bi, qi, gi)),
        out_shape=jax.ShapeDtypeStruct((b, s, NSA_HEADS * HEAD_DIM), BF),
        compiler_params=_params(3),
        name="nsa",
    )(q, gates, kc, vc, ks, vs, *([kw] * nwin), *([vw] * nwin))


def _l0_out_kernel(x_ref, a_ref, c_ref, wa_ref, wc_ref, post_ref, o_ref):
    y = _dot(a_ref[...], wa_ref[...]) + _dot(c_ref[...], wc_ref[...])
    o_ref[...] = x_ref[...] + _rmsnorm(y, post_ref[...])


def _l0_out(x2, o_nsa, o_conv, w_out, post_g, *, tm=512):
    t, d = x2.shape
    na, nc = o_nsa.shape[1], o_conv.shape[1]
    row = lambda w: pl.BlockSpec((tm, w), lambda i: (i, 0))
    return pl.pallas_call(
        _l0_out_kernel,
        grid=(t // tm,),
        in_specs=[row(d), row(na), row(nc), _const_spec((na, d)), _const_spec((nc, d)), _const_spec((1, d))],
        out_specs=row(d),
        out_shape=jax.ShapeDtypeStruct((t, d), F32),
        compiler_params=_params(1),
        name="l0_out",
    )(x2, o_nsa, o_conv, w_out[:na].astype(BF), w_out[na:].astype(BF), post_g.reshape(1, d))


def _head_indicators(d):
    ch = jnp.arange(d)[:, None] // RWKV_HEAD
    ind = (ch == jnp.arange(LANES)[None, :]).astype(BF)
    return ind, ind.T


def _rwkv_pre_kernel(x_ref, pre_ref, mu_ref, wr_ref, wk_ref, wv_ref, w0_ref, wd1_ref, wd2_ref,
                     a0_ref, wa1_ref, wa2_ref, wg1_ref, wg2_ref, kk_ref, ka_ref, ind_ref, indt_ref,
                     r_out, lw_out, k_out, v_out, kk_out, kka_out, g_out, hprev_ref):
    si = pl.program_id(1)
    tm = x_ref.shape[1]

    @pl.when(si == 0)
    def _():
        hprev_ref[...] = jnp.zeros(hprev_ref.shape, F32)

    h = _rmsnorm(x_ref[0], pre_ref[...])
    rows = lax.broadcasted_iota(jnp.int32, (tm, 1), 0)
    xx = jnp.where(rows == 0, hprev_ref[7:8], pltpu.roll(h, 1, axis=0)) - h
    hprev_ref[...] = h[tm - 8:tm]
    mix = lambda i: (h + xx * mu_ref[i:i + 1]).astype(BF)
    xr, xw, xk, xv, xa, xg = [mix(i) for i in range(6)]

    r = _dot(xr, wr_ref[...])
    k = _dot(xk, wk_ref[...])
    v = _dot(xv, wv_ref[...])
    z = w0_ref[...] + _dot(jnp.tanh(_dot(xw, wd1_ref[...])).astype(BF), wd2_ref[...])
    softplus = jnp.maximum(-z, 0.0) + jnp.log(1.0 + jnp.exp(-jnp.abs(z)))
    lw = -jnp.exp(-softplus - 0.5)
    a = _sigmoid(a0_ref[...] + _dot(_dot(xa, wa1_ref[...]).astype(BF), wa2_ref[...]))
    g = _dot(_sigmoid(_dot(xg, wg1_ref[...])).astype(BF), wg2_ref[...])

    kraw = k * kk_ref[...]
    ss = _dot((kraw * kraw).astype(BF), ind_ref[...])
    inv = lax.rsqrt(jnp.maximum(ss, 1e-12))
    kk = kraw * _dot(inv.astype(BF), indt_ref[...])

    r_out[0] = r
    lw_out[0] = lw
    k_out[0] = k * (1.0 + (a - 1.0) * ka_ref[...])
    v_out[0] = v
    kk_out[0] = kk
    kka_out[0] = kk * a
    g_out[0] = g


def _rwkv_pre(x, pre_g, mu, w_r, w_k, w_v, w0, w_dec1, w_dec2, a0, w_a1, w_a2, w_g1, w_g2,
              k_k, k_a, ind, indt, *, tm=256):
    b, s, d = x.shape
    row = pl.BlockSpec((1, tm, d), lambda bi, si: (bi, si, 0))
    vec = lambda a: a.reshape(1, d)
    ops = [x, vec(pre_g), mu, w_r.astype(BF), w_k.astype(BF), w_v.astype(BF), vec(w0),
           w_dec1.astype(BF), w_dec2.astype(BF), vec(a0), w_a1.astype(BF), w_a2.astype(BF),
           w_g1.astype(BF), w_g2.astype(BF), vec(k_k), vec(k_a), ind, indt]
    return pl.pallas_call(
        _rwkv_pre_kernel,
        grid=(b, s // tm),
        in_specs=[row] + [_const_spec(o.shape) for o in ops[1:]],
        out_specs=[row] * 7,
        out_shape=[jax.ShapeDtypeStruct((b, s, d), F32)] * 7,
        scratch_shapes=[pltpu.VMEM((8, d), F32)],
        compiler_params=_params(2),
        name="rwkv_pre",
    )(*ops)


def _rwkv_scan_kernel(r_ref, lw_ref, k_ref, v_ref, kk_ref, kka_ref, y_ref, state_ref):
    c = RWKV_CHUNK
    slab = SLAB_HEADS * RWKV_HEAD
    n_slab = r_ref.shape[2] // slab
    rows = SLAB_HEADS * c
    subs = [slice(i * c, (i + 1) * c) for i in range(r_ref.shape[1] // c)]

    @pl.when(pl.program_id(1) == 0)
    def _():
        state_ref[...] = jnp.zeros(state_ref.shape, F32)

    ri = lax.broadcasted_iota(jnp.int32, (rows, 1), 0)
    ci = lax.broadcasted_iota(jnp.int32, (1, rows), 1)
    li = lax.broadcasted_iota(jnp.int32, (1, slab), 1)
    c_shift = c.bit_length() - 1
    h_shift = RWKV_HEAD.bit_length() - 1
    assert c == 1 << c_shift and RWKV_HEAD == 1 << h_shift
    head_match = (ri >> c_shift) == (li >> h_shift)
    t_r, t_c = ri & (c - 1), ci & (c - 1)
    strict = t_c < t_r
    incl = t_c <= t_r
    eye = jnp.where(ri == ci, 1.0, 0.0)
    tri = jnp.where(lax.broadcasted_iota(jnp.int32, (c, c), 1) <= lax.broadcasted_iota(jnp.int32, (c, c), 0),
                    1.0, 0.0).astype(BF)
    state_match = (lax.broadcasted_iota(jnp.int32, (slab, 1), 0) >> h_shift) == (li >> h_shift)

    incl2 = jnp.concatenate([incl, incl], axis=1)

    def block_diag(x):
        return jnp.where(head_match, jnp.concatenate([x] * SLAB_HEADS, axis=0), 0.0)

    def fold(x):
        out = x[0:c]
        for h in range(1, SLAB_HEADS):
            out = out + x[h * c:(h + 1) * c]
        return out

    bf = lambda x: x.astype(BF)
    slabs = [slice(hs * slab, (hs + 1) * slab) for hs in range(n_slab)]

    def prepare(rs, sl):
        lw = lw_ref[0, rs, sl]
        hi = bf(lw)
        rem = lw - hi.astype(F32)
        mid = bf(rem)
        lo = bf(rem - mid.astype(F32))
        cw = _dot(tri, hi) + _dot(tri, mid) + _dot(tri, lo)
        w_in = jnp.exp(cw)
        w_inv = jnp.exp(-cw)
        w_prev = jnp.exp(cw - lw)
        w_end = w_in[c - 1:c]

        kk = kk_ref[0, rs, sl]
        a_t = -kk * w_prev
        b_t = kka_ref[0, rs, sl] * w_inv
        k_t = k_ref[0, rs, sl] * w_inv
        r_t = r_ref[0, rs, sl] * w_in
        return a_t, b_t, k_t, r_t, w_end

    def interactions(a_t, b_t, k_t, r_t):
        lhs = bf(jnp.concatenate([block_diag(a_t), block_diag(r_t)], axis=0))
        rhs = bf(jnp.concatenate([block_diag(b_t), block_diag(k_t)], axis=0))
        big = _dot_nt(lhs, rhs)
        a_ab = jnp.where(strict, big[:rows, :rows], 0.0)
        a_ak = jnp.where(strict, big[:rows, rows:], 0.0)
        a_r = jnp.where(incl2, big[rows:], 0.0)
        return a_ab, bf(a_ak), bf(a_r)

    every = range(n_slab)
    units = [(rs, sl) for rs in subs for sl in slabs]
    prep = [prepare(rs, sl) for rs, sl in units]
    inter = [interactions(*p[:4]) for p in prep]
    vs = [v_ref[0, rs, sl] for rs, sl in units]
    vbd = [bf(block_diag(v)) for v in vs]
    akv = [_dot(inter[u][1], vbd[u]) for u in range(len(units))]

    levels = c.bit_length() - 2
    assert levels >= 1
    invs = [eye + x[0] for x in inter]
    pws = [_dot(bf(x[0]), bf(x[0])) for x in inter]
    for _ in range(levels - 1):
        both = [_dot(bf(pw), bf(jnp.concatenate([pw, inv], axis=1))) for pw, inv in zip(pws, invs)]
        pws = [x[:, :rows] for x in both]
        invs = [inv + x[:, rows:] for inv, x in zip(invs, both)]
    invs = [bf(inv + _dot(bf(pw), bf(inv))) for pw, inv in zip(pws, invs)]

    states = [state_ref[hs] for hs in every]
    for si, rs in enumerate(subs):
        us = [si * n_slab + hs for hs in every]
        ar = [_dot_nt(bf(jnp.concatenate([prep[u][0], prep[u][3]], axis=0)), bf(states[hs]))
              for hs, u in zip(every, us)]
        u_bd = [_dot(invs[u], bf(block_diag(ar[hs][:c]) + akv[u])) for hs, u in zip(every, us)]
        y_bd = [block_diag(ar[hs][c:]) + _dot(inter[u][2], jnp.concatenate([bf(u_bd[hs]), vbd[u]], axis=0))
                for hs, u in zip(every, us)]
        for hs, sl in enumerate(slabs):
            y_ref[0, rs, sl] = fold(y_bd[hs])
        new_states = []
        for hs, u in zip(every, us):
            _, b_t, k_t, _, w_end = prep[u]
            uv = jnp.concatenate([bf(fold(u_bd[hs])), bf(vs[u])], axis=0)
            bk = jnp.concatenate([bf(b_t * w_end), bf(k_t * w_end)], axis=0)
            new_states.append(jnp.where(state_match, states[hs] * w_end + _dot_tn(uv, bk), 0.0))
        states = new_states
    for hs in every:
        state_ref[hs] = states[hs]


def _rwkv_scan(r, lw, k, v, kk, kka):
    b, s, d = r.shape
    c = RWKV_CHUNK * SCAN_SUB
    slab = SLAB_HEADS * RWKV_HEAD
    blk = pl.BlockSpec((1, c, d), lambda bi, ci: (bi, ci, 0))
    return pl.pallas_call(
        _rwkv_scan_kernel,
        grid=(b, s // c),
        in_specs=[blk] * 6,
        out_specs=blk,
        out_shape=jax.ShapeDtypeStruct((b, s, d), F32),
        scratch_shapes=[pltpu.VMEM((d // slab, slab, slab), F32)],
        compiler_params=_params(2),
        name="rwkv_scan",
    )(r, lw, k, v, kk, kka)


def _rwkv_post_kernel(x_ref, y_ref, r_ref, k_ref, v_ref, g_ref, wo_ref, lng_ref, lnb_ref, rk_ref,
                      post_ref, ind_ref, indt_ref, o_ref):
    ind, indt = ind_ref[...], indt_ref[...]
    y = y_ref[...]
    inv_n = 1.0 / RWKV_HEAD
    head_sum = lambda a: _dot(a.astype(BF), ind)
    spread = lambda a: _dot(a.astype(BF), indt)
    mean = _split_dot(head_sum(y) * inv_n, indt)
    yc = y - mean
    var = head_sum(yc * yc) * inv_n
    rstd = spread(lax.rsqrt(var + LNX_EPS))
    yn = yc * rstd * lng_ref[...] + lnb_ref[...]
    bonus = spread(head_sum(r_ref[...] * k_ref[...] * rk_ref[...])) * v_ref[...]
    out = _dot(((yn + bonus) * g_ref[...]).astype(BF), wo_ref[...])
    o_ref[...] = x_ref[...] + _rmsnorm(out, post_ref[...])


def _rwkv_post(x2, y, r, k, v, g, w_o, lnx_g, lnx_b, r_k, post_g, ind, indt, *, tm=256):
    t, d = x2.shape
    row = pl.BlockSpec((tm, d), lambda i: (i, 0))
    vec = lambda a: a.reshape(1, d)
    ops = [x2, y, r, k, v, g, w_o.astype(BF), vec(lnx_g), vec(lnx_b), vec(r_k), vec(post_g), ind, indt]
    return pl.pallas_call(
        _rwkv_post_kernel,
        grid=(t // tm,),
        in_specs=[row] * 6 + [_const_spec(o.shape) for o in ops[6:]],
        out_specs=row,
        out_shape=jax.ShapeDtypeStruct((t, d), F32),
        compiler_params=_params(1),
        name="rwkv_post",
    )(*ops)


def _layer0_mixer(x, pre_g, post_g, w_in, pe_k, w1_k, w2_k, pe_v, w1_v, w2_v, conv_w, conv_b, w_out):
    b, s, d = x.shape
    q, gates, kcr, vcr, ks, vs, kw, vw, o_conv = _l0_in(x, pre_g, _l0_in_weight(w_in), conv_w, conv_b)
    kc = _compress(kcr, _cmp_weights(pe_k, w1_k, w2_k), is_key=True)
    vc = _compress(vcr, _cmp_weights(pe_v, w1_v, w2_v), is_key=False)
    o_cmp, sel, flags = _select(q, kc, vc)
    o_nsa = _attend(flags, q, gates, o_cmp, sel, ks, vs, kw, vw)
    t = b * s
    return _l0_out(x.reshape(t, d), o_nsa.reshape(t, -1), o_conv.reshape(t, -1), w_out, post_g).reshape(b, s, d)


def _layer1_mixer(x, pre_g, post_g, mu, w_r, w_k, w_v, w_o, w0, w_dec1, w_dec2, a0, w_a1, w_a2,
                  w_g1, w_g2, k_k, k_a, r_k, lnx_g, lnx_b):
    b, s, d = x.shape
    ind, indt = _head_indicators(d)
    r, lw, k, v, kk, kka, g = _rwkv_pre(x, pre_g, mu, w_r, w_k, w_v, w0, w_dec1, w_dec2, a0,
                                        w_a1, w_a2, w_g1, w_g2, k_k, k_a, ind, indt)
    y = _rwkv_scan(r, lw, k, v, kk, kka)
    f = lambda a: a.reshape(b * s, d)
    return _rwkv_post(f(x), f(y), f(r), f(k), f(v), f(g), w_o, lnx_g, lnx_b, r_k, post_g,
                      ind, indt).reshape(b, s, d)


def kernel(x, l0_ffn1_pre_g, l0_ffn1_post_g, l0_ffn1_w_gate, l0_ffn1_w_up, l0_ffn1_w_down, l0_mix_pre_g, l0_mix_post_g, l0_w_in, l0_cmp_pe_k, l0_cmp_w1_k, l0_cmp_w2_k, l0_cmp_pe_v, l0_cmp_w1_v, l0_cmp_w2_v, l0_conv_w, l0_conv_b, l0_w_out, l0_ffn2_pre_g, l0_ffn2_post_g, l0_ffn2_w_gate, l0_ffn2_w_up, l0_ffn2_w_down, l1_ffn1_pre_g, l1_ffn1_post_g, l1_ffn1_w_gate, l1_ffn1_w_up, l1_ffn1_w_down, l1_mix_pre_g, l1_mix_post_g, l1_mu, l1_w_r, l1_w_k, l1_w_v, l1_w_o, l1_w0, l1_w_dec1, l1_w_dec2, l1_a0, l1_w_a1, l1_w_a2, l1_w_g1, l1_w_g2, l1_k_k, l1_k_a, l1_r_k, l1_lnx_g, l1_lnx_b, l1_ffn2_pre_g, l1_ffn2_post_g, l1_ffn2_w_gate, l1_ffn2_w_up, l1_ffn2_w_down):
    b, s, d = x.shape
    ffn = lambda a, *w: _ffn(a.reshape(b * s, d), *w).reshape(b, s, d)
    x = ffn(x, l0_ffn1_pre_g, l0_ffn1_post_g, l0_ffn1_w_gate, l0_ffn1_w_up, l0_ffn1_w_down)
    x = _layer0_mixer(x, l0_mix_pre_g, l0_mix_post_g, l0_w_in, l0_cmp_pe_k, l0_cmp_w1_k, l0_cmp_w2_k,
                      l0_cmp_pe_v, l0_cmp_w1_v, l0_cmp_w2_v, l0_conv_w, l0_conv_b, l0_w_out)
    x = ffn(x, l0_ffn2_pre_g, l0_ffn2_post_g, l0_ffn2_w_gate, l0_ffn2_w_up, l0_ffn2_w_down)
    x = ffn(x, l1_ffn1_pre_g, l1_ffn1_post_g, l1_ffn1_w_gate, l1_ffn1_w_up, l1_ffn1_w_down)
    x = _layer1_mixer(x, l1_mix_pre_g, l1_mix_post_g, l1_mu, l1_w_r, l1_w_k, l1_w_v, l1_w_o, l1_w0,
                      l1_w_dec1, l1_w_dec2, l1_a0, l1_w_a1, l1_w_a2, l1_w_g1, l1_w_g2, l1_k_k, l1_k_a,
                      l1_r_k, l1_lnx_g, l1_lnx_b)
    x = ffn(x, l1_ffn2_pre_g, l1_ffn2_post_g, l1_ffn2_w_gate, l1_ffn2_w_up, l1_ffn2_w_down)
    return x
```

```python
import functools

import jax
import jax.numpy as jnp
from jax import lax
from jax.experimental import pallas as pl
from jax.experimental.pallas import tpu as pltpu

BF = jnp.bfloat16
F32 = jnp.float32

EPS = 1e-6
LNX_EPS = 64e-5
HEAD_DIM = 64
NSA_HEADS = 8
NSA_GROUPS = 2
HEADS_PER_GROUP = NSA_HEADS // NSA_GROUPS
CMP_STRIDE = 16
CMP_BLOCK = 32
CMP_HIDDEN = 128
SLC_BLOCK = 64
SLC_TOP_N = 16
WINDOW = 512
Q_BLOCK = 128
FORCE_BONUS = 1e4
CONV_WIDTH = 512
RWKV_HEAD = 64
RWKV_CHUNK = 64
SLAB_HEADS = 2
SCAN_SUB = 2
LANES = 128
MXU_DIM = 256
POS_SHIFT = 7
POS_SPLIT = 1 << POS_SHIFT
NEG = -1e30
VMEM_LIMIT = 56 * 1024 * 1024

NT_DIMS = (((1,), (1,)), ((), ()))
TN_DIMS = (((0,), (0,)), ((), ()))


def _dot(a, b):
    return jnp.dot(a, b, preferred_element_type=F32)


def _dot_nt(a, b):
    return lax.dot_general(a, b, NT_DIMS, preferred_element_type=F32)


def _dot_tn(a, b):
    return lax.dot_general(a, b, TN_DIMS, preferred_element_type=F32)


def _split_dot(x, w):
    hi = x.astype(BF)
    lo = (x - hi.astype(F32)).astype(BF)
    return _dot(hi, w) + _dot(lo, w)


def _rmsnorm(x, g):
    ms = jnp.mean(x * x, axis=-1, keepdims=True)
    return x * lax.rsqrt(ms + EPS) * g


def _sigmoid(x):
    return 1.0 / (1.0 + jnp.exp(-x))


def _const_spec(shape):
    zeros = (0,) * len(shape)
    return pl.BlockSpec(shape, lambda *_: zeros)


def _params(n_grid):
    return pltpu.CompilerParams(dimension_semantics=("arbitrary",) * n_grid,
                                vmem_limit_bytes=VMEM_LIMIT)


def _ffn_kernel(x_ref, pre_ref, post_ref, wg_ref, wu_ref, wd_ref, o_ref, *, ff_chunk):
    x = x_ref[...]
    h = _rmsnorm(x, pre_ref[...]).astype(BF)
    acc = jnp.zeros(x.shape, F32)
    dff = wg_ref.shape[1]
    for c0 in range(0, dff, ff_chunk):
        c1 = min(c0 + ff_chunk, dff)
        g = _dot(h, wg_ref[:, c0:c1])
        u = _dot(h, wu_ref[:, c0:c1])
        a = (g * _sigmoid(g) * u).astype(BF)
        acc = acc + _dot(a, wd_ref[c0:c1, :])
    o_ref[...] = x + 0.5 * _rmsnorm(acc, post_ref[...])


def _ffn(x2, pre_g, post_g, w_gate, w_up, w_down, *, tm=512):
    t, d = x2.shape
    dff = w_gate.shape[1]
    ff_chunk = MXU_DIM * pl.cdiv(pl.cdiv(dff, MXU_DIM), 2)
    row = pl.BlockSpec((tm, d), lambda i: (i, 0))
    return pl.pallas_call(
        functools.partial(_ffn_kernel, ff_chunk=ff_chunk),
        grid=(t // tm,),
        in_specs=[row, _const_spec((1, d)), _const_spec((1, d)),
                  _const_spec((d, dff)), _const_spec((d, dff)), _const_spec((dff, d))],
        out_specs=row,
        out_shape=jax.ShapeDtypeStruct((t, d), F32),
        compiler_params=_params(1),
        name="ffn",
    )(x2, pre_g.reshape(1, d), post_g.reshape(1, d),
      w_gate.astype(BF), w_up.astype(BF), w_down.astype(BF))


_C_Q = 0
_C_G = _C_Q + NSA_HEADS * LANES
_C_KC = _C_G + NSA_GROUPS * LANES
_C_VC = _C_KC + LANES
_C_KS = _C_VC + LANES
_C_CONV = _C_KS + 4 * NSA_GROUPS * LANES
_C_END = _C_CONV + 3 * CONV_WIDTH


def _l0_in_weight(w_in):
    d = w_in.shape[0]
    nsa_w = NSA_HEADS * HEAD_DIM
    kv_w = NSA_GROUPS * HEAD_DIM
    o = 0
    q = w_in[:, o:o + nsa_w]; o += nsa_w
    g = w_in[:, o:o + 3 * NSA_HEADS]; o += 3 * NSA_HEADS
    kvs = []
    for _ in range(6):
        kvs.append(w_in[:, o:o + kv_w]); o += kv_w
    conv = w_in[:, o:]
    zpad = lambda w, n: jnp.pad(w, ((0, 0), (0, n - w.shape[1])))
    cols = [zpad(q[:, h * HEAD_DIM:(h + 1) * HEAD_DIM], LANES) for h in range(NSA_HEADS)]
    gpg = 3 * HEADS_PER_GROUP
    cols += [zpad(g[:, i * gpg:(i + 1) * gpg], LANES) for i in range(NSA_GROUPS)]
    cols += [kvs[0], kvs[1]]
    for w in kvs[2:]:
        cols += [zpad(w[:, i * HEAD_DIM:(i + 1) * HEAD_DIM], LANES) for i in range(NSA_GROUPS)]
    cols.append(conv)
    w = jnp.concatenate(cols, axis=1)
    assert w.shape == (d, _C_END)
    return w.astype(BF)


def _l0_in_kernel(x_ref, pre_ref, w_ref, cw_ref, cb_ref,
                  q_ref, g_ref, kcr_ref, vcr_ref, ks_ref, vs_ref, kw_ref, vw_ref, oc_ref,
                  zprev_ref):
    si = pl.program_id(1)
    tm = x_ref.shape[1]
    h = _rmsnorm(x_ref[0], pre_ref[...]).astype(BF)
    lane = lax.broadcasted_iota(jnp.int32, (1, LANES), 1)
    pos = si * tm + lax.broadcasted_iota(jnp.int32, (tm, 1), 0)
    pos_hi = (pos >> POS_SHIFT).astype(F32)
    pos_lo = (pos & (POS_SPLIT - 1)).astype(F32)

    wide = 4 * LANES

    def project(c0):
        res = _dot(h, w_ref[:, c0:c0 + wide])
        return [res[:, j * LANES:(j + 1) * LANES] for j in range(4)]

    for hd0 in range(0, NSA_HEADS, 4):
        for j, qh in enumerate(project(_C_Q + hd0 * LANES)):
            hd = hd0 + j
            slope = 2.0 ** (-(hd + 1))
            qh = qh * (HEAD_DIM ** -0.5)
            qh = jnp.where(lane == HEAD_DIM, POS_SPLIT * slope, jnp.where(lane == HEAD_DIM + 1, slope, qh))
            q_ref[0, :, hd * LANES:(hd + 1) * LANES] = qh.astype(BF)

    assert _C_KC == _C_G + 2 * LANES and _C_VC == _C_KC + LANES and NSA_GROUPS == 2
    g0, g1, kcr, vcr = project(_C_G)
    g_ref[0] = _sigmoid(jnp.concatenate([g0, g1], axis=1))
    kcr_ref[0] = kcr
    vcr_ref[0] = vcr

    slabs = project(_C_KS) + project(_C_KS + wide)
    for n, (ref, is_key) in enumerate(((ks_ref, True), (vs_ref, False), (kw_ref, True), (vw_ref, False))):
        for gi in range(NSA_GROUPS):
            t = slabs[n * NSA_GROUPS + gi]
            if is_key:
                t = jnp.where(lane == HEAD_DIM, pos_hi, jnp.where(lane == HEAD_DIM + 1, pos_lo, t))
            else:
                t = jnp.where(lane == HEAD_DIM, 1.0, t)
            ref[0, gi] = t.astype(BF)

    @pl.when(si == 0)
    def _():
        zprev_ref[...] = jnp.zeros(zprev_ref.shape, F32)

    cb = _dot(h, w_ref[:, _C_CONV:_C_CONV + CONV_WIDTH])
    cc = _dot(h, w_ref[:, _C_CONV + CONV_WIDTH:_C_CONV + 2 * CONV_WIDTH])
    cu = _dot(h, w_ref[:, _C_CONV + 2 * CONV_WIDTH:_C_CONV + 3 * CONV_WIDTH])
    z = cc * cu
    prev = zprev_ref[...]
    rows = lax.broadcasted_iota(jnp.int32, (tm, 1), 0)
    zm1 = jnp.where(rows == 0, prev[7:8], pltpu.roll(z, 1, axis=0))
    zm2 = jnp.where(rows == 0, prev[6:7], jnp.where(rows == 1, prev[7:8], pltpu.roll(z, 2, axis=0)))
    y = cw_ref[0:1] * zm2 + cw_ref[1:2] * zm1 + cw_ref[2:3] * z
    oc_ref[0] = (cb * (y + cb_ref[...])).astype(BF)
    zprev_ref[...] = z[tm - 8:tm]


def _l0_in(x, pre_g, w_all, conv_w, conv_b, *, tm=512):
    b, s, d = x.shape
    row = lambda w: pl.BlockSpec((1, tm, w), lambda bi, si: (bi, si, 0))
    grp = pl.BlockSpec((1, NSA_GROUPS, tm, LANES), lambda bi, si: (bi, 0, si, 0))
    sds = jax.ShapeDtypeStruct
    kv_shape = sds((b, NSA_GROUPS, s, LANES), BF)
    return pl.pallas_call(
        _l0_in_kernel,
        grid=(b, s // tm),
        in_specs=[row(d), _const_spec((1, d)), _const_spec(w_all.shape),
                  _const_spec(conv_w.shape), _const_spec((1, CONV_WIDTH))],
        out_specs=[row(NSA_HEADS * LANES), row(NSA_GROUPS * LANES), row(LANES), row(LANES),
                   grp, grp, grp, grp, row(CONV_WIDTH)],
        out_shape=[sds((b, s, NSA_HEADS * LANES), BF), sds((b, s, NSA_GROUPS * LANES), F32),
                   sds((b, s, LANES), F32), sds((b, s, LANES), F32),
                   kv_shape, kv_shape, kv_shape, kv_shape, sds((b, s, CONV_WIDTH), BF)],
        scratch_shapes=[pltpu.VMEM((8, CONV_WIDTH), F32)],
        compiler_params=_params(2),
        name="l0_in",
    )(x, pre_g.reshape(1, d), w_all, conv_w, conv_b.reshape(1, CONV_WIDTH))


def _cmp_weights(pe, w1, w2):
    half = CMP_STRIDE * HEAD_DIM
    def expand(w):
        w = w.reshape(CMP_STRIDE, HEAD_DIM, CMP_HIDDEN)
        out = jnp.zeros((CMP_STRIDE, NSA_GROUPS, HEAD_DIM, NSA_GROUPS, CMP_HIDDEN), F32)
        for gi in range(NSA_GROUPS):
            out = out.at[:, gi, :, gi, :].set(w)
        return out.reshape(CMP_STRIDE * NSA_GROUPS * HEAD_DIM, NSA_GROUPS * CMP_HIDDEN).astype(BF)
    pe_row = lambda p: jnp.tile(p, (1, NSA_GROUPS)).reshape(1, CMP_STRIDE * NSA_GROUPS * HEAD_DIM)
    w2p = jnp.pad(w2, ((0, 0), (0, LANES - HEAD_DIM))).astype(BF)
    return (pe_row(pe[:CMP_STRIDE]), pe_row(pe[CMP_STRIDE:]), expand(w1[:half]), expand(w1[half:]), w2p)


def _gelu_tanh(x):
    return 0.5 * x * (1.0 + jnp.tanh(0.7978845608028654 * (x + 0.044715 * (x * x * x))))


def _cmp_kernel(x_ref, pet_ref, peb_ref, wt_ref, wb_ref, w2_ref, o_ref, *, is_key):
    nb = x_ref.shape[1]
    hw = x_ref.shape[2] // 4
    top, bot = [], []
    for c in range(4):
        xc = x_ref[0, :, c * hw:(c + 1) * hw]
        top.append(_dot((xc + pet_ref[...]).astype(BF), wt_ref[...]))
        bot.append(_dot((xc + peb_ref[...]).astype(BF), wb_ref[...]))
    lane = lax.broadcasted_iota(jnp.int32, (1, LANES), 1)
    j = lax.broadcasted_iota(jnp.int32, (nb, 1), 0)
    for c in range(4):
        nxt = bot[c + 1] if c < 3 else pltpu.roll(bot[0], nb - 1, axis=0)
        hid = _gelu_tanh(top[c] + nxt)
        n = 4 * j + c
        end = n * CMP_STRIDE + (CMP_BLOCK - 1)
        exists = n < 4 * nb - 1
        for gi in range(NSA_GROUPS):
            t = _dot(hid[:, gi * CMP_HIDDEN:(gi + 1) * CMP_HIDDEN].astype(BF), w2_ref[...])
            t = jnp.where(exists, t, 0.0)
            if is_key:
                t = jnp.where(lane == HEAD_DIM, (end >> POS_SHIFT).astype(F32),
                              jnp.where(lane == HEAD_DIM + 1, (end & (POS_SPLIT - 1)).astype(F32), t))
            else:
                t = jnp.where(lane == HEAD_DIM, 1.0, t)
            o_ref[0, gi, c * nb:(c + 1) * nb, :] = t.astype(BF)


def _compress(raw, weights, *, is_key):
    b, s, _ = raw.shape
    nb = s // (4 * CMP_STRIDE)
    x = raw.reshape(b, nb, 4 * CMP_STRIDE * LANES)
    pet, peb, wt, wb, w2p = weights
    return pl.pallas_call(
        functools.partial(_cmp_kernel, is_key=is_key),
        grid=(b,),
        in_specs=[pl.BlockSpec((1, nb, x.shape[2]), lambda bi: (bi, 0, 0)),
                  _const_spec(pet.shape), _const_spec(peb.shape), _const_spec(wt.shape),
                  _const_spec(wb.shape), _const_spec(w2p.shape)],
        out_specs=pl.BlockSpec((1, NSA_GROUPS, 4 * nb, LANES), lambda bi: (bi, 0, 0, 0)),
        out_shape=jax.ShapeDtypeStruct((b, NSA_GROUPS, 4 * nb, LANES), BF),
        compiler_params=_params(1),
        name="compress",
    )(x, pet, peb, wt, wb, w2p)


def _softmax_update(carry, s, v):
    m, acc = carry
    m_new = jnp.maximum(m, jnp.max(s, axis=-1, keepdims=True))
    p = jnp.exp(s - m_new)
    acc = jnp.exp(m - m_new) * acc + _dot(p.astype(BF), v)
    return m_new, acc


def _normalize(acc):
    return acc[:, :HEAD_DIM] / jnp.maximum(acc[:, HEAD_DIM:HEAD_DIM + 1], 1e-30)


N_FORCED = 3
FLAG_BITS = 16
ATT_Q = 256
KEY_TILE = 128
TILE_BLOCKS = KEY_TILE // SLC_BLOCK
GROUP_TILES = 4


def _stack_heads(q_ref, g):
    base = g * HEADS_PER_GROUP
    return jnp.concatenate([q_ref[0, :, (base + r) * LANES:(base + r + 1) * LANES]
                            for r in range(HEADS_PER_GROUP)], axis=0)


def _flag_weights(nb):
    j = jnp.arange(nb)[:, None]
    w = jnp.arange(LANES)[None, :]
    return jnp.where(j // FLAG_BITS == w, 2.0 ** (j % FLAG_BITS), 0.0).astype(BF)


def _prefix_weights(nb):
    return (jnp.arange(nb)[:, None] <= jnp.arange(nb)[None, :]).astype(BF)


def _select_kernel(q_ref, kc_ref, vc_ref, pw_ref, tri_ref, oc_ref, sel_ref, flag_ref):
    qi = pl.program_id(1)
    q0 = qi * Q_BLOCK
    rq = HEADS_PER_GROUP * Q_BLOCK
    nb = kc_ref.shape[2] // 4
    t1 = q0 + lax.broadcasted_iota(jnp.int32, (Q_BLOCK, 1), 0)
    t4 = q0 + (lax.broadcasted_iota(jnp.int32, (rq, 1), 0) & (Q_BLOCK - 1))
    blk_t = t1 >> 6
    groups = range(NSA_GROUPS)

    def run(w):
        w_shift = w.bit_length() - 1
        assert w == 1 << w_shift and w % LANES == 0
        col = lax.broadcasted_iota(jnp.int32, (1, 4 * w), 1)
        cmp_end = (4 * (col & (w - 1)) + (col >> w_shift)) * CMP_STRIDE + (CMP_BLOCK - 1)
        valid = cmp_end <= t4
        jl = lax.broadcasted_iota(jnp.int32, (1, w), 1)
        forced = (jl == 0) | (jl == blk_t) | (jl == blk_t - 1)
        eligible = jl <= blk_t

        def columns(ref, g):
            if w == nb:
                return ref[0, g]
            return jnp.concatenate([ref[0, g, c * nb:c * nb + w] for c in range(4)], axis=0)

        def importance(p):
            ps = p[0:Q_BLOCK]
            for r in range(1, HEADS_PER_GROUP):
                ps = ps + p[r * Q_BLOCK:(r + 1) * Q_BLOCK]
            parts = [ps[:, c * w:(c + 1) * w] for c in range(4)]
            prev = jnp.where(jl == 0, 0.0, pltpu.roll(parts[3], 1, axis=1))
            return (parts[0] + parts[1] + parts[2] + parts[3]) - 0.5 * parts[3] + 0.5 * prev

        s = [jnp.where(valid, _dot_nt(_stack_heads(q_ref, g), columns(kc_ref, g)), NEG) for g in groups]
        m = [jnp.max(s[g], axis=-1, keepdims=True) for g in groups]
        p = [jnp.where(valid, jnp.exp(s[g] - m[g]), 0.0) for g in groups]
        p = [p[g] / jnp.maximum(jnp.sum(p[g], axis=-1, keepdims=True), 1e-30) for g in groups]
        o_cmp = [_dot(p[g].astype(BF), columns(vc_ref, g)) for g in groups]
        pickable = eligible & ~forced
        n_pick = min(SLC_TOP_N, nb) - N_FORCED
        score = [jnp.where(pickable, importance(p[g]), NEG) for g in groups]
        work = score
        for _ in range(n_pick - 1):
            hit = [jl == jnp.argmax(work[g], axis=-1, keepdims=True).astype(jnp.int32) for g in groups]
            work = [jnp.where(hit[g], 3.0 * NEG, work[g]) for g in groups]
        thr = [jnp.max(work[g], axis=-1, keepdims=True) for g in groups]
        above = [score[g] > thr[g] for g in groups]
        tie = [score[g] == thr[g] for g in groups]
        n_above = [jnp.sum(jnp.where(above[g], 1.0, 0.0), axis=-1, keepdims=True) for g in groups]
        tie_rank = [_dot(jnp.where(tie[g], 1.0, 0.0).astype(BF), tri_ref[0:w, 0:w]) for g in groups]
        sel = [forced | above[g] | (tie[g] & (tie_rank[g] <= n_pick - n_above[g])) for g in groups]
        for g in groups:
            for r in range(HEADS_PER_GROUP):
                h0 = (g * HEADS_PER_GROUP + r) * HEAD_DIM
                oc_ref[0, :, h0:h0 + HEAD_DIM] = o_cmp[g][r * Q_BLOCK:(r + 1) * Q_BLOCK, :HEAD_DIM]
            sel_g = jnp.where(eligible & sel[g], 1.0, 0.0)
            sel_ref[0, g, :, 0:w] = sel_g.astype(BF)
            if w < nb:
                sel_ref[0, g, :, w:nb] = jnp.zeros((Q_BLOCK, nb - w), BF)
            any_sel = jnp.broadcast_to(jnp.max(sel_g, axis=0, keepdims=True), (8, w)).astype(BF)
            flag_ref[0, g, 0] = _dot(any_sel, pw_ref[0:w]).astype(jnp.int32)

    half = nb // 2
    if half % LANES == 0:
        in_first_half = q0 + Q_BLOCK <= half * SLC_BLOCK
        pl.when(in_first_half)(lambda: run(half))
        pl.when(jnp.logical_not(in_first_half))(lambda: run(nb))
    else:
        run(nb)


def _select(q, kc, vc):
    b, s, _ = q.shape
    ncp = kc.shape[2]
    nb = ncp // 4
    nq = s // Q_BLOCK
    pw = _flag_weights(nb)
    tri = _prefix_weights(nb)
    sds = jax.ShapeDtypeStruct
    cmp_spec = pl.BlockSpec((1, NSA_GROUPS, ncp, LANES), lambda bi, qi: (bi, 0, 0, 0))
    return pl.pallas_call(
        _select_kernel,
        grid=(b, nq),
        in_specs=[pl.BlockSpec((1, Q_BLOCK, NSA_HEADS * LANES), lambda bi, qi: (bi, qi, 0)),
                  cmp_spec, cmp_spec, _const_spec(pw.shape), _const_spec(tri.shape)],
        out_specs=[pl.BlockSpec((1, Q_BLOCK, NSA_HEADS * HEAD_DIM), lambda bi, qi: (bi, qi, 0)),
                   pl.BlockSpec((1, NSA_GROUPS, Q_BLOCK, nb), lambda bi, qi: (bi, 0, qi, 0)),
                   pl.BlockSpec((1, NSA_GROUPS, 1, 8, LANES), lambda bi, qi: (bi, 0, qi, 0, 0))],
        out_shape=[sds((b, s, NSA_HEADS * HEAD_DIM), F32), sds((b, NSA_GROUPS, s, nb), BF),
                   sds((b, NSA_GROUPS, nq, 8, LANES), jnp.int32)],
        compiler_params=_params(2),
        name="nsa_select",
    )(q, kc, vc, pw, tri)


def _attend_kernel(flag_ref, q_ref, g_ref, oc_ref, sel_ref, ks_ref, vs_ref, *rest, n_words):
    nwin = (WINDOW + ATT_Q) // KEY_TILE
    kw_refs, vw_refs = rest[:nwin], rest[nwin:2 * nwin]
    o_ref, list_ref = rest[2 * nwin], rest[2 * nwin + 1]
    bi, gi, qi = pl.program_id(0), pl.program_id(1), pl.program_id(2)
    q0 = qi * ATT_Q
    diag0 = qi * (ATT_Q // KEY_TILE)
    rq = HEADS_PER_GROUP * ATT_Q
    nb = sel_ref.shape[3]
    tiles_per_word = FLAG_BITS // TILE_BLOCKS
    tile_bits = (1 << TILE_BLOCKS) - 1
    sel_per_step = ATT_Q // Q_BLOCK

    qs = _stack_heads(q_ref, 0)
    selb = sel_ref[0, 0]
    t1 = q0 + lax.broadcasted_iota(jnp.int32, (ATT_Q, 1), 0)
    t4 = q0 + (lax.broadcasted_iota(jnp.int32, (rq, 1), 0) & (ATT_Q - 1))

    base = ((bi * NSA_GROUPS + gi) * pl.num_programs(2) + qi) * sel_per_step * n_words

    def word_body(wi, n):
        word = flag_ref[base + wi]
        for extra in range(1, sel_per_step):
            word = word | flag_ref[base + extra * n_words + wi]

        def scan_word(n):
            def tile_body(u, n):
                kt = wi * tiles_per_word + u
                active = (((word >> (u * TILE_BLOCKS)) & tile_bits) != 0) & (kt < diag0)

                @pl.when(active)
                def _():
                    list_ref[n] = kt
                return n + active.astype(jnp.int32)
            return lax.fori_loop(0, tiles_per_word, tile_body, n)
        return lax.cond(word != 0, scan_word, lambda n: n, n)

    n_active = lax.fori_loop(0, (diag0 + tiles_per_word - 1) // tiles_per_word, word_body, 0)
    for u in range(GROUP_TILES):
        list_ref[n_active + u] = 0

    jrow = lax.broadcasted_iota(jnp.int32, (nb, KEY_TILE), 0)
    kblk = lax.broadcasted_iota(jnp.int32, (nb, KEY_TILE), 1) >> 6
    kcol = lax.broadcasted_iota(jnp.int32, (1, KEY_TILE), 1)

    def masked_scores(k_all, member):
        s_ = _dot_nt(qs, k_all)
        return jnp.concatenate([jnp.where(member, s_[r * ATT_Q:(r + 1) * ATT_Q], NEG)
                                for r in range(HEADS_PER_GROUP)], axis=0)

    def tile_rows(ref, kt, n=KEY_TILE):
        return ref[0, 0, pl.ds(pl.multiple_of(kt * KEY_TILE, KEY_TILE), n), :]

    def group_body(it, carry):
        ks_t, vs_t, ex_t = [], [], []
        for u in range(GROUP_TILES):
            pos = it * GROUP_TILES + u
            kt = list_ref[pos]
            first_blk = jnp.where(pos < n_active, kt * TILE_BLOCKS, -TILE_BLOCKS - nb)
            ks_t.append(tile_rows(ks_ref, kt))
            vs_t.append(tile_rows(vs_ref, kt))
            ex_t.append(jnp.where(jrow == kblk + first_blk, 1.0, 0.0).astype(BF))
        member = _dot(selb, jnp.concatenate(ex_t, axis=1)) > 0.5
        s_ = masked_scores(jnp.concatenate(ks_t, axis=0), member)
        return _softmax_update(carry, s_, jnp.concatenate(vs_t, axis=0))

    init = (jnp.full((rq, 1), NEG, F32), jnp.zeros((rq, LANES), F32))
    carry = lax.fori_loop(0, (n_active + GROUP_TILES - 1) // GROUP_TILES, group_body, init)
    drow = lax.broadcasted_iota(jnp.int32, (nb, ATT_Q), 0)
    dcol = lax.broadcasted_iota(jnp.int32, (1, ATT_Q), 1)
    expand = jnp.where(drow == (dcol >> 6) + diag0 * TILE_BLOCKS, 1.0, 0.0).astype(BF)
    member = (_dot(selb, expand) > 0.5) & (q0 + dcol <= t1)
    kpos = []
    for i in range(nwin):
        kti = diag0 - WINDOW // KEY_TILE + i
        kpos.append(jnp.where(kti >= 0, kti * KEY_TILE, 1 << 24) + kcol)
    d = t4 - jnp.concatenate(kpos, axis=1)
    s_d = masked_scores(tile_rows(ks_ref, diag0, ATT_Q), member)
    s_w = _dot_nt(qs, jnp.concatenate([r[0, 0] for r in kw_refs], axis=0))
    s_w = jnp.where((d >= 0) & (d < WINDOW), s_w, NEG)
    m_d = jnp.maximum(carry[0], jnp.max(s_d, axis=-1, keepdims=True))
    m_w = jnp.max(s_w, axis=-1, keepdims=True)
    p_d = jnp.exp(s_d - m_d).astype(BF)
    p_w = jnp.exp(s_w - m_w).astype(BF)
    acc_d = jnp.exp(carry[0] - m_d) * carry[1] + _dot(p_d, tile_rows(vs_ref, diag0, ATT_Q))
    acc_w = _dot(p_w, jnp.concatenate([r[0, 0] for r in vw_refs], axis=0))
    o_slc = _normalize(acc_d)
    o_win = _normalize(acc_w)

    for r in range(HEADS_PER_GROUP):
        rows = slice(r * ATT_Q, (r + 1) * ATT_Q)
        gate = lambda c: g_ref[0, :, 3 * r + c:3 * r + c + 1]
        o = (gate(0) * oc_ref[0, :, r * HEAD_DIM:(r + 1) * HEAD_DIM]
             + gate(1) * o_slc[rows] + gate(2) * o_win[rows])
        o_ref[0, :, r * HEAD_DIM:(r + 1) * HEAD_DIM] = o.astype(BF)


def _attend(flags, q, gates, o_cmp, sel, ks, vs, kw, vw):
    b, s, _ = q.shape
    nb = sel.shape[3]
    n_words = nb // FLAG_BITS
    nwin = (WINDOW + ATT_Q) // KEY_TILE
    per_q = ATT_Q // KEY_TILE
    gw = HEADS_PER_GROUP * LANES
    flat = flags[:, :, :, 0, :n_words].reshape(-1)
    full = pl.BlockSpec((1, 1, s, LANES), lambda bi, gi, qi, f: (bi, gi, 0, 0))
    win = lambda i: pl.BlockSpec(
        (1, 1, KEY_TILE, LANES),
        lambda bi, gi, qi, f: (bi, gi, jnp.maximum(qi * per_q - WINDOW // KEY_TILE + i, 0), 0))
    grid_spec = pltpu.PrefetchScalarGridSpec(
        num_scalar_prefetch=1,
        grid=(b, NSA_GROUPS, s // ATT_Q),
        in_specs=[pl.BlockSpec((1, ATT_Q, gw), lambda bi, gi, qi, f: (bi, qi, gi)),
                  pl.BlockSpec((1, ATT_Q, LANES), lambda bi, gi, qi, f: (bi, qi, gi)),
                  pl.BlockSpec((1, ATT_Q, HEADS_PER_GROUP * HEAD_DIM), lambda bi, gi, qi, f: (bi, qi, gi)),
                  pl.BlockSpec((1, 1, ATT_Q, nb), lambda bi, gi, qi, f: (bi, gi, qi, 0)),
                  full, full] + [win(i) for i in range(nwin)] * 2,
        out_specs=pl.BlockSpec((1, ATT_Q, HEADS_PER_GROUP * HEAD_DIM), lambda bi, gi, qi, f: (bi, qi, gi)),
        scratch_shapes=[pltpu.SMEM((s // KEY_TILE + GROUP_TILES,), jnp.int32)],
    )
    return pl.pallas_call(
        functools.partial(_attend_kernel, n_words=n_words),
        grid_spec=grid_spec,
        out_shape=jax.ShapeDtypeStruct((b, s, NSA_HEADS * HEAD_DIM), BF),
        compiler_params=_params(3),
        name="nsa_attend",
    )(flat, q, gates, o_cmp, sel, ks, vs, *([kw] * nwin), *([vw] * nwin))


def _nsa_kernel(q_ref, g_ref, kc_ref, vc_ref, ks_ref, vs_ref, *rest):
    nwin = WINDOW // Q_BLOCK + 1
    kw_refs, vw_refs, o_ref = rest[:nwin], rest[nwin:2 * nwin], rest[2 * nwin]
    qi = pl.program_id(2)
    q0 = qi * Q_BLOCK
    rq = HEADS_PER_GROUP * Q_BLOCK
    ncp = kc_ref.shape[2]
    nb = ncp // 4
    nb_shift = nb.bit_length() - 1
    assert nb == 1 << nb_shift

    qs = jnp.concatenate([q_ref[0, :, r * LANES:(r + 1) * LANES] for r in range(HEADS_PER_GROUP)], axis=0)
    t1 = q0 + lax.broadcasted_iota(jnp.int32, (Q_BLOCK, 1), 0)
    t4 = q0 + (lax.broadcasted_iota(jnp.int32, (rq, 1), 0) & (Q_BLOCK - 1))

    s = _dot_nt(qs, kc_ref[0, 0])
    col = lax.broadcasted_iota(jnp.int32, (1, ncp), 1)
    cmp_end = (4 * (col & (nb - 1)) + (col >> nb_shift)) * CMP_STRIDE + (CMP_BLOCK - 1)
    valid = cmp_end <= t4
    s = jnp.where(valid, s, NEG)
    m = jnp.max(s, axis=-1, keepdims=True)
    p = jnp.where(valid, jnp.exp(s - m), 0.0)
    p = p / jnp.maximum(jnp.sum(p, axis=-1, keepdims=True), 1e-30)
    o_cmp = _dot(p.astype(BF), vc_ref[0, 0])[:, :HEAD_DIM]

    ps = p[0:Q_BLOCK]
    for r in range(1, HEADS_PER_GROUP):
        ps = ps + p[r * Q_BLOCK:(r + 1) * Q_BLOCK]
    parts = [ps[:, c * nb:(c + 1) * nb] for c in range(4)]
    jl = lax.broadcasted_iota(jnp.int32, (1, nb), 1)
    prev = jnp.where(jl == 0, 0.0, pltpu.roll(parts[3], 1, axis=1))
    imp = (parts[0] + parts[1] + parts[2] + parts[3]) - 0.5 * parts[3] + 0.5 * prev
    blk_t = t1 >> 6
    forced = (jl == 0) | (jl == blk_t) | (jl == blk_t - 1)
    imp = jnp.where(forced, imp + FORCE_BONUS, imp)
    eligible = jl <= blk_t
    work = jnp.where(eligible, imp, NEG)
    jf = jl.astype(F32)
    sel = jnp.zeros((Q_BLOCK, nb), F32)
    for _ in range(min(SLC_TOP_N, nb)):
        mx = jnp.max(work, axis=-1, keepdims=True)
        first = jnp.min(jnp.where(work == mx, jf, float(nb)), axis=-1, keepdims=True)
        hit = jf == first
        sel = jnp.where(hit, 1.0, sel)
        work = jnp.where(hit, 3.0 * NEG, work)
    selb = jnp.where(eligible, sel, 0.0).astype(BF)

    tk = Q_BLOCK
    jrow = lax.broadcasted_iota(jnp.int32, (nb, tk), 0)
    kblk = lax.broadcasted_iota(jnp.int32, (nb, tk), 1) >> 6
    kcol = lax.broadcasted_iota(jnp.int32, (1, tk), 1)

    def slc_tile(kt, carry, causal):
        k0 = pl.multiple_of(kt * tk, tk)
        kt_ = ks_ref[0, 0, pl.ds(k0, tk), :]
        vt_ = vs_ref[0, 0, pl.ds(k0, tk), :]
        expand = jnp.where(jrow == kblk + kt * (tk // SLC_BLOCK), 1.0, 0.0).astype(BF)
        mk = _dot(selb, expand)
        if causal:
            mk = jnp.where(k0 + kcol <= t1, mk, 0.0)
        mk4 = jnp.concatenate([mk] * HEADS_PER_GROUP, axis=0)
        s_ = jnp.where(mk4 > 0.5, _dot_nt(qs, kt_), NEG)
        return _softmax_update(carry, s_, vt_)

    init = (jnp.full((rq, 1), NEG, F32), jnp.zeros((rq, LANES), F32))
    carry = lax.fori_loop(0, qi, lambda kt, c: slc_tile(kt, c, False), init)
    o_slc = _normalize(slc_tile(qi, carry, True)[1])

    carry = init
    for i in range(nwin):
        kti = qi - (nwin - 1) + i
        k0 = jnp.where(kti >= 0, kti * Q_BLOCK, 1 << 24)
        d = t4 - (k0 + kcol)
        ok = (d >= 0) & (d < WINDOW)
        s_ = jnp.where(ok, _dot_nt(qs, kw_refs[i][0, 0]), NEG)
        carry = _softmax_update(carry, s_, vw_refs[i][0, 0])
    o_win = _normalize(carry[1])

    for r in range(HEADS_PER_GROUP):
        rows = slice(r * Q_BLOCK, (r + 1) * Q_BLOCK)
        gate = lambda c: g_ref[0, :, 3 * r + c:3 * r + c + 1]
        o = gate(0) * o_cmp[rows] + gate(1) * o_slc[rows] + gate(2) * o_win[rows]
        o_ref[0, :, r * HEAD_DIM:(r + 1) * HEAD_DIM] = o.astype(BF)


def _nsa(q, gates, kc, vc, ks, vs, kw, vw):
    b, s, _ = q.shape
    ncp = kc.shape[2]
    nwin = WINDOW // Q_BLOCK + 1
    gw = HEADS_PER_GROUP * LANES
    full = lambda n: pl.BlockSpec((1, 1, n, LANES), lambda bi, gi, qi: (bi, gi, 0, 0))
    win = lambda i: pl.BlockSpec((1, 1, Q_BLOCK, LANES),
                                 lambda bi, gi, qi: (bi, gi, jnp.maximum(qi - (nwin - 1) + i, 0), 0))
    return pl.pallas_call(
        _nsa_kernel,
        grid=(b, NSA_GROUPS, s // Q_BLOCK),
        in_specs=[pl.BlockSpec((1, Q_BLOCK, gw), lambda bi, gi, qi: (bi, qi, gi)),
                  pl.BlockSpec((1, Q_BLOCK, LANES), lambda bi, gi, qi: (bi, qi, gi)),
                  full(ncp), full(ncp), full(s), full(s)]
                 + [win(i) for i in range(nwin)] * 2,
        out_specs=pl.BlockSpec((1, Q_BLOCK, HEADS_PER_GROUP * HEAD_DIM), lambda bi, gi, qi: (bi, qi, gi)),
        out_shape=jax.ShapeDtypeStruct((b, s, NSA_HEADS * HEAD_DIM), BF),
        compiler_params=_params(3),
        name="nsa",
    )(q, gates, kc, vc, ks, vs, *([kw] * nwin), *([vw] * nwin))


def _l0_out_kernel(x_ref, a_ref, c_ref, wa_ref, wc_ref, post_ref, o_ref):
    y = _dot(a_ref[...], wa_ref[...]) + _dot(c_ref[...], wc_ref[...])
    o_ref[...] = x_ref[...] + _rmsnorm(y, post_ref[...])


def _l0_out(x2, o_nsa, o_conv, w_out, post_g, *, tm=512):
    t, d = x2.shape
    na, nc = o_nsa.shape[1], o_conv.shape[1]
    row = lambda w: pl.BlockSpec((tm, w), lambda i: (i, 0))
    return pl.pallas_call(
        _l0_out_kernel,
        grid=(t // tm,),
        in_specs=[row(d), row(na), row(nc), _const_spec((na, d)), _const_spec((nc, d)), _const_spec((1, d))],
        out_specs=row(d),
        out_shape=jax.ShapeDtypeStruct((t, d), F32),
        compiler_params=_params(1),
        name="l0_out",
    )(x2, o_nsa, o_conv, w_out[:na].astype(BF), w_out[na:].astype(BF), post_g.reshape(1, d))


def _head_indicators(d):
    ch = jnp.arange(d)[:, None] // RWKV_HEAD
    ind = (ch == jnp.arange(LANES)[None, :]).astype(BF)
    return ind, ind.T


def _rwkv_pre_kernel(x_ref, pre_ref, mu_ref, wr_ref, wk_ref, wv_ref, w0_ref, wd1_ref, wd2_ref,
                     a0_ref, wa1_ref, wa2_ref, wg1_ref, wg2_ref, kk_ref, ka_ref, ind_ref, indt_ref,
                     r_out, lw_out, k_out, v_out, kk_out, kka_out, g_out, hprev_ref):
    si = pl.program_id(1)
    tm = x_ref.shape[1]

    @pl.when(si == 0)
    def _():
        hprev_ref[...] = jnp.zeros(hprev_ref.shape, F32)

    h = _rmsnorm(x_ref[0], pre_ref[...])
    rows = lax.broadcasted_iota(jnp.int32, (tm, 1), 0)
    xx = jnp.where(rows == 0, hprev_ref[7:8], pltpu.roll(h, 1, axis=0)) - h
    hprev_ref[...] = h[tm - 8:tm]
    mix = lambda i: (h + xx * mu_ref[i:i + 1]).astype(BF)
    xr, xw, xk, xv, xa, xg = [mix(i) for i in range(6)]

    r = _dot(xr, wr_ref[...])
    k = _dot(xk, wk_ref[...])
    v = _dot(xv, wv_ref[...])
    z = w0_ref[...] + _dot(jnp.tanh(_dot(xw, wd1_ref[...])).astype(BF), wd2_ref[...])
    softplus = jnp.maximum(-z, 0.0) + jnp.log(1.0 + jnp.exp(-jnp.abs(z)))
    lw = -jnp.exp(-softplus - 0.5)
    a = _sigmoid(a0_ref[...] + _dot(_dot(xa, wa1_ref[...]).astype(BF), wa2_ref[...]))
    g = _dot(_sigmoid(_dot(xg, wg1_ref[...])).astype(BF), wg2_ref[...])

    kraw = k * kk_ref[...]
    ss = _dot((kraw * kraw).astype(BF), ind_ref[...])
    inv = lax.rsqrt(jnp.maximum(ss, 1e-12))
    kk = kraw * _dot(inv.astype(BF), indt_ref[...])

    r_out[0] = r.astype(BF)
    lw_out[0] = lw
    k_out[0] = (k * (1.0 + (a - 1.0) * ka_ref[...])).astype(BF)
    v_out[0] = v.astype(BF)
    kk_out[0] = kk.astype(BF)
    kka_out[0] = (kk * a).astype(BF)
    g_out[0] = g.astype(BF)


def _rwkv_pre(x, pre_g, mu, w_r, w_k, w_v, w0, w_dec1, w_dec2, a0, w_a1, w_a2, w_g1, w_g2,
              k_k, k_a, ind, indt, *, tm=512):
    b, s, d = x.shape
    row = pl.BlockSpec((1, tm, d), lambda bi, si: (bi, si, 0))
    vec = lambda a: a.reshape(1, d)
    ops = [x, vec(pre_g), mu, w_r.astype(BF), w_k.astype(BF), w_v.astype(BF), vec(w0),
           w_dec1.astype(BF), w_dec2.astype(BF), vec(a0), w_a1.astype(BF), w_a2.astype(BF),
           w_g1.astype(BF), w_g2.astype(BF), vec(k_k), vec(k_a), ind, indt]
    return pl.pallas_call(
        _rwkv_pre_kernel,
        grid=(b, s // tm),
        in_specs=[row] + [_const_spec(o.shape) for o in ops[1:]],
        out_specs=[row] * 7,
        out_shape=[jax.ShapeDtypeStruct((b, s, d), F32 if i == 1 else BF) for i in range(7)],
        scratch_shapes=[pltpu.VMEM((8, d), F32)],
        compiler_params=_params(2),
        name="rwkv_pre",
    )(*ops)


def _rwkv_scan_kernel(r_ref, lw_ref, k_ref, v_ref, kk_ref, kka_ref, y_ref, state_ref):
    c = RWKV_CHUNK
    slab = SLAB_HEADS * RWKV_HEAD
    n_slab = r_ref.shape[2] // slab
    rows = SLAB_HEADS * c
    subs = [slice(i * c, (i + 1) * c) for i in range(r_ref.shape[1] // c)]

    @pl.when(pl.program_id(1) == 0)
    def _():
        state_ref[...] = jnp.zeros(state_ref.shape, F32)

    ri = lax.broadcasted_iota(jnp.int32, (rows, 1), 0)
    ci = lax.broadcasted_iota(jnp.int32, (1, rows), 1)
    li = lax.broadcasted_iota(jnp.int32, (1, slab), 1)
    c_shift = c.bit_length() - 1
    h_shift = RWKV_HEAD.bit_length() - 1
    assert c == 1 << c_shift and RWKV_HEAD == 1 << h_shift
    head_match = (ri >> c_shift) == (li >> h_shift)
    t_r, t_c = ri & (c - 1), ci & (c - 1)
    strict = t_c < t_r
    incl = t_c <= t_r
    eye = jnp.where(ri == ci, 1.0, 0.0)
    tri = jnp.where(lax.broadcasted_iota(jnp.int32, (c, c), 1) <= lax.broadcasted_iota(jnp.int32, (c, c), 0),
                    1.0, 0.0).astype(BF)
    state_match = (lax.broadcasted_iota(jnp.int32, (slab, 1), 0) >> h_shift) == (li >> h_shift)

    incl2 = jnp.concatenate([incl, incl], axis=1)

    def block_diag(x):
        return jnp.where(head_match, jnp.concatenate([x] * SLAB_HEADS, axis=0), 0.0)

    def fold(x):
        out = x[0:c]
        for h in range(1, SLAB_HEADS):
            out = out + x[h * c:(h + 1) * c]
        return out

    bf = lambda x: x.astype(BF)
    slabs = [slice(hs * slab, (hs + 1) * slab) for hs in range(n_slab)]

    def prepare(rs, sl):
        lw = lw_ref[0, rs, sl]
        hi = bf(lw)
        rem = lw - hi.astype(F32)
        mid = bf(rem)
        lo = bf(rem - mid.astype(F32))
        cw = _dot(tri, hi) + _dot(tri, mid) + _dot(tri, lo)
        w_in = jnp.exp(cw)
        w_inv = jnp.exp(-cw)
        w_prev = jnp.exp(cw - lw)
        w_end = w_in[c - 1:c]

        a_t = -kk_ref[0, rs, sl].astype(F32) * w_prev
        b_t = kka_ref[0, rs, sl].astype(F32) * w_inv
        k_t = k_ref[0, rs, sl].astype(F32) * w_inv
        r_t = r_ref[0, rs, sl].astype(F32) * w_in
        return a_t, b_t, k_t, r_t, w_end

    def interactions(a_t, b_t, k_t, r_t):
        lhs = bf(jnp.concatenate([block_diag(a_t), block_diag(r_t)], axis=0))
        rhs = bf(jnp.concatenate([block_diag(b_t), block_diag(k_t)], axis=0))
        big = _dot_nt(lhs, rhs)
        a_ab = jnp.where(strict, big[:rows, :rows], 0.0)
        a_ak = jnp.where(strict, big[:rows, rows:], 0.0)
        a_r = jnp.where(incl2, big[rows:], 0.0)
        return a_ab, bf(a_ak), bf(a_r)

    every = range(n_slab)
    units = [(rs, sl) for rs in subs for sl in slabs]
    prep = [prepare(rs, sl) for rs, sl in units]
    inter = [interactions(*p[:4]) for p in prep]
    vs = [v_ref[0, rs, sl] for rs, sl in units]
    vbd = [bf(block_diag(v)) for v in vs]
    akv = [_dot(inter[u][1], vbd[u]) for u in range(len(units))]

    levels = c.bit_length() - 2
    assert levels >= 1
    invs = [eye + x[0] for x in inter]
    pws = [_dot(bf(x[0]), bf(x[0])) for x in inter]
    for _ in range(levels - 1):
        both = [_dot(bf(pw), bf(jnp.concatenate([pw, inv], axis=1))) for pw, inv in zip(pws, invs)]
        pws = [x[:, :rows] for x in both]
        invs = [inv + x[:, rows:] for inv, x in zip(invs, both)]
    invs = [bf(inv + _dot(bf(pw), bf(inv))) for pw, inv in zip(pws, invs)]

    states = [state_ref[hs] for hs in every]
    for si, rs in enumerate(subs):
        us = [si * n_slab + hs for hs in every]
        ar = [_dot_nt(bf(jnp.concatenate([prep[u][0], prep[u][3]], axis=0)), bf(states[hs]))
              for hs, u in zip(every, us)]
        u_bd = [_dot(invs[u], bf(block_diag(ar[hs][:c]) + akv[u])) for hs, u in zip(every, us)]
        y_bd = [block_diag(ar[hs][c:]) + _dot(inter[u][2], jnp.concatenate([bf(u_bd[hs]), vbd[u]], axis=0))
                for hs, u in zip(every, us)]
        for hs, sl in enumerate(slabs):
            y_ref[0, rs, sl] = fold(y_bd[hs])
        new_states = []
        for hs, u in zip(every, us):
            _, b_t, k_t, _, w_end = prep[u]
            uv = jnp.concatenate([bf(fold(u_bd[hs])), bf(vs[u])], axis=0)
            bk = jnp.concatenate([bf(b_t * w_end), bf(k_t * w_end)], axis=0)
            new_states.append(jnp.where(state_match, states[hs] * w_end + _dot_tn(uv, bk), 0.0))
        states = new_states
    for hs in every:
        state_ref[hs] = states[hs]


def _rwkv_scan(r, lw, k, v, kk, kka):
    b, s, d = r.shape
    c = RWKV_CHUNK * SCAN_SUB
    slab = SLAB_HEADS * RWKV_HEAD
    blk = pl.BlockSpec((1, c, d), lambda bi, ci: (bi, ci, 0))
    return pl.pallas_call(
        _rwkv_scan_kernel,
        grid=(b, s // c),
        in_specs=[blk] * 6,
        out_specs=blk,
        out_shape=jax.ShapeDtypeStruct((b, s, d), F32),
        scratch_shapes=[pltpu.VMEM((d // slab, slab, slab), F32)],
        compiler_params=_params(2),
        name="rwkv_scan",
    )(r, lw, k, v, kk, kka)


def _rwkv_post_kernel(x_ref, y_ref, r_ref, k_ref, v_ref, g_ref, wo_ref, lng_ref, lnb_ref, rk_ref,
                      post_ref, ind_ref, indt_ref, o_ref):
    ind, indt = ind_ref[...], indt_ref[...]
    y = y_ref[...]
    inv_n = 1.0 / RWKV_HEAD
    head_sum = lambda a: _dot(a.astype(BF), ind)
    spread = lambda a: _dot(a.astype(BF), indt)
    mean = _split_dot(head_sum(y) * inv_n, indt)
    yc = y - mean
    var = head_sum(yc * yc) * inv_n
    rstd = spread(lax.rsqrt(var + LNX_EPS))
    yn = yc * rstd * lng_ref[...] + lnb_ref[...]
    f32 = lambda ref: ref[...].astype(F32)
    bonus = spread(head_sum(f32(r_ref) * f32(k_ref) * rk_ref[...])) * f32(v_ref)
    out = _dot(((yn + bonus) * f32(g_ref)).astype(BF), wo_ref[...])
    o_ref[...] = x_ref[...] + _rmsnorm(out, post_ref[...])


def _rwkv_post(x2, y, r, k, v, g, w_o, lnx_g, lnx_b, r_k, post_g, ind, indt, *, tm=256):
    t, d = x2.shape
    row = pl.BlockSpec((tm, d), lambda i: (i, 0))
    vec = lambda a: a.reshape(1, d)
    ops = [x2, y, r, k, v, g, w_o.astype(BF), vec(lnx_g), vec(lnx_b), vec(r_k), vec(post_g), ind, indt]
    return pl.pallas_call(
        _rwkv_post_kernel,
        grid=(t // tm,),
        in_specs=[row] * 6 + [_const_spec(o.shape) for o in ops[6:]],
        out_specs=row,
        out_shape=jax.ShapeDtypeStruct((t, d), F32),
        compiler_params=_params(1),
        name="rwkv_post",
    )(*ops)


def _layer0_mixer(x, pre_g, post_g, w_in, pe_k, w1_k, w2_k, pe_v, w1_v, w2_v, conv_w, conv_b, w_out):
    b, s, d = x.shape
    q, gates, kcr, vcr, ks, vs, kw, vw, o_conv = _l0_in(x, pre_g, _l0_in_weight(w_in), conv_w, conv_b)
    kc = _compress(kcr, _cmp_weights(pe_k, w1_k, w2_k), is_key=True)
    vc = _compress(vcr, _cmp_weights(pe_v, w1_v, w2_v), is_key=False)
    o_cmp, sel, flags = _select(q, kc, vc)
    o_nsa = _attend(flags, q, gates, o_cmp, sel, ks, vs, kw, vw)
    t = b * s
    return _l0_out(x.reshape(t, d), o_nsa.reshape(t, -1), o_conv.reshape(t, -1), w_out, post_g).reshape(b, s, d)


def _layer1_mixer(x, pre_g, post_g, mu, w_r, w_k, w_v, w_o, w0, w_dec1, w_dec2, a0, w_a1, w_a2,
                  w_g1, w_g2, k_k, k_a, r_k, lnx_g, lnx_b):
    b, s, d = x.shape
    ind, indt = _head_indicators(d)
    r, lw, k, v, kk, kka, g = _rwkv_pre(x, pre_g, mu, w_r, w_k, w_v, w0, w_dec1, w_dec2, a0,
                                        w_a1, w_a2, w_g1, w_g2, k_k, k_a, ind, indt)
    y = _rwkv_scan(r, lw, k, v, kk, kka)
    f = lambda a: a.reshape(b * s, d)
    return _rwkv_post(f(x), f(y), f(r), f(k), f(v), f(g), w_o, lnx_g, lnx_b, r_k, post_g,
                      ind, indt).reshape(b, s, d)


def kernel(x, l0_ffn1_pre_g, l0_ffn1_post_g, l0_ffn1_w_gate, l0_ffn1_w_up, l0_ffn1_w_down, l0_mix_pre_g, l0_mix_post_g, l0_w_in, l0_cmp_pe_k, l0_cmp_w1_k, l0_cmp_w2_k, l0_cmp_pe_v, l0_cmp_w1_v, l0_cmp_w2_v, l0_conv_w, l0_conv_b, l0_w_out, l0_ffn2_pre_g, l0_ffn2_post_g, l0_ffn2_w_gate, l0_ffn2_w_up, l0_ffn2_w_down, l1_ffn1_pre_g, l1_ffn1_post_g, l1_ffn1_w_gate, l1_ffn1_w_up, l1_ffn1_w_down, l1_mix_pre_g, l1_mix_post_g, l1_mu, l1_w_r, l1_w_k, l1_w_v, l1_w_o, l1_w0, l1_w_dec1, l1_w_dec2, l1_a0, l1_w_a1, l1_w_a2, l1_w_g1, l1_w_g2, l1_k_k, l1_k_a, l1_r_k, l1_lnx_g, l1_lnx_b, l1_ffn2_pre_g, l1_ffn2_post_g, l1_ffn2_w_gate, l1_ffn2_w_up, l1_ffn2_w_down):
    b, s, d = x.shape
    ffn = lambda a, *w: _ffn(a.reshape(b * s, d), *w).reshape(b, s, d)
    x = ffn(x, l0_ffn1_pre_g, l0_ffn1_post_g, l0_ffn1_w_gate, l0_ffn1_w_up, l0_ffn1_w_down)
    x = _layer0_mixer(x, l0_mix_pre_g, l0_mix_post_g, l0_w_in, l0_cmp_pe_k, l0_cmp_w1_k, l0_cmp_w2_k,
                      l0_cmp_pe_v, l0_cmp_w1_v, l0_cmp_w2_v, l0_conv_w, l0_conv_b, l0_w_out)
    x = ffn(x, l0_ffn2_pre_g, l0_ffn2_post_g, l0_ffn2_w_gate, l0_ffn2_w_up, l0_ffn2_w_down)
    x = ffn(x, l1_ffn1_pre_g, l1_ffn1_post_g, l1_ffn1_w_gate, l1_ffn1_w_up, l1_ffn1_w_down)
    x = _layer1_mixer(x, l1_mix_pre_g, l1_mix_post_g, l1_mu, l1_w_r, l1_w_k, l1_w_v, l1_w_o, l1_w0,
                      l1_w_dec1, l1_w_dec2, l1_a0, l1_w_a1, l1_w_a2, l1_w_g1, l1_w_g2, l1_k_k, l1_k_a,
                      l1_r_k, l1_lnx_g, l1_lnx_b)
    x = ffn(x, l1_ffn2_pre_g, l1_ffn2_post_g, l1_ffn2_w_gate, l1_ffn2_w_up, l1_ffn2_w_down)
    return x
```

```python
import functools

import jax
import jax.numpy as jnp
from jax import lax
from jax.experimental import pallas as pl
from jax.experimental.pallas import tpu as pltpu

BF = jnp.bfloat16
F32 = jnp.float32

EPS = 1e-6
LNX_EPS = 64e-5
HEAD_DIM = 64
NSA_HEADS = 8
NSA_GROUPS = 2
HEADS_PER_GROUP = NSA_HEADS // NSA_GROUPS
CMP_STRIDE = 16
CMP_BLOCK = 32
CMP_HIDDEN = 128
SLC_BLOCK = 64
SLC_TOP_N = 16
WINDOW = 512
Q_BLOCK = 128
FORCE_BONUS = 1e4
CONV_WIDTH = 512
RWKV_HEAD = 64
RWKV_CHUNK = 64
SLAB_HEADS = 2
SCAN_SUB = 2
LANES = 128
MXU_DIM = 256
POS_SHIFT = 7
POS_SPLIT = 1 << POS_SHIFT
NEG = -1e30
VMEM_LIMIT = 56 * 1024 * 1024

NT_DIMS = (((1,), (1,)), ((), ()))
TN_DIMS = (((0,), (0,)), ((), ()))


def _dot(a, b):
    return jnp.dot(a, b, preferred_element_type=F32)


def _dot_nt(a, b):
    return lax.dot_general(a, b, NT_DIMS, preferred_element_type=F32)


def _dot_tn(a, b):
    return lax.dot_general(a, b, TN_DIMS, preferred_element_type=F32)


def _split_dot(x, w):
    hi = x.astype(BF)
    lo = (x - hi.astype(F32)).astype(BF)
    return _dot(hi, w) + _dot(lo, w)


def _rmsnorm(x, g):
    ms = jnp.mean(x * x, axis=-1, keepdims=True)
    return x * lax.rsqrt(ms + EPS) * g


def _sigmoid(x):
    return 1.0 / (1.0 + jnp.exp(-x))


def _round_robin(stage_generators):
    pending = list(stage_generators)
    while pending:
        still = []
        for gen in pending:
            try:
                next(gen)
                still.append(gen)
            except StopIteration:
                pass
        pending = still


def _const_spec(shape):
    zeros = (0,) * len(shape)
    return pl.BlockSpec(shape, lambda *_: zeros)


def _params(n_grid):
    return pltpu.CompilerParams(dimension_semantics=("arbitrary",) * n_grid,
                                vmem_limit_bytes=VMEM_LIMIT)


def _ffn_kernel(x_ref, pre_ref, post_ref, wg_ref, wu_ref, wd_ref, o_ref, *, ff_chunk, n_sub):
    sub = x_ref.shape[0] // n_sub
    dff = wg_ref.shape[1]
    chunks = [(c0, min(c0 + ff_chunk, dff)) for c0 in range(0, dff, ff_chunk)]
    rows = [slice(i * sub, (i + 1) * sub) for i in range(n_sub)]
    xs, hs, accs = [], [], []
    for ci, (c0, c1) in enumerate(chunks):
        gu = []
        for i, rs in enumerate(rows):
            if ci == 0:
                xs.append(x_ref[rs])
                hs.append(_rmsnorm(xs[i], pre_ref[...]).astype(BF))
                accs.append(jnp.zeros(xs[i].shape, F32))
            gu.append((_dot(hs[i], wg_ref[:, c0:c1]), _dot(hs[i], wu_ref[:, c0:c1])))
        for i, (g, u) in enumerate(gu):
            a = (g * _sigmoid(g) * u).astype(BF)
            accs[i] = accs[i] + _dot(a, wd_ref[c0:c1, :])
    for i, rs in enumerate(rows):
        o_ref[rs] = xs[i] + 0.5 * _rmsnorm(accs[i], post_ref[...])


def _ffn(x2, pre_g, post_g, w_gate, w_up, w_down, *, tm=1024, n_sub=2):
    t, d = x2.shape
    dff = w_gate.shape[1]
    ff_chunk = MXU_DIM * pl.cdiv(pl.cdiv(dff, MXU_DIM), 2)
    row = pl.BlockSpec((tm, d), lambda i: (i, 0))
    weight = lambda shape: pl.BlockSpec(shape, lambda i: (0, 0), pipeline_mode=pl.Buffered(1))
    return pl.pallas_call(
        functools.partial(_ffn_kernel, ff_chunk=ff_chunk, n_sub=n_sub),
        grid=(t // tm,),
        in_specs=[row, _const_spec((1, d)), _const_spec((1, d)),
                  weight((d, dff)), weight((d, dff)), weight((dff, d))],
        out_specs=row,
        out_shape=jax.ShapeDtypeStruct((t, d), F32),
        compiler_params=_params(1),
        name="ffn",
    )(x2, pre_g.reshape(1, d), post_g.reshape(1, d),
      w_gate.astype(BF), w_up.astype(BF), w_down.astype(BF))


_C_Q = 0
_C_G = _C_Q + NSA_HEADS * LANES
_C_KC = _C_G + NSA_GROUPS * LANES
_C_VC = _C_KC + LANES
_C_KS = _C_VC + LANES
_C_CONV = _C_KS + 4 * NSA_GROUPS * LANES
_C_END = _C_CONV + 3 * CONV_WIDTH


def _l0_in_weight(w_in):
    d = w_in.shape[0]
    nsa_w = NSA_HEADS * HEAD_DIM
    kv_w = NSA_GROUPS * HEAD_DIM
    o = 0
    q = w_in[:, o:o + nsa_w]; o += nsa_w
    g = w_in[:, o:o + 3 * NSA_HEADS]; o += 3 * NSA_HEADS
    kvs = []
    for _ in range(6):
        kvs.append(w_in[:, o:o + kv_w]); o += kv_w
    conv = w_in[:, o:]
    zpad = lambda w, n: jnp.pad(w, ((0, 0), (0, n - w.shape[1])))
    cols = [zpad(q[:, h * HEAD_DIM:(h + 1) * HEAD_DIM], LANES) for h in range(NSA_HEADS)]
    gpg = 3 * HEADS_PER_GROUP
    cols += [zpad(g[:, i * gpg:(i + 1) * gpg], LANES) for i in range(NSA_GROUPS)]
    cols += [kvs[0], kvs[1]]
    for w in kvs[2:]:
        cols += [zpad(w[:, i * HEAD_DIM:(i + 1) * HEAD_DIM], LANES) for i in range(NSA_GROUPS)]
    cols.append(conv)
    w = jnp.concatenate(cols, axis=1)
    assert w.shape == (d, _C_END)
    return w.astype(BF)


def _l0_in_kernel(x_ref, pre_ref, w_ref, cw_ref, cb_ref,
                  q_ref, g_ref, kcr_ref, vcr_ref, ks_ref, vs_ref, kw_ref, vw_ref, oc_ref,
                  zprev_ref):
    si = pl.program_id(1)
    tm = x_ref.shape[1]
    h = _rmsnorm(x_ref[0], pre_ref[...]).astype(BF)
    lane = lax.broadcasted_iota(jnp.int32, (1, LANES), 1)
    pos = si * tm + lax.broadcasted_iota(jnp.int32, (tm, 1), 0)
    pos_hi = (pos >> POS_SHIFT).astype(F32)
    pos_lo = (pos & (POS_SPLIT - 1)).astype(F32)

    wide = 4 * LANES

    def project(c0):
        res = _dot(h, w_ref[:, c0:c0 + wide])
        return [res[:, j * LANES:(j + 1) * LANES] for j in range(4)]

    for hd0 in range(0, NSA_HEADS, 4):
        for j, qh in enumerate(project(_C_Q + hd0 * LANES)):
            hd = hd0 + j
            slope = 2.0 ** (-(hd + 1))
            qh = qh * (HEAD_DIM ** -0.5)
            qh = jnp.where(lane == HEAD_DIM, POS_SPLIT * slope, jnp.where(lane == HEAD_DIM + 1, slope, qh))
            q_ref[0, :, hd * LANES:(hd + 1) * LANES] = qh.astype(BF)

    assert _C_KC == _C_G + 2 * LANES and _C_VC == _C_KC + LANES and NSA_GROUPS == 2
    g0, g1, kcr, vcr = project(_C_G)
    g_ref[0] = _sigmoid(jnp.concatenate([g0, g1], axis=1))
    kcr_ref[0] = kcr
    vcr_ref[0] = vcr

    slabs = project(_C_KS) + project(_C_KS + wide)
    for n, (ref, is_key) in enumerate(((ks_ref, True), (vs_ref, False), (kw_ref, True), (vw_ref, False))):
        for gi in range(NSA_GROUPS):
            t = slabs[n * NSA_GROUPS + gi]
            if is_key:
                t = jnp.where(lane == HEAD_DIM, pos_hi, jnp.where(lane == HEAD_DIM + 1, pos_lo, t))
            else:
                t = jnp.where(lane == HEAD_DIM, 1.0, t)
            ref[0, gi] = t.astype(BF)

    @pl.when(si == 0)
    def _():
        zprev_ref[...] = jnp.zeros(zprev_ref.shape, F32)

    cb = _dot(h, w_ref[:, _C_CONV:_C_CONV + CONV_WIDTH])
    cc = _dot(h, w_ref[:, _C_CONV + CONV_WIDTH:_C_CONV + 2 * CONV_WIDTH])
    cu = _dot(h, w_ref[:, _C_CONV + 2 * CONV_WIDTH:_C_CONV + 3 * CONV_WIDTH])
    z = cc * cu
    prev = zprev_ref[...]
    rows = lax.broadcasted_iota(jnp.int32, (tm, 1), 0)
    zm1 = jnp.where(rows == 0, prev[7:8], pltpu.roll(z, 1, axis=0))
    zm2 = jnp.where(rows == 0, prev[6:7], jnp.where(rows == 1, prev[7:8], pltpu.roll(z, 2, axis=0)))
    y = cw_ref[0:1] * zm2 + cw_ref[1:2] * zm1 + cw_ref[2:3] * z
    oc_ref[0] = (cb * (y + cb_ref[...])).astype(BF)
    zprev_ref[...] = z[tm - 8:tm]


def _l0_in(x, pre_g, w_all, conv_w, conv_b, *, tm=512):
    b, s, d = x.shape
    row = lambda w: pl.BlockSpec((1, tm, w), lambda bi, si: (bi, si, 0))
    grp = pl.BlockSpec((1, NSA_GROUPS, tm, LANES), lambda bi, si: (bi, 0, si, 0))
    sds = jax.ShapeDtypeStruct
    kv_shape = sds((b, NSA_GROUPS, s, LANES), BF)
    return pl.pallas_call(
        _l0_in_kernel,
        grid=(b, s // tm),
        in_specs=[row(d), _const_spec((1, d)), _const_spec(w_all.shape),
                  _const_spec(conv_w.shape), _const_spec((1, CONV_WIDTH))],
        out_specs=[row(NSA_HEADS * LANES), row(NSA_GROUPS * LANES), row(LANES), row(LANES),
                   grp, grp, grp, grp, row(CONV_WIDTH)],
        out_shape=[sds((b, s, NSA_HEADS * LANES), BF), sds((b, s, NSA_GROUPS * LANES), F32),
                   sds((b, s, LANES), F32), sds((b, s, LANES), F32),
                   kv_shape, kv_shape, kv_shape, kv_shape, sds((b, s, CONV_WIDTH), BF)],
        scratch_shapes=[pltpu.VMEM((8, CONV_WIDTH), F32)],
        compiler_params=_params(2),
        name="l0_in",
    )(x, pre_g.reshape(1, d), w_all, conv_w, conv_b.reshape(1, CONV_WIDTH))


def _cmp_weights(pe, w1, w2):
    half = CMP_STRIDE * HEAD_DIM
    def expand(w):
        w = w.reshape(CMP_STRIDE, 1, HEAD_DIM, 1, CMP_HIDDEN)
        same_group = jnp.eye(NSA_GROUPS, dtype=F32).reshape(1, NSA_GROUPS, 1, NSA_GROUPS, 1)
        return (w * same_group).reshape(CMP_STRIDE * NSA_GROUPS * HEAD_DIM, NSA_GROUPS * CMP_HIDDEN).astype(BF)
    pe_row = lambda p: jnp.tile(p, (1, NSA_GROUPS)).reshape(1, CMP_STRIDE * NSA_GROUPS * HEAD_DIM)
    w2p = jnp.pad(w2, ((0, 0), (0, LANES - HEAD_DIM))).astype(BF)
    return (pe_row(pe[:CMP_STRIDE]), pe_row(pe[CMP_STRIDE:]), expand(w1[:half]), expand(w1[half:]), w2p)


def _gelu_tanh(x):
    return 0.5 * x * (1.0 + jnp.tanh(0.7978845608028654 * (x + 0.044715 * (x * x * x))))


def _cmp_kernel(x_ref, pet_ref, peb_ref, wt_ref, wb_ref, w2_ref, o_ref, *, is_key):
    nb = x_ref.shape[1]
    hw = x_ref.shape[2] // 4
    top, bot = [], []
    for c in range(4):
        xc = x_ref[0, :, c * hw:(c + 1) * hw]
        top.append(_dot((xc + pet_ref[...]).astype(BF), wt_ref[...]))
        bot.append(_dot((xc + peb_ref[...]).astype(BF), wb_ref[...]))
    lane = lax.broadcasted_iota(jnp.int32, (1, LANES), 1)
    j = lax.broadcasted_iota(jnp.int32, (nb, 1), 0)
    for c in range(4):
        nxt = bot[c + 1] if c < 3 else pltpu.roll(bot[0], nb - 1, axis=0)
        hid = _gelu_tanh(top[c] + nxt)
        n = 4 * j + c
        end = n * CMP_STRIDE + (CMP_BLOCK - 1)
        exists = n < 4 * nb - 1
        for gi in range(NSA_GROUPS):
            t = _dot(hid[:, gi * CMP_HIDDEN:(gi + 1) * CMP_HIDDEN].astype(BF), w2_ref[...])
            t = jnp.where(exists, t, 0.0)
            if is_key:
                t = jnp.where(lane == HEAD_DIM, (end >> POS_SHIFT).astype(F32),
                              jnp.where(lane == HEAD_DIM + 1, (end & (POS_SPLIT - 1)).astype(F32), t))
            else:
                t = jnp.where(lane == HEAD_DIM, 1.0, t)
            o_ref[0, gi, c * nb:(c + 1) * nb, :] = t.astype(BF)


def _compress(raw, weights, *, is_key):
    b, s, _ = raw.shape
    nb = s // (4 * CMP_STRIDE)
    x = raw.reshape(b, nb, 4 * CMP_STRIDE * LANES)
    pet, peb, wt, wb, w2p = weights
    return pl.pallas_call(
        functools.partial(_cmp_kernel, is_key=is_key),
        grid=(b,),
        in_specs=[pl.BlockSpec((1, nb, x.shape[2]), lambda bi: (bi, 0, 0)),
                  _const_spec(pet.shape), _const_spec(peb.shape), _const_spec(wt.shape),
                  _const_spec(wb.shape), _const_spec(w2p.shape)],
        out_specs=pl.BlockSpec((1, NSA_GROUPS, 4 * nb, LANES), lambda bi: (bi, 0, 0, 0)),
        out_shape=jax.ShapeDtypeStruct((b, NSA_GROUPS, 4 * nb, LANES), BF),
        compiler_params=_params(1),
        name="compress",
    )(x, pet, peb, wt, wb, w2p)


def _softmax_update(carry, s, v):
    m, acc = carry
    m_new = jnp.maximum(m, jnp.max(s, axis=-1, keepdims=True))
    p = jnp.exp(s - m_new)
    acc = jnp.exp(m - m_new) * acc + _dot(p.astype(BF), v)
    return m_new, acc


def _normalize(acc):
    return acc[:, :HEAD_DIM] / jnp.maximum(acc[:, HEAD_DIM:HEAD_DIM + 1], 1e-30)


N_FORCED = 3
FLAG_BITS = 16
ATT_Q = 256
KEY_TILE = 128
TILE_BLOCKS = KEY_TILE // SLC_BLOCK
GROUP_TILES = 4


def _stack_heads(q_ref, g):
    base = g * HEADS_PER_GROUP
    return jnp.concatenate([q_ref[0, :, (base + r) * LANES:(base + r + 1) * LANES]
                            for r in range(HEADS_PER_GROUP)], axis=0)


def _flag_weights(nb):
    j = jnp.arange(nb)[:, None]
    w = jnp.arange(LANES)[None, :]
    return jnp.where(j // FLAG_BITS == w, 2.0 ** (j % FLAG_BITS), 0.0).astype(BF)


def _prefix_weights(nb):
    return (jnp.arange(nb)[:, None] <= jnp.arange(nb)[None, :]).astype(BF)


def _select_kernel(q_ref, kc_ref, vc_ref, pw_ref, tri_ref, oc_ref, sel_ref, flag_ref):
    qi = pl.program_id(1)
    q0 = qi * Q_BLOCK
    rq = HEADS_PER_GROUP * Q_BLOCK
    nb = kc_ref.shape[2] // 4
    t1 = q0 + lax.broadcasted_iota(jnp.int32, (Q_BLOCK, 1), 0)
    t4 = q0 + (lax.broadcasted_iota(jnp.int32, (rq, 1), 0) & (Q_BLOCK - 1))
    blk_t = t1 >> 6
    groups = range(NSA_GROUPS)

    def run(w):
        w_shift = w.bit_length() - 1
        assert w == 1 << w_shift and w % LANES == 0
        col = lax.broadcasted_iota(jnp.int32, (1, 4 * w), 1)
        cmp_end = (4 * (col & (w - 1)) + (col >> w_shift)) * CMP_STRIDE + (CMP_BLOCK - 1)
        valid = cmp_end <= t4
        jl = lax.broadcasted_iota(jnp.int32, (1, w), 1)
        forced = (jl == 0) | (jl == blk_t) | (jl == blk_t - 1)
        eligible = jl <= blk_t

        def columns(ref, g):
            if w == nb:
                return ref[0, g]
            return jnp.concatenate([ref[0, g, c * nb:c * nb + w] for c in range(4)], axis=0)

        def importance(p):
            ps = p[0:Q_BLOCK]
            for r in range(1, HEADS_PER_GROUP):
                ps = ps + p[r * Q_BLOCK:(r + 1) * Q_BLOCK]
            parts = [ps[:, c * w:(c + 1) * w] for c in range(4)]
            prev = jnp.where(jl == 0, 0.0, pltpu.roll(parts[3], 1, axis=1))
            return (parts[0] + parts[1] + parts[2] + parts[3]) - 0.5 * parts[3] + 0.5 * prev

        s = [jnp.where(valid, _dot_nt(_stack_heads(q_ref, g), columns(kc_ref, g)), NEG) for g in groups]
        m = [jnp.max(s[g], axis=-1, keepdims=True) for g in groups]
        p = [jnp.where(valid, jnp.exp(s[g] - m[g]), 0.0) for g in groups]
        p = [p[g] / jnp.maximum(jnp.sum(p[g], axis=-1, keepdims=True), 1e-30) for g in groups]
        o_cmp = [_dot(p[g].astype(BF), columns(vc_ref, g)) for g in groups]
        pickable = eligible & ~forced
        n_pick = min(SLC_TOP_N, nb) - N_FORCED
        score = [jnp.where(pickable, importance(p[g]), NEG) for g in groups]
        work = score
        for _ in range(n_pick - 1):
            hit = [jl == jnp.argmax(work[g], axis=-1, keepdims=True).astype(jnp.int32) for g in groups]
            work = [jnp.where(hit[g], 3.0 * NEG, work[g]) for g in groups]
        thr = [jnp.max(work[g], axis=-1, keepdims=True) for g in groups]
        above = [score[g] > thr[g] for g in groups]
        tie = [score[g] == thr[g] for g in groups]
        n_above = [jnp.sum(jnp.where(above[g], 1.0, 0.0), axis=-1, keepdims=True) for g in groups]
        tie_rank = [_dot(jnp.where(tie[g], 1.0, 0.0).astype(BF), tri_ref[0:w, 0:w]) for g in groups]
        sel = [forced | above[g] | (tie[g] & (tie_rank[g] <= n_pick - n_above[g])) for g in groups]
        for g in groups:
            for r in range(HEADS_PER_GROUP):
                h0 = (g * HEADS_PER_GROUP + r) * HEAD_DIM
                oc_ref[0, :, h0:h0 + HEAD_DIM] = o_cmp[g][r * Q_BLOCK:(r + 1) * Q_BLOCK, :HEAD_DIM]
            sel_g = jnp.where(eligible & sel[g], 1.0, 0.0)
            sel_ref[0, g, :, 0:w] = sel_g.astype(BF)
            if w < nb:
                sel_ref[0, g, :, w:nb] = jnp.zeros((Q_BLOCK, nb - w), BF)
            any_sel = jnp.broadcast_to(jnp.max(sel_g, axis=0, keepdims=True), (8, w)).astype(BF)
            flag_ref[0, g, 0] = _dot(any_sel, pw_ref[0:w]).astype(jnp.int32)

    half = nb // 2
    if half % LANES == 0:
        in_first_half = q0 + Q_BLOCK <= half * SLC_BLOCK
        pl.when(in_first_half)(lambda: run(half))
        pl.when(jnp.logical_not(in_first_half))(lambda: run(nb))
    else:
        run(nb)


def _select(q, kc, vc):
    b, s, _ = q.shape
    ncp = kc.shape[2]
    nb = ncp // 4
    nq = s // Q_BLOCK
    pw = _flag_weights(nb)
    tri = _prefix_weights(nb)
    sds = jax.ShapeDtypeStruct
    cmp_spec = pl.BlockSpec((1, NSA_GROUPS, ncp, LANES), lambda bi, qi: (bi, 0, 0, 0))
    return pl.pallas_call(
        _select_kernel,
        grid=(b, nq),
        in_specs=[pl.BlockSpec((1, Q_BLOCK, NSA_HEADS * LANES), lambda bi, qi: (bi, qi, 0)),
                  cmp_spec, cmp_spec, _const_spec(pw.shape), _const_spec(tri.shape)],
        out_specs=[pl.BlockSpec((1, Q_BLOCK, NSA_HEADS * HEAD_DIM), lambda bi, qi: (bi, qi, 0)),
                   pl.BlockSpec((1, NSA_GROUPS, Q_BLOCK, nb), lambda bi, qi: (bi, 0, qi, 0)),
                   pl.BlockSpec((1, NSA_GROUPS, 1, 8, LANES), lambda bi, qi: (bi, 0, qi, 0, 0))],
        out_shape=[sds((b, s, NSA_HEADS * HEAD_DIM), F32), sds((b, NSA_GROUPS, s, nb), BF),
                   sds((b, NSA_GROUPS, nq, 8, LANES), jnp.int32)],
        compiler_params=_params(2),
        name="nsa_select",
    )(q, kc, vc, pw, tri)


def _attend_kernel(flag_ref, q_ref, g_ref, oc_ref, sel_ref, ks_ref, vs_ref, *rest, n_words):
    nwin = (WINDOW + ATT_Q) // KEY_TILE
    kw_refs, vw_refs = rest[:nwin], rest[nwin:2 * nwin]
    o_ref, list_ref = rest[2 * nwin], rest[2 * nwin + 1]
    bi, gi, qi = pl.program_id(0), pl.program_id(1), pl.program_id(2)
    q0 = qi * ATT_Q
    diag0 = qi * (ATT_Q // KEY_TILE)
    rq = HEADS_PER_GROUP * ATT_Q
    nb = sel_ref.shape[3]
    tiles_per_word = FLAG_BITS // TILE_BLOCKS
    tile_bits = (1 << TILE_BLOCKS) - 1
    sel_per_step = ATT_Q // Q_BLOCK

    qs = _stack_heads(q_ref, 0)
    selb = sel_ref[0, 0]
    t1 = q0 + lax.broadcasted_iota(jnp.int32, (ATT_Q, 1), 0)
    t4 = q0 + (lax.broadcasted_iota(jnp.int32, (rq, 1), 0) & (ATT_Q - 1))

    base = ((bi * NSA_GROUPS + gi) * pl.num_programs(2) + qi) * sel_per_step * n_words

    def word_body(wi, n):
        word = flag_ref[base + wi]
        for extra in range(1, sel_per_step):
            word = word | flag_ref[base + extra * n_words + wi]

        def scan_word(n):
            def tile_body(u, n):
                kt = wi * tiles_per_word + u
                active = (((word >> (u * TILE_BLOCKS)) & tile_bits) != 0) & (kt < diag0)

                @pl.when(active)
                def _():
                    list_ref[n] = kt
                return n + active.astype(jnp.int32)
            return lax.fori_loop(0, tiles_per_word, tile_body, n)
        return lax.cond(word != 0, scan_word, lambda n: n, n)

    n_active = lax.fori_loop(0, (diag0 + tiles_per_word - 1) // tiles_per_word, word_body, 0)
    for u in range(GROUP_TILES):
        list_ref[n_active + u] = 0

    jrow = lax.broadcasted_iota(jnp.int32, (nb, KEY_TILE), 0)
    kblk = lax.broadcasted_iota(jnp.int32, (nb, KEY_TILE), 1) >> 6
    kcol = lax.broadcasted_iota(jnp.int32, (1, KEY_TILE), 1)

    def masked_scores(k_all, member):
        s_ = _dot_nt(qs, k_all)
        return jnp.concatenate([jnp.where(member, s_[r * ATT_Q:(r + 1) * ATT_Q], NEG)
                                for r in range(HEADS_PER_GROUP)], axis=0)

    def tile_rows(ref, kt, n=KEY_TILE):
        return ref[0, 0, pl.ds(pl.multiple_of(kt * KEY_TILE, KEY_TILE), n), :]

    def group_body(it, carry):
        ks_t, vs_t, ex_t = [], [], []
        for u in range(GROUP_TILES):
            pos = it * GROUP_TILES + u
            kt = list_ref[pos]
            first_blk = jnp.where(pos < n_active, kt * TILE_BLOCKS, -TILE_BLOCKS - nb)
            ks_t.append(tile_rows(ks_ref, kt))
            vs_t.append(tile_rows(vs_ref, kt))
            ex_t.append(jnp.where(jrow == kblk + first_blk, 1.0, 0.0).astype(BF))
        member = _dot(selb, jnp.concatenate(ex_t, axis=1)) > 0.5
        s_ = masked_scores(jnp.concatenate(ks_t, axis=0), member)
        return _softmax_update(carry, s_, jnp.concatenate(vs_t, axis=0))

    init = (jnp.full((rq, 1), NEG, F32), jnp.zeros((rq, LANES), F32))
    carry = lax.fori_loop(0, (n_active + GROUP_TILES - 1) // GROUP_TILES, group_body, init)
    drow = lax.broadcasted_iota(jnp.int32, (nb, ATT_Q), 0)
    dcol = lax.broadcasted_iota(jnp.int32, (1, ATT_Q), 1)
    expand = jnp.where(drow == (dcol >> 6) + diag0 * TILE_BLOCKS, 1.0, 0.0).astype(BF)
    member = (_dot(selb, expand) > 0.5) & (q0 + dcol <= t1)
    kpos = []
    for i in range(nwin):
        kti = diag0 - WINDOW // KEY_TILE + i
        kpos.append(jnp.where(kti >= 0, kti * KEY_TILE, 1 << 24) + kcol)
    d = t4 - jnp.concatenate(kpos, axis=1)
    s_d = masked_scores(tile_rows(ks_ref, diag0, ATT_Q), member)
    s_w = _dot_nt(qs, jnp.concatenate([r[0, 0] for r in kw_refs], axis=0))
    s_w = jnp.where((d >= 0) & (d < WINDOW), s_w, NEG)
    m_d = jnp.maximum(carry[0], jnp.max(s_d, axis=-1, keepdims=True))
    m_w = jnp.max(s_w, axis=-1, keepdims=True)
    p_d = jnp.exp(s_d - m_d).astype(BF)
    p_w = jnp.exp(s_w - m_w).astype(BF)
    acc_d = jnp.exp(carry[0] - m_d) * carry[1] + _dot(p_d, tile_rows(vs_ref, diag0, ATT_Q))
    acc_w = _dot(p_w, jnp.concatenate([r[0, 0] for r in vw_refs], axis=0))
    o_slc = _normalize(acc_d)
    o_win = _normalize(acc_w)

    for r in range(HEADS_PER_GROUP):
        rows = slice(r * ATT_Q, (r + 1) * ATT_Q)
        gate = lambda c: g_ref[0, :, 3 * r + c:3 * r + c + 1]
        o = (gate(0) * oc_ref[0, :, r * HEAD_DIM:(r + 1) * HEAD_DIM]
             + gate(1) * o_slc[rows] + gate(2) * o_win[rows])
        o_ref[0, :, r * HEAD_DIM:(r + 1) * HEAD_DIM] = o.astype(BF)


def _attend(flags, q, gates, o_cmp, sel, ks, vs, kw, vw):
    b, s, _ = q.shape
    nb = sel.shape[3]
    n_words = nb // FLAG_BITS
    nwin = (WINDOW + ATT_Q) // KEY_TILE
    per_q = ATT_Q // KEY_TILE
    gw = HEADS_PER_GROUP * LANES
    flat = flags[:, :, :, 0, :n_words].reshape(-1)
    full = pl.BlockSpec((1, 1, s, LANES), lambda bi, gi, qi, f: (bi, gi, 0, 0))
    win = lambda i: pl.BlockSpec(
        (1, 1, KEY_TILE, LANES),
        lambda bi, gi, qi, f: (bi, gi, jnp.maximum(qi * per_q - WINDOW // KEY_TILE + i, 0), 0))
    grid_spec = pltpu.PrefetchScalarGridSpec(
        num_scalar_prefetch=1,
        grid=(b, NSA_GROUPS, s // ATT_Q),
        in_specs=[pl.BlockSpec((1, ATT_Q, gw), lambda bi, gi, qi, f: (bi, qi, gi)),
                  pl.BlockSpec((1, ATT_Q, LANES), lambda bi, gi, qi, f: (bi, qi, gi)),
                  pl.BlockSpec((1, ATT_Q, HEADS_PER_GROUP * HEAD_DIM), lambda bi, gi, qi, f: (bi, qi, gi)),
                  pl.BlockSpec((1, 1, ATT_Q, nb), lambda bi, gi, qi, f: (bi, gi, qi, 0)),
                  full, full] + [win(i) for i in range(nwin)] * 2,
        out_specs=pl.BlockSpec((1, ATT_Q, HEADS_PER_GROUP * HEAD_DIM), lambda bi, gi, qi, f: (bi, qi, gi)),
        scratch_shapes=[pltpu.SMEM((s // KEY_TILE + GROUP_TILES,), jnp.int32)],
    )
    return pl.pallas_call(
        functools.partial(_attend_kernel, n_words=n_words),
        grid_spec=grid_spec,
        out_shape=jax.ShapeDtypeStruct((b, s, NSA_HEADS * HEAD_DIM), BF),
        compiler_params=_params(3),
        name="nsa_attend",
    )(flat, q, gates, o_cmp, sel, ks, vs, *([kw] * nwin), *([vw] * nwin))


def _nsa_kernel(q_ref, g_ref, kc_ref, vc_ref, ks_ref, vs_ref, *rest):
    nwin = WINDOW // Q_BLOCK + 1
    kw_refs, vw_refs, o_ref = rest[:nwin], rest[nwin:2 * nwin], rest[2 * nwin]
    qi = pl.program_id(2)
    q0 = qi * Q_BLOCK
    rq = HEADS_PER_GROUP * Q_BLOCK
    ncp = kc_ref.shape[2]
    nb = ncp // 4
    nb_shift = nb.bit_length() - 1
    assert nb == 1 << nb_shift

    qs = jnp.concatenate([q_ref[0, :, r * LANES:(r + 1) * LANES] for r in range(HEADS_PER_GROUP)], axis=0)
    t1 = q0 + lax.broadcasted_iota(jnp.int32, (Q_BLOCK, 1), 0)
    t4 = q0 + (lax.broadcasted_iota(jnp.int32, (rq, 1), 0) & (Q_BLOCK - 1))

    s = _dot_nt(qs, kc_ref[0, 0])
    col = lax.broadcasted_iota(jnp.int32, (1, ncp), 1)
    cmp_end = (4 * (col & (nb - 1)) + (col >> nb_shift)) * CMP_STRIDE + (CMP_BLOCK - 1)
    valid = cmp_end <= t4
    s = jnp.where(valid, s, NEG)
    m = jnp.max(s, axis=-1, keepdims=True)
    p = jnp.where(valid, jnp.exp(s - m), 0.0)
    p = p / jnp.maximum(jnp.sum(p, axis=-1, keepdims=True), 1e-30)
    o_cmp = _dot(p.astype(BF), vc_ref[0, 0])[:, :HEAD_DIM]

    ps = p[0:Q_BLOCK]
    for r in range(1, HEADS_PER_GROUP):
        ps = ps + p[r * Q_BLOCK:(r + 1) * Q_BLOCK]
    parts = [ps[:, c * nb:(c + 1) * nb] for c in range(4)]
    jl = lax.broadcasted_iota(jnp.int32, (1, nb), 1)
    prev = jnp.where(jl == 0, 0.0, pltpu.roll(parts[3], 1, axis=1))
    imp = (parts[0] + parts[1] + parts[2] + parts[3]) - 0.5 * parts[3] + 0.5 * prev
    blk_t = t1 >> 6
    forced = (jl == 0) | (jl == blk_t) | (jl == blk_t - 1)
    imp = jnp.where(forced, imp + FORCE_BONUS, imp)
    eligible = jl <= blk_t
    work = jnp.where(eligible, imp, NEG)
    jf = jl.astype(F32)
    sel = jnp.zeros((Q_BLOCK, nb), F32)
    for _ in range(min(SLC_TOP_N, nb)):
        mx = jnp.max(work, axis=-1, keepdims=True)
        first = jnp.min(jnp.where(work == mx, jf, float(nb)), axis=-1, keepdims=True)
        hit = jf == first
        sel = jnp.where(hit, 1.0, sel)
        work = jnp.where(hit, 3.0 * NEG, work)
    selb = jnp.where(eligible, sel, 0.0).astype(BF)

    tk = Q_BLOCK
    jrow = lax.broadcasted_iota(jnp.int32, (nb, tk), 0)
    kblk = lax.broadcasted_iota(jnp.int32, (nb, tk), 1) >> 6
    kcol = lax.broadcasted_iota(jnp.int32, (1, tk), 1)

    def slc_tile(kt, carry, causal):
        k0 = pl.multiple_of(kt * tk, tk)
        kt_ = ks_ref[0, 0, pl.ds(k0, tk), :]
        vt_ = vs_ref[0, 0, pl.ds(k0, tk), :]
        expand = jnp.where(jrow == kblk + kt * (tk // SLC_BLOCK), 1.0, 0.0).astype(BF)
        mk = _dot(selb, expand)
        if causal:
            mk = jnp.where(k0 + kcol <= t1, mk, 0.0)
        mk4 = jnp.concatenate([mk] * HEADS_PER_GROUP, axis=0)
        s_ = jnp.where(mk4 > 0.5, _dot_nt(qs, kt_), NEG)
        return _softmax_update(carry, s_, vt_)

    init = (jnp.full((rq, 1), NEG, F32), jnp.zeros((rq, LANES), F32))
    carry = lax.fori_loop(0, qi, lambda kt, c: slc_tile(kt, c, False), init)
    o_slc = _normalize(slc_tile(qi, carry, True)[1])

    carry = init
    for i in range(nwin):
        kti = qi - (nwin - 1) + i
        k0 = jnp.where(kti >= 0, kti * Q_BLOCK, 1 << 24)
        d = t4 - (k0 + kcol)
        ok = (d >= 0) & (d < WINDOW)
        s_ = jnp.where(ok, _dot_nt(qs, kw_refs[i][0, 0]), NEG)
        carry = _softmax_update(carry, s_, vw_refs[i][0, 0])
    o_win = _normalize(carry[1])

    for r in range(HEADS_PER_GROUP):
        rows = slice(r * Q_BLOCK, (r + 1) * Q_BLOCK)
        gate = lambda c: g_ref[0, :, 3 * r + c:3 * r + c + 1]
        o = gate(0) * o_cmp[rows] + gate(1) * o_slc[rows] + gate(2) * o_win[rows]
        o_ref[0, :, r * HEAD_DIM:(r + 1) * HEAD_DIM] = o.astype(BF)


def _nsa(q, gates, kc, vc, ks, vs, kw, vw):
    b, s, _ = q.shape
    ncp = kc.shape[2]
    nwin = WINDOW // Q_BLOCK + 1
    gw = HEADS_PER_GROUP * LANES
    full = lambda n: pl.BlockSpec((1, 1, n, LANES), lambda bi, gi, qi: (bi, gi, 0, 0))
    win = lambda i: pl.BlockSpec((1, 1, Q_BLOCK, LANES),
                                 lambda bi, gi, qi: (bi, gi, jnp.maximum(qi - (nwin - 1) + i, 0), 0))
    return pl.pallas_call(
        _nsa_kernel,
        grid=(b, NSA_GROUPS, s // Q_BLOCK),
        in_specs=[pl.BlockSpec((1, Q_BLOCK, gw), lambda bi, gi, qi: (bi, qi, gi)),
                  pl.BlockSpec((1, Q_BLOCK, LANES), lambda bi, gi, qi: (bi, qi, gi)),
                  full(ncp), full(ncp), full(s), full(s)]
                 + [win(i) for i in range(nwin)] * 2,
        out_specs=pl.BlockSpec((1, Q_BLOCK, HEADS_PER_GROUP * HEAD_DIM), lambda bi, gi, qi: (bi, qi, gi)),
        out_shape=jax.ShapeDtypeStruct((b, s, NSA_HEADS * HEAD_DIM), BF),
        compiler_params=_params(3),
        name="nsa",
    )(q, gates, kc, vc, ks, vs, *([kw] * nwin), *([vw] * nwin))


def _l0_out_kernel(x_ref, a_ref, c_ref, wa_ref, wc_ref, post_ref, o_ref):
    y = _dot(a_ref[...], wa_ref[...]) + _dot(c_ref[...], wc_ref[...])
    o_ref[...] = x_ref[...] + _rmsnorm(y, post_ref[...])


def _l0_out(x2, o_nsa, o_conv, w_out, post_g, *, tm=512):
    t, d = x2.shape
    na, nc = o_nsa.shape[1], o_conv.shape[1]
    row = lambda w: pl.BlockSpec((tm, w), lambda i: (i, 0))
    return pl.pallas_call(
        _l0_out_kernel,
        grid=(t // tm,),
        in_specs=[row(d), row(na), row(nc), _const_spec((na, d)), _const_spec((nc, d)), _const_spec((1, d))],
        out_specs=row(d),
        out_shape=jax.ShapeDtypeStruct((t, d), F32),
        compiler_params=_params(1),
        name="l0_out",
    )(x2, o_nsa, o_conv, w_out[:na].astype(BF), w_out[na:].astype(BF), post_g.reshape(1, d))


def _head_indicators(d):
    ch = jnp.arange(d)[:, None] // RWKV_HEAD
    ind = (ch == jnp.arange(LANES)[None, :]).astype(BF)
    return ind, ind.T


def _rwkv_pre_kernel(x_ref, pre_ref, mu_ref, wr_ref, wk_ref, wv_ref, w0_ref, wd1_ref, wd2_ref,
                     a0_ref, wa1_ref, wa2_ref, wg1_ref, wg2_ref, kk_ref, ka_ref, ind_ref, indt_ref,
                     r_out, lw_out, k_out, v_out, kk_out, kka_out, g_out, hprev_ref, *, n_sub):
    si = pl.program_id(1)
    tm = x_ref.shape[1]

    @pl.when(si == 0)
    def _():
        hprev_ref[...] = jnp.zeros(hprev_ref.shape, F32)

    h_all = _rmsnorm(x_ref[0], pre_ref[...])
    rows = lax.broadcasted_iota(jnp.int32, (tm, 1), 0)
    xx_all = jnp.where(rows == 0, hprev_ref[7:8], pltpu.roll(h_all, 1, axis=0)) - h_all
    hprev_ref[...] = h_all[tm - 8:tm]

    def sub_tile(rs):
        h, xx = h_all[rs], xx_all[rs]
        mix = lambda i: (h + xx * mu_ref[i:i + 1]).astype(BF)
        xr, xw, xk, xv, xa, xg = [mix(i) for i in range(6)]
        r = _dot(xr, wr_ref[...])
        k = _dot(xk, wk_ref[...])
        v = _dot(xv, wv_ref[...])
        yield
        z = w0_ref[...] + _dot(jnp.tanh(_dot(xw, wd1_ref[...])).astype(BF), wd2_ref[...])
        softplus = jnp.maximum(-z, 0.0) + jnp.log(1.0 + jnp.exp(-jnp.abs(z)))
        lw = -jnp.exp(-softplus - 0.5)
        a = _sigmoid(a0_ref[...] + _dot(_dot(xa, wa1_ref[...]).astype(BF), wa2_ref[...]))
        g = _dot(_sigmoid(_dot(xg, wg1_ref[...])).astype(BF), wg2_ref[...])
        yield
        kraw = k * kk_ref[...]
        ss = _dot((kraw * kraw).astype(BF), ind_ref[...])
        yield
        inv = lax.rsqrt(jnp.maximum(ss, 1e-12))
        kk = kraw * _dot(inv.astype(BF), indt_ref[...])
        yield
        r_out[0, rs] = r.astype(BF)
        lw_out[0, rs] = lw
        k_out[0, rs] = (k * (1.0 + (a - 1.0) * ka_ref[...])).astype(BF)
        v_out[0, rs] = v.astype(BF)
        kk_out[0, rs] = kk.astype(BF)
        kka_out[0, rs] = (kk * a).astype(BF)
        g_out[0, rs] = g.astype(BF)

    _round_robin(sub_tile(slice(i * tm // n_sub, (i + 1) * tm // n_sub)) for i in range(n_sub))


def _rwkv_pre(x, pre_g, mu, w_r, w_k, w_v, w0, w_dec1, w_dec2, a0, w_a1, w_a2, w_g1, w_g2,
              k_k, k_a, ind, indt, *, tm=512, n_sub=2):
    b, s, d = x.shape
    row = pl.BlockSpec((1, tm, d), lambda bi, si: (bi, si, 0))
    vec = lambda a: a.reshape(1, d)
    ops = [x, vec(pre_g), mu, w_r.astype(BF), w_k.astype(BF), w_v.astype(BF), vec(w0),
           w_dec1.astype(BF), w_dec2.astype(BF), vec(a0), w_a1.astype(BF), w_a2.astype(BF),
           w_g1.astype(BF), w_g2.astype(BF), vec(k_k), vec(k_a), ind, indt]
    return pl.pallas_call(
        functools.partial(_rwkv_pre_kernel, n_sub=n_sub),
        grid=(b, s // tm),
        in_specs=[row] + [_const_spec(o.shape) for o in ops[1:]],
        out_specs=[row] * 7,
        out_shape=[jax.ShapeDtypeStruct((b, s, d), F32 if i == 1 else BF) for i in range(7)],
        scratch_shapes=[pltpu.VMEM((8, d), F32)],
        compiler_params=_params(2),
        name="rwkv_pre",
    )(*ops)


def _rwkv_scan_kernel(r_ref, lw_ref, k_ref, v_ref, kk_ref, kka_ref, y_ref, state_ref):
    c = RWKV_CHUNK
    slab = SLAB_HEADS * RWKV_HEAD
    n_slab = r_ref.shape[2] // slab
    rows = SLAB_HEADS * c
    subs = [slice(i * c, (i + 1) * c) for i in range(r_ref.shape[1] // c)]

    @pl.when(pl.program_id(1) == 0)
    def _():
        state_ref[...] = jnp.zeros(state_ref.shape, F32)

    ri = lax.broadcasted_iota(jnp.int32, (rows, 1), 0)
    ci = lax.broadcasted_iota(jnp.int32, (1, rows), 1)
    li = lax.broadcasted_iota(jnp.int32, (1, slab), 1)
    c_shift = c.bit_length() - 1
    h_shift = RWKV_HEAD.bit_length() - 1
    assert c == 1 << c_shift and RWKV_HEAD == 1 << h_shift
    head_match = (ri >> c_shift) == (li >> h_shift)
    t_r, t_c = ri & (c - 1), ci & (c - 1)
    strict = t_c < t_r
    incl = t_c <= t_r
    eye = jnp.where(ri == ci, 1.0, 0.0)
    tri = jnp.where(lax.broadcasted_iota(jnp.int32, (c, c), 1) <= lax.broadcasted_iota(jnp.int32, (c, c), 0),
                    1.0, 0.0).astype(BF)
    state_match = (lax.broadcasted_iota(jnp.int32, (slab, 1), 0) >> h_shift) == (li >> h_shift)

    incl2 = jnp.concatenate([incl, incl], axis=1)

    def block_diag(x):
        return jnp.where(head_match, jnp.concatenate([x] * SLAB_HEADS, axis=0), 0.0)

    def fold(x):
        out = x[0:c]
        for h in range(1, SLAB_HEADS):
            out = out + x[h * c:(h + 1) * c]
        return out

    bf = lambda x: x.astype(BF)
    slabs = [slice(hs * slab, (hs + 1) * slab) for hs in range(n_slab)]

    def prepare(rs, sl):
        lw = lw_ref[0, rs, sl]
        hi = bf(lw)
        rem = lw - hi.astype(F32)
        mid = bf(rem)
        lo = bf(rem - mid.astype(F32))
        cw = _dot(tri, hi) + _dot(tri, mid) + _dot(tri, lo)
        w_in = jnp.exp(cw)
        w_inv = jnp.exp(-cw)
        w_prev = jnp.exp(cw - lw)
        w_end = w_in[c - 1:c]

        a_t = -kk_ref[0, rs, sl].astype(F32) * w_prev
        b_t = kka_ref[0, rs, sl].astype(F32) * w_inv
        k_t = k_ref[0, rs, sl].astype(F32) * w_inv
        r_t = r_ref[0, rs, sl].astype(F32) * w_in
        return a_t, b_t, k_t, r_t, w_end

    def interactions(a_t, b_t, k_t, r_t):
        lhs = bf(jnp.concatenate([block_diag(a_t), block_diag(r_t)], axis=0))
        rhs = bf(jnp.concatenate([block_diag(b_t), block_diag(k_t)], axis=0))
        big = _dot_nt(lhs, rhs)
        a_ab = jnp.where(strict, big[:rows, :rows], 0.0)
        a_ak = jnp.where(strict, big[:rows, rows:], 0.0)
        a_r = jnp.where(incl2, big[rows:], 0.0)
        return a_ab, bf(a_ak), bf(a_r)

    every = range(n_slab)
    units = [(rs, sl) for rs in subs for sl in slabs]
    prep = [prepare(rs, sl) for rs, sl in units]
    inter = [interactions(*p[:4]) for p in prep]
    vs = [v_ref[0, rs, sl] for rs, sl in units]
    vbd = [bf(block_diag(v)) for v in vs]
    akv = [_dot(inter[u][1], vbd[u]) for u in range(len(units))]

    levels = c.bit_length() - 2
    assert levels >= 1
    invs = [eye + x[0] for x in inter]
    pws = [_dot(bf(x[0]), bf(x[0])) for x in inter]
    for _ in range(levels - 1):
        both = [_dot(bf(pw), bf(jnp.concatenate([pw, inv], axis=1))) for pw, inv in zip(pws, invs)]
        pws = [x[:, :rows] for x in both]
        invs = [inv + x[:, rows:] for inv, x in zip(invs, both)]
    invs = [bf(inv + _dot(bf(pw), bf(inv))) for pw, inv in zip(pws, invs)]

    states = [state_ref[hs] for hs in every]
    for si, rs in enumerate(subs):
        us = [si * n_slab + hs for hs in every]
        ar = [_dot_nt(bf(jnp.concatenate([prep[u][0], prep[u][3]], axis=0)), bf(states[hs]))
              for hs, u in zip(every, us)]
        u_bd = [_dot(invs[u], bf(block_diag(ar[hs][:c]) + akv[u])) for hs, u in zip(every, us)]
        y_bd = [block_diag(ar[hs][c:]) + _dot(inter[u][2], jnp.concatenate([bf(u_bd[hs]), vbd[u]], axis=0))
                for hs, u in zip(every, us)]
        for hs, sl in enumerate(slabs):
            y_ref[0, rs, sl] = fold(y_bd[hs])
        new_states = []
        for hs, u in zip(every, us):
            _, b_t, k_t, _, w_end = prep[u]
            uv = jnp.concatenate([bf(fold(u_bd[hs])), bf(vs[u])], axis=0)
            bk = jnp.concatenate([bf(b_t * w_end), bf(k_t * w_end)], axis=0)
            new_states.append(jnp.where(state_match, states[hs] * w_end + _dot_tn(uv, bk), 0.0))
        states = new_states
    for hs in every:
        state_ref[hs] = states[hs]


def _rwkv_scan(r, lw, k, v, kk, kka):
    b, s, d = r.shape
    c = RWKV_CHUNK * SCAN_SUB
    slab = SLAB_HEADS * RWKV_HEAD
    blk = pl.BlockSpec((1, c, d), lambda bi, ci: (bi, ci, 0))
    return pl.pallas_call(
        _rwkv_scan_kernel,
        grid=(b, s // c),
        in_specs=[blk] * 6,
        out_specs=blk,
        out_shape=jax.ShapeDtypeStruct((b, s, d), F32),
        scratch_shapes=[pltpu.VMEM((d // slab, slab, slab), F32)],
        compiler_params=_params(2),
        name="rwkv_scan",
    )(r, lw, k, v, kk, kka)


def _rwkv_post_kernel(x_ref, y_ref, r_ref, k_ref, v_ref, g_ref, wo_ref, lng_ref, lnb_ref, rk_ref,
                      post_ref, ind_ref, indt_ref, o_ref, *, n_sub):
    ind, indt = ind_ref[...], indt_ref[...]
    inv_n = 1.0 / RWKV_HEAD
    head_sum = lambda a: _dot(a.astype(BF), ind)
    spread = lambda a: _dot(a.astype(BF), indt)
    tm = x_ref.shape[0]

    def sub_tile(rs):
        f32 = lambda ref: ref[rs].astype(F32)
        y = y_ref[rs]
        sums = head_sum(y)
        rk_sum = head_sum(f32(r_ref) * f32(k_ref) * rk_ref[...])
        yield
        mean = _split_dot(sums * inv_n, indt)
        bonus = spread(rk_sum) * f32(v_ref)
        yield
        yc = y - mean
        var = head_sum(yc * yc) * inv_n
        yield
        rstd = spread(lax.rsqrt(var + LNX_EPS))
        yield
        yn = yc * rstd * lng_ref[...] + lnb_ref[...]
        out = _dot(((yn + bonus) * f32(g_ref)).astype(BF), wo_ref[...])
        yield
        o_ref[rs] = x_ref[rs] + _rmsnorm(out, post_ref[...])

    _round_robin(sub_tile(slice(i * tm // n_sub, (i + 1) * tm // n_sub)) for i in range(n_sub))


def _rwkv_post(x2, y, r, k, v, g, w_o, lnx_g, lnx_b, r_k, post_g, ind, indt, *, tm=512, n_sub=2):
    t, d = x2.shape
    row = pl.BlockSpec((tm, d), lambda i: (i, 0))
    vec = lambda a: a.reshape(1, d)
    ops = [x2, y, r, k, v, g, w_o.astype(BF), vec(lnx_g), vec(lnx_b), vec(r_k), vec(post_g), ind, indt]
    return pl.pallas_call(
        functools.partial(_rwkv_post_kernel, n_sub=n_sub),
        grid=(t // tm,),
        in_specs=[row] * 6 + [_const_spec(o.shape) for o in ops[6:]],
        out_specs=row,
        out_shape=jax.ShapeDtypeStruct((t, d), F32),
        compiler_params=_params(1),
        name="rwkv_post",
    )(*ops)


def _layer0_mixer(x, pre_g, post_g, w_in, pe_k, w1_k, w2_k, pe_v, w1_v, w2_v, conv_w, conv_b, w_out):
    b, s, d = x.shape
    q, gates, kcr, vcr, ks, vs, kw, vw, o_conv = _l0_in(x, pre_g, _l0_in_weight(w_in), conv_w, conv_b)
    kc = _compress(kcr, _cmp_weights(pe_k, w1_k, w2_k), is_key=True)
    vc = _compress(vcr, _cmp_weights(pe_v, w1_v, w2_v), is_key=False)
    o_cmp, sel, flags = _select(q, kc, vc)
    o_nsa = _attend(flags, q, gates, o_cmp, sel, ks, vs, kw, vw)
    t = b * s
    return _l0_out(x.reshape(t, d), o_nsa.reshape(t, -1), o_conv.reshape(t, -1), w_out, post_g).reshape(b, s, d)


def _layer1_mixer(x, pre_g, post_g, mu, w_r, w_k, w_v, w_o, w0, w_dec1, w_dec2, a0, w_a1, w_a2,
                  w_g1, w_g2, k_k, k_a, r_k, lnx_g, lnx_b):
    b, s, d = x.shape
    ind, indt = _head_indicators(d)
    r, lw, k, v, kk, kka, g = _rwkv_pre(x, pre_g, mu, w_r, w_k, w_v, w0, w_dec1, w_dec2, a0,
                                        w_a1, w_a2, w_g1, w_g2, k_k, k_a, ind, indt)
    y = _rwkv_scan(r, lw, k, v, kk, kka)
    f = lambda a: a.reshape(b * s, d)
    return _rwkv_post(f(x), f(y), f(r), f(k), f(v), f(g), w_o, lnx_g, lnx_b, r_k, post_g,
                      ind, indt).reshape(b, s, d)


def kernel(x, l0_ffn1_pre_g, l0_ffn1_post_g, l0_ffn1_w_gate, l0_ffn1_w_up, l0_ffn1_w_down, l0_mix_pre_g, l0_mix_post_g, l0_w_in, l0_cmp_pe_k, l0_cmp_w1_k, l0_cmp_w2_k, l0_cmp_pe_v, l0_cmp_w1_v, l0_cmp_w2_v, l0_conv_w, l0_conv_b, l0_w_out, l0_ffn2_pre_g, l0_ffn2_post_g, l0_ffn2_w_gate, l0_ffn2_w_up, l0_ffn2_w_down, l1_ffn1_pre_g, l1_ffn1_post_g, l1_ffn1_w_gate, l1_ffn1_w_up, l1_ffn1_w_down, l1_mix_pre_g, l1_mix_post_g, l1_mu, l1_w_r, l1_w_k, l1_w_v, l1_w_o, l1_w0, l1_w_dec1, l1_w_dec2, l1_a0, l1_w_a1, l1_w_a2, l1_w_g1, l1_w_g2, l1_k_k, l1_k_a, l1_r_k, l1_lnx_g, l1_lnx_b, l1_ffn2_pre_g, l1_ffn2_post_g, l1_ffn2_w_gate, l1_ffn2_w_up, l1_ffn2_w_down):
    b, s, d = x.shape
    ffn = lambda a, *w: _ffn(a.reshape(b * s, d), *w).reshape(b, s, d)
    x = ffn(x, l0_ffn1_pre_g, l0_ffn1_post_g, l0_ffn1_w_gate, l0_ffn1_w_up, l0_ffn1_w_down)
    x = _layer0_mixer(x, l0_mix_pre_g, l0_mix_post_g, l0_w_in, l0_cmp_pe_k, l0_cmp_w1_k, l0_cmp_w2_k,
                      l0_cmp_pe_v, l0_cmp_w1_v, l0_cmp_w2_v, l0_conv_w, l0_conv_b, l0_w_out)
    x = ffn(x, l0_ffn2_pre_g, l0_ffn2_post_g, l0_ffn2_w_gate, l0_ffn2_w_up, l0_ffn2_w_down)
    x = ffn(x, l1_ffn1_pre_g, l1_ffn1_post_g, l1_ffn1_w_gate, l1_ffn1_w_up, l1_ffn1_w_down)
    x = _layer1_mixer(x, l1_mix_pre_g, l1_mix_post_g, l1_mu, l1_w_r, l1_w_k, l1_w_v, l1_w_o, l1_w0,
                      l1_w_dec1, l1_w_dec2, l1_a0, l1_w_a1, l1_w_a2, l1_w_g1, l1_w_g2, l1_k_k, l1_k_a,
                      l1_r_k, l1_lnx_g, l1_lnx_b)
    x = ffn(x, l1_ffn2_pre_g, l1_ffn2_post_g, l1_ffn2_w_gate, l1_ffn2_w_up, l1_ffn2_w_down)
    return x
```

```python
import functools

import jax
import jax.numpy as jnp
from jax import lax
from jax.experimental import pallas as pl
from jax.experimental.pallas import tpu as pltpu

BF = jnp.bfloat16
F32 = jnp.float32

EPS = 1e-6
LNX_EPS = 64e-5
HEAD_DIM = 64
NSA_HEADS = 8
NSA_GROUPS = 2
HEADS_PER_GROUP = NSA_HEADS // NSA_GROUPS
CMP_STRIDE = 16
CMP_BLOCK = 32
CMP_HIDDEN = 128
SLC_BLOCK = 64
SLC_TOP_N = 16
WINDOW = 512
Q_BLOCK = 128
CONV_WIDTH = 512
RWKV_HEAD = 64
RWKV_CHUNK = 64
SLAB_HEADS = 2
SCAN_SUB = 2
LANES = 128
MXU_DIM = 256
POS_SHIFT = 7
POS_SPLIT = 1 << POS_SHIFT
NEG = -1e30
VMEM_LIMIT = 56 * 1024 * 1024

NT_DIMS = (((1,), (1,)), ((), ()))
TN_DIMS = (((0,), (0,)), ((), ()))


def _dot(a, b):
    return jnp.dot(a, b, preferred_element_type=F32)


def _dot_nt(a, b):
    return lax.dot_general(a, b, NT_DIMS, preferred_element_type=F32)


def _dot_tn(a, b):
    return lax.dot_general(a, b, TN_DIMS, preferred_element_type=F32)


def _split_dot(x, w):
    hi = x.astype(BF)
    lo = (x - hi.astype(F32)).astype(BF)
    return _dot(hi, w) + _dot(lo, w)


def _rmsnorm(x, g):
    ms = jnp.mean(x * x, axis=-1, keepdims=True)
    return x * lax.rsqrt(ms + EPS) * g


def _sigmoid(x):
    return 1.0 / (1.0 + jnp.exp(-x))


def _round_robin(stage_generators):
    pending = list(stage_generators)
    while pending:
        still = []
        for gen in pending:
            try:
                next(gen)
                still.append(gen)
            except StopIteration:
                pass
        pending = still


def _const_spec(shape):
    zeros = (0,) * len(shape)
    return pl.BlockSpec(shape, lambda *_: zeros)


def _params(n_grid):
    return pltpu.CompilerParams(dimension_semantics=("arbitrary",) * n_grid,
                                vmem_limit_bytes=VMEM_LIMIT)


def _ffn_kernel(x_ref, pre_ref, post_ref, wg_ref, wu_ref, wd_ref, o_ref, *, ff_chunk, n_sub):
    sub = x_ref.shape[0] // n_sub
    dff = wg_ref.shape[1]
    chunks = [(c0, min(c0 + ff_chunk, dff)) for c0 in range(0, dff, ff_chunk)]
    rows = [slice(i * sub, (i + 1) * sub) for i in range(n_sub)]
    xs, hs, accs = [], [], []
    for ci, (c0, c1) in enumerate(chunks):
        gu = []
        for i, rs in enumerate(rows):
            if ci == 0:
                xs.append(x_ref[rs])
                hs.append(_rmsnorm(xs[i], pre_ref[...]).astype(BF))
                accs.append(jnp.zeros(xs[i].shape, F32))
            gu.append((_dot(hs[i], wg_ref[:, c0:c1]), _dot(hs[i], wu_ref[:, c0:c1])))
        for i, (g, u) in enumerate(gu):
            a = (g * _sigmoid(g) * u).astype(BF)
            accs[i] = accs[i] + _dot(a, wd_ref[c0:c1, :])
    for i, rs in enumerate(rows):
        o_ref[rs] = xs[i] + 0.5 * _rmsnorm(accs[i], post_ref[...])


def _ffn(x2, pre_g, post_g, w_gate, w_up, w_down, *, tm=1024, n_sub=2):
    t, d = x2.shape
    dff = w_gate.shape[1]
    ff_chunk = MXU_DIM * pl.cdiv(pl.cdiv(dff, MXU_DIM), 2)
    row = pl.BlockSpec((tm, d), lambda i: (i, 0))
    weight = lambda shape: pl.BlockSpec(shape, lambda i: (0, 0), pipeline_mode=pl.Buffered(1))
    return pl.pallas_call(
        functools.partial(_ffn_kernel, ff_chunk=ff_chunk, n_sub=n_sub),
        grid=(t // tm,),
        in_specs=[row, _const_spec((1, d)), _const_spec((1, d)),
                  weight((d, dff)), weight((d, dff)), weight((dff, d))],
        out_specs=row,
        out_shape=jax.ShapeDtypeStruct((t, d), F32),
        compiler_params=_params(1),
        name="ffn",
    )(x2, pre_g.reshape(1, d), post_g.reshape(1, d),
      w_gate.astype(BF), w_up.astype(BF), w_down.astype(BF))


_C_Q = 0
_C_G = _C_Q + NSA_HEADS * LANES
_C_KC = _C_G + NSA_GROUPS * LANES
_C_VC = _C_KC + LANES
_C_KS = _C_VC + LANES
_C_CONV = _C_KS + 4 * NSA_GROUPS * LANES
_C_END = _C_CONV + 3 * CONV_WIDTH


def _l0_in_weight(w_in):
    d = w_in.shape[0]
    nsa_w = NSA_HEADS * HEAD_DIM
    kv_w = NSA_GROUPS * HEAD_DIM
    o = 0
    q = w_in[:, o:o + nsa_w]; o += nsa_w
    g = w_in[:, o:o + 3 * NSA_HEADS]; o += 3 * NSA_HEADS
    kvs = []
    for _ in range(6):
        kvs.append(w_in[:, o:o + kv_w]); o += kv_w
    conv = w_in[:, o:]
    zpad = lambda w, n: jnp.pad(w, ((0, 0), (0, n - w.shape[1])))
    cols = [zpad(q[:, h * HEAD_DIM:(h + 1) * HEAD_DIM], LANES) for h in range(NSA_HEADS)]
    gpg = 3 * HEADS_PER_GROUP
    cols += [zpad(g[:, i * gpg:(i + 1) * gpg], LANES) for i in range(NSA_GROUPS)]
    cols += [kvs[0], kvs[1]]
    for w in kvs[2:]:
        cols += [zpad(w[:, i * HEAD_DIM:(i + 1) * HEAD_DIM], LANES) for i in range(NSA_GROUPS)]
    cols.append(conv)
    w = jnp.concatenate(cols, axis=1)
    assert w.shape == (d, _C_END)
    return w.astype(BF)


def _l0_in_kernel(x_ref, pre_ref, w_ref, cw_ref, cb_ref,
                  q_ref, g_ref, kcr_ref, vcr_ref, ks_ref, vs_ref, kw_ref, vw_ref, oc_ref,
                  zprev_ref):
    si = pl.program_id(1)
    tm = x_ref.shape[1]
    h = _rmsnorm(x_ref[0], pre_ref[...]).astype(BF)
    lane = lax.broadcasted_iota(jnp.int32, (1, LANES), 1)
    pos = si * tm + lax.broadcasted_iota(jnp.int32, (tm, 1), 0)
    pos_hi = (pos >> POS_SHIFT).astype(F32)
    pos_lo = (pos & (POS_SPLIT - 1)).astype(F32)

    wide = 4 * LANES

    def project(c0):
        res = _dot(h, w_ref[:, c0:c0 + wide])
        return [res[:, j * LANES:(j + 1) * LANES] for j in range(4)]

    for hd0 in range(0, NSA_HEADS, 4):
        for j, qh in enumerate(project(_C_Q + hd0 * LANES)):
            hd = hd0 + j
            slope = 2.0 ** (-(hd + 1))
            qh = qh * (HEAD_DIM ** -0.5)
            qh = jnp.where(lane == HEAD_DIM, POS_SPLIT * slope, jnp.where(lane == HEAD_DIM + 1, slope, qh))
            q_ref[0, :, hd * LANES:(hd + 1) * LANES] = qh.astype(BF)

    assert _C_KC == _C_G + 2 * LANES and _C_VC == _C_KC + LANES and NSA_GROUPS == 2
    g0, g1, kcr, vcr = project(_C_G)
    g_ref[0] = _sigmoid(jnp.concatenate([g0, g1], axis=1))
    kcr_ref[0] = kcr
    vcr_ref[0] = vcr

    slabs = project(_C_KS) + project(_C_KS + wide)
    for n, (ref, is_key) in enumerate(((ks_ref, True), (vs_ref, False), (kw_ref, True), (vw_ref, False))):
        for gi in range(NSA_GROUPS):
            t = slabs[n * NSA_GROUPS + gi]
            if is_key:
                t = jnp.where(lane == HEAD_DIM, pos_hi, jnp.where(lane == HEAD_DIM + 1, pos_lo, t))
            else:
                t = jnp.where(lane == HEAD_DIM, 1.0, t)
            ref[0, gi] = t.astype(BF)

    @pl.when(si == 0)
    def _():
        zprev_ref[...] = jnp.zeros(zprev_ref.shape, F32)

    cb = _dot(h, w_ref[:, _C_CONV:_C_CONV + CONV_WIDTH])
    cc = _dot(h, w_ref[:, _C_CONV + CONV_WIDTH:_C_CONV + 2 * CONV_WIDTH])
    cu = _dot(h, w_ref[:, _C_CONV + 2 * CONV_WIDTH:_C_CONV + 3 * CONV_WIDTH])
    z = cc * cu
    prev = zprev_ref[...]
    rows = lax.broadcasted_iota(jnp.int32, (tm, 1), 0)
    zm1 = jnp.where(rows == 0, prev[7:8], pltpu.roll(z, 1, axis=0))
    zm2 = jnp.where(rows == 0, prev[6:7], jnp.where(rows == 1, prev[7:8], pltpu.roll(z, 2, axis=0)))
    y = cw_ref[0:1] * zm2 + cw_ref[1:2] * zm1 + cw_ref[2:3] * z
    oc_ref[0] = (cb * (y + cb_ref[...])).astype(BF)
    zprev_ref[...] = z[tm - 8:tm]


def _l0_in(x, pre_g, w_all, conv_w, conv_b, *, tm=512):
    b, s, d = x.shape
    row = lambda w: pl.BlockSpec((1, tm, w), lambda bi, si: (bi, si, 0))
    grp = pl.BlockSpec((1, NSA_GROUPS, tm, LANES), lambda bi, si: (bi, 0, si, 0))
    sds = jax.ShapeDtypeStruct
    kv_shape = sds((b, NSA_GROUPS, s, LANES), BF)
    return pl.pallas_call(
        _l0_in_kernel,
        grid=(b, s // tm),
        in_specs=[row(d), _const_spec((1, d)), _const_spec(w_all.shape),
                  _const_spec(conv_w.shape), _const_spec((1, CONV_WIDTH))],
        out_specs=[row(NSA_HEADS * LANES), row(NSA_GROUPS * LANES), row(LANES), row(LANES),
                   grp, grp, grp, grp, row(CONV_WIDTH)],
        out_shape=[sds((b, s, NSA_HEADS * LANES), BF), sds((b, s, NSA_GROUPS * LANES), F32),
                   sds((b, s, LANES), F32), sds((b, s, LANES), F32),
                   kv_shape, kv_shape, kv_shape, kv_shape, sds((b, s, CONV_WIDTH), BF)],
        scratch_shapes=[pltpu.VMEM((8, CONV_WIDTH), F32)],
        compiler_params=_params(2),
        name="l0_in",
    )(x, pre_g.reshape(1, d), w_all, conv_w, conv_b.reshape(1, CONV_WIDTH))


def _cmp_weights(pe, w1, w2):
    half = CMP_STRIDE * HEAD_DIM
    def expand(w):
        w = w.reshape(CMP_STRIDE, 1, HEAD_DIM, 1, CMP_HIDDEN)
        same_group = jnp.eye(NSA_GROUPS, dtype=F32).reshape(1, NSA_GROUPS, 1, NSA_GROUPS, 1)
        return (w * same_group).reshape(CMP_STRIDE * NSA_GROUPS * HEAD_DIM, NSA_GROUPS * CMP_HIDDEN).astype(BF)
    pe_row = lambda p: jnp.tile(p, (1, NSA_GROUPS)).reshape(1, CMP_STRIDE * NSA_GROUPS * HEAD_DIM)
    w2p = jnp.pad(w2, ((0, 0), (0, LANES - HEAD_DIM))).astype(BF)
    return (pe_row(pe[:CMP_STRIDE]), pe_row(pe[CMP_STRIDE:]), expand(w1[:half]), expand(w1[half:]), w2p)


def _gelu_tanh(x):
    return 0.5 * x * (1.0 + jnp.tanh(0.7978845608028654 * (x + 0.044715 * (x * x * x))))


def _cmp_kernel(x_ref, pet_ref, peb_ref, wt_ref, wb_ref, w2_ref, o_ref, *, is_key):
    nb = x_ref.shape[1]
    hw = x_ref.shape[2] // 4
    top, bot = [], []
    for c in range(4):
        xc = x_ref[0, :, c * hw:(c + 1) * hw]
        top.append(_dot((xc + pet_ref[...]).astype(BF), wt_ref[...]))
        bot.append(_dot((xc + peb_ref[...]).astype(BF), wb_ref[...]))
    lane = lax.broadcasted_iota(jnp.int32, (1, LANES), 1)
    j = lax.broadcasted_iota(jnp.int32, (nb, 1), 0)
    for c in range(4):
        nxt = bot[c + 1] if c < 3 else pltpu.roll(bot[0], nb - 1, axis=0)
        hid = _gelu_tanh(top[c] + nxt)
        n = 4 * j + c
        end = n * CMP_STRIDE + (CMP_BLOCK - 1)
        exists = n < 4 * nb - 1
        for gi in range(NSA_GROUPS):
            t = _dot(hid[:, gi * CMP_HIDDEN:(gi + 1) * CMP_HIDDEN].astype(BF), w2_ref[...])
            t = jnp.where(exists, t, 0.0)
            if is_key:
                t = jnp.where(lane == HEAD_DIM, (end >> POS_SHIFT).astype(F32),
                              jnp.where(lane == HEAD_DIM + 1, (end & (POS_SPLIT - 1)).astype(F32), t))
            else:
                t = jnp.where(lane == HEAD_DIM, 1.0, t)
            o_ref[0, gi, c * nb:(c + 1) * nb, :] = t.astype(BF)


def _compress(raw, weights, *, is_key):
    b, s, _ = raw.shape
    nb = s // (4 * CMP_STRIDE)
    x = raw.reshape(b, nb, 4 * CMP_STRIDE * LANES)
    pet, peb, wt, wb, w2p = weights
    return pl.pallas_call(
        functools.partial(_cmp_kernel, is_key=is_key),
        grid=(b,),
        in_specs=[pl.BlockSpec((1, nb, x.shape[2]), lambda bi: (bi, 0, 0)),
                  _const_spec(pet.shape), _const_spec(peb.shape), _const_spec(wt.shape),
                  _const_spec(wb.shape), _const_spec(w2p.shape)],
        out_specs=pl.BlockSpec((1, NSA_GROUPS, 4 * nb, LANES), lambda bi: (bi, 0, 0, 0)),
        out_shape=jax.ShapeDtypeStruct((b, NSA_GROUPS, 4 * nb, LANES), BF),
        compiler_params=_params(1),
        name="compress",
    )(x, pet, peb, wt, wb, w2p)


def _softmax_update(carry, s, v):
    m, acc = carry
    m_new = jnp.maximum(m, jnp.max(s, axis=-1, keepdims=True))
    p = jnp.exp(s - m_new)
    acc = jnp.exp(m - m_new) * acc + _dot(p.astype(BF), v)
    return m_new, acc


def _normalize(acc):
    return acc[:, :HEAD_DIM] / jnp.maximum(acc[:, HEAD_DIM:HEAD_DIM + 1], 1e-30)


N_FORCED = 3
FLAG_BITS = 16
ATT_Q = 256
KEY_TILE = 128
TILE_BLOCKS = KEY_TILE // SLC_BLOCK
GROUP_TILES = 4


def _stack_heads(q_ref, g):
    base = g * HEADS_PER_GROUP
    return jnp.concatenate([q_ref[0, :, (base + r) * LANES:(base + r + 1) * LANES]
                            for r in range(HEADS_PER_GROUP)], axis=0)


def _flag_weights(nb):
    j = jnp.arange(nb)[:, None]
    w = jnp.arange(LANES)[None, :]
    return jnp.where(j // FLAG_BITS == w, 2.0 ** (j % FLAG_BITS), 0.0).astype(BF)


def _prefix_weights(nb):
    return (jnp.arange(nb)[:, None] <= jnp.arange(nb)[None, :]).astype(BF)


def _select_kernel(q_ref, kc_ref, vc_ref, pw_ref, tri_ref, oc_ref, sel_ref, flag_ref):
    qi = pl.program_id(1)
    q0 = qi * Q_BLOCK
    rq = HEADS_PER_GROUP * Q_BLOCK
    nb = kc_ref.shape[2] // 4
    t1 = q0 + lax.broadcasted_iota(jnp.int32, (Q_BLOCK, 1), 0)
    t4 = q0 + (lax.broadcasted_iota(jnp.int32, (rq, 1), 0) & (Q_BLOCK - 1))
    blk_t = t1 >> 6
    groups = range(NSA_GROUPS)

    def run(w):
        w_shift = w.bit_length() - 1
        assert w == 1 << w_shift and w % LANES == 0
        col = lax.broadcasted_iota(jnp.int32, (1, 4 * w), 1)
        cmp_end = (4 * (col & (w - 1)) + (col >> w_shift)) * CMP_STRIDE + (CMP_BLOCK - 1)
        valid = cmp_end <= t4
        jl = lax.broadcasted_iota(jnp.int32, (1, w), 1)
        forced = (jl == 0) | (jl == blk_t) | (jl == blk_t - 1)
        eligible = jl <= blk_t

        def columns(ref, g):
            if w == nb:
                return ref[0, g]
            return jnp.concatenate([ref[0, g, c * nb:c * nb + w] for c in range(4)], axis=0)

        def importance(p):
            ps = p[0:Q_BLOCK]
            for r in range(1, HEADS_PER_GROUP):
                ps = ps + p[r * Q_BLOCK:(r + 1) * Q_BLOCK]
            parts = [ps[:, c * w:(c + 1) * w] for c in range(4)]
            prev = jnp.where(jl == 0, 0.0, pltpu.roll(parts[3], 1, axis=1))
            return (parts[0] + parts[1] + parts[2] + parts[3]) - 0.5 * parts[3] + 0.5 * prev

        s = [jnp.where(valid, _dot_nt(_stack_heads(q_ref, g), columns(kc_ref, g)), NEG) for g in groups]
        m = [jnp.max(s[g], axis=-1, keepdims=True) for g in groups]
        m = [jnp.where(m[g] <= NEG, 0.0, m[g]) for g in groups]
        p = [jnp.exp(s[g] - m[g]) for g in groups]
        p = [p[g] / jnp.maximum(jnp.sum(p[g], axis=-1, keepdims=True), 1e-30) for g in groups]
        o_cmp = [_dot(p[g].astype(BF), columns(vc_ref, g)) for g in groups]
        pickable = eligible & ~forced
        n_pick = min(SLC_TOP_N, nb) - N_FORCED
        score = [jnp.where(pickable, importance(p[g]), NEG) for g in groups]
        work = score
        for _ in range(n_pick - 1):
            hit = [jl == jnp.argmax(work[g], axis=-1, keepdims=True).astype(jnp.int32) for g in groups]
            work = [jnp.where(hit[g], 3.0 * NEG, work[g]) for g in groups]
        thr = [jnp.max(work[g], axis=-1, keepdims=True) for g in groups]
        above = [score[g] > thr[g] for g in groups]
        tie = [score[g] == thr[g] for g in groups]
        n_above = [jnp.sum(jnp.where(above[g], 1.0, 0.0), axis=-1, keepdims=True) for g in groups]
        tie_rank = [_dot(jnp.where(tie[g], 1.0, 0.0).astype(BF), tri_ref[0:w, 0:w]) for g in groups]
        sel = [forced | above[g] | (tie[g] & (tie_rank[g] <= n_pick - n_above[g])) for g in groups]
        for g in groups:
            for r in range(HEADS_PER_GROUP):
                h0 = (g * HEADS_PER_GROUP + r) * HEAD_DIM
                oc_ref[0, :, h0:h0 + HEAD_DIM] = o_cmp[g][r * Q_BLOCK:(r + 1) * Q_BLOCK, :HEAD_DIM]
            sel_g = jnp.where(eligible & sel[g], 1.0, 0.0)
            sel_ref[0, g, :, 0:w] = sel_g.astype(BF)
            if w < nb:
                sel_ref[0, g, :, w:nb] = jnp.zeros((Q_BLOCK, nb - w), BF)
            any_sel = jnp.broadcast_to(jnp.max(sel_g, axis=0, keepdims=True), (8, w)).astype(BF)
            flag_ref[0, g, 0] = _dot(any_sel, pw_ref[0:w]).astype(jnp.int32)

    half = nb // 2
    if half % LANES == 0:
        in_first_half = q0 + Q_BLOCK <= half * SLC_BLOCK
        pl.when(in_first_half)(lambda: run(half))
        pl.when(jnp.logical_not(in_first_half))(lambda: run(nb))
    else:
        run(nb)


def _select(q, kc, vc):
    b, s, _ = q.shape
    ncp = kc.shape[2]
    nb = ncp // 4
    nq = s // Q_BLOCK
    pw = _flag_weights(nb)
    tri = _prefix_weights(nb)
    sds = jax.ShapeDtypeStruct
    cmp_spec = pl.BlockSpec((1, NSA_GROUPS, ncp, LANES), lambda bi, qi: (bi, 0, 0, 0))
    return pl.pallas_call(
        _select_kernel,
        grid=(b, nq),
        in_specs=[pl.BlockSpec((1, Q_BLOCK, NSA_HEADS * LANES), lambda bi, qi: (bi, qi, 0)),
                  cmp_spec, cmp_spec, _const_spec(pw.shape), _const_spec(tri.shape)],
        out_specs=[pl.BlockSpec((1, Q_BLOCK, NSA_HEADS * HEAD_DIM), lambda bi, qi: (bi, qi, 0)),
                   pl.BlockSpec((1, NSA_GROUPS, Q_BLOCK, nb), lambda bi, qi: (bi, 0, qi, 0)),
                   pl.BlockSpec((1, NSA_GROUPS, 1, 8, LANES), lambda bi, qi: (bi, 0, qi, 0, 0))],
        out_shape=[sds((b, s, NSA_HEADS * HEAD_DIM), F32), sds((b, NSA_GROUPS, s, nb), BF),
                   sds((b, NSA_GROUPS, nq, 8, LANES), jnp.int32)],
        compiler_params=_params(2),
        name="nsa_select",
    )(q, kc, vc, pw, tri)


def _attend_kernel(flag_ref, q_ref, g_ref, oc_ref, sel_ref, ks_ref, vs_ref, *rest, n_words):
    nwin = (WINDOW + ATT_Q) // KEY_TILE
    kw_refs, vw_refs = rest[:nwin], rest[nwin:2 * nwin]
    o_ref, list_ref = rest[2 * nwin], rest[2 * nwin + 1]
    bi, gi, qi = pl.program_id(0), pl.program_id(1), pl.program_id(2)
    q0 = qi * ATT_Q
    diag0 = qi * (ATT_Q // KEY_TILE)
    rq = HEADS_PER_GROUP * ATT_Q
    nb = sel_ref.shape[3]
    tiles_per_word = FLAG_BITS // TILE_BLOCKS
    tile_bits = (1 << TILE_BLOCKS) - 1
    sel_per_step = ATT_Q // Q_BLOCK

    qs = _stack_heads(q_ref, 0)
    selb = sel_ref[0, 0]
    t1 = q0 + lax.broadcasted_iota(jnp.int32, (ATT_Q, 1), 0)

    base = ((bi * NSA_GROUPS + gi) * pl.num_programs(2) + qi) * sel_per_step * n_words

    def word_body(wi, n):
        word = flag_ref[base + wi]
        for extra in range(1, sel_per_step):
            word = word | flag_ref[base + extra * n_words + wi]

        def scan_word(n):
            def tile_body(u, n):
                kt = wi * tiles_per_word + u
                active = (((word >> (u * TILE_BLOCKS)) & tile_bits) != 0) & (kt < diag0)

                @pl.when(active)
                def _():
                    list_ref[n] = kt
                return n + active.astype(jnp.int32)
            return lax.fori_loop(0, tiles_per_word, tile_body, n)
        return lax.cond(word != 0, scan_word, lambda n: n, n)

    n_active = lax.fori_loop(0, (diag0 + tiles_per_word - 1) // tiles_per_word, word_body, 0)
    for u in range(GROUP_TILES):
        list_ref[n_active + u] = 0

    jrow = lax.broadcasted_iota(jnp.int32, (nb, KEY_TILE), 0)
    kblk = lax.broadcasted_iota(jnp.int32, (nb, KEY_TILE), 1) >> 6
    kcol = lax.broadcasted_iota(jnp.int32, (1, KEY_TILE), 1)

    def masked_scores(k_all, member):
        s_ = _dot_nt(qs, k_all)
        return jnp.concatenate([jnp.where(member, s_[r * ATT_Q:(r + 1) * ATT_Q], NEG)
                                for r in range(HEADS_PER_GROUP)], axis=0)

    def tile_rows(ref, kt, n=KEY_TILE):
        return ref[0, 0, pl.ds(pl.multiple_of(kt * KEY_TILE, KEY_TILE), n), :]

    def group_body(it, carry):
        ks_t, vs_t, ex_t = [], [], []
        for u in range(GROUP_TILES):
            pos = it * GROUP_TILES + u
            kt = list_ref[pos]
            first_blk = jnp.where(pos < n_active, kt * TILE_BLOCKS, -TILE_BLOCKS - nb)
            ks_t.append(tile_rows(ks_ref, kt))
            vs_t.append(tile_rows(vs_ref, kt))
            ex_t.append(jnp.where(jrow == kblk + first_blk, 1.0, 0.0).astype(BF))
        member = _dot(selb, jnp.concatenate(ex_t, axis=1)) > 0.5
        s_ = masked_scores(jnp.concatenate(ks_t, axis=0), member)
        return _softmax_update(carry, s_, jnp.concatenate(vs_t, axis=0))

    init = (jnp.full((rq, 1), NEG, F32), jnp.zeros((rq, LANES), F32))
    carry = lax.fori_loop(0, (n_active + GROUP_TILES - 1) // GROUP_TILES, group_body, init)
    drow = lax.broadcasted_iota(jnp.int32, (nb, ATT_Q), 0)
    dcol = lax.broadcasted_iota(jnp.int32, (1, ATT_Q), 1)
    expand = jnp.where(drow == (dcol >> 6) + diag0 * TILE_BLOCKS, 1.0, 0.0).astype(BF)
    member = (_dot(selb, expand) > 0.5) & (q0 + dcol <= t1)
    kpos = []
    for i in range(nwin):
        kti = diag0 - WINDOW // KEY_TILE + i
        kpos.append(jnp.where(kti >= 0, kti * KEY_TILE, 1 << 24) + kcol)
    d = t1 - jnp.concatenate(kpos, axis=1)
    in_window = (d >= 0) & (d < WINDOW)
    s_d = masked_scores(tile_rows(ks_ref, diag0, ATT_Q), member)
    s_w = masked_scores(jnp.concatenate([r[0, 0] for r in kw_refs], axis=0), in_window)
    m_d = jnp.maximum(carry[0], jnp.max(s_d, axis=-1, keepdims=True))
    m_w = jnp.max(s_w, axis=-1, keepdims=True)
    p_d = jnp.exp(s_d - m_d).astype(BF)
    p_w = jnp.exp(s_w - m_w).astype(BF)
    acc_d = jnp.exp(carry[0] - m_d) * carry[1] + _dot(p_d, tile_rows(vs_ref, diag0, ATT_Q))
    acc_w = _dot(p_w, jnp.concatenate([r[0, 0] for r in vw_refs], axis=0))
    o_slc = _normalize(acc_d)
    o_win = _normalize(acc_w)

    for r in range(HEADS_PER_GROUP):
        rows = slice(r * ATT_Q, (r + 1) * ATT_Q)
        gate = lambda c: g_ref[0, :, 3 * r + c:3 * r + c + 1]
        o = (gate(0) * oc_ref[0, :, r * HEAD_DIM:(r + 1) * HEAD_DIM]
             + gate(1) * o_slc[rows] + gate(2) * o_win[rows])
        o_ref[0, :, r * HEAD_DIM:(r + 1) * HEAD_DIM] = o.astype(BF)


def _attend(flags, q, gates, o_cmp, sel, ks, vs, kw, vw):
    b, s, _ = q.shape
    nb = sel.shape[3]
    n_words = nb // FLAG_BITS
    nwin = (WINDOW + ATT_Q) // KEY_TILE
    per_q = ATT_Q // KEY_TILE
    gw = HEADS_PER_GROUP * LANES
    flat = flags[:, :, :, 0, :n_words].reshape(-1)
    full = pl.BlockSpec((1, 1, s, LANES), lambda bi, gi, qi, f: (bi, gi, 0, 0))
    win = lambda i: pl.BlockSpec(
        (1, 1, KEY_TILE, LANES),
        lambda bi, gi, qi, f: (bi, gi, jnp.maximum(qi * per_q - WINDOW // KEY_TILE + i, 0), 0))
    grid_spec = pltpu.PrefetchScalarGridSpec(
        num_scalar_prefetch=1,
        grid=(b, NSA_GROUPS, s // ATT_Q),
        in_specs=[pl.BlockSpec((1, ATT_Q, gw), lambda bi, gi, qi, f: (bi, qi, gi)),
                  pl.BlockSpec((1, ATT_Q, LANES), lambda bi, gi, qi, f: (bi, qi, gi)),
                  pl.BlockSpec((1, ATT_Q, HEADS_PER_GROUP * HEAD_DIM), lambda bi, gi, qi, f: (bi, qi, gi)),
                  pl.BlockSpec((1, 1, ATT_Q, nb), lambda bi, gi, qi, f: (bi, gi, qi, 0)),
                  full, full] + [win(i) for i in range(nwin)] * 2,
        out_specs=pl.BlockSpec((1, ATT_Q, HEADS_PER_GROUP * HEAD_DIM), lambda bi, gi, qi, f: (bi, qi, gi)),
        scratch_shapes=[pltpu.SMEM((s // KEY_TILE + GROUP_TILES,), jnp.int32)],
    )
    return pl.pallas_call(
        functools.partial(_attend_kernel, n_words=n_words),
        grid_spec=grid_spec,
        out_shape=jax.ShapeDtypeStruct((b, s, NSA_HEADS * HEAD_DIM), BF),
        compiler_params=_params(3),
        name="nsa_attend",
    )(flat, q, gates, o_cmp, sel, ks, vs, *([kw] * nwin), *([vw] * nwin))


def _l0_out_kernel(x_ref, a_ref, c_ref, wa_ref, wc_ref, post_ref, o_ref):
    y = _dot(a_ref[...], wa_ref[...]) + _dot(c_ref[...], wc_ref[...])
    o_ref[...] = x_ref[...] + _rmsnorm(y, post_ref[...])


def _l0_out(x2, o_nsa, o_conv, w_out, post_g, *, tm=512):
    t, d = x2.shape
    na, nc = o_nsa.shape[1], o_conv.shape[1]
    row = lambda w: pl.BlockSpec((tm, w), lambda i: (i, 0))
    return pl.pallas_call(
        _l0_out_kernel,
        grid=(t // tm,),
        in_specs=[row(d), row(na), row(nc), _const_spec((na, d)), _const_spec((nc, d)), _const_spec((1, d))],
        out_specs=row(d),
        out_shape=jax.ShapeDtypeStruct((t, d), F32),
        compiler_params=_params(1),
        name="l0_out",
    )(x2, o_nsa, o_conv, w_out[:na].astype(BF), w_out[na:].astype(BF), post_g.reshape(1, d))


def _head_indicators(d):
    ch = jnp.arange(d)[:, None] // RWKV_HEAD
    ind = (ch == jnp.arange(LANES)[None, :]).astype(BF)
    return ind, ind.T


def _rwkv_pre_kernel(x_ref, pre_ref, mu_ref, wr_ref, wk_ref, wv_ref, w0_ref, wd1_ref, wd2_ref,
                     a0_ref, wa1_ref, wa2_ref, wg1_ref, wg2_ref, kk_ref, ka_ref, ind_ref, indt_ref,
                     r_out, lw_out, k_out, v_out, kk_out, kka_out, g_out, hprev_ref, *, n_sub):
    si = pl.program_id(1)
    tm = x_ref.shape[1]

    @pl.when(si == 0)
    def _():
        hprev_ref[...] = jnp.zeros(hprev_ref.shape, F32)

    h_all = _rmsnorm(x_ref[0], pre_ref[...])
    rows = lax.broadcasted_iota(jnp.int32, (tm, 1), 0)
    xx_all = jnp.where(rows == 0, hprev_ref[7:8], pltpu.roll(h_all, 1, axis=0)) - h_all
    hprev_ref[...] = h_all[tm - 8:tm]

    def sub_tile(rs):
        h, xx = h_all[rs], xx_all[rs]
        mix = lambda i: (h + xx * mu_ref[i:i + 1]).astype(BF)
        xr, xw, xk, xv, xa, xg = [mix(i) for i in range(6)]
        r = _dot(xr, wr_ref[...])
        k = _dot(xk, wk_ref[...])
        v = _dot(xv, wv_ref[...])
        yield
        z = w0_ref[...] + _dot(jnp.tanh(_dot(xw, wd1_ref[...])).astype(BF), wd2_ref[...])
        softplus = jnp.maximum(-z, 0.0) + jnp.log(1.0 + jnp.exp(-jnp.abs(z)))
        lw = -jnp.exp(-softplus - 0.5)
        a = _sigmoid(a0_ref[...] + _dot(_dot(xa, wa1_ref[...]).astype(BF), wa2_ref[...]))
        g = _dot(_sigmoid(_dot(xg, wg1_ref[...])).astype(BF), wg2_ref[...])
        yield
        kraw = k * kk_ref[...]
        ss = _dot((kraw * kraw).astype(BF), ind_ref[...])
        yield
        inv = lax.rsqrt(jnp.maximum(ss, 1e-12))
        kk = kraw * _dot(inv.astype(BF), indt_ref[...])
        yield
        r_out[0, rs] = r.astype(BF)
        lw_out[0, rs] = lw
        k_out[0, rs] = (k * (1.0 + (a - 1.0) * ka_ref[...])).astype(BF)
        v_out[0, rs] = v.astype(BF)
        kk_out[0, rs] = kk.astype(BF)
        kka_out[0, rs] = (kk * a).astype(BF)
        g_out[0, rs] = g.astype(BF)

    _round_robin(sub_tile(slice(i * tm // n_sub, (i + 1) * tm // n_sub)) for i in range(n_sub))


def _rwkv_pre(x, pre_g, mu, w_r, w_k, w_v, w0, w_dec1, w_dec2, a0, w_a1, w_a2, w_g1, w_g2,
              k_k, k_a, ind, indt, *, tm=512, n_sub=1):
    b, s, d = x.shape
    row = pl.BlockSpec((1, tm, d), lambda bi, si: (bi, si, 0))
    vec = lambda a: a.reshape(1, d)
    ops = [x, vec(pre_g), mu, w_r.astype(BF), w_k.astype(BF), w_v.astype(BF), vec(w0),
           w_dec1.astype(BF), w_dec2.astype(BF), vec(a0), w_a1.astype(BF), w_a2.astype(BF),
           w_g1.astype(BF), w_g2.astype(BF), vec(k_k), vec(k_a), ind, indt]
    return pl.pallas_call(
        functools.partial(_rwkv_pre_kernel, n_sub=n_sub),
        grid=(b, s // tm),
        in_specs=[row] + [_const_spec(o.shape) for o in ops[1:]],
        out_specs=[row] * 7,
        out_shape=[jax.ShapeDtypeStruct((b, s, d), F32 if i == 1 else BF) for i in range(7)],
        scratch_shapes=[pltpu.VMEM((8, d), F32)],
        compiler_params=_params(2),
        name="rwkv_pre",
    )(*ops)


def _rwkv_scan_kernel(r_ref, lw_ref, k_ref, v_ref, kk_ref, kka_ref, y_ref, state_ref):
    c = RWKV_CHUNK
    slab = SLAB_HEADS * RWKV_HEAD
    n_slab = r_ref.shape[2] // slab
    rows = SLAB_HEADS * c
    subs = [slice(i * c, (i + 1) * c) for i in range(r_ref.shape[1] // c)]

    @pl.when(pl.program_id(1) == 0)
    def _():
        state_ref[...] = jnp.zeros(state_ref.shape, F32)

    ri = lax.broadcasted_iota(jnp.int32, (rows, 1), 0)
    ci = lax.broadcasted_iota(jnp.int32, (1, rows), 1)
    li = lax.broadcasted_iota(jnp.int32, (1, slab), 1)
    c_shift = c.bit_length() - 1
    h_shift = RWKV_HEAD.bit_length() - 1
    assert c == 1 << c_shift and RWKV_HEAD == 1 << h_shift
    head_match = (ri >> c_shift) == (li >> h_shift)
    t_r, t_c = ri & (c - 1), ci & (c - 1)
    strict = t_c < t_r
    incl = t_c <= t_r
    eye = jnp.where(ri == ci, 1.0, 0.0)
    tri = jnp.where(lax.broadcasted_iota(jnp.int32, (c, c), 1) <= lax.broadcasted_iota(jnp.int32, (c, c), 0),
                    1.0, 0.0).astype(BF)
    state_match = (lax.broadcasted_iota(jnp.int32, (slab, 1), 0) >> h_shift) == (li >> h_shift)

    incl2 = jnp.concatenate([incl, incl], axis=1)

    def block_diag(x):
        return jnp.where(head_match, jnp.concatenate([x] * SLAB_HEADS, axis=0), 0.0)

    def fold(x):
        out = x[0:c]
        for h in range(1, SLAB_HEADS):
            out = out + x[h * c:(h + 1) * c]
        return out

    bf = lambda x: x.astype(BF)
    slabs = [slice(hs * slab, (hs + 1) * slab) for hs in range(n_slab)]

    def prepare(rs, sl):
        lw = lw_ref[0, rs, sl]
        hi = bf(lw)
        rem = lw - hi.astype(F32)
        mid = bf(rem)
        lo = bf(rem - mid.astype(F32))
        cw = _dot(tri, hi) + _dot(tri, mid) + _dot(tri, lo)
        w_in = jnp.exp(cw)
        w_inv = jnp.exp(-cw)
        w_prev = jnp.exp(cw - lw)
        w_end = w_in[c - 1:c]

        a_t = -kk_ref[0, rs, sl].astype(F32) * w_prev
        b_t = kka_ref[0, rs, sl].astype(F32) * w_inv
        k_t = k_ref[0, rs, sl].astype(F32) * w_inv
        r_t = r_ref[0, rs, sl].astype(F32) * w_in
        return a_t, b_t, k_t, r_t, w_end

    def interactions(a_t, b_t, k_t, r_t):
        lhs = bf(jnp.concatenate([block_diag(a_t), block_diag(r_t)], axis=0))
        rhs = bf(jnp.concatenate([block_diag(b_t), block_diag(k_t)], axis=0))
        big = _dot_nt(lhs, rhs)
        a_ab = jnp.where(strict, big[:rows, :rows], 0.0)
        a_ak = jnp.where(strict, big[:rows, rows:], 0.0)
        a_r = jnp.where(incl2, big[rows:], 0.0)
        return a_ab, bf(a_ak), bf(a_r)

    every = range(n_slab)
    units = [(rs, sl) for rs in subs for sl in slabs]
    prep = [prepare(rs, sl) for rs, sl in units]
    inter = [interactions(*p[:4]) for p in prep]
    vs = [v_ref[0, rs, sl] for rs, sl in units]
    vbd = [bf(block_diag(v)) for v in vs]
    akv = [_dot(inter[u][1], vbd[u]) for u in range(len(units))]

    levels = c.bit_length() - 2
    assert levels >= 1
    invs = [eye + x[0] for x in inter]
    pws = [_dot(bf(x[0]), bf(x[0])) for x in inter]
    for _ in range(levels - 1):
        both = [_dot(bf(pw), bf(jnp.concatenate([pw, inv], axis=1))) for pw, inv in zip(pws, invs)]
        pws = [x[:, :rows] for x in both]
        invs = [inv + x[:, rows:] for inv, x in zip(invs, both)]
    invs = [bf(inv + _dot(bf(pw), bf(inv))) for pw, inv in zip(pws, invs)]

    states = [state_ref[hs] for hs in every]
    for si, rs in enumerate(subs):
        us = [si * n_slab + hs for hs in every]
        ar = [_dot_nt(bf(jnp.concatenate([prep[u][0], prep[u][3]], axis=0)), bf(states[hs]))
              for hs, u in zip(every, us)]
        u_bd = [_dot(invs[u], bf(block_diag(ar[hs][:c]) + akv[u])) for hs, u in zip(every, us)]
        y_bd = [block_diag(ar[hs][c:]) + _dot(inter[u][2], jnp.concatenate([bf(u_bd[hs]), vbd[u]], axis=0))
                for hs, u in zip(every, us)]
        for hs, sl in enumerate(slabs):
            y_ref[0, rs, sl] = fold(y_bd[hs])
        new_states = []
        for hs, u in zip(every, us):
            _, b_t, k_t, _, w_end = prep[u]
            uv = jnp.concatenate([bf(fold(u_bd[hs])), bf(vs[u])], axis=0)
            bk = jnp.concatenate([bf(b_t * w_end), bf(k_t * w_end)], axis=0)
            new_states.append(jnp.where(state_match, states[hs] * w_end + _dot_tn(uv, bk), 0.0))
        states = new_states
    for hs in every:
        state_ref[hs] = states[hs]


def _rwkv_scan(r, lw, k, v, kk, kka):
    b, s, d = r.shape
    c = RWKV_CHUNK * SCAN_SUB
    slab = SLAB_HEADS * RWKV_HEAD
    blk = pl.BlockSpec((1, c, d), lambda bi, ci: (bi, ci, 0))
    return pl.pallas_call(
        _rwkv_scan_kernel,
        grid=(b, s // c),
        in_specs=[blk] * 6,
        out_specs=blk,
        out_shape=jax.ShapeDtypeStruct((b, s, d), F32),
        scratch_shapes=[pltpu.VMEM((d // slab, slab, slab), F32)],
        compiler_params=_params(2),
        name="rwkv_scan",
    )(r, lw, k, v, kk, kka)


def _rwkv_post_kernel(x_ref, y_ref, r_ref, k_ref, v_ref, g_ref, wo_ref, lng_ref, lnb_ref, rk_ref,
                      post_ref, ind_ref, indt_ref, o_ref, *, n_sub):
    ind, indt = ind_ref[...], indt_ref[...]
    inv_n = 1.0 / RWKV_HEAD
    head_sum = lambda a: _dot(a.astype(BF), ind)
    spread = lambda a: _dot(a.astype(BF), indt)
    tm = x_ref.shape[0]

    def sub_tile(rs):
        f32 = lambda ref: ref[rs].astype(F32)
        y = y_ref[rs]
        sums = head_sum(y)
        rk_sum = head_sum(f32(r_ref) * f32(k_ref) * rk_ref[...])
        yield
        mean = _split_dot(sums * inv_n, indt)
        bonus = spread(rk_sum) * f32(v_ref)
        yield
        yc = y - mean
        var = head_sum(yc * yc) * inv_n
        yield
        rstd = spread(lax.rsqrt(var + LNX_EPS))
        yield
        yn = yc * rstd * lng_ref[...] + lnb_ref[...]
        out = _dot(((yn + bonus) * f32(g_ref)).astype(BF), wo_ref[...])
        yield
        o_ref[rs] = x_ref[rs] + _rmsnorm(out, post_ref[...])

    _round_robin(sub_tile(slice(i * tm // n_sub, (i + 1) * tm // n_sub)) for i in range(n_sub))


def _rwkv_post(x2, y, r, k, v, g, w_o, lnx_g, lnx_b, r_k, post_g, ind, indt, *, tm=512, n_sub=2):
    t, d = x2.shape
    row = pl.BlockSpec((tm, d), lambda i: (i, 0))
    vec = lambda a: a.reshape(1, d)
    ops = [x2, y, r, k, v, g, w_o.astype(BF), vec(lnx_g), vec(lnx_b), vec(r_k), vec(post_g), ind, indt]
    return pl.pallas_call(
        functools.partial(_rwkv_post_kernel, n_sub=n_sub),
        grid=(t // tm,),
        in_specs=[row] * 6 + [_const_spec(o.shape) for o in ops[6:]],
        out_specs=row,
        out_shape=jax.ShapeDtypeStruct((t, d), F32),
        compiler_params=_params(1),
        name="rwkv_post",
    )(*ops)


def _layer0_mixer(x, pre_g, post_g, w_in, pe_k, w1_k, w2_k, pe_v, w1_v, w2_v, conv_w, conv_b, w_out):
    b, s, d = x.shape
    q, gates, kcr, vcr, ks, vs, kw, vw, o_conv = _l0_in(x, pre_g, _l0_in_weight(w_in), conv_w, conv_b)
    kc = _compress(kcr, _cmp_weights(pe_k, w1_k, w2_k), is_key=True)
    vc = _compress(vcr, _cmp_weights(pe_v, w1_v, w2_v), is_key=False)
    o_cmp, sel, flags = _select(q, kc, vc)
    o_nsa = _attend(flags, q, gates, o_cmp, sel, ks, vs, kw, vw)
    t = b * s
    return _l0_out(x.reshape(t, d), o_nsa.reshape(t, -1), o_conv.reshape(t, -1), w_out, post_g).reshape(b, s, d)


def _layer1_mixer(x, pre_g, post_g, mu, w_r, w_k, w_v, w_o, w0, w_dec1, w_dec2, a0, w_a1, w_a2,
                  w_g1, w_g2, k_k, k_a, r_k, lnx_g, lnx_b):
    b, s, d = x.shape
    ind, indt = _head_indicators(d)
    r, lw, k, v, kk, kka, g = _rwkv_pre(x, pre_g, mu, w_r, w_k, w_v, w0, w_dec1, w_dec2, a0,
                                        w_a1, w_a2, w_g1, w_g2, k_k, k_a, ind, indt)
    y = _rwkv_scan(r, lw, k, v, kk, kka)
    f = lambda a: a.reshape(b * s, d)
    return _rwkv_post(f(x), f(y), f(r), f(k), f(v), f(g), w_o, lnx_g, lnx_b, r_k, post_g,
                      ind, indt).reshape(b, s, d)


def kernel(x, l0_ffn1_pre_g, l0_ffn1_post_g, l0_ffn1_w_gate, l0_ffn1_w_up, l0_ffn1_w_down, l0_mix_pre_g, l0_mix_post_g, l0_w_in, l0_cmp_pe_k, l0_cmp_w1_k, l0_cmp_w2_k, l0_cmp_pe_v, l0_cmp_w1_v, l0_cmp_w2_v, l0_conv_w, l0_conv_b, l0_w_out, l0_ffn2_pre_g, l0_ffn2_post_g, l0_ffn2_w_gate, l0_ffn2_w_up, l0_ffn2_w_down, l1_ffn1_pre_g, l1_ffn1_post_g, l1_ffn1_w_gate, l1_ffn1_w_up, l1_ffn1_w_down, l1_mix_pre_g, l1_mix_post_g, l1_mu, l1_w_r, l1_w_k, l1_w_v, l1_w_o, l1_w0, l1_w_dec1, l1_w_dec2, l1_a0, l1_w_a1, l1_w_a2, l1_w_g1, l1_w_g2, l1_k_k, l1_k_a, l1_r_k, l1_lnx_g, l1_lnx_b, l1_ffn2_pre_g, l1_ffn2_post_g, l1_ffn2_w_gate, l1_ffn2_w_up, l1_ffn2_w_down):
    b, s, d = x.shape
    ffn = lambda a, *w: _ffn(a.reshape(b * s, d), *w).reshape(b, s, d)
    x = ffn(x, l0_ffn1_pre_g, l0_ffn1_post_g, l0_ffn1_w_gate, l0_ffn1_w_up, l0_ffn1_w_down)
    x = _layer0_mixer(x, l0_mix_pre_g, l0_mix_post_g, l0_w_in, l0_cmp_pe_k, l0_cmp_w1_k, l0_cmp_w2_k,
                      l0_cmp_pe_v, l0_cmp_w1_v, l0_cmp_w2_v, l0_conv_w, l0_conv_b, l0_w_out)
    x = ffn(x, l0_ffn2_pre_g, l0_ffn2_post_g, l0_ffn2_w_gate, l0_ffn2_w_up, l0_ffn2_w_down)
    x = ffn(x, l1_ffn1_pre_g, l1_ffn1_post_g, l1_ffn1_w_gate, l1_ffn1_w_up, l1_ffn1_w_down)
    x = _layer1_mixer(x, l1_mix_pre_g, l1_mix_post_g, l1_mu, l1_w_r, l1_w_k, l1_w_v, l1_w_o, l1_w0,
                      l1_w_dec1, l1_w_dec2, l1_a0, l1_w_a1, l1_w_a2, l1_w_g1, l1_w_g2, l1_k_k, l1_k_a,
                      l1_r_k, l1_lnx_g, l1_lnx_b)
    x = ffn(x, l1_ffn2_pre_g, l1_ffn2_post_g, l1_ffn2_w_gate, l1_ffn2_w_up, l1_ffn2_w_down)
    return x
```

```python
import functools

import jax
import jax.numpy as jnp
from jax import lax
from jax.experimental import pallas as pl
from jax.experimental.pallas import tpu as pltpu

BF = jnp.bfloat16
F32 = jnp.float32

EPS = 1e-6
LNX_EPS = 64e-5
HEAD_DIM = 64
NSA_HEADS = 8
NSA_GROUPS = 2
HEADS_PER_GROUP = NSA_HEADS // NSA_GROUPS
CMP_STRIDE = 16
CMP_BLOCK = 32
CMP_HIDDEN = 128
SLC_BLOCK = 64
SLC_TOP_N = 16
WINDOW = 512
Q_BLOCK = 128
CONV_WIDTH = 512
RWKV_HEAD = 64
RWKV_CHUNK = 64
SLAB_HEADS = 2
SCAN_SUB = 2
LANES = 128
MXU_DIM = 256
POS_SHIFT = 7
POS_SPLIT = 1 << POS_SHIFT
NEG = -1e30
VMEM_LIMIT = 56 * 1024 * 1024

NT_DIMS = (((1,), (1,)), ((), ()))
TN_DIMS = (((0,), (0,)), ((), ()))


def _dot(a, b):
    return jnp.dot(a, b, preferred_element_type=F32)


def _dot_nt(a, b):
    return lax.dot_general(a, b, NT_DIMS, preferred_element_type=F32)


def _dot_tn(a, b):
    return lax.dot_general(a, b, TN_DIMS, preferred_element_type=F32)


def _split_dot(x, w):
    hi = x.astype(BF)
    lo = (x - hi.astype(F32)).astype(BF)
    return _dot(hi, w) + _dot(lo, w)


def _rmsnorm(x, g):
    ms = jnp.mean(x * x, axis=-1, keepdims=True)
    return x * lax.rsqrt(ms + EPS) * g


def _sigmoid(x):
    return 1.0 / (1.0 + jnp.exp(-x))


def _round_robin(stage_generators):
    pending = list(stage_generators)
    while pending:
        still = []
        for gen in pending:
            try:
                next(gen)
                still.append(gen)
            except StopIteration:
                pass
        pending = still


def _const_spec(shape):
    zeros = (0,) * len(shape)
    return pl.BlockSpec(shape, lambda *_: zeros)


def _params(n_grid):
    return pltpu.CompilerParams(dimension_semantics=("arbitrary",) * n_grid,
                                vmem_limit_bytes=VMEM_LIMIT)


def _ffn_kernel(*refs, ff_chunk, n_sub, after_mixer):
    if after_mixer:
        x_ref, a_ref, c_ref, wa_ref, wc_ref, mixpost_ref = refs[:6]
        refs = refs[:1] + refs[6:]
    x_ref, pre_ref, post_ref, wg_ref, wu_ref, wd_ref, o_ref = refs

    def residual_rows(rs):
        if not after_mixer:
            return x_ref[rs]
        y = _dot(a_ref[rs], wa_ref[...]) + _dot(c_ref[rs], wc_ref[...])
        return x_ref[rs] + _rmsnorm(y, mixpost_ref[...])

    sub = x_ref.shape[0] // n_sub
    dff = wg_ref.shape[1]
    chunks = [(c0, min(c0 + ff_chunk, dff)) for c0 in range(0, dff, ff_chunk)]
    rows = [slice(i * sub, (i + 1) * sub) for i in range(n_sub)]
    xs, hs, accs = [], [], []
    for ci, (c0, c1) in enumerate(chunks):
        gu = []
        for i, rs in enumerate(rows):
            if ci == 0:
                xs.append(residual_rows(rs))
                hs.append(_rmsnorm(xs[i], pre_ref[...]).astype(BF))
                accs.append(jnp.zeros(xs[i].shape, F32))
            gu.append((_dot(hs[i], wg_ref[:, c0:c1]), _dot(hs[i], wu_ref[:, c0:c1])))
        for i, (g, u) in enumerate(gu):
            a = (g * _sigmoid(g) * u).astype(BF)
            accs[i] = accs[i] + _dot(a, wd_ref[c0:c1, :])
    for i, rs in enumerate(rows):
        o_ref[rs] = xs[i] + 0.5 * _rmsnorm(accs[i], post_ref[...])


def _ffn(x2, pre_g, post_g, w_gate, w_up, w_down, *, mixer=None, tm=1024, n_sub=2):
    t, d = x2.shape
    dff = w_gate.shape[1]
    ff_chunk = MXU_DIM * pl.cdiv(pl.cdiv(dff, MXU_DIM), 2)
    row = lambda w: pl.BlockSpec((tm, w), lambda i: (i, 0))
    weight = lambda shape: pl.BlockSpec(shape, lambda i: (0, 0), pipeline_mode=pl.Buffered(1))
    ops, specs = [x2], [row(d)]
    if mixer is not None:
        o_nsa, o_conv, w_out, mix_post_g = mixer
        na, nc = o_nsa.shape[1], o_conv.shape[1]
        ops += [o_nsa, o_conv, w_out[:na].astype(BF), w_out[na:].astype(BF), mix_post_g.reshape(1, d)]
        specs += [row(na), row(nc), weight((na, d)), weight((nc, d)), _const_spec((1, d))]
    ops += [pre_g.reshape(1, d), post_g.reshape(1, d), w_gate.astype(BF), w_up.astype(BF), w_down.astype(BF)]
    specs += [_const_spec((1, d)), _const_spec((1, d)), weight((d, dff)), weight((d, dff)), weight((dff, d))]
    return pl.pallas_call(
        functools.partial(_ffn_kernel, ff_chunk=ff_chunk, n_sub=n_sub, after_mixer=mixer is not None),
        grid=(t // tm,),
        in_specs=specs,
        out_specs=row(d),
        out_shape=jax.ShapeDtypeStruct((t, d), F32),
        compiler_params=_params(1),
        name="ffn",
    )(*ops)


_C_Q = 0
_C_G = _C_Q + NSA_HEADS * LANES
_C_KC = _C_G + NSA_GROUPS * LANES
_C_VC = _C_KC + LANES
_C_KS = _C_VC + LANES
_C_CONV = _C_KS + 4 * NSA_GROUPS * LANES
_C_END = _C_CONV + 3 * CONV_WIDTH


def _l0_in_weight(w_in):
    d = w_in.shape[0]
    nsa_w = NSA_HEADS * HEAD_DIM
    kv_w = NSA_GROUPS * HEAD_DIM
    o = 0
    q = w_in[:, o:o + nsa_w]; o += nsa_w
    g = w_in[:, o:o + 3 * NSA_HEADS]; o += 3 * NSA_HEADS
    kvs = []
    for _ in range(6):
        kvs.append(w_in[:, o:o + kv_w]); o += kv_w
    conv = w_in[:, o:]
    zpad = lambda w, n: jnp.pad(w, ((0, 0), (0, n - w.shape[1])))
    cols = [zpad(q[:, h * HEAD_DIM:(h + 1) * HEAD_DIM], LANES) for h in range(NSA_HEADS)]
    gpg = 3 * HEADS_PER_GROUP
    cols += [zpad(g[:, i * gpg:(i + 1) * gpg], LANES) for i in range(NSA_GROUPS)]
    cols += [kvs[0], kvs[1]]
    for w in kvs[2:]:
        cols += [zpad(w[:, i * HEAD_DIM:(i + 1) * HEAD_DIM], LANES) for i in range(NSA_GROUPS)]
    cols.append(conv)
    w = jnp.concatenate(cols, axis=1)
    assert w.shape == (d, _C_END)
    return w.astype(BF)


def _l0_in_kernel(x_ref, pre_ref, w_ref, cw_ref, cb_ref,
                  q_ref, g_ref, kcr_ref, vcr_ref, ks_ref, vs_ref, kw_ref, vw_ref, oc_ref,
                  zprev_ref):
    si = pl.program_id(1)
    tm = x_ref.shape[1]
    h = _rmsnorm(x_ref[0], pre_ref[...]).astype(BF)
    lane = lax.broadcasted_iota(jnp.int32, (1, LANES), 1)
    pos = si * tm + lax.broadcasted_iota(jnp.int32, (tm, 1), 0)
    pos_hi = (pos >> POS_SHIFT).astype(F32)
    pos_lo = (pos & (POS_SPLIT - 1)).astype(F32)

    wide = 4 * LANES

    def project(c0):
        res = _dot(h, w_ref[:, c0:c0 + wide])
        return [res[:, j * LANES:(j + 1) * LANES] for j in range(4)]

    for hd0 in range(0, NSA_HEADS, 4):
        for j, qh in enumerate(project(_C_Q + hd0 * LANES)):
            hd = hd0 + j
            slope = 2.0 ** (-(hd + 1))
            qh = qh * (HEAD_DIM ** -0.5)
            qh = jnp.where(lane == HEAD_DIM, POS_SPLIT * slope, jnp.where(lane == HEAD_DIM + 1, slope, qh))
            q_ref[0, :, hd * LANES:(hd + 1) * LANES] = qh.astype(BF)

    assert _C_KC == _C_G + 2 * LANES and _C_VC == _C_KC + LANES and NSA_GROUPS == 2
    g0, g1, kcr, vcr = project(_C_G)
    g_ref[0] = _sigmoid(jnp.concatenate([g0, g1], axis=1))
    kcr_ref[0] = kcr
    vcr_ref[0] = vcr

    slabs = project(_C_KS) + project(_C_KS + wide)
    for n, (ref, is_key) in enumerate(((ks_ref, True), (vs_ref, False), (kw_ref, True), (vw_ref, False))):
        for gi in range(NSA_GROUPS):
            t = slabs[n * NSA_GROUPS + gi]
            if is_key:
                t = jnp.where(lane == HEAD_DIM, pos_hi, jnp.where(lane == HEAD_DIM + 1, pos_lo, t))
            else:
                t = jnp.where(lane >= HEAD_DIM, 1.0, t)
            ref[0, gi] = t.astype(BF)

    @pl.when(si == 0)
    def _():
        zprev_ref[...] = jnp.zeros(zprev_ref.shape, F32)

    cb = _dot(h, w_ref[:, _C_CONV:_C_CONV + CONV_WIDTH])
    cc = _dot(h, w_ref[:, _C_CONV + CONV_WIDTH:_C_CONV + 2 * CONV_WIDTH])
    cu = _dot(h, w_ref[:, _C_CONV + 2 * CONV_WIDTH:_C_CONV + 3 * CONV_WIDTH])
    z = cc * cu
    prev = zprev_ref[...]
    rows = lax.broadcasted_iota(jnp.int32, (tm, 1), 0)
    zm1 = jnp.where(rows == 0, prev[7:8], pltpu.roll(z, 1, axis=0))
    zm2 = jnp.where(rows == 0, prev[6:7], jnp.where(rows == 1, prev[7:8], pltpu.roll(z, 2, axis=0)))
    y = cw_ref[0:1] * zm2 + cw_ref[1:2] * zm1 + cw_ref[2:3] * z
    oc_ref[0] = (cb * (y + cb_ref[...])).astype(BF)
    zprev_ref[...] = z[tm - 8:tm]


def _l0_in(x, pre_g, w_all, conv_w, conv_b, *, tm=512):
    b, s, d = x.shape
    row = lambda w: pl.BlockSpec((1, tm, w), lambda bi, si: (bi, si, 0))
    grp = pl.BlockSpec((1, NSA_GROUPS, tm, LANES), lambda bi, si: (bi, 0, si, 0))
    sds = jax.ShapeDtypeStruct
    kv_shape = sds((b, NSA_GROUPS, s, LANES), BF)
    return pl.pallas_call(
        _l0_in_kernel,
        grid=(b, s // tm),
        in_specs=[row(d), _const_spec((1, d)), _const_spec(w_all.shape),
                  _const_spec(conv_w.shape), _const_spec((1, CONV_WIDTH))],
        out_specs=[row(NSA_HEADS * LANES), row(NSA_GROUPS * LANES), row(LANES), row(LANES),
                   grp, grp, grp, grp, row(CONV_WIDTH)],
        out_shape=[sds((b, s, NSA_HEADS * LANES), BF), sds((b, s, NSA_GROUPS * LANES), F32),
                   sds((b, s, LANES), F32), sds((b, s, LANES), F32),
                   kv_shape, kv_shape, kv_shape, kv_shape, sds((b, s, CONV_WIDTH), BF)],
        scratch_shapes=[pltpu.VMEM((8, CONV_WIDTH), F32)],
        compiler_params=_params(2),
        name="l0_in",
    )(x, pre_g.reshape(1, d), w_all, conv_w, conv_b.reshape(1, CONV_WIDTH))


def _cmp_weights(pe, w1, w2):
    half = CMP_STRIDE * HEAD_DIM
    def expand(w):
        w = w.reshape(CMP_STRIDE, 1, HEAD_DIM, 1, CMP_HIDDEN)
        same_group = jnp.eye(NSA_GROUPS, dtype=F32).reshape(1, NSA_GROUPS, 1, NSA_GROUPS, 1)
        return (w * same_group).reshape(CMP_STRIDE * NSA_GROUPS * HEAD_DIM, NSA_GROUPS * CMP_HIDDEN).astype(BF)
    pe_row = lambda p: jnp.tile(p, (1, NSA_GROUPS)).reshape(1, CMP_STRIDE * NSA_GROUPS * HEAD_DIM)
    w2p = jnp.pad(w2, ((0, 0), (0, LANES - HEAD_DIM))).astype(BF)
    return (pe_row(pe[:CMP_STRIDE]), pe_row(pe[CMP_STRIDE:]), expand(w1[:half]), expand(w1[half:]), w2p)


def _gelu_tanh(x):
    return 0.5 * x * (1.0 + jnp.tanh(0.7978845608028654 * (x + 0.044715 * (x * x * x))))


def _cmp_kernel(x_ref, pet_ref, peb_ref, wt_ref, wb_ref, w2_ref, o_ref, *, is_key):
    nb = x_ref.shape[1]
    hw = x_ref.shape[2] // 4
    top, bot = [], []
    for c in range(4):
        xc = x_ref[0, :, c * hw:(c + 1) * hw]
        top.append(_dot((xc + pet_ref[...]).astype(BF), wt_ref[...]))
        bot.append(_dot((xc + peb_ref[...]).astype(BF), wb_ref[...]))
    lane = lax.broadcasted_iota(jnp.int32, (1, LANES), 1)
    j = lax.broadcasted_iota(jnp.int32, (nb, 1), 0)
    for c in range(4):
        nxt = bot[c + 1] if c < 3 else pltpu.roll(bot[0], nb - 1, axis=0)
        hid = _gelu_tanh(top[c] + nxt)
        n = 4 * j + c
        end = n * CMP_STRIDE + (CMP_BLOCK - 1)
        exists = n < 4 * nb - 1
        for gi in range(NSA_GROUPS):
            t = _dot(hid[:, gi * CMP_HIDDEN:(gi + 1) * CMP_HIDDEN].astype(BF), w2_ref[...])
            t = jnp.where(exists, t, 0.0)
            if is_key:
                t = jnp.where(lane == HEAD_DIM, (end >> POS_SHIFT).astype(F32),
                              jnp.where(lane == HEAD_DIM + 1, (end & (POS_SPLIT - 1)).astype(F32), t))
            else:
                t = jnp.where(lane == HEAD_DIM, 1.0, t)
            o_ref[0, gi, c * nb:(c + 1) * nb, :] = t.astype(BF)


def _compress(raw, weights, *, is_key):
    b, s, _ = raw.shape
    nb = s // (4 * CMP_STRIDE)
    x = raw.reshape(b, nb, 4 * CMP_STRIDE * LANES)
    pet, peb, wt, wb, w2p = weights
    return pl.pallas_call(
        functools.partial(_cmp_kernel, is_key=is_key),
        grid=(b,),
        in_specs=[pl.BlockSpec((1, nb, x.shape[2]), lambda bi: (bi, 0, 0)),
                  _const_spec(pet.shape), _const_spec(peb.shape), _const_spec(wt.shape),
                  _const_spec(wb.shape), _const_spec(w2p.shape)],
        out_specs=pl.BlockSpec((1, NSA_GROUPS, 4 * nb, LANES), lambda bi: (bi, 0, 0, 0)),
        out_shape=jax.ShapeDtypeStruct((b, NSA_GROUPS, 4 * nb, LANES), BF),
        compiler_params=_params(1),
        name="compress",
    )(x, pet, peb, wt, wb, w2p)


def _softmax_update(carry, s, v):
    m, acc = carry
    m_new = jnp.maximum(m, jnp.max(s, axis=-1, keepdims=True))
    p = jnp.exp(s - m_new)
    acc = jnp.exp(m - m_new) * acc + _dot(p.astype(BF), v)
    return m_new, acc


N_FORCED = 3
FLAG_BITS = 16
ATT_Q = 256
KEY_TILE = 128
TILE_BLOCKS = KEY_TILE // SLC_BLOCK
GROUP_TILES = 4


def _stack_heads(q_ref, g):
    base = g * HEADS_PER_GROUP
    return jnp.concatenate([q_ref[0, :, (base + r) * LANES:(base + r + 1) * LANES]
                            for r in range(HEADS_PER_GROUP)], axis=0)


def _flag_weights(nb):
    j = jnp.arange(nb)[:, None]
    w = jnp.arange(LANES)[None, :]
    return jnp.where(j // FLAG_BITS == w, 2.0 ** (j % FLAG_BITS), 0.0).astype(BF)


def _prefix_weights(nb):
    return (jnp.arange(nb)[:, None] <= jnp.arange(nb)[None, :]).astype(BF)


def _select_kernel(q_ref, kc_ref, vc_ref, pw_ref, tri_ref, oc_ref, sel_ref, flag_ref):
    qi = pl.program_id(1)
    q0 = qi * Q_BLOCK
    rq = HEADS_PER_GROUP * Q_BLOCK
    nb = kc_ref.shape[2] // 4
    t1 = q0 + lax.broadcasted_iota(jnp.int32, (Q_BLOCK, 1), 0)
    t4 = q0 + (lax.broadcasted_iota(jnp.int32, (rq, 1), 0) & (Q_BLOCK - 1))
    blk_t = t1 >> 6
    groups = range(NSA_GROUPS)

    def run(w):
        w_shift = w.bit_length() - 1
        assert w == 1 << w_shift and w % LANES == 0
        col = lax.broadcasted_iota(jnp.int32, (1, 4 * w), 1)
        cmp_end = (4 * (col & (w - 1)) + (col >> w_shift)) * CMP_STRIDE + (CMP_BLOCK - 1)
        valid = cmp_end <= t4
        jl = lax.broadcasted_iota(jnp.int32, (1, w), 1)
        forced = (jl == 0) | (jl == blk_t) | (jl == blk_t - 1)
        eligible = jl <= blk_t

        def columns(ref, g):
            if w == nb:
                return ref[0, g]
            return jnp.concatenate([ref[0, g, c * nb:c * nb + w] for c in range(4)], axis=0)

        def importance(p):
            ps = p[0:Q_BLOCK]
            for r in range(1, HEADS_PER_GROUP):
                ps = ps + p[r * Q_BLOCK:(r + 1) * Q_BLOCK]
            parts = [ps[:, c * w:(c + 1) * w] for c in range(4)]
            prev = jnp.where(jl == 0, 0.0, pltpu.roll(parts[3], 1, axis=1))
            return (parts[0] + parts[1] + parts[2] + parts[3]) - 0.5 * parts[3] + 0.5 * prev

        s = [jnp.where(valid, _dot_nt(_stack_heads(q_ref, g), columns(kc_ref, g)), NEG) for g in groups]
        m = [jnp.max(s[g], axis=-1, keepdims=True) for g in groups]
        m = [jnp.where(m[g] <= NEG, 0.0, m[g]) for g in groups]
        p = [jnp.exp(s[g] - m[g]) for g in groups]
        p = [p[g] / jnp.maximum(jnp.sum(p[g], axis=-1, keepdims=True), 1e-30) for g in groups]
        o_cmp = [_dot(p[g].astype(BF), columns(vc_ref, g)) for g in groups]
        pickable = eligible & ~forced
        n_pick = min(SLC_TOP_N, nb) - N_FORCED
        score = [jnp.where(pickable, importance(p[g]), NEG) for g in groups]
        work = score
        for _ in range(n_pick - 1):
            hit = [jl == jnp.argmax(work[g], axis=-1, keepdims=True).astype(jnp.int32) for g in groups]
            work = [jnp.where(hit[g], 3.0 * NEG, work[g]) for g in groups]
        thr = [jnp.max(work[g], axis=-1, keepdims=True) for g in groups]
        above = [score[g] > thr[g] for g in groups]
        tie = [score[g] == thr[g] for g in groups]
        n_above = [jnp.sum(jnp.where(above[g], 1.0, 0.0), axis=-1, keepdims=True) for g in groups]
        tie_rank = [_dot(jnp.where(tie[g], 1.0, 0.0).astype(BF), tri_ref[0:w, 0:w]) for g in groups]
        sel = [forced | above[g] | (tie[g] & (tie_rank[g] <= n_pick - n_above[g])) for g in groups]
        for g in groups:
            for r in range(HEADS_PER_GROUP):
                h0 = (g * HEADS_PER_GROUP + r) * HEAD_DIM
                oc_ref[0, :, h0:h0 + HEAD_DIM] = o_cmp[g][r * Q_BLOCK:(r + 1) * Q_BLOCK, :HEAD_DIM]
            sel_g = jnp.where(eligible & sel[g], 1.0, 0.0)
            sel_ref[0, g, :, 0:w] = sel_g.astype(BF)
            if w < nb:
                sel_ref[0, g, :, w:nb] = jnp.zeros((Q_BLOCK, nb - w), BF)
            any_sel = jnp.broadcast_to(jnp.max(sel_g, axis=0, keepdims=True), (8, w)).astype(BF)
            flag_ref[0, g, 0] = _dot(any_sel, pw_ref[0:w]).astype(jnp.int32)

    half = nb // 2
    if half % LANES == 0:
        in_first_half = q0 + Q_BLOCK <= half * SLC_BLOCK
        pl.when(in_first_half)(lambda: run(half))
        pl.when(jnp.logical_not(in_first_half))(lambda: run(nb))
    else:
        run(nb)


def _select(q, kc, vc):
    b, s, _ = q.shape
    ncp = kc.shape[2]
    nb = ncp // 4
    nq = s // Q_BLOCK
    pw = _flag_weights(nb)
    tri = _prefix_weights(nb)
    sds = jax.ShapeDtypeStruct
    cmp_spec = pl.BlockSpec((1, NSA_GROUPS, ncp, LANES), lambda bi, qi: (bi, 0, 0, 0))
    return pl.pallas_call(
        _select_kernel,
        grid=(b, nq),
        in_specs=[pl.BlockSpec((1, Q_BLOCK, NSA_HEADS * LANES), lambda bi, qi: (bi, qi, 0)),
                  cmp_spec, cmp_spec, _const_spec(pw.shape), _const_spec(tri.shape)],
        out_specs=[pl.BlockSpec((1, Q_BLOCK, NSA_HEADS * HEAD_DIM), lambda bi, qi: (bi, qi, 0)),
                   pl.BlockSpec((1, NSA_GROUPS, Q_BLOCK, nb), lambda bi, qi: (bi, 0, qi, 0)),
                   pl.BlockSpec((1, NSA_GROUPS, 1, 8, LANES), lambda bi, qi: (bi, 0, qi, 0, 0))],
        out_shape=[sds((b, s, NSA_HEADS * HEAD_DIM), F32), sds((b, NSA_GROUPS, s, nb), BF),
                   sds((b, NSA_GROUPS, nq, 8, LANES), jnp.int32)],
        compiler_params=_params(2),
        name="nsa_select",
    )(q, kc, vc, pw, tri)


def _gate_spread():
    gw = HEADS_PER_GROUP * HEAD_DIM
    src = jnp.arange(LANES)[:, None]
    dst = jnp.arange(3 * gw)[None, :]
    c, r = dst // gw, (dst % gw) // HEAD_DIM
    return (src == 3 * r + c).astype(BF)


def _attend_kernel(flag_ref, q_ref, g_ref, oc_ref, sel_ref, spread_ref, ks_ref, vs_ref, *rest, n_words):
    nwin = (WINDOW + ATT_Q) // KEY_TILE
    kw_refs, vw_refs = rest[:nwin], rest[nwin:2 * nwin]
    o_ref, list_ref = rest[2 * nwin], rest[2 * nwin + 1]
    bi, gi, qi = pl.program_id(0), pl.program_id(1), pl.program_id(2)
    q0 = qi * ATT_Q
    diag0 = qi * (ATT_Q // KEY_TILE)
    rq = HEADS_PER_GROUP * ATT_Q
    nb = sel_ref.shape[3]
    tiles_per_word = FLAG_BITS // TILE_BLOCKS
    tile_bits = (1 << TILE_BLOCKS) - 1
    sel_per_step = ATT_Q // Q_BLOCK

    qs = _stack_heads(q_ref, 0)
    selb = sel_ref[0, 0]
    t1 = q0 + lax.broadcasted_iota(jnp.int32, (ATT_Q, 1), 0)

    base = ((bi * NSA_GROUPS + gi) * pl.num_programs(2) + qi) * sel_per_step * n_words

    def word_body(wi, n):
        word = flag_ref[base + wi]
        for extra in range(1, sel_per_step):
            word = word | flag_ref[base + extra * n_words + wi]

        def scan_word(n):
            def tile_body(u, n):
                kt = wi * tiles_per_word + u
                active = (((word >> (u * TILE_BLOCKS)) & tile_bits) != 0) & (kt < diag0)

                @pl.when(active)
                def _():
                    list_ref[n] = kt
                return n + active.astype(jnp.int32)
            return lax.fori_loop(0, tiles_per_word, tile_body, n)
        return lax.cond(word != 0, scan_word, lambda n: n, n)

    n_active = lax.fori_loop(0, (diag0 + tiles_per_word - 1) // tiles_per_word, word_body, 0)
    for u in range(GROUP_TILES):
        list_ref[n_active + u] = 0

    jrow = lax.broadcasted_iota(jnp.int32, (nb, KEY_TILE), 0)
    kblk = lax.broadcasted_iota(jnp.int32, (nb, KEY_TILE), 1) >> 6
    kcol = lax.broadcasted_iota(jnp.int32, (1, KEY_TILE), 1)

    def masked_scores(k_all, member):
        s_ = _dot_nt(qs, k_all)
        return jnp.concatenate([jnp.where(member, s_[r * ATT_Q:(r + 1) * ATT_Q], NEG)
                                for r in range(HEADS_PER_GROUP)], axis=0)

    def tile_rows(ref, kt, n=KEY_TILE):
        return ref[0, 0, pl.ds(pl.multiple_of(kt * KEY_TILE, KEY_TILE), n), :]

    def group_body(it, carry):
        ks_t, vs_t, ex_t = [], [], []
        for u in range(GROUP_TILES):
            pos = it * GROUP_TILES + u
            kt = list_ref[pos]
            first_blk = jnp.where(pos < n_active, kt * TILE_BLOCKS, -TILE_BLOCKS - nb)
            ks_t.append(tile_rows(ks_ref, kt))
            vs_t.append(tile_rows(vs_ref, kt))
            ex_t.append(jnp.where(jrow == kblk + first_blk, 1.0, 0.0).astype(BF))
        member = _dot(selb, jnp.concatenate(ex_t, axis=1)) > 0.5
        s_ = masked_scores(jnp.concatenate(ks_t, axis=0), member)
        return _softmax_update(carry, s_, jnp.concatenate(vs_t, axis=0))

    init = (jnp.full((rq, 1), NEG, F32), jnp.zeros((rq, LANES), F32))
    carry = lax.fori_loop(0, (n_active + GROUP_TILES - 1) // GROUP_TILES, group_body, init)
    drow = lax.broadcasted_iota(jnp.int32, (nb, ATT_Q), 0)
    dcol = lax.broadcasted_iota(jnp.int32, (1, ATT_Q), 1)
    expand = jnp.where(drow == (dcol >> 6) + diag0 * TILE_BLOCKS, 1.0, 0.0).astype(BF)
    member = (_dot(selb, expand) > 0.5) & (q0 + dcol <= t1)
    kpos = []
    for i in range(nwin):
        kti = diag0 - WINDOW // KEY_TILE + i
        kpos.append(jnp.where(kti >= 0, kti * KEY_TILE, 1 << 24) + kcol)
    d = t1 - jnp.concatenate(kpos, axis=1)
    in_window = (d >= 0) & (d < WINDOW)
    s_d = masked_scores(tile_rows(ks_ref, diag0, ATT_Q), member)
    s_w = masked_scores(jnp.concatenate([r[0, 0] for r in kw_refs], axis=0), in_window)
    m_d = jnp.maximum(carry[0], jnp.max(s_d, axis=-1, keepdims=True))
    m_w = jnp.max(s_w, axis=-1, keepdims=True)
    p_d = jnp.exp(s_d - m_d).astype(BF)
    p_w = jnp.exp(s_w - m_w).astype(BF)
    acc_d = jnp.exp(carry[0] - m_d) * carry[1] + _dot(p_d, tile_rows(vs_ref, diag0, ATT_Q))
    acc_w = _dot(p_w, jnp.concatenate([r[0, 0] for r in vw_refs], axis=0))

    assert LANES == 2 * HEAD_DIM and HEADS_PER_GROUP % 2 == 0
    low_half = lax.broadcasted_iota(jnp.int32, (1, LANES), 1) < HEAD_DIM

    def heads_on_lanes(acc):
        inv = 1.0 / jnp.maximum(acc, 1e-30)
        pairs = []
        for r in range(0, HEADS_PER_GROUP, 2):
            ev, od = slice(r * ATT_Q, (r + 1) * ATT_Q), slice((r + 1) * ATT_Q, (r + 2) * ATT_Q)
            even = acc[ev] * pltpu.roll(inv[ev], HEAD_DIM, axis=1)
            odd = pltpu.roll(acc[od], HEAD_DIM, axis=1) * inv[od]
            pairs.append(jnp.where(low_half, even, odd))
        return jnp.concatenate(pairs, axis=1)

    gw = HEADS_PER_GROUP * HEAD_DIM
    gates = _split_dot(g_ref[0], spread_ref[...])
    out = (gates[:, 0:gw] * oc_ref[0] + gates[:, gw:2 * gw] * heads_on_lanes(acc_d)
           + gates[:, 2 * gw:3 * gw] * heads_on_lanes(acc_w))
    o_ref[0] = out.astype(BF)


def _attend(flags, q, gates, o_cmp, sel, ks, vs, kw, vw):
    b, s, _ = q.shape
    nb = sel.shape[3]
    n_words = nb // FLAG_BITS
    nwin = (WINDOW + ATT_Q) // KEY_TILE
    per_q = ATT_Q // KEY_TILE
    gw = HEADS_PER_GROUP * LANES
    flat = flags[:, :, :, 0, :n_words].reshape(-1)
    spread = _gate_spread()
    full = pl.BlockSpec((1, 1, s, LANES), lambda bi, gi, qi, f: (bi, gi, 0, 0))
    win = lambda i: pl.BlockSpec(
        (1, 1, KEY_TILE, LANES),
        lambda bi, gi, qi, f: (bi, gi, jnp.maximum(qi * per_q - WINDOW // KEY_TILE + i, 0), 0))
    grid_spec = pltpu.PrefetchScalarGridSpec(
        num_scalar_prefetch=1,
        grid=(b, NSA_GROUPS, s // ATT_Q),
        in_specs=[pl.BlockSpec((1, ATT_Q, gw), lambda bi, gi, qi, f: (bi, qi, gi)),
                  pl.BlockSpec((1, ATT_Q, LANES), lambda bi, gi, qi, f: (bi, qi, gi)),
                  pl.BlockSpec((1, ATT_Q, HEADS_PER_GROUP * HEAD_DIM), lambda bi, gi, qi, f: (bi, qi, gi)),
                  pl.BlockSpec((1, 1, ATT_Q, nb), lambda bi, gi, qi, f: (bi, gi, qi, 0)),
                  pl.BlockSpec(spread.shape, lambda bi, gi, qi, f: (0, 0)),
                  full, full] + [win(i) for i in range(nwin)] * 2,
        out_specs=pl.BlockSpec((1, ATT_Q, HEADS_PER_GROUP * HEAD_DIM), lambda bi, gi, qi, f: (bi, qi, gi)),
        scratch_shapes=[pltpu.SMEM((s // KEY_TILE + GROUP_TILES,), jnp.int32)],
    )
    return pl.pallas_call(
        functools.partial(_attend_kernel, n_words=n_words),
        grid_spec=grid_spec,
        out_shape=jax.ShapeDtypeStruct((b, s, NSA_HEADS * HEAD_DIM), BF),
        compiler_params=_params(3),
        name="nsa_attend",
    )(flat, q, gates, o_cmp, sel, spread, ks, vs, *([kw] * nwin), *([vw] * nwin))


def _head_indicators(d):
    ch = jnp.arange(d)[:, None] // RWKV_HEAD
    ind = (ch == jnp.arange(LANES)[None, :]).astype(BF)
    return ind, ind.T


def _rwkv_pre_kernel(x_ref, pre_ref, mu_ref, wr_ref, wk_ref, wv_ref, w0_ref, wd1_ref, wd2_ref,
                     a0_ref, wa1_ref, wa2_ref, wg1_ref, wg2_ref, kk_ref, ka_ref, ind_ref, indt_ref,
                     r_out, lw_out, k_out, v_out, kk_out, kka_out, g_out, hprev_ref, *, n_sub):
    si = pl.program_id(1)
    tm = x_ref.shape[1]

    @pl.when(si == 0)
    def _():
        hprev_ref[...] = jnp.zeros(hprev_ref.shape, F32)

    h_all = _rmsnorm(x_ref[0], pre_ref[...])
    rows = lax.broadcasted_iota(jnp.int32, (tm, 1), 0)
    xx_all = jnp.where(rows == 0, hprev_ref[7:8], pltpu.roll(h_all, 1, axis=0)) - h_all
    hprev_ref[...] = h_all[tm - 8:tm]

    def sub_tile(rs):
        h, xx = h_all[rs], xx_all[rs]
        mix = lambda i: (h + xx * mu_ref[i:i + 1]).astype(BF)
        xr, xw, xk, xv, xa, xg = [mix(i) for i in range(6)]
        r = _dot(xr, wr_ref[...])
        k = _dot(xk, wk_ref[...])
        v = _dot(xv, wv_ref[...])
        yield
        z = w0_ref[...] + _dot(jnp.tanh(_dot(xw, wd1_ref[...])).astype(BF), wd2_ref[...])
        softplus = jnp.maximum(-z, 0.0) + jnp.log(1.0 + jnp.exp(-jnp.abs(z)))
        lw = -jnp.exp(-softplus - 0.5)
        a = _sigmoid(a0_ref[...] + _dot(_dot(xa, wa1_ref[...]).astype(BF), wa2_ref[...]))
        g = _dot(_sigmoid(_dot(xg, wg1_ref[...])).astype(BF), wg2_ref[...])
        yield
        kraw = k * kk_ref[...]
        ss = _dot((kraw * kraw).astype(BF), ind_ref[...])
        yield
        inv = lax.rsqrt(jnp.maximum(ss, 1e-12))
        kk = kraw * _dot(inv.astype(BF), indt_ref[...])
        yield
        r_out[0, rs] = r.astype(BF)
        lw_out[0, rs] = lw
        k_out[0, rs] = (k * (1.0 + (a - 1.0) * ka_ref[...])).astype(BF)
        v_out[0, rs] = v.astype(BF)
        kk_out[0, rs] = kk.astype(BF)
        kka_out[0, rs] = (kk * a).astype(BF)
        g_out[0, rs] = g.astype(BF)

    _round_robin(sub_tile(slice(i * tm // n_sub, (i + 1) * tm // n_sub)) for i in range(n_sub))


def _rwkv_pre(x, pre_g, mu, w_r, w_k, w_v, w0, w_dec1, w_dec2, a0, w_a1, w_a2, w_g1, w_g2,
              k_k, k_a, ind, indt, *, tm=512, n_sub=1):
    b, s, d = x.shape
    row = pl.BlockSpec((1, tm, d), lambda bi, si: (bi, si, 0))
    vec = lambda a: a.reshape(1, d)
    ops = [x, vec(pre_g), mu, w_r.astype(BF), w_k.astype(BF), w_v.astype(BF), vec(w0),
           w_dec1.astype(BF), w_dec2.astype(BF), vec(a0), w_a1.astype(BF), w_a2.astype(BF),
           w_g1.astype(BF), w_g2.astype(BF), vec(k_k), vec(k_a), ind, indt]
    return pl.pallas_call(
        functools.partial(_rwkv_pre_kernel, n_sub=n_sub),
        grid=(b, s // tm),
        in_specs=[row] + [_const_spec(o.shape) for o in ops[1:]],
        out_specs=[row] * 7,
        out_shape=[jax.ShapeDtypeStruct((b, s, d), F32 if i == 1 else BF) for i in range(7)],
        scratch_shapes=[pltpu.VMEM((8, d), F32)],
        compiler_params=_params(2),
        name="rwkv_pre",
    )(*ops)


def _rwkv_scan_kernel(r_ref, lw_ref, k_ref, v_ref, kk_ref, kka_ref, y_ref, state_ref):
    c = RWKV_CHUNK
    slab = SLAB_HEADS * RWKV_HEAD
    n_slab = r_ref.shape[2] // slab
    rows = SLAB_HEADS * c
    subs = [slice(i * c, (i + 1) * c) for i in range(r_ref.shape[1] // c)]

    @pl.when(pl.program_id(1) == 0)
    def _():
        state_ref[...] = jnp.zeros(state_ref.shape, F32)

    ri = lax.broadcasted_iota(jnp.int32, (rows, 1), 0)
    ci = lax.broadcasted_iota(jnp.int32, (1, rows), 1)
    li = lax.broadcasted_iota(jnp.int32, (1, slab), 1)
    c_shift = c.bit_length() - 1
    h_shift = RWKV_HEAD.bit_length() - 1
    assert c == 1 << c_shift and RWKV_HEAD == 1 << h_shift
    head_match = (ri >> c_shift) == (li >> h_shift)
    t_r, t_c = ri & (c - 1), ci & (c - 1)
    strict = t_c < t_r
    incl = t_c <= t_r
    eye = jnp.where(ri == ci, 1.0, 0.0)
    tri = jnp.where(lax.broadcasted_iota(jnp.int32, (c, c), 1) <= lax.broadcasted_iota(jnp.int32, (c, c), 0),
                    1.0, 0.0).astype(BF)
    state_match = (lax.broadcasted_iota(jnp.int32, (slab, 1), 0) >> h_shift) == (li >> h_shift)

    incl2 = jnp.concatenate([incl, incl], axis=1)

    def block_diag(x):
        return jnp.where(head_match, jnp.concatenate([x] * SLAB_HEADS, axis=0), 0.0)

    def fold(x):
        out = x[0:c]
        for h in range(1, SLAB_HEADS):
            out = out + x[h * c:(h + 1) * c]
        return out

    bf = lambda x: x.astype(BF)
    slabs = [slice(hs * slab, (hs + 1) * slab) for hs in range(n_slab)]

    def prepare(rs, sl):
        lw = lw_ref[0, rs, sl]
        hi = bf(lw)
        rem = lw - hi.astype(F32)
        mid = bf(rem)
        lo = bf(rem - mid.astype(F32))
        cw = _dot(tri, hi) + _dot(tri, mid) + _dot(tri, lo)
        w_in = jnp.exp(cw)
        w_inv = jnp.exp(-cw)
        w_prev = jnp.exp(cw - lw)
        w_end = w_in[c - 1:c]

        a_t = -kk_ref[0, rs, sl].astype(F32) * w_prev
        b_t = kka_ref[0, rs, sl].astype(F32) * w_inv
        k_t = k_ref[0, rs, sl].astype(F32) * w_inv
        r_t = r_ref[0, rs, sl].astype(F32) * w_in
        return a_t, b_t, k_t, r_t, w_end

    def interactions(a_t, b_t, k_t, r_t):
        lhs = bf(jnp.concatenate([block_diag(a_t), block_diag(r_t)], axis=0))
        rhs = bf(jnp.concatenate([block_diag(b_t), block_diag(k_t)], axis=0))
        big = _dot_nt(lhs, rhs)
        a_ab = jnp.where(strict, big[:rows, :rows], 0.0)
        a_ak = jnp.where(strict, big[:rows, rows:], 0.0)
        a_r = jnp.where(incl2, big[rows:], 0.0)
        return a_ab, bf(a_ak), bf(a_r)

    every = range(n_slab)
    units = [(rs, sl) for rs in subs for sl in slabs]
    prep = [prepare(rs, sl) for rs, sl in units]
    inter = [interactions(*p[:4]) for p in prep]
    vs = [v_ref[0, rs, sl] for rs, sl in units]
    vbd = [bf(block_diag(v)) for v in vs]
    akv = [_dot(inter[u][1], vbd[u]) for u in range(len(units))]

    levels = c.bit_length() - 2
    assert levels >= 1
    invs = [eye + x[0] for x in inter]
    pws = [_dot(bf(x[0]), bf(x[0])) for x in inter]
    for _ in range(levels - 1):
        both = [_dot(bf(pw), bf(jnp.concatenate([pw, inv], axis=1))) for pw, inv in zip(pws, invs)]
        pws = [x[:, :rows] for x in both]
        invs = [inv + x[:, rows:] for inv, x in zip(invs, both)]
    invs = [bf(inv + _dot(bf(pw), bf(inv))) for pw, inv in zip(pws, invs)]

    states = [state_ref[hs] for hs in every]
    for si, rs in enumerate(subs):
        us = [si * n_slab + hs for hs in every]
        ar = [_dot_nt(bf(jnp.concatenate([prep[u][0], prep[u][3]], axis=0)), bf(states[hs]))
              for hs, u in zip(every, us)]
        u_bd = [_dot(invs[u], bf(block_diag(ar[hs][:c]) + akv[u])) for hs, u in zip(every, us)]
        y_bd = [block_diag(ar[hs][c:]) + _dot(inter[u][2], jnp.concatenate([bf(u_bd[hs]), vbd[u]], axis=0))
                for hs, u in zip(every, us)]
        for hs, sl in enumerate(slabs):
            y_ref[0, rs, sl] = fold(y_bd[hs])
        new_states = []
        for hs, u in zip(every, us):
            _, b_t, k_t, _, w_end = prep[u]
            uv = jnp.concatenate([bf(fold(u_bd[hs])), bf(vs[u])], axis=0)
            bk = jnp.concatenate([bf(b_t * w_end), bf(k_t * w_end)], axis=0)
            new_states.append(jnp.where(state_match, states[hs] * w_end + _dot_tn(uv, bk), 0.0))
        states = new_states
    for hs in every:
        state_ref[hs] = states[hs]


def _rwkv_scan(r, lw, k, v, kk, kka):
    b, s, d = r.shape
    c = RWKV_CHUNK * SCAN_SUB
    slab = SLAB_HEADS * RWKV_HEAD
    blk = pl.BlockSpec((1, c, d), lambda bi, ci: (bi, ci, 0))
    return pl.pallas_call(
        _rwkv_scan_kernel,
        grid=(b, s // c),
        in_specs=[blk] * 6,
        out_specs=blk,
        out_shape=jax.ShapeDtypeStruct((b, s, d), F32),
        scratch_shapes=[pltpu.VMEM((d // slab, slab, slab), F32)],
        compiler_params=_params(2),
        name="rwkv_scan",
    )(r, lw, k, v, kk, kka)


def _rwkv_post_kernel(x_ref, y_ref, r_ref, k_ref, v_ref, g_ref, wo_ref, lng_ref, lnb_ref, rk_ref,
                      post_ref, ind_ref, indt_ref, o_ref, *, n_sub):
    ind, indt = ind_ref[...], indt_ref[...]
    inv_n = 1.0 / RWKV_HEAD
    head_sum = lambda a: _dot(a.astype(BF), ind)
    spread = lambda a: _dot(a.astype(BF), indt)
    tm = x_ref.shape[0]

    def sub_tile(rs):
        f32 = lambda ref: ref[rs].astype(F32)
        y = y_ref[rs]
        sums = head_sum(y)
        rk_sum = head_sum(f32(r_ref) * f32(k_ref) * rk_ref[...])
        yield
        mean = _split_dot(sums * inv_n, indt)
        bonus = spread(rk_sum) * f32(v_ref)
        yield
        yc = y - mean
        var = head_sum(yc * yc) * inv_n
        yield
        rstd = spread(lax.rsqrt(var + LNX_EPS))
        yield
        yn = yc * rstd * lng_ref[...] + lnb_ref[...]
        out = _dot(((yn + bonus) * f32(g_ref)).astype(BF), wo_ref[...])
        yield
        o_ref[rs] = x_ref[rs] + _rmsnorm(out, post_ref[...])

    _round_robin(sub_tile(slice(i * tm // n_sub, (i + 1) * tm // n_sub)) for i in range(n_sub))


def _rwkv_post(x2, y, r, k, v, g, w_o, lnx_g, lnx_b, r_k, post_g, ind, indt, *, tm=512, n_sub=2):
    t, d = x2.shape
    row = pl.BlockSpec((tm, d), lambda i: (i, 0))
    vec = lambda a: a.reshape(1, d)
    ops = [x2, y, r, k, v, g, w_o.astype(BF), vec(lnx_g), vec(lnx_b), vec(r_k), vec(post_g), ind, indt]
    return pl.pallas_call(
        functools.partial(_rwkv_post_kernel, n_sub=n_sub),
        grid=(t // tm,),
        in_specs=[row] * 6 + [_const_spec(o.shape) for o in ops[6:]],
        out_specs=row,
        out_shape=jax.ShapeDtypeStruct((t, d), F32),
        compiler_params=_params(1),
        name="rwkv_post",
    )(*ops)


def _layer0_mixer(x, pre_g, w_in, pe_k, w1_k, w2_k, pe_v, w1_v, w2_v, conv_w, conv_b):
    b, s, _ = x.shape
    q, gates, kcr, vcr, ks, vs, kw, vw, o_conv = _l0_in(x, pre_g, _l0_in_weight(w_in), conv_w, conv_b)
    kc = _compress(kcr, _cmp_weights(pe_k, w1_k, w2_k), is_key=True)
    vc = _compress(vcr, _cmp_weights(pe_v, w1_v, w2_v), is_key=False)
    o_cmp, sel, flags = _select(q, kc, vc)
    o_nsa = _attend(flags, q, gates, o_cmp, sel, ks, vs, kw, vw)
    return o_nsa.reshape(b * s, -1), o_conv.reshape(b * s, -1)


def _layer1_mixer(x, pre_g, post_g, mu, w_r, w_k, w_v, w_o, w0, w_dec1, w_dec2, a0, w_a1, w_a2,
                  w_g1, w_g2, k_k, k_a, r_k, lnx_g, lnx_b):
    b, s, d = x.shape
    ind, indt = _head_indicators(d)
    r, lw, k, v, kk, kka, g = _rwkv_pre(x, pre_g, mu, w_r, w_k, w_v, w0, w_dec1, w_dec2, a0,
                                        w_a1, w_a2, w_g1, w_g2, k_k, k_a, ind, indt)
    y = _rwkv_scan(r, lw, k, v, kk, kka)
    f = lambda a: a.reshape(b * s, d)
    return _rwkv_post(f(x), f(y), f(r), f(k), f(v), f(g), w_o, lnx_g, lnx_b, r_k, post_g,
                      ind, indt).reshape(b, s, d)


def kernel(x, l0_ffn1_pre_g, l0_ffn1_post_g, l0_ffn1_w_gate, l0_ffn1_w_up, l0_ffn1_w_down, l0_mix_pre_g, l0_mix_post_g, l0_w_in, l0_cmp_pe_k, l0_cmp_w1_k, l0_cmp_w2_k, l0_cmp_pe_v, l0_cmp_w1_v, l0_cmp_w2_v, l0_conv_w, l0_conv_b, l0_w_out, l0_ffn2_pre_g, l0_ffn2_post_g, l0_ffn2_w_gate, l0_ffn2_w_up, l0_ffn2_w_down, l1_ffn1_pre_g, l1_ffn1_post_g, l1_ffn1_w_gate, l1_ffn1_w_up, l1_ffn1_w_down, l1_mix_pre_g, l1_mix_post_g, l1_mu, l1_w_r, l1_w_k, l1_w_v, l1_w_o, l1_w0, l1_w_dec1, l1_w_dec2, l1_a0, l1_w_a1, l1_w_a2, l1_w_g1, l1_w_g2, l1_k_k, l1_k_a, l1_r_k, l1_lnx_g, l1_lnx_b, l1_ffn2_pre_g, l1_ffn2_post_g, l1_ffn2_w_gate, l1_ffn2_w_up, l1_ffn2_w_down):
    b, s, d = x.shape
    ffn = lambda a, *w, **kw: _ffn(a.reshape(b * s, d), *w, **kw).reshape(b, s, d)
    x = ffn(x, l0_ffn1_pre_g, l0_ffn1_post_g, l0_ffn1_w_gate, l0_ffn1_w_up, l0_ffn1_w_down)
    o_nsa, o_conv = _layer0_mixer(x, l0_mix_pre_g, l0_w_in, l0_cmp_pe_k, l0_cmp_w1_k, l0_cmp_w2_k,
                                  l0_cmp_pe_v, l0_cmp_w1_v, l0_cmp_w2_v, l0_conv_w, l0_conv_b)
    x = ffn(x, l0_ffn2_pre_g, l0_ffn2_post_g, l0_ffn2_w_gate, l0_ffn2_w_up, l0_ffn2_w_down,
            mixer=(o_nsa, o_conv, l0_w_out, l0_mix_post_g))
    x = ffn(x, l1_ffn1_pre_g, l1_ffn1_post_g, l1_ffn1_w_gate, l1_ffn1_w_up, l1_ffn1_w_down)
    x = _layer1_mixer(x, l1_mix_pre_g, l1_mix_post_g, l1_mu, l1_w_r, l1_w_k, l1_w_v, l1_w_o, l1_w0,
                      l1_w_dec1, l1_w_dec2, l1_a0, l1_w_a1, l1_w_a2, l1_w_g1, l1_w_g2, l1_k_k, l1_k_a,
                      l1_r_k, l1_lnx_g, l1_lnx_b)
    x = ffn(x, l1_ffn2_pre_g, l1_ffn2_post_g, l1_ffn2_w_gate, l1_ffn2_w_up, l1_ffn2_w_down)
    return x
```

```python
import functools

import jax
import jax.numpy as jnp
from jax import lax
from jax.experimental import pallas as pl
from jax.experimental.pallas import tpu as pltpu

BF = jnp.bfloat16
F32 = jnp.float32

EPS = 1e-6
LNX_EPS = 64e-5
HEAD_DIM = 64
NSA_HEADS = 8
NSA_GROUPS = 2
HEADS_PER_GROUP = NSA_HEADS // NSA_GROUPS
CMP_STRIDE = 16
CMP_BLOCK = 32
CMP_HIDDEN = 128
SLC_BLOCK = 64
SLC_TOP_N = 16
WINDOW = 512
Q_BLOCK = 128
CONV_WIDTH = 512
RWKV_HEAD = 64
RWKV_CHUNK = 64
SLAB_HEADS = 2
SCAN_SUB = 2
LANES = 128
MXU_DIM = 256
POS_SHIFT = 7
POS_SPLIT = 1 << POS_SHIFT
NEG = -1e30
VMEM_LIMIT = 56 * 1024 * 1024

NT_DIMS = (((1,), (1,)), ((), ()))
TN_DIMS = (((0,), (0,)), ((), ()))


def _dot(a, b):
    return jnp.dot(a, b, preferred_element_type=F32)


def _dot_nt(a, b):
    return lax.dot_general(a, b, NT_DIMS, preferred_element_type=F32)


def _dot_tn(a, b):
    return lax.dot_general(a, b, TN_DIMS, preferred_element_type=F32)


def _split_dot(x, w):
    hi = x.astype(BF)
    lo = (x - hi.astype(F32)).astype(BF)
    return _dot(hi, w) + _dot(lo, w)


def _rmsnorm(x, g):
    ms = jnp.mean(x * x, axis=-1, keepdims=True)
    return x * lax.rsqrt(ms + EPS) * g


def _sigmoid(x):
    return 1.0 / (1.0 + jnp.exp(-x))


def _round_robin(stage_generators):
    pending = list(stage_generators)
    while pending:
        still = []
        for gen in pending:
            try:
                next(gen)
                still.append(gen)
            except StopIteration:
                pass
        pending = still


def _const_spec(shape):
    zeros = (0,) * len(shape)
    return pl.BlockSpec(shape, lambda *_: zeros)


def _params(n_grid):
    return pltpu.CompilerParams(dimension_semantics=("arbitrary",) * n_grid,
                                vmem_limit_bytes=VMEM_LIMIT)


def _ffn_kernel(*refs, ff_chunk, n_sub, after_mixer):
    if after_mixer:
        x_ref, a_ref, c_ref, wa_ref, wc_ref, mixpost_ref = refs[:6]
        refs = refs[:1] + refs[6:]
    x_ref, pre_ref, post_ref, wg_ref, wu_ref, wd_ref, o_ref = refs

    def residual_rows(rs):
        if not after_mixer:
            return x_ref[rs]
        y = _dot(a_ref[rs], wa_ref[...]) + _dot(c_ref[rs], wc_ref[...])
        return x_ref[rs] + _rmsnorm(y, mixpost_ref[...])

    sub = x_ref.shape[0] // n_sub
    dff = wg_ref.shape[1]
    chunks = [(c0, min(c0 + ff_chunk, dff)) for c0 in range(0, dff, ff_chunk)]
    rows = [slice(i * sub, (i + 1) * sub) for i in range(n_sub)]
    xs, hs, accs = [], [], []
    for ci, (c0, c1) in enumerate(chunks):
        gu = []
        for i, rs in enumerate(rows):
            if ci == 0:
                xs.append(residual_rows(rs))
                hs.append(_rmsnorm(xs[i], pre_ref[...]).astype(BF))
                accs.append(jnp.zeros(xs[i].shape, F32))
            gu.append((_dot(hs[i], wg_ref[:, c0:c1]), _dot(hs[i], wu_ref[:, c0:c1])))
        for i, (g, u) in enumerate(gu):
            a = (g * _sigmoid(g) * u).astype(BF)
            accs[i] = accs[i] + _dot(a, wd_ref[c0:c1, :])
    for i, rs in enumerate(rows):
        o_ref[rs] = xs[i] + 0.5 * _rmsnorm(accs[i], post_ref[...])


def _ffn(x2, pre_g, post_g, w_gate, w_up, w_down, *, mixer=None, tm=1024, n_sub=2):
    t, d = x2.shape
    dff = w_gate.shape[1]
    ff_chunk = MXU_DIM * pl.cdiv(pl.cdiv(dff, MXU_DIM), 2)
    row = lambda w: pl.BlockSpec((tm, w), lambda i: (i, 0))
    weight = lambda shape: pl.BlockSpec(shape, lambda i: (0, 0), pipeline_mode=pl.Buffered(1))
    ops, specs = [x2], [row(d)]
    if mixer is not None:
        o_nsa, o_conv, w_out, mix_post_g = mixer
        na, nc = o_nsa.shape[1], o_conv.shape[1]
        ops += [o_nsa, o_conv, w_out[:na].astype(BF), w_out[na:].astype(BF), mix_post_g.reshape(1, d)]
        specs += [row(na), row(nc), weight((na, d)), weight((nc, d)), _const_spec((1, d))]
    ops += [pre_g.reshape(1, d), post_g.reshape(1, d), w_gate.astype(BF), w_up.astype(BF), w_down.astype(BF)]
    specs += [_const_spec((1, d)), _const_spec((1, d)), weight((d, dff)), weight((d, dff)), weight((dff, d))]
    return pl.pallas_call(
        functools.partial(_ffn_kernel, ff_chunk=ff_chunk, n_sub=n_sub, after_mixer=mixer is not None),
        grid=(t // tm,),
        in_specs=specs,
        out_specs=row(d),
        out_shape=jax.ShapeDtypeStruct((t, d), F32),
        compiler_params=_params(1),
        name="ffn",
    )(*ops)


_C_Q = 0
_C_G = _C_Q + NSA_HEADS * LANES
_C_KC = _C_G + NSA_GROUPS * LANES
_C_VC = _C_KC + LANES
_C_KS = _C_VC + LANES
_C_CONV = _C_KS + 4 * NSA_GROUPS * LANES
_C_END = _C_CONV + 3 * CONV_WIDTH


def _l0_in_weight(w_in):
    d = w_in.shape[0]
    nsa_w = NSA_HEADS * HEAD_DIM
    kv_w = NSA_GROUPS * HEAD_DIM
    o = 0
    q = w_in[:, o:o + nsa_w]; o += nsa_w
    g = w_in[:, o:o + 3 * NSA_HEADS]; o += 3 * NSA_HEADS
    kvs = []
    for _ in range(6):
        kvs.append(w_in[:, o:o + kv_w]); o += kv_w
    conv = w_in[:, o:]
    zpad = lambda w, n: jnp.pad(w, ((0, 0), (0, n - w.shape[1])))
    cols = [zpad(q[:, h * HEAD_DIM:(h + 1) * HEAD_DIM], LANES) for h in range(NSA_HEADS)]
    gpg = 3 * HEADS_PER_GROUP
    cols += [zpad(g[:, i * gpg:(i + 1) * gpg], LANES) for i in range(NSA_GROUPS)]
    cols += [kvs[0], kvs[1]]
    for w in kvs[2:]:
        cols += [zpad(w[:, i * HEAD_DIM:(i + 1) * HEAD_DIM], LANES) for i in range(NSA_GROUPS)]
    cols.append(conv)
    w = jnp.concatenate(cols, axis=1)
    assert w.shape == (d, _C_END)
    return w.astype(BF)


def _l0_in_kernel(x_ref, pre_ref, w_ref, cw_ref, cb_ref,
                  q_ref, g_ref, kcr_ref, vcr_ref, ks_ref, vs_ref, kw_ref, vw_ref, oc_ref,
                  zprev_ref):
    si = pl.program_id(1)
    tm = x_ref.shape[1]
    h = _rmsnorm(x_ref[0], pre_ref[...]).astype(BF)
    lane = lax.broadcasted_iota(jnp.int32, (1, LANES), 1)
    pos = si * tm + lax.broadcasted_iota(jnp.int32, (tm, 1), 0)
    pos_hi = (pos >> POS_SHIFT).astype(F32)
    pos_lo = (pos & (POS_SPLIT - 1)).astype(F32)

    wide = 4 * LANES

    def project(c0):
        res = _dot(h, w_ref[:, c0:c0 + wide])
        return [res[:, j * LANES:(j + 1) * LANES] for j in range(4)]

    for hd0 in range(0, NSA_HEADS, 4):
        for j, qh in enumerate(project(_C_Q + hd0 * LANES)):
            hd = hd0 + j
            slope = 2.0 ** (-(hd + 1))
            qh = qh * (HEAD_DIM ** -0.5)
            qh = jnp.where(lane == HEAD_DIM, POS_SPLIT * slope, jnp.where(lane == HEAD_DIM + 1, slope, qh))
            q_ref[0, :, hd * LANES:(hd + 1) * LANES] = qh.astype(BF)

    assert _C_KC == _C_G + 2 * LANES and _C_VC == _C_KC + LANES and NSA_GROUPS == 2
    g0, g1, kcr, vcr = project(_C_G)
    g_ref[0] = _sigmoid(jnp.concatenate([g0, g1], axis=1))
    kcr_ref[0] = kcr
    vcr_ref[0] = vcr

    slabs = project(_C_KS) + project(_C_KS + wide)
    for n, (ref, is_key) in enumerate(((ks_ref, True), (vs_ref, False), (kw_ref, True), (vw_ref, False))):
        for gi in range(NSA_GROUPS):
            t = slabs[n * NSA_GROUPS + gi]
            if is_key:
                t = jnp.where(lane == HEAD_DIM, pos_hi, jnp.where(lane == HEAD_DIM + 1, pos_lo, t))
            else:
                t = jnp.where(lane >= HEAD_DIM, 1.0, t)
            ref[0, gi] = t.astype(BF)

    @pl.when(si == 0)
    def _():
        zprev_ref[...] = jnp.zeros(zprev_ref.shape, F32)

    cb = _dot(h, w_ref[:, _C_CONV:_C_CONV + CONV_WIDTH])
    cc = _dot(h, w_ref[:, _C_CONV + CONV_WIDTH:_C_CONV + 2 * CONV_WIDTH])
    cu = _dot(h, w_ref[:, _C_CONV + 2 * CONV_WIDTH:_C_CONV + 3 * CONV_WIDTH])
    z = cc * cu
    prev = zprev_ref[...]
    rows = lax.broadcasted_iota(jnp.int32, (tm, 1), 0)
    zm1 = jnp.where(rows == 0, prev[7:8], pltpu.roll(z, 1, axis=0))
    zm2 = jnp.where(rows == 0, prev[6:7], jnp.where(rows == 1, prev[7:8], pltpu.roll(z, 2, axis=0)))
    y = cw_ref[0:1] * zm2 + cw_ref[1:2] * zm1 + cw_ref[2:3] * z
    oc_ref[0] = (cb * (y + cb_ref[...])).astype(BF)
    zprev_ref[...] = z[tm - 8:tm]


def _l0_in(x, pre_g, w_all, conv_w, conv_b, *, tm=512):
    b, s, d = x.shape
    row = lambda w: pl.BlockSpec((1, tm, w), lambda bi, si: (bi, si, 0))
    grp = pl.BlockSpec((1, NSA_GROUPS, tm, LANES), lambda bi, si: (bi, 0, si, 0))
    sds = jax.ShapeDtypeStruct
    kv_shape = sds((b, NSA_GROUPS, s, LANES), BF)
    return pl.pallas_call(
        _l0_in_kernel,
        grid=(b, s // tm),
        in_specs=[row(d), _const_spec((1, d)), _const_spec(w_all.shape),
                  _const_spec(conv_w.shape), _const_spec((1, CONV_WIDTH))],
        out_specs=[row(NSA_HEADS * LANES), row(NSA_GROUPS * LANES), row(LANES), row(LANES),
                   grp, grp, grp, grp, row(CONV_WIDTH)],
        out_shape=[sds((b, s, NSA_HEADS * LANES), BF), sds((b, s, NSA_GROUPS * LANES), F32),
                   sds((b, s, LANES), F32), sds((b, s, LANES), F32),
                   kv_shape, kv_shape, kv_shape, kv_shape, sds((b, s, CONV_WIDTH), BF)],
        scratch_shapes=[pltpu.VMEM((8, CONV_WIDTH), F32)],
        compiler_params=_params(2),
        name="l0_in",
    )(x, pre_g.reshape(1, d), w_all, conv_w, conv_b.reshape(1, CONV_WIDTH))


def _cmp_weights(pe, w1, w2):
    half = CMP_STRIDE * HEAD_DIM
    def expand(w):
        w = w.reshape(CMP_STRIDE, 1, HEAD_DIM, 1, CMP_HIDDEN)
        same_group = jnp.eye(NSA_GROUPS, dtype=F32).reshape(1, NSA_GROUPS, 1, NSA_GROUPS, 1)
        return (w * same_group).reshape(CMP_STRIDE * NSA_GROUPS * HEAD_DIM, NSA_GROUPS * CMP_HIDDEN).astype(BF)
    pe_row = lambda p: jnp.tile(p, (1, NSA_GROUPS)).reshape(1, CMP_STRIDE * NSA_GROUPS * HEAD_DIM)
    w2p = jnp.pad(w2, ((0, 0), (0, LANES - HEAD_DIM))).astype(BF)
    return (pe_row(pe[:CMP_STRIDE]), pe_row(pe[CMP_STRIDE:]), expand(w1[:half]), expand(w1[half:]), w2p)


def _gelu_tanh(x):
    return 0.5 * x * (1.0 + jnp.tanh(0.7978845608028654 * (x + 0.044715 * (x * x * x))))


def _cmp_kernel(x_ref, pet_ref, peb_ref, wt_ref, wb_ref, w2_ref, o_ref, *, is_key):
    nb = x_ref.shape[1]
    hw = x_ref.shape[2] // 4
    top, bot = [], []
    for c in range(4):
        xc = x_ref[0, :, c * hw:(c + 1) * hw]
        top.append(_dot((xc + pet_ref[...]).astype(BF), wt_ref[...]))
        bot.append(_dot((xc + peb_ref[...]).astype(BF), wb_ref[...]))
    lane = lax.broadcasted_iota(jnp.int32, (1, LANES), 1)
    j = lax.broadcasted_iota(jnp.int32, (nb, 1), 0)
    for c in range(4):
        nxt = bot[c + 1] if c < 3 else pltpu.roll(bot[0], nb - 1, axis=0)
        hid = _gelu_tanh(top[c] + nxt)
        n = 4 * j + c
        end = n * CMP_STRIDE + (CMP_BLOCK - 1)
        exists = n < 4 * nb - 1
        for gi in range(NSA_GROUPS):
            t = _dot(hid[:, gi * CMP_HIDDEN:(gi + 1) * CMP_HIDDEN].astype(BF), w2_ref[...])
            t = jnp.where(exists, t, 0.0)
            if is_key:
                t = jnp.where(lane == HEAD_DIM, (end >> POS_SHIFT).astype(F32),
                              jnp.where(lane == HEAD_DIM + 1, (end & (POS_SPLIT - 1)).astype(F32), t))
            else:
                t = jnp.where(lane == HEAD_DIM, 1.0, t)
            o_ref[0, gi, c * nb:(c + 1) * nb, :] = t.astype(BF)


def _compress(raw, weights, *, is_key):
    b, s, _ = raw.shape
    nb = s // (4 * CMP_STRIDE)
    x = raw.reshape(b, nb, 4 * CMP_STRIDE * LANES)
    pet, peb, wt, wb, w2p = weights
    return pl.pallas_call(
        functools.partial(_cmp_kernel, is_key=is_key),
        grid=(b,),
        in_specs=[pl.BlockSpec((1, nb, x.shape[2]), lambda bi: (bi, 0, 0)),
                  _const_spec(pet.shape), _const_spec(peb.shape), _const_spec(wt.shape),
                  _const_spec(wb.shape), _const_spec(w2p.shape)],
        out_specs=pl.BlockSpec((1, NSA_GROUPS, 4 * nb, LANES), lambda bi: (bi, 0, 0, 0)),
        out_shape=jax.ShapeDtypeStruct((b, NSA_GROUPS, 4 * nb, LANES), BF),
        compiler_params=_params(1),
        name="compress",
    )(x, pet, peb, wt, wb, w2p)


def _softmax_update(carry, s, v):
    m, acc = carry
    m_new = jnp.maximum(m, jnp.max(s, axis=-1, keepdims=True))
    p = jnp.exp(s - m_new)
    acc = jnp.exp(m - m_new) * acc + _dot(p.astype(BF), v)
    return m_new, acc


SEL_SUB = 2
N_FORCED = 3
FLAG_BITS = 16
ATT_Q = 256
KEY_TILE = 128
TILE_BLOCKS = KEY_TILE // SLC_BLOCK
GROUP_TILES = 4


def _stack_heads(q_ref, g):
    base = g * HEADS_PER_GROUP
    return jnp.concatenate([q_ref[0, :, (base + r) * LANES:(base + r + 1) * LANES]
                            for r in range(HEADS_PER_GROUP)], axis=0)


def _flag_weights(nb):
    j = jnp.arange(nb)[:, None]
    w = jnp.arange(LANES)[None, :]
    return jnp.where(j // FLAG_BITS == w, 2.0 ** (j % FLAG_BITS), 0.0).astype(BF)


def _prefix_weights(nb):
    return (jnp.arange(nb)[:, None] <= jnp.arange(nb)[None, :]).astype(BF)


def _select_kernel(q_ref, kc_ref, vc_ref, pw_ref, tri_ref, oc_ref, sel_ref, flag_ref):
    qi = pl.program_id(1)
    q0 = qi * (SEL_SUB * Q_BLOCK)
    rq = HEADS_PER_GROUP * Q_BLOCK
    nb = kc_ref.shape[2] // 4
    subs = range(SEL_SUB)
    units = [(sb, g) for sb in subs for g in range(NSA_GROUPS)]
    rows = [slice(sb * Q_BLOCK, (sb + 1) * Q_BLOCK) for sb in subs]
    t1 = [q0 + sb * Q_BLOCK + lax.broadcasted_iota(jnp.int32, (Q_BLOCK, 1), 0) for sb in subs]
    t4 = [q0 + sb * Q_BLOCK + (lax.broadcasted_iota(jnp.int32, (rq, 1), 0) & (Q_BLOCK - 1))
          for sb in subs]
    blk_t = [t >> 6 for t in t1]

    def stacked_queries(sb, g):
        base = g * HEADS_PER_GROUP
        return jnp.concatenate([q_ref[0, rows[sb], (base + r) * LANES:(base + r + 1) * LANES]
                                for r in range(HEADS_PER_GROUP)], axis=0)

    def run(w):
        w_shift = w.bit_length() - 1
        assert w == 1 << w_shift and w % LANES == 0
        col = lax.broadcasted_iota(jnp.int32, (1, 4 * w), 1)
        cmp_end = (4 * (col & (w - 1)) + (col >> w_shift)) * CMP_STRIDE + (CMP_BLOCK - 1)
        jl = lax.broadcasted_iota(jnp.int32, (1, w), 1)
        valid = [cmp_end <= t4[sb] for sb in subs]
        forced = [(jl == 0) | (jl == blk_t[sb]) | (jl == blk_t[sb] - 1) for sb in subs]
        eligible = [jl <= blk_t[sb] for sb in subs]
        pickable = [eligible[sb] & ~forced[sb] for sb in subs]
        n_pick = min(SLC_TOP_N, nb) - N_FORCED

        def columns(ref, g):
            if w == nb:
                return ref[0, g]
            return jnp.concatenate([ref[0, g, c * nb:c * nb + w] for c in range(4)], axis=0)

        def importance(p):
            ps = p[0:Q_BLOCK]
            for r in range(1, HEADS_PER_GROUP):
                ps = ps + p[r * Q_BLOCK:(r + 1) * Q_BLOCK]
            parts = [ps[:, c * w:(c + 1) * w] for c in range(4)]
            prev = jnp.where(jl == 0, 0.0, pltpu.roll(parts[3], 1, axis=1))
            return (parts[0] + parts[1] + parts[2] + parts[3]) - 0.5 * parts[3] + 0.5 * prev

        s = [jnp.where(valid[sb], _dot_nt(stacked_queries(sb, g), columns(kc_ref, g)), NEG) for sb, g in units]
        m = [jnp.max(x, axis=-1, keepdims=True) for x in s]
        m = [jnp.where(x <= NEG, 0.0, x) for x in m]
        p = [jnp.exp(x - mx) for x, mx in zip(s, m)]
        p = [x / jnp.maximum(jnp.sum(x, axis=-1, keepdims=True), 1e-30) for x in p]
        o_cmp = [_dot(x.astype(BF), columns(vc_ref, g)) for x, (sb, g) in zip(p, units)]
        score = [jnp.where(pickable[sb], importance(x), NEG) for x, (sb, g) in zip(p, units)]
        work = score
        for _ in range(n_pick - 1):
            hit = [jl == jnp.argmax(x, axis=-1, keepdims=True).astype(jnp.int32) for x in work]
            work = [jnp.where(h, 3.0 * NEG, x) for h, x in zip(hit, work)]
        thr = [jnp.max(x, axis=-1, keepdims=True) for x in work]
        above = [x > t for x, t in zip(score, thr)]
        tie = [x == t for x, t in zip(score, thr)]
        n_above = [jnp.sum(jnp.where(x, 1.0, 0.0), axis=-1, keepdims=True) for x in above]
        tie_rank = [_dot(jnp.where(x, 1.0, 0.0).astype(BF), tri_ref[0:w, 0:w]) for x in tie]
        for u, (sb, g) in enumerate(units):
            sel = forced[sb] | above[u] | (tie[u] & (tie_rank[u] <= n_pick - n_above[u]))
            for r in range(HEADS_PER_GROUP):
                h0 = (g * HEADS_PER_GROUP + r) * HEAD_DIM
                oc_ref[0, rows[sb], h0:h0 + HEAD_DIM] = o_cmp[u][r * Q_BLOCK:(r + 1) * Q_BLOCK, :HEAD_DIM]
            sel_u = jnp.where(eligible[sb] & sel, 1.0, 0.0)
            sel_ref[0, g, rows[sb], 0:w] = sel_u.astype(BF)
            if w < nb:
                sel_ref[0, g, rows[sb], w:nb] = jnp.zeros((Q_BLOCK, nb - w), BF)
            any_sel = jnp.broadcast_to(jnp.max(sel_u, axis=0, keepdims=True), (8, w)).astype(BF)
            flag_ref[0, g, sb] = _dot(any_sel, pw_ref[0:w]).astype(jnp.int32)

    half = nb // 2
    if half % LANES == 0:
        in_first_half = q0 + SEL_SUB * Q_BLOCK <= half * SLC_BLOCK
        pl.when(in_first_half)(lambda: run(half))
        pl.when(jnp.logical_not(in_first_half))(lambda: run(nb))
    else:
        run(nb)


def _select(q, kc, vc):
    b, s, _ = q.shape
    ncp = kc.shape[2]
    nb = ncp // 4
    nq = s // Q_BLOCK
    step_q = SEL_SUB * Q_BLOCK
    pw = _flag_weights(nb)
    tri = _prefix_weights(nb)
    sds = jax.ShapeDtypeStruct
    cmp_spec = pl.BlockSpec((1, NSA_GROUPS, ncp, LANES), lambda bi, qi: (bi, 0, 0, 0))
    return pl.pallas_call(
        _select_kernel,
        grid=(b, s // step_q),
        in_specs=[pl.BlockSpec((1, step_q, NSA_HEADS * LANES), lambda bi, qi: (bi, qi, 0)),
                  cmp_spec, cmp_spec, _const_spec(pw.shape), _const_spec(tri.shape)],
        out_specs=[pl.BlockSpec((1, step_q, NSA_HEADS * HEAD_DIM), lambda bi, qi: (bi, qi, 0)),
                   pl.BlockSpec((1, NSA_GROUPS, step_q, nb), lambda bi, qi: (bi, 0, qi, 0)),
                   pl.BlockSpec((1, NSA_GROUPS, SEL_SUB, 8, LANES), lambda bi, qi: (bi, 0, qi, 0, 0))],
        out_shape=[sds((b, s, NSA_HEADS * HEAD_DIM), F32), sds((b, NSA_GROUPS, s, nb), BF),
                   sds((b, NSA_GROUPS, nq, 8, LANES), jnp.int32)],
        compiler_params=_params(2),
        name="nsa_select",
    )(q, kc, vc, pw, tri)


def _gate_spread():
    gw = HEADS_PER_GROUP * HEAD_DIM
    src = jnp.arange(LANES)[:, None]
    dst = jnp.arange(3 * gw)[None, :]
    c, r = dst // gw, (dst % gw) // HEAD_DIM
    return (src == 3 * r + c).astype(BF)


def _attend_kernel(flag_ref, q_ref, g_ref, oc_ref, sel_ref, spread_ref, ks_ref, vs_ref, *rest, n_words):
    nwin = (WINDOW + ATT_Q) // KEY_TILE
    kw_refs, vw_refs = rest[:nwin], rest[nwin:2 * nwin]
    o_ref, list_ref = rest[2 * nwin], rest[2 * nwin + 1]
    bi, gi, qi = pl.program_id(0), pl.program_id(1), pl.program_id(2)
    q0 = qi * ATT_Q
    diag0 = qi * (ATT_Q // KEY_TILE)
    rq = HEADS_PER_GROUP * ATT_Q
    nb = sel_ref.shape[3]
    tiles_per_word = FLAG_BITS // TILE_BLOCKS
    tile_bits = (1 << TILE_BLOCKS) - 1
    sel_per_step = ATT_Q // Q_BLOCK

    qs = _stack_heads(q_ref, 0)
    selb = sel_ref[0, 0]
    t1 = q0 + lax.broadcasted_iota(jnp.int32, (ATT_Q, 1), 0)

    base = ((bi * NSA_GROUPS + gi) * pl.num_programs(2) + qi) * sel_per_step * n_words

    def word_body(wi, n):
        word = flag_ref[base + wi]
        for extra in range(1, sel_per_step):
            word = word | flag_ref[base + extra * n_words + wi]

        def scan_word(n):
            def tile_body(u, n):
                kt = wi * tiles_per_word + u
                active = (((word >> (u * TILE_BLOCKS)) & tile_bits) != 0) & (kt < diag0)

                @pl.when(active)
                def _():
                    list_ref[n] = kt
                return n + active.astype(jnp.int32)
            return lax.fori_loop(0, tiles_per_word, tile_body, n)
        return lax.cond(word != 0, scan_word, lambda n: n, n)

    n_active = lax.fori_loop(0, (diag0 + tiles_per_word - 1) // tiles_per_word, word_body, 0)
    for u in range(GROUP_TILES):
        list_ref[n_active + u] = 0

    jrow = lax.broadcasted_iota(jnp.int32, (nb, KEY_TILE), 0)
    kblk = lax.broadcasted_iota(jnp.int32, (nb, KEY_TILE), 1) >> 6
    kcol = lax.broadcasted_iota(jnp.int32, (1, KEY_TILE), 1)

    def masked_scores(k_all, member):
        s_ = _dot_nt(qs, k_all)
        return jnp.concatenate([jnp.where(member, s_[r * ATT_Q:(r + 1) * ATT_Q], NEG)
                                for r in range(HEADS_PER_GROUP)], axis=0)

    def tile_rows(ref, kt, n=KEY_TILE):
        return ref[0, 0, pl.ds(pl.multiple_of(kt * KEY_TILE, KEY_TILE), n), :]

    def group_body(it, carry):
        ks_t, vs_t, ex_t = [], [], []
        for u in range(GROUP_TILES):
            pos = it * GROUP_TILES + u
            kt = list_ref[pos]
            first_blk = jnp.where(pos < n_active, kt * TILE_BLOCKS, -TILE_BLOCKS - nb)
            ks_t.append(tile_rows(ks_ref, kt))
            vs_t.append(tile_rows(vs_ref, kt))
            ex_t.append(jnp.where(jrow == kblk + first_blk, 1.0, 0.0).astype(BF))
        member = _dot(selb, jnp.concatenate(ex_t, axis=1)) > 0.5
        s_ = masked_scores(jnp.concatenate(ks_t, axis=0), member)
        return _softmax_update(carry, s_, jnp.concatenate(vs_t, axis=0))

    init = (jnp.full((rq, 1), NEG, F32), jnp.zeros((rq, LANES), F32))
    carry = lax.fori_loop(0, (n_active + GROUP_TILES - 1) // GROUP_TILES, group_body, init)
    drow = lax.broadcasted_iota(jnp.int32, (nb, ATT_Q), 0)
    dcol = lax.broadcasted_iota(jnp.int32, (1, ATT_Q), 1)
    expand = jnp.where(drow == (dcol >> 6) + diag0 * TILE_BLOCKS, 1.0, 0.0).astype(BF)
    member = (_dot(selb, expand) > 0.5) & (q0 + dcol <= t1)
    kpos = []
    for i in range(nwin):
        kti = diag0 - WINDOW // KEY_TILE + i
        kpos.append(jnp.where(kti >= 0, kti * KEY_TILE, 1 << 24) + kcol)
    d = t1 - jnp.concatenate(kpos, axis=1)
    in_window = (d >= 0) & (d < WINDOW)
    s_d = masked_scores(tile_rows(ks_ref, diag0, ATT_Q), member)
    s_w = masked_scores(jnp.concatenate([r[0, 0] for r in kw_refs], axis=0), in_window)
    m_d = jnp.maximum(carry[0], jnp.max(s_d, axis=-1, keepdims=True))
    m_w = jnp.max(s_w, axis=-1, keepdims=True)
    p_d = jnp.exp(s_d - m_d).astype(BF)
    p_w = jnp.exp(s_w - m_w).astype(BF)
    acc_d = jnp.exp(carry[0] - m_d) * carry[1] + _dot(p_d, tile_rows(vs_ref, diag0, ATT_Q))
    acc_w = _dot(p_w, jnp.concatenate([r[0, 0] for r in vw_refs], axis=0))

    assert LANES == 2 * HEAD_DIM and HEADS_PER_GROUP % 2 == 0
    low_half = lax.broadcasted_iota(jnp.int32, (1, LANES), 1) < HEAD_DIM

    def heads_on_lanes(acc):
        inv = 1.0 / jnp.maximum(acc, 1e-30)
        pairs = []
        for r in range(0, HEADS_PER_GROUP, 2):
            ev, od = slice(r * ATT_Q, (r + 1) * ATT_Q), slice((r + 1) * ATT_Q, (r + 2) * ATT_Q)
            even = acc[ev] * pltpu.roll(inv[ev], HEAD_DIM, axis=1)
            odd = pltpu.roll(acc[od], HEAD_DIM, axis=1) * inv[od]
            pairs.append(jnp.where(low_half, even, odd))
        return jnp.concatenate(pairs, axis=1)

    gw = HEADS_PER_GROUP * HEAD_DIM
    gates = _split_dot(g_ref[0], spread_ref[...])
    out = (gates[:, 0:gw] * oc_ref[0] + gates[:, gw:2 * gw] * heads_on_lanes(acc_d)
           + gates[:, 2 * gw:3 * gw] * heads_on_lanes(acc_w))
    o_ref[0] = out.astype(BF)


def _attend(flags, q, gates, o_cmp, sel, ks, vs, kw, vw):
    b, s, _ = q.shape
    nb = sel.shape[3]
    n_words = nb // FLAG_BITS
    nwin = (WINDOW + ATT_Q) // KEY_TILE
    per_q = ATT_Q // KEY_TILE
    gw = HEADS_PER_GROUP * LANES
    flat = flags[:, :, :, 0, :n_words].reshape(-1)
    spread = _gate_spread()
    full = pl.BlockSpec((1, 1, s, LANES), lambda bi, gi, qi, f: (bi, gi, 0, 0))
    win = lambda i: pl.BlockSpec(
        (1, 1, KEY_TILE, LANES),
        lambda bi, gi, qi, f: (bi, gi, jnp.maximum(qi * per_q - WINDOW // KEY_TILE + i, 0), 0))
    grid_spec = pltpu.PrefetchScalarGridSpec(
        num_scalar_prefetch=1,
        grid=(b, NSA_GROUPS, s // ATT_Q),
        in_specs=[pl.BlockSpec((1, ATT_Q, gw), lambda bi, gi, qi, f: (bi, qi, gi)),
                  pl.BlockSpec((1, ATT_Q, LANES), lambda bi, gi, qi, f: (bi, qi, gi)),
                  pl.BlockSpec((1, ATT_Q, HEADS_PER_GROUP * HEAD_DIM), lambda bi, gi, qi, f: (bi, qi, gi)),
                  pl.BlockSpec((1, 1, ATT_Q, nb), lambda bi, gi, qi, f: (bi, gi, qi, 0)),
                  pl.BlockSpec(spread.shape, lambda bi, gi, qi, f: (0, 0)),
                  full, full] + [win(i) for i in range(nwin)] * 2,
        out_specs=pl.BlockSpec((1, ATT_Q, HEADS_PER_GROUP * HEAD_DIM), lambda bi, gi, qi, f: (bi, qi, gi)),
        scratch_shapes=[pltpu.SMEM((s // KEY_TILE + GROUP_TILES,), jnp.int32)],
    )
    return pl.pallas_call(
        functools.partial(_attend_kernel, n_words=n_words),
        grid_spec=grid_spec,
        out_shape=jax.ShapeDtypeStruct((b, s, NSA_HEADS * HEAD_DIM), BF),
        compiler_params=_params(3),
        name="nsa_attend",
    )(flat, q, gates, o_cmp, sel, spread, ks, vs, *([kw] * nwin), *([vw] * nwin))


def _head_indicators(d):
    ch = jnp.arange(d)[:, None] // RWKV_HEAD
    ind = (ch == jnp.arange(LANES)[None, :]).astype(BF)
    return ind, ind.T


def _rwkv_pre_kernel(x_ref, pre_ref, mu_ref, wr_ref, wk_ref, wv_ref, w0_ref, wd1_ref, wd2_ref,
                     a0_ref, wa1_ref, wa2_ref, wg1_ref, wg2_ref, kk_ref, ka_ref, ind_ref, indt_ref,
                     r_out, lw_out, k_out, v_out, kk_out, kka_out, g_out, hprev_ref, *, n_sub):
    si = pl.program_id(1)
    tm = x_ref.shape[1]

    @pl.when(si == 0)
    def _():
        hprev_ref[...] = jnp.zeros(hprev_ref.shape, F32)

    h_all = _rmsnorm(x_ref[0], pre_ref[...])
    rows = lax.broadcasted_iota(jnp.int32, (tm, 1), 0)
    xx_all = jnp.where(rows == 0, hprev_ref[7:8], pltpu.roll(h_all, 1, axis=0)) - h_all
    hprev_ref[...] = h_all[tm - 8:tm]

    def sub_tile(rs):
        h, xx = h_all[rs], xx_all[rs]
        mix = lambda i: (h + xx * mu_ref[i:i + 1]).astype(BF)
        xr, xw, xk, xv, xa, xg = [mix(i) for i in range(6)]
        r = _dot(xr, wr_ref[...])
        k = _dot(xk, wk_ref[...])
        v = _dot(xv, wv_ref[...])
        yield
        z = w0_ref[...] + _dot(jnp.tanh(_dot(xw, wd1_ref[...])).astype(BF), wd2_ref[...])
        softplus = jnp.maximum(-z, 0.0) + jnp.log(1.0 + jnp.exp(-jnp.abs(z)))
        lw = -jnp.exp(-softplus - 0.5)
        a = _sigmoid(a0_ref[...] + _dot(_dot(xa, wa1_ref[...]).astype(BF), wa2_ref[...]))
        g = _dot(_sigmoid(_dot(xg, wg1_ref[...])).astype(BF), wg2_ref[...])
        yield
        kraw = k * kk_ref[...]
        ss = _dot((kraw * kraw).astype(BF), ind_ref[...])
        yield
        inv = lax.rsqrt(jnp.maximum(ss, 1e-12))
        kk = kraw * _dot(inv.astype(BF), indt_ref[...])
        yield
        r_out[0, rs] = r.astype(BF)
        lw_out[0, rs] = lw
        k_out[0, rs] = (k * (1.0 + (a - 1.0) * ka_ref[...])).astype(BF)
        v_out[0, rs] = v.astype(BF)
        kk_out[0, rs] = kk.astype(BF)
        kka_out[0, rs] = (kk * a).astype(BF)
        g_out[0, rs] = g.astype(BF)

    _round_robin(sub_tile(slice(i * tm // n_sub, (i + 1) * tm // n_sub)) for i in range(n_sub))


def _rwkv_pre(x, pre_g, mu, w_r, w_k, w_v, w0, w_dec1, w_dec2, a0, w_a1, w_a2, w_g1, w_g2,
              k_k, k_a, ind, indt, *, tm=512, n_sub=1):
    b, s, d = x.shape
    row = pl.BlockSpec((1, tm, d), lambda bi, si: (bi, si, 0))
    vec = lambda a: a.reshape(1, d)
    ops = [x, vec(pre_g), mu, w_r.astype(BF), w_k.astype(BF), w_v.astype(BF), vec(w0),
           w_dec1.astype(BF), w_dec2.astype(BF), vec(a0), w_a1.astype(BF), w_a2.astype(BF),
           w_g1.astype(BF), w_g2.astype(BF), vec(k_k), vec(k_a), ind, indt]
    return pl.pallas_call(
        functools.partial(_rwkv_pre_kernel, n_sub=n_sub),
        grid=(b, s // tm),
        in_specs=[row] + [_const_spec(o.shape) for o in ops[1:]],
        out_specs=[row] * 7,
        out_shape=[jax.ShapeDtypeStruct((b, s, d), F32 if i == 1 else BF) for i in range(7)],
        scratch_shapes=[pltpu.VMEM((8, d), F32)],
        compiler_params=_params(2),
        name="rwkv_pre",
    )(*ops)


def _rwkv_scan_kernel(r_ref, lw_ref, k_ref, v_ref, kk_ref, kka_ref, y_ref, state_ref):
    c = RWKV_CHUNK
    slab = SLAB_HEADS * RWKV_HEAD
    n_slab = r_ref.shape[2] // slab
    rows = SLAB_HEADS * c
    subs = [slice(i * c, (i + 1) * c) for i in range(r_ref.shape[1] // c)]

    @pl.when(pl.program_id(1) == 0)
    def _():
        state_ref[...] = jnp.zeros(state_ref.shape, F32)

    ri = lax.broadcasted_iota(jnp.int32, (rows, 1), 0)
    ci = lax.broadcasted_iota(jnp.int32, (1, rows), 1)
    li = lax.broadcasted_iota(jnp.int32, (1, slab), 1)
    c_shift = c.bit_length() - 1
    h_shift = RWKV_HEAD.bit_length() - 1
    assert c == 1 << c_shift and RWKV_HEAD == 1 << h_shift
    head_match = (ri >> c_shift) == (li >> h_shift)
    t_r, t_c = ri & (c - 1), ci & (c - 1)
    strict = t_c < t_r
    incl = t_c <= t_r
    eye = jnp.where(ri == ci, 1.0, 0.0)
    tri = jnp.where(lax.broadcasted_iota(jnp.int32, (c, c), 1) <= lax.broadcasted_iota(jnp.int32, (c, c), 0),
                    1.0, 0.0).astype(BF)
    state_match = (lax.broadcasted_iota(jnp.int32, (slab, 1), 0) >> h_shift) == (li >> h_shift)

    incl2 = jnp.concatenate([incl, incl], axis=1)

    def block_diag(x):
        return jnp.where(head_match, jnp.concatenate([x] * SLAB_HEADS, axis=0), 0.0)

    def fold(x):
        out = x[0:c]
        for h in range(1, SLAB_HEADS):
            out = out + x[h * c:(h + 1) * c]
        return out

    bf = lambda x: x.astype(BF)
    slabs = [slice(hs * slab, (hs + 1) * slab) for hs in range(n_slab)]

    def prepare(rs, sl):
        lw = lw_ref[0, rs, sl]
        hi = bf(lw)
        rem = lw - hi.astype(F32)
        mid = bf(rem)
        lo = bf(rem - mid.astype(F32))
        cw = _dot(tri, hi) + _dot(tri, mid) + _dot(tri, lo)
        w_in = jnp.exp(cw)
        w_inv = jnp.exp(-cw)
        w_prev = jnp.exp(cw - lw)
        w_end = w_in[c - 1:c]

        a_t = -kk_ref[0, rs, sl].astype(F32) * w_prev
        b_t = kka_ref[0, rs, sl].astype(F32) * w_inv
        k_t = k_ref[0, rs, sl].astype(F32) * w_inv
        r_t = r_ref[0, rs, sl].astype(F32) * w_in
        return a_t, b_t, k_t, r_t, w_end

    def interactions(a_t, b_t, k_t, r_t):
        lhs = bf(jnp.concatenate([block_diag(a_t), block_diag(r_t)], axis=0))
        rhs = bf(jnp.concatenate([block_diag(b_t), block_diag(k_t)], axis=0))
        big = _dot_nt(lhs, rhs)
        a_ab = jnp.where(strict, big[:rows, :rows], 0.0)
        a_ak = jnp.where(strict, big[:rows, rows:], 0.0)
        a_r = jnp.where(incl2, big[rows:], 0.0)
        return a_ab, bf(a_ak), bf(a_r)

    every = range(n_slab)
    units = [(rs, sl) for rs in subs for sl in slabs]
    prep = [prepare(rs, sl) for rs, sl in units]
    inter = [interactions(*p[:4]) for p in prep]
    vs = [v_ref[0, rs, sl] for rs, sl in units]
    vbd = [bf(block_diag(v)) for v in vs]
    akv = [_dot(inter[u][1], vbd[u]) for u in range(len(units))]

    levels = c.bit_length() - 2
    assert levels >= 1
    invs = [eye + x[0] for x in inter]
    pws = [_dot(bf(x[0]), bf(x[0])) for x in inter]
    for _ in range(levels - 1):
        both = [_dot(bf(pw), bf(jnp.concatenate([pw, inv], axis=1))) for pw, inv in zip(pws, invs)]
        pws = [x[:, :rows] for x in both]
        invs = [inv + x[:, rows:] for inv, x in zip(invs, both)]
    invs = [bf(inv + _dot(bf(pw), bf(inv))) for pw, inv in zip(pws, invs)]

    states = [state_ref[hs] for hs in every]
    for si, rs in enumerate(subs):
        us = [si * n_slab + hs for hs in every]
        ar = [_dot_nt(bf(jnp.concatenate([prep[u][0], prep[u][3]], axis=0)), bf(states[hs]))
              for hs, u in zip(every, us)]
        u_bd = [_dot(invs[u], bf(block_diag(ar[hs][:c]) + akv[u])) for hs, u in zip(every, us)]
        y_bd = [block_diag(ar[hs][c:]) + _dot(inter[u][2], jnp.concatenate([bf(u_bd[hs]), vbd[u]], axis=0))
                for hs, u in zip(every, us)]
        for hs, sl in enumerate(slabs):
            y_ref[0, rs, sl] = fold(y_bd[hs])
        new_states = []
        for hs, u in zip(every, us):
            _, b_t, k_t, _, w_end = prep[u]
            uv = jnp.concatenate([bf(fold(u_bd[hs])), bf(vs[u])], axis=0)
            bk = jnp.concatenate([bf(b_t * w_end), bf(k_t * w_end)], axis=0)
            new_states.append(jnp.where(state_match, states[hs] * w_end + _dot_tn(uv, bk), 0.0))
        states = new_states
    for hs in every:
        state_ref[hs] = states[hs]


def _rwkv_scan(r, lw, k, v, kk, kka):
    b, s, d = r.shape
    c = RWKV_CHUNK * SCAN_SUB
    slab = SLAB_HEADS * RWKV_HEAD
    blk = pl.BlockSpec((1, c, d), lambda bi, ci: (bi, ci, 0))
    return pl.pallas_call(
        _rwkv_scan_kernel,
        grid=(b, s // c),
        in_specs=[blk] * 6,
        out_specs=blk,
        out_shape=jax.ShapeDtypeStruct((b, s, d), F32),
        scratch_shapes=[pltpu.VMEM((d // slab, slab, slab), F32)],
        compiler_params=_params(2),
        name="rwkv_scan",
    )(r, lw, k, v, kk, kka)


def _rwkv_post_kernel(x_ref, y_ref, r_ref, k_ref, v_ref, g_ref, wo_ref, lng_ref, lnb_ref, rk_ref,
                      post_ref, ind_ref, indt_ref, o_ref, *, n_sub):
    ind, indt = ind_ref[...], indt_ref[...]
    inv_n = 1.0 / RWKV_HEAD
    head_sum = lambda a: _dot(a.astype(BF), ind)
    spread = lambda a: _dot(a.astype(BF), indt)
    tm = x_ref.shape[0]

    def sub_tile(rs):
        f32 = lambda ref: ref[rs].astype(F32)
        y = y_ref[rs]
        sums = head_sum(y)
        rk_sum = head_sum(f32(r_ref) * f32(k_ref) * rk_ref[...])
        yield
        mean = _split_dot(sums * inv_n, indt)
        bonus = spread(rk_sum) * f32(v_ref)
        yield
        yc = y - mean
        var = head_sum(yc * yc) * inv_n
        yield
        rstd = spread(lax.rsqrt(var + LNX_EPS))
        yield
        yn = yc * rstd * lng_ref[...] + lnb_ref[...]
        out = _dot(((yn + bonus) * f32(g_ref)).astype(BF), wo_ref[...])
        yield
        o_ref[rs] = x_ref[rs] + _rmsnorm(out, post_ref[...])

    _round_robin(sub_tile(slice(i * tm // n_sub, (i + 1) * tm // n_sub)) for i in range(n_sub))


def _rwkv_post(x2, y, r, k, v, g, w_o, lnx_g, lnx_b, r_k, post_g, ind, indt, *, tm=512, n_sub=2):
    t, d = x2.shape
    row = pl.BlockSpec((tm, d), lambda i: (i, 0))
    vec = lambda a: a.reshape(1, d)
    ops = [x2, y, r, k, v, g, w_o.astype(BF), vec(lnx_g), vec(lnx_b), vec(r_k), vec(post_g), ind, indt]
    return pl.pallas_call(
        functools.partial(_rwkv_post_kernel, n_sub=n_sub),
        grid=(t // tm,),
        in_specs=[row] * 6 + [_const_spec(o.shape) for o in ops[6:]],
        out_specs=row,
        out_shape=jax.ShapeDtypeStruct((t, d), F32),
        compiler_params=_params(1),
        name="rwkv_post",
    )(*ops)


def _layer0_mixer(x, pre_g, w_in, pe_k, w1_k, w2_k, pe_v, w1_v, w2_v, conv_w, conv_b):
    b, s, _ = x.shape
    q, gates, kcr, vcr, ks, vs, kw, vw, o_conv = _l0_in(x, pre_g, _l0_in_weight(w_in), conv_w, conv_b)
    kc = _compress(kcr, _cmp_weights(pe_k, w1_k, w2_k), is_key=True)
    vc = _compress(vcr, _cmp_weights(pe_v, w1_v, w2_v), is_key=False)
    o_cmp, sel, flags = _select(q, kc, vc)
    o_nsa = _attend(flags, q, gates, o_cmp, sel, ks, vs, kw, vw)
    return o_nsa.reshape(b * s, -1), o_conv.reshape(b * s, -1)


def _layer1_mixer(x, pre_g, post_g, mu, w_r, w_k, w_v, w_o, w0, w_dec1, w_dec2, a0, w_a1, w_a2,
                  w_g1, w_g2, k_k, k_a, r_k, lnx_g, lnx_b):
    b, s, d = x.shape
    ind, indt = _head_indicators(d)
    r, lw, k, v, kk, kka, g = _rwkv_pre(x, pre_g, mu, w_r, w_k, w_v, w0, w_dec1, w_dec2, a0,
                                        w_a1, w_a2, w_g1, w_g2, k_k, k_a, ind, indt)
    y = _rwkv_scan(r, lw, k, v, kk, kka)
    f = lambda a: a.reshape(b * s, d)
    return _rwkv_post(f(x), f(y), f(r), f(k), f(v), f(g), w_o, lnx_g, lnx_b, r_k, post_g,
                      ind, indt).reshape(b, s, d)


def kernel(x, l0_ffn1_pre_g, l0_ffn1_post_g, l0_ffn1_w_gate, l0_ffn1_w_up, l0_ffn1_w_down, l0_mix_pre_g, l0_mix_post_g, l0_w_in, l0_cmp_pe_k, l0_cmp_w1_k, l0_cmp_w2_k, l0_cmp_pe_v, l0_cmp_w1_v, l0_cmp_w2_v, l0_conv_w, l0_conv_b, l0_w_out, l0_ffn2_pre_g, l0_ffn2_post_g, l0_ffn2_w_gate, l0_ffn2_w_up, l0_ffn2_w_down, l1_ffn1_pre_g, l1_ffn1_post_g, l1_ffn1_w_gate, l1_ffn1_w_up, l1_ffn1_w_down, l1_mix_pre_g, l1_mix_post_g, l1_mu, l1_w_r, l1_w_k, l1_w_v, l1_w_o, l1_w0, l1_w_dec1, l1_w_dec2, l1_a0, l1_w_a1, l1_w_a2, l1_w_g1, l1_w_g2, l1_k_k, l1_k_a, l1_r_k, l1_lnx_g, l1_lnx_b, l1_ffn2_pre_g, l1_ffn2_post_g, l1_ffn2_w_gate, l1_ffn2_w_up, l1_ffn2_w_down):
    b, s, d = x.shape
    ffn = lambda a, *w, **kw: _ffn(a.reshape(b * s, d), *w, **kw).reshape(b, s, d)
    x = ffn(x, l0_ffn1_pre_g, l0_ffn1_post_g, l0_ffn1_w_gate, l0_ffn1_w_up, l0_ffn1_w_down)
    o_nsa, o_conv = _layer0_mixer(x, l0_mix_pre_g, l0_w_in, l0_cmp_pe_k, l0_cmp_w1_k, l0_cmp_w2_k,
                                  l0_cmp_pe_v, l0_cmp_w1_v, l0_cmp_w2_v, l0_conv_w, l0_conv_b)
    x = ffn(x, l0_ffn2_pre_g, l0_ffn2_post_g, l0_ffn2_w_gate, l0_ffn2_w_up, l0_ffn2_w_down,
            mixer=(o_nsa, o_conv, l0_w_out, l0_mix_post_g))
    x = ffn(x, l1_ffn1_pre_g, l1_ffn1_post_g, l1_ffn1_w_gate, l1_ffn1_w_up, l1_ffn1_w_down)
    x = _layer1_mixer(x, l1_mix_pre_g, l1_mix_post_g, l1_mu, l1_w_r, l1_w_k, l1_w_v, l1_w_o, l1_w0,
                      l1_w_dec1, l1_w_dec2, l1_a0, l1_w_a1, l1_w_a2, l1_w_g1, l1_w_g2, l1_k_k, l1_k_a,
                      l1_r_k, l1_lnx_g, l1_lnx_b)
    x = ffn(x, l1_ffn2_pre_g, l1_ffn2_post_g, l1_ffn2_w_gate, l1_ffn2_w_up, l1_ffn2_w_down)
    return x
```

```python
import functools

import jax
import jax.numpy as jnp
from jax import lax
from jax.experimental import pallas as pl
from jax.experimental.pallas import tpu as pltpu

BF = jnp.bfloat16
F32 = jnp.float32

EPS = 1e-6
LNX_EPS = 64e-5
HEAD_DIM = 64
NSA_HEADS = 8
NSA_GROUPS = 2
HEADS_PER_GROUP = NSA_HEADS // NSA_GROUPS
CMP_STRIDE = 16
CMP_BLOCK = 32
CMP_HIDDEN = 128
SLC_BLOCK = 64
SLC_TOP_N = 16
WINDOW = 512
Q_BLOCK = 128
CONV_WIDTH = 512
RWKV_HEAD = 64
RWKV_CHUNK = 64
SLAB_HEADS = 2
SCAN_SUB = 4
LANES = 128
MXU_DIM = 256
POS_SHIFT = 7
POS_SPLIT = 1 << POS_SHIFT
NEG = -1e30
VMEM_LIMIT = 56 * 1024 * 1024

NT_DIMS = (((1,), (1,)), ((), ()))
TN_DIMS = (((0,), (0,)), ((), ()))


def _dot(a, b):
    return jnp.dot(a, b, preferred_element_type=F32)


def _dot_nt(a, b):
    return lax.dot_general(a, b, NT_DIMS, preferred_element_type=F32)


def _dot_tn(a, b):
    return lax.dot_general(a, b, TN_DIMS, preferred_element_type=F32)


def _split_dot(x, w):
    hi = x.astype(BF)
    lo = (x - hi.astype(F32)).astype(BF)
    return _dot(hi, w) + _dot(lo, w)


def _rmsnorm(x, g):
    ms = jnp.mean(x * x, axis=-1, keepdims=True)
    return x * lax.rsqrt(ms + EPS) * g


def _sigmoid(x):
    return 1.0 / (1.0 + jnp.exp(-x))


def _round_robin(stage_generators):
    pending = list(stage_generators)
    while pending:
        still = []
        for gen in pending:
            try:
                next(gen)
                still.append(gen)
            except StopIteration:
                pass
        pending = still


def _const_spec(shape):
    zeros = (0,) * len(shape)
    return pl.BlockSpec(shape, lambda *_: zeros)


def _params(n_grid):
    return pltpu.CompilerParams(dimension_semantics=("arbitrary",) * n_grid,
                                vmem_limit_bytes=VMEM_LIMIT)


def _ffn_kernel(*refs, ff_chunk, n_sub, after_mixer):
    if after_mixer:
        x_ref, a_ref, c_ref, wa_ref, wc_ref, mixpost_ref = refs[:6]
        refs = refs[:1] + refs[6:]
    x_ref, pre_ref, post_ref, wg_ref, wu_ref, wd_ref, o_ref = refs

    def residual_rows(rs):
        if not after_mixer:
            return x_ref[rs]
        y = _dot(a_ref[rs], wa_ref[...]) + _dot(c_ref[rs], wc_ref[...])
        return x_ref[rs] + _rmsnorm(y, mixpost_ref[...])

    sub = x_ref.shape[0] // n_sub
    dff = wg_ref.shape[1]
    chunks = [(c0, min(c0 + ff_chunk, dff)) for c0 in range(0, dff, ff_chunk)]
    rows = [slice(i * sub, (i + 1) * sub) for i in range(n_sub)]
    xs, hs, accs = [], [], []
    for ci, (c0, c1) in enumerate(chunks):
        gu = []
        for i, rs in enumerate(rows):
            if ci == 0:
                xs.append(residual_rows(rs))
                hs.append(_rmsnorm(xs[i], pre_ref[...]).astype(BF))
                accs.append(jnp.zeros(xs[i].shape, F32))
            gu.append((_dot(hs[i], wg_ref[:, c0:c1]), _dot(hs[i], wu_ref[:, c0:c1])))
        for i, (g, u) in enumerate(gu):
            a = (g * _sigmoid(g) * u).astype(BF)
            accs[i] = accs[i] + _dot(a, wd_ref[c0:c1, :])
    for i, rs in enumerate(rows):
        o_ref[rs] = xs[i] + 0.5 * _rmsnorm(accs[i], post_ref[...])


def _ffn(x2, pre_g, post_g, w_gate, w_up, w_down, *, mixer=None, tm=1024, n_sub=2):
    t, d = x2.shape
    dff = w_gate.shape[1]
    ff_chunk = MXU_DIM * pl.cdiv(pl.cdiv(dff, MXU_DIM), 2)
    row = lambda w: pl.BlockSpec((tm, w), lambda i: (i, 0))
    weight = lambda shape: pl.BlockSpec(shape, lambda i: (0, 0), pipeline_mode=pl.Buffered(1))
    ops, specs = [x2], [row(d)]
    if mixer is not None:
        o_nsa, o_conv, w_out, mix_post_g = mixer
        na, nc = o_nsa.shape[1], o_conv.shape[1]
        ops += [o_nsa, o_conv, w_out[:na].astype(BF), w_out[na:].astype(BF), mix_post_g.reshape(1, d)]
        specs += [row(na), row(nc), weight((na, d)), weight((nc, d)), _const_spec((1, d))]
    ops += [pre_g.reshape(1, d), post_g.reshape(1, d), w_gate.astype(BF), w_up.astype(BF), w_down.astype(BF)]
    specs += [_const_spec((1, d)), _const_spec((1, d)), weight((d, dff)), weight((d, dff)), weight((dff, d))]
    return pl.pallas_call(
        functools.partial(_ffn_kernel, ff_chunk=ff_chunk, n_sub=n_sub, after_mixer=mixer is not None),
        grid=(t // tm,),
        in_specs=specs,
        out_specs=row(d),
        out_shape=jax.ShapeDtypeStruct((t, d), F32),
        compiler_params=_params(1),
        name="ffn",
    )(*ops)


_C_Q = 0
_C_G = _C_Q + NSA_HEADS * LANES
_C_KC = _C_G + NSA_GROUPS * LANES
_C_VC = _C_KC + LANES
_C_KS = _C_VC + LANES
_C_CONV = _C_KS + 4 * NSA_GROUPS * LANES
_C_END = _C_CONV + 3 * CONV_WIDTH


def _l0_in_weight(w_in):
    d = w_in.shape[0]
    nsa_w = NSA_HEADS * HEAD_DIM
    kv_w = NSA_GROUPS * HEAD_DIM
    o = 0
    q = w_in[:, o:o + nsa_w]; o += nsa_w
    g = w_in[:, o:o + 3 * NSA_HEADS]; o += 3 * NSA_HEADS
    kvs = []
    for _ in range(6):
        kvs.append(w_in[:, o:o + kv_w]); o += kv_w
    conv = w_in[:, o:]
    zpad = lambda w, n: jnp.pad(w, ((0, 0), (0, n - w.shape[1])))
    cols = [zpad(q[:, h * HEAD_DIM:(h + 1) * HEAD_DIM], LANES) for h in range(NSA_HEADS)]
    gpg = 3 * HEADS_PER_GROUP
    cols += [zpad(g[:, i * gpg:(i + 1) * gpg], LANES) for i in range(NSA_GROUPS)]
    cols += [kvs[0], kvs[1]]
    for w in kvs[2:]:
        cols += [zpad(w[:, i * HEAD_DIM:(i + 1) * HEAD_DIM], LANES) for i in range(NSA_GROUPS)]
    cols.append(conv)
    w = jnp.concatenate(cols, axis=1)
    assert w.shape == (d, _C_END)
    return w.astype(BF)


def _l0_in_kernel(x_ref, pre_ref, w_ref, cw_ref, cb_ref,
                  q_ref, g_ref, kcr_ref, vcr_ref, ks_ref, vs_ref, kw_ref, vw_ref, oc_ref,
                  zprev_ref):
    si = pl.program_id(1)
    tm = x_ref.shape[1]
    h = _rmsnorm(x_ref[0], pre_ref[...]).astype(BF)
    lane = lax.broadcasted_iota(jnp.int32, (1, LANES), 1)
    pos = si * tm + lax.broadcasted_iota(jnp.int32, (tm, 1), 0)
    pos_hi = (pos >> POS_SHIFT).astype(F32)
    pos_lo = (pos & (POS_SPLIT - 1)).astype(F32)

    wide = 4 * LANES

    def project(c0):
        res = _dot(h, w_ref[:, c0:c0 + wide])
        return [res[:, j * LANES:(j + 1) * LANES] for j in range(4)]

    for hd0 in range(0, NSA_HEADS, 4):
        for j, qh in enumerate(project(_C_Q + hd0 * LANES)):
            hd = hd0 + j
            slope = 2.0 ** (-(hd + 1))
            qh = qh * (HEAD_DIM ** -0.5)
            qh = jnp.where(lane == HEAD_DIM, POS_SPLIT * slope, jnp.where(lane == HEAD_DIM + 1, slope, qh))
            q_ref[0, :, hd * LANES:(hd + 1) * LANES] = qh.astype(BF)

    assert _C_KC == _C_G + 2 * LANES and _C_VC == _C_KC + LANES and NSA_GROUPS == 2
    g0, g1, kcr, vcr = project(_C_G)
    g_ref[0] = _sigmoid(jnp.concatenate([g0, g1], axis=1))
    kcr_ref[0] = kcr
    vcr_ref[0] = vcr

    slabs = project(_C_KS) + project(_C_KS + wide)
    for n, (ref, is_key) in enumerate(((ks_ref, True), (vs_ref, False), (kw_ref, True), (vw_ref, False))):
        for gi in range(NSA_GROUPS):
            t = slabs[n * NSA_GROUPS + gi]
            if is_key:
                t = jnp.where(lane == HEAD_DIM, pos_hi, jnp.where(lane == HEAD_DIM + 1, pos_lo, t))
            else:
                t = jnp.where(lane >= HEAD_DIM, 1.0, t)
            ref[0, gi] = t.astype(BF)

    @pl.when(si == 0)
    def _():
        zprev_ref[...] = jnp.zeros(zprev_ref.shape, F32)

    cb = _dot(h, w_ref[:, _C_CONV:_C_CONV + CONV_WIDTH])
    cc = _dot(h, w_ref[:, _C_CONV + CONV_WIDTH:_C_CONV + 2 * CONV_WIDTH])
    cu = _dot(h, w_ref[:, _C_CONV + 2 * CONV_WIDTH:_C_CONV + 3 * CONV_WIDTH])
    z = cc * cu
    prev = zprev_ref[...]
    rows = lax.broadcasted_iota(jnp.int32, (tm, 1), 0)
    zm1 = jnp.where(rows == 0, prev[7:8], pltpu.roll(z, 1, axis=0))
    zm2 = jnp.where(rows == 0, prev[6:7], jnp.where(rows == 1, prev[7:8], pltpu.roll(z, 2, axis=0)))
    y = cw_ref[0:1] * zm2 + cw_ref[1:2] * zm1 + cw_ref[2:3] * z
    oc_ref[0] = (cb * (y + cb_ref[...])).astype(BF)
    zprev_ref[...] = z[tm - 8:tm]


def _l0_in(x, pre_g, w_all, conv_w, conv_b, *, tm=512):
    b, s, d = x.shape
    row = lambda w: pl.BlockSpec((1, tm, w), lambda bi, si: (bi, si, 0))
    grp = pl.BlockSpec((1, NSA_GROUPS, tm, LANES), lambda bi, si: (bi, 0, si, 0))
    sds = jax.ShapeDtypeStruct
    kv_shape = sds((b, NSA_GROUPS, s, LANES), BF)
    return pl.pallas_call(
        _l0_in_kernel,
        grid=(b, s // tm),
        in_specs=[row(d), _const_spec((1, d)), _const_spec(w_all.shape),
                  _const_spec(conv_w.shape), _const_spec((1, CONV_WIDTH))],
        out_specs=[row(NSA_HEADS * LANES), row(NSA_GROUPS * LANES), row(LANES), row(LANES),
                   grp, grp, grp, grp, row(CONV_WIDTH)],
        out_shape=[sds((b, s, NSA_HEADS * LANES), BF), sds((b, s, NSA_GROUPS * LANES), F32),
                   sds((b, s, LANES), F32), sds((b, s, LANES), F32),
                   kv_shape, kv_shape, kv_shape, kv_shape, sds((b, s, CONV_WIDTH), BF)],
        scratch_shapes=[pltpu.VMEM((8, CONV_WIDTH), F32)],
        compiler_params=_params(2),
        name="l0_in",
    )(x, pre_g.reshape(1, d), w_all, conv_w, conv_b.reshape(1, CONV_WIDTH))


def _cmp_weights(pe, w1, w2):
    half = CMP_STRIDE * HEAD_DIM
    def expand(w):
        w = w.reshape(CMP_STRIDE, 1, HEAD_DIM, 1, CMP_HIDDEN)
        same_group = jnp.eye(NSA_GROUPS, dtype=F32).reshape(1, NSA_GROUPS, 1, NSA_GROUPS, 1)
        return (w * same_group).reshape(CMP_STRIDE * NSA_GROUPS * HEAD_DIM, NSA_GROUPS * CMP_HIDDEN).astype(BF)
    pe_row = lambda p: jnp.tile(p, (1, NSA_GROUPS)).reshape(1, CMP_STRIDE * NSA_GROUPS * HEAD_DIM)
    w2p = jnp.pad(w2, ((0, 0), (0, LANES - HEAD_DIM))).astype(BF)
    return (pe_row(pe[:CMP_STRIDE]), pe_row(pe[CMP_STRIDE:]), expand(w1[:half]), expand(w1[half:]), w2p)


def _gelu_tanh(x):
    return 0.5 * x * (1.0 + jnp.tanh(0.7978845608028654 * (x + 0.044715 * (x * x * x))))


def _cmp_kernel(x_ref, pet_ref, peb_ref, wt_ref, wb_ref, w2_ref, o_ref, *, is_key):
    nb = x_ref.shape[1]
    hw = x_ref.shape[2] // 4
    top, bot = [], []
    for c in range(4):
        xc = x_ref[0, :, c * hw:(c + 1) * hw]
        top.append(_dot((xc + pet_ref[...]).astype(BF), wt_ref[...]))
        bot.append(_dot((xc + peb_ref[...]).astype(BF), wb_ref[...]))
    lane = lax.broadcasted_iota(jnp.int32, (1, LANES), 1)
    j = lax.broadcasted_iota(jnp.int32, (nb, 1), 0)
    for c in range(4):
        nxt = bot[c + 1] if c < 3 else pltpu.roll(bot[0], nb - 1, axis=0)
        hid = _gelu_tanh(top[c] + nxt)
        n = 4 * j + c
        end = n * CMP_STRIDE + (CMP_BLOCK - 1)
        exists = n < 4 * nb - 1
        for gi in range(NSA_GROUPS):
            t = _dot(hid[:, gi * CMP_HIDDEN:(gi + 1) * CMP_HIDDEN].astype(BF), w2_ref[...])
            t = jnp.where(exists, t, 0.0)
            if is_key:
                t = jnp.where(lane == HEAD_DIM, (end >> POS_SHIFT).astype(F32),
                              jnp.where(lane == HEAD_DIM + 1, (end & (POS_SPLIT - 1)).astype(F32), t))
            else:
                t = jnp.where(lane == HEAD_DIM, 1.0, t)
            o_ref[0, gi, c * nb:(c + 1) * nb, :] = t.astype(BF)


def _compress(raw, weights, *, is_key):
    b, s, _ = raw.shape
    nb = s // (4 * CMP_STRIDE)
    x = raw.reshape(b, nb, 4 * CMP_STRIDE * LANES)
    pet, peb, wt, wb, w2p = weights
    return pl.pallas_call(
        functools.partial(_cmp_kernel, is_key=is_key),
        grid=(b,),
        in_specs=[pl.BlockSpec((1, nb, x.shape[2]), lambda bi: (bi, 0, 0)),
                  _const_spec(pet.shape), _const_spec(peb.shape), _const_spec(wt.shape),
                  _const_spec(wb.shape), _const_spec(w2p.shape)],
        out_specs=pl.BlockSpec((1, NSA_GROUPS, 4 * nb, LANES), lambda bi: (bi, 0, 0, 0)),
        out_shape=jax.ShapeDtypeStruct((b, NSA_GROUPS, 4 * nb, LANES), BF),
        compiler_params=_params(1),
        name="compress",
    )(x, pet, peb, wt, wb, w2p)


def _softmax_update(carry, s, v):
    m, acc = carry
    m_new = jnp.maximum(m, jnp.max(s, axis=-1, keepdims=True))
    p = jnp.exp(s - m_new)
    acc = jnp.exp(m - m_new) * acc + _dot(p.astype(BF), v)
    return m_new, acc


SEL_SUB = 2
N_FORCED = 3
FLAG_BITS = 16
ATT_Q = 256
KEY_TILE = 128
TILE_BLOCKS = KEY_TILE // SLC_BLOCK
GROUP_TILES = 4


def _stack_heads(q_ref, g):
    base = g * HEADS_PER_GROUP
    return jnp.concatenate([q_ref[0, :, (base + r) * LANES:(base + r + 1) * LANES]
                            for r in range(HEADS_PER_GROUP)], axis=0)


def _flag_weights(nb):
    j = jnp.arange(nb)[:, None]
    w = jnp.arange(LANES)[None, :]
    return jnp.where(j // FLAG_BITS == w, 2.0 ** (j % FLAG_BITS), 0.0).astype(BF)


def _prefix_weights(nb):
    return (jnp.arange(nb)[:, None] <= jnp.arange(nb)[None, :]).astype(BF)


def _select_kernel(q_ref, kc_ref, vc_ref, pw_ref, tri_ref, oc_ref, sel_ref, flag_ref):
    qi = pl.program_id(1)
    q0 = qi * (SEL_SUB * Q_BLOCK)
    rq = HEADS_PER_GROUP * Q_BLOCK
    nb = kc_ref.shape[2] // 4
    subs = range(SEL_SUB)
    units = [(sb, g) for sb in subs for g in range(NSA_GROUPS)]
    rows = [slice(sb * Q_BLOCK, (sb + 1) * Q_BLOCK) for sb in subs]
    t1 = [q0 + sb * Q_BLOCK + lax.broadcasted_iota(jnp.int32, (Q_BLOCK, 1), 0) for sb in subs]
    t4 = [q0 + sb * Q_BLOCK + (lax.broadcasted_iota(jnp.int32, (rq, 1), 0) & (Q_BLOCK - 1))
          for sb in subs]
    blk_t = [t >> 6 for t in t1]

    def stacked_queries(sb, g):
        base = g * HEADS_PER_GROUP
        return jnp.concatenate([q_ref[0, rows[sb], (base + r) * LANES:(base + r + 1) * LANES]
                                for r in range(HEADS_PER_GROUP)], axis=0)

    def run(w):
        w_shift = w.bit_length() - 1
        assert w == 1 << w_shift and w % LANES == 0
        col = lax.broadcasted_iota(jnp.int32, (1, 4 * w), 1)
        cmp_end = (4 * (col & (w - 1)) + (col >> w_shift)) * CMP_STRIDE + (CMP_BLOCK - 1)
        jl = lax.broadcasted_iota(jnp.int32, (1, w), 1)
        valid = [cmp_end <= t4[sb] for sb in subs]
        forced = [(jl == 0) | (jl == blk_t[sb]) | (jl == blk_t[sb] - 1) for sb in subs]
        eligible = [jl <= blk_t[sb] for sb in subs]
        pickable = [eligible[sb] & ~forced[sb] for sb in subs]
        n_pick = min(SLC_TOP_N, nb) - N_FORCED

        def columns(ref, g):
            if w == nb:
                return ref[0, g]
            return jnp.concatenate([ref[0, g, c * nb:c * nb + w] for c in range(4)], axis=0)

        def importance(p):
            ps = p[0:Q_BLOCK]
            for r in range(1, HEADS_PER_GROUP):
                ps = ps + p[r * Q_BLOCK:(r + 1) * Q_BLOCK]
            parts = [ps[:, c * w:(c + 1) * w] for c in range(4)]
            prev = jnp.where(jl == 0, 0.0, pltpu.roll(parts[3], 1, axis=1))
            return (parts[0] + parts[1] + parts[2] + parts[3]) - 0.5 * parts[3] + 0.5 * prev

        s = [jnp.where(valid[sb], _dot_nt(stacked_queries(sb, g), columns(kc_ref, g)), NEG) for sb, g in units]
        m = [jnp.max(x, axis=-1, keepdims=True) for x in s]
        m = [jnp.where(x <= NEG, 0.0, x) for x in m]
        p = [jnp.exp(x - mx) for x, mx in zip(s, m)]
        p = [x / jnp.maximum(jnp.sum(x, axis=-1, keepdims=True), 1e-30) for x in p]
        o_cmp = [_dot(x.astype(BF), columns(vc_ref, g)) for x, (sb, g) in zip(p, units)]
        score = [jnp.where(pickable[sb], importance(x), NEG) for x, (sb, g) in zip(p, units)]
        work = score
        for _ in range(n_pick - 1):
            hit = [jl == jnp.argmax(x, axis=-1, keepdims=True).astype(jnp.int32) for x in work]
            work = [jnp.where(h, 3.0 * NEG, x) for h, x in zip(hit, work)]
        thr = [jnp.max(x, axis=-1, keepdims=True) for x in work]
        above = [x > t for x, t in zip(score, thr)]
        tie = [x == t for x, t in zip(score, thr)]
        n_above = [jnp.sum(jnp.where(x, 1.0, 0.0), axis=-1, keepdims=True) for x in above]
        tie_rank = [_dot(jnp.where(x, 1.0, 0.0).astype(BF), tri_ref[0:w, 0:w]) for x in tie]
        for u, (sb, g) in enumerate(units):
            sel = forced[sb] | above[u] | (tie[u] & (tie_rank[u] <= n_pick - n_above[u]))
            for r in range(HEADS_PER_GROUP):
                h0 = (g * HEADS_PER_GROUP + r) * HEAD_DIM
                oc_ref[0, rows[sb], h0:h0 + HEAD_DIM] = o_cmp[u][r * Q_BLOCK:(r + 1) * Q_BLOCK, :HEAD_DIM]
            sel_u = jnp.where(eligible[sb] & sel, 1.0, 0.0)
            sel_ref[0, g, rows[sb], 0:w] = sel_u.astype(BF)
            if w < nb:
                sel_ref[0, g, rows[sb], w:nb] = jnp.zeros((Q_BLOCK, nb - w), BF)
            any_sel = jnp.broadcast_to(jnp.max(sel_u, axis=0, keepdims=True), (8, w)).astype(BF)
            flag_ref[0, g, sb] = _dot(any_sel, pw_ref[0:w]).astype(jnp.int32)

    half = nb // 2
    if half % LANES == 0:
        in_first_half = q0 + SEL_SUB * Q_BLOCK <= half * SLC_BLOCK
        pl.when(in_first_half)(lambda: run(half))
        pl.when(jnp.logical_not(in_first_half))(lambda: run(nb))
    else:
        run(nb)


def _select(q, kc, vc):
    b, s, _ = q.shape
    ncp = kc.shape[2]
    nb = ncp // 4
    nq = s // Q_BLOCK
    step_q = SEL_SUB * Q_BLOCK
    pw = _flag_weights(nb)
    tri = _prefix_weights(nb)
    sds = jax.ShapeDtypeStruct
    cmp_spec = pl.BlockSpec((1, NSA_GROUPS, ncp, LANES), lambda bi, qi: (bi, 0, 0, 0))
    return pl.pallas_call(
        _select_kernel,
        grid=(b, s // step_q),
        in_specs=[pl.BlockSpec((1, step_q, NSA_HEADS * LANES), lambda bi, qi: (bi, qi, 0)),
                  cmp_spec, cmp_spec, _const_spec(pw.shape), _const_spec(tri.shape)],
        out_specs=[pl.BlockSpec((1, step_q, NSA_HEADS * HEAD_DIM), lambda bi, qi: (bi, qi, 0)),
                   pl.BlockSpec((1, NSA_GROUPS, step_q, nb), lambda bi, qi: (bi, 0, qi, 0)),
                   pl.BlockSpec((1, NSA_GROUPS, SEL_SUB, 8, LANES), lambda bi, qi: (bi, 0, qi, 0, 0))],
        out_shape=[sds((b, s, NSA_HEADS * HEAD_DIM), F32), sds((b, NSA_GROUPS, s, nb), BF),
                   sds((b, NSA_GROUPS, nq, 8, LANES), jnp.int32)],
        compiler_params=_params(2),
        name="nsa_select",
    )(q, kc, vc, pw, tri)


def _gate_spread():
    gw = HEADS_PER_GROUP * HEAD_DIM
    src = jnp.arange(LANES)[:, None]
    dst = jnp.arange(3 * gw)[None, :]
    c, r = dst // gw, (dst % gw) // HEAD_DIM
    return (src == 3 * r + c).astype(BF)


def _attend_kernel(flag_ref, q_ref, g_ref, oc_ref, sel_ref, spread_ref, ks_ref, vs_ref, *rest, n_words):
    nwin = (WINDOW + ATT_Q) // KEY_TILE
    kw_refs, vw_refs = rest[:nwin], rest[nwin:2 * nwin]
    o_ref, list_ref = rest[2 * nwin], rest[2 * nwin + 1]
    bi, gi, qi = pl.program_id(0), pl.program_id(1), pl.program_id(2)
    q0 = qi * ATT_Q
    diag0 = qi * (ATT_Q // KEY_TILE)
    rq = HEADS_PER_GROUP * ATT_Q
    nb = sel_ref.shape[3]
    tiles_per_word = FLAG_BITS // TILE_BLOCKS
    tile_bits = (1 << TILE_BLOCKS) - 1
    sel_per_step = ATT_Q // Q_BLOCK

    qs = _stack_heads(q_ref, 0)
    selb = sel_ref[0, 0]
    t1 = q0 + lax.broadcasted_iota(jnp.int32, (ATT_Q, 1), 0)

    base = ((bi * NSA_GROUPS + gi) * pl.num_programs(2) + qi) * sel_per_step * n_words

    def word_body(wi, n):
        word = flag_ref[base + wi]
        for extra in range(1, sel_per_step):
            word = word | flag_ref[base + extra * n_words + wi]

        def scan_word(n):
            def tile_body(u, n):
                kt = wi * tiles_per_word + u
                active = (((word >> (u * TILE_BLOCKS)) & tile_bits) != 0) & (kt < diag0)

                @pl.when(active)
                def _():
                    list_ref[n] = kt
                return n + active.astype(jnp.int32)
            return lax.fori_loop(0, tiles_per_word, tile_body, n)
        return lax.cond(word != 0, scan_word, lambda n: n, n)

    n_active = lax.fori_loop(0, (diag0 + tiles_per_word - 1) // tiles_per_word, word_body, 0)
    for u in range(GROUP_TILES):
        list_ref[n_active + u] = 0

    jrow = lax.broadcasted_iota(jnp.int32, (nb, KEY_TILE), 0)
    kblk = lax.broadcasted_iota(jnp.int32, (nb, KEY_TILE), 1) >> 6
    kcol = lax.broadcasted_iota(jnp.int32, (1, KEY_TILE), 1)

    def masked_scores(k_all, member):
        s_ = _dot_nt(qs, k_all)
        return jnp.concatenate([jnp.where(member, s_[r * ATT_Q:(r + 1) * ATT_Q], NEG)
                                for r in range(HEADS_PER_GROUP)], axis=0)

    def tile_rows(ref, kt, n=KEY_TILE):
        return ref[0, 0, pl.ds(pl.multiple_of(kt * KEY_TILE, KEY_TILE), n), :]

    def group_body(it, carry):
        ks_t, vs_t, ex_t = [], [], []
        for u in range(GROUP_TILES):
            pos = it * GROUP_TILES + u
            kt = list_ref[pos]
            first_blk = jnp.where(pos < n_active, kt * TILE_BLOCKS, -TILE_BLOCKS - nb)
            ks_t.append(tile_rows(ks_ref, kt))
            vs_t.append(tile_rows(vs_ref, kt))
            ex_t.append(jnp.where(jrow == kblk + first_blk, 1.0, 0.0).astype(BF))
        member = _dot(selb, jnp.concatenate(ex_t, axis=1)) > 0.5
        s_ = masked_scores(jnp.concatenate(ks_t, axis=0), member)
        return _softmax_update(carry, s_, jnp.concatenate(vs_t, axis=0))

    init = (jnp.full((rq, 1), NEG, F32), jnp.zeros((rq, LANES), F32))
    carry = lax.fori_loop(0, (n_active + GROUP_TILES - 1) // GROUP_TILES, group_body, init)
    drow = lax.broadcasted_iota(jnp.int32, (nb, ATT_Q), 0)
    dcol = lax.broadcasted_iota(jnp.int32, (1, ATT_Q), 1)
    expand = jnp.where(drow == (dcol >> 6) + diag0 * TILE_BLOCKS, 1.0, 0.0).astype(BF)
    member = (_dot(selb, expand) > 0.5) & (q0 + dcol <= t1)
    kpos = []
    for i in range(nwin):
        kti = diag0 - WINDOW // KEY_TILE + i
        kpos.append(jnp.where(kti >= 0, kti * KEY_TILE, 1 << 24) + kcol)
    d = t1 - jnp.concatenate(kpos, axis=1)
    in_window = (d >= 0) & (d < WINDOW)
    s_d = masked_scores(tile_rows(ks_ref, diag0, ATT_Q), member)
    s_w = masked_scores(jnp.concatenate([r[0, 0] for r in kw_refs], axis=0), in_window)
    m_d = jnp.maximum(carry[0], jnp.max(s_d, axis=-1, keepdims=True))
    m_w = jnp.max(s_w, axis=-1, keepdims=True)
    p_d = jnp.exp(s_d - m_d).astype(BF)
    p_w = jnp.exp(s_w - m_w).astype(BF)
    acc_d = jnp.exp(carry[0] - m_d) * carry[1] + _dot(p_d, tile_rows(vs_ref, diag0, ATT_Q))
    acc_w = _dot(p_w, jnp.concatenate([r[0, 0] for r in vw_refs], axis=0))

    assert LANES == 2 * HEAD_DIM and HEADS_PER_GROUP % 2 == 0
    low_half = lax.broadcasted_iota(jnp.int32, (1, LANES), 1) < HEAD_DIM

    def heads_on_lanes(acc):
        inv = 1.0 / jnp.maximum(acc, 1e-30)
        pairs = []
        for r in range(0, HEADS_PER_GROUP, 2):
            ev, od = slice(r * ATT_Q, (r + 1) * ATT_Q), slice((r + 1) * ATT_Q, (r + 2) * ATT_Q)
            even = acc[ev] * pltpu.roll(inv[ev], HEAD_DIM, axis=1)
            odd = pltpu.roll(acc[od], HEAD_DIM, axis=1) * inv[od]
            pairs.append(jnp.where(low_half, even, odd))
        return jnp.concatenate(pairs, axis=1)

    gw = HEADS_PER_GROUP * HEAD_DIM
    gates = _split_dot(g_ref[0], spread_ref[...])
    out = (gates[:, 0:gw] * oc_ref[0] + gates[:, gw:2 * gw] * heads_on_lanes(acc_d)
           + gates[:, 2 * gw:3 * gw] * heads_on_lanes(acc_w))
    o_ref[0] = out.astype(BF)


def _attend(flags, q, gates, o_cmp, sel, ks, vs, kw, vw):
    b, s, _ = q.shape
    nb = sel.shape[3]
    n_words = nb // FLAG_BITS
    nwin = (WINDOW + ATT_Q) // KEY_TILE
    per_q = ATT_Q // KEY_TILE
    gw = HEADS_PER_GROUP * LANES
    flat = flags[:, :, :, 0, :n_words].reshape(-1)
    spread = _gate_spread()
    full = pl.BlockSpec((1, 1, s, LANES), lambda bi, gi, qi, f: (bi, gi, 0, 0))
    win = lambda i: pl.BlockSpec(
        (1, 1, KEY_TILE, LANES),
        lambda bi, gi, qi, f: (bi, gi, jnp.maximum(qi * per_q - WINDOW // KEY_TILE + i, 0), 0))
    grid_spec = pltpu.PrefetchScalarGridSpec(
        num_scalar_prefetch=1,
        grid=(b, NSA_GROUPS, s // ATT_Q),
        in_specs=[pl.BlockSpec((1, ATT_Q, gw), lambda bi, gi, qi, f: (bi, qi, gi)),
                  pl.BlockSpec((1, ATT_Q, LANES), lambda bi, gi, qi, f: (bi, qi, gi)),
                  pl.BlockSpec((1, ATT_Q, HEADS_PER_GROUP * HEAD_DIM), lambda bi, gi, qi, f: (bi, qi, gi)),
                  pl.BlockSpec((1, 1, ATT_Q, nb), lambda bi, gi, qi, f: (bi, gi, qi, 0)),
                  pl.BlockSpec(spread.shape, lambda bi, gi, qi, f: (0, 0)),
                  full, full] + [win(i) for i in range(nwin)] * 2,
        out_specs=pl.BlockSpec((1, ATT_Q, HEADS_PER_GROUP * HEAD_DIM), lambda bi, gi, qi, f: (bi, qi, gi)),
        scratch_shapes=[pltpu.SMEM((s // KEY_TILE + GROUP_TILES,), jnp.int32)],
    )
    return pl.pallas_call(
        functools.partial(_attend_kernel, n_words=n_words),
        grid_spec=grid_spec,
        out_shape=jax.ShapeDtypeStruct((b, s, NSA_HEADS * HEAD_DIM), BF),
        compiler_params=_params(3),
        name="nsa_attend",
    )(flat, q, gates, o_cmp, sel, spread, ks, vs, *([kw] * nwin), *([vw] * nwin))


def _head_indicators(d):
    ch = jnp.arange(d)[:, None] // RWKV_HEAD
    ind = (ch == jnp.arange(LANES)[None, :]).astype(BF)
    return ind, ind.T


def _rwkv_pre_kernel(x_ref, pre_ref, mu_ref, wr_ref, wk_ref, wv_ref, w0_ref, wd1_ref, wd2_ref,
                     a0_ref, wa1_ref, wa2_ref, wg1_ref, wg2_ref, kk_ref, ka_ref, ind_ref, indt_ref,
                     r_out, lw_out, k_out, v_out, kk_out, kka_out, g_out, hprev_ref, *, n_sub):
    si = pl.program_id(1)
    tm = x_ref.shape[1]

    @pl.when(si == 0)
    def _():
        hprev_ref[...] = jnp.zeros(hprev_ref.shape, F32)

    h_all = _rmsnorm(x_ref[0], pre_ref[...])
    rows = lax.broadcasted_iota(jnp.int32, (tm, 1), 0)
    xx_all = jnp.where(rows == 0, hprev_ref[7:8], pltpu.roll(h_all, 1, axis=0)) - h_all
    hprev_ref[...] = h_all[tm - 8:tm]

    def sub_tile(rs):
        h, xx = h_all[rs], xx_all[rs]
        mix = lambda i: (h + xx * mu_ref[i:i + 1]).astype(BF)
        xr, xw, xk, xv, xa, xg = [mix(i) for i in range(6)]
        r = _dot(xr, wr_ref[...])
        k = _dot(xk, wk_ref[...])
        v = _dot(xv, wv_ref[...])
        yield
        z = w0_ref[...] + _dot(jnp.tanh(_dot(xw, wd1_ref[...])).astype(BF), wd2_ref[...])
        softplus = jnp.maximum(-z, 0.0) + jnp.log(1.0 + jnp.exp(-jnp.abs(z)))
        lw = -jnp.exp(-softplus - 0.5)
        a = _sigmoid(a0_ref[...] + _dot(_dot(xa, wa1_ref[...]).astype(BF), wa2_ref[...]))
        g = _dot(_sigmoid(_dot(xg, wg1_ref[...])).astype(BF), wg2_ref[...])
        yield
        kraw = k * kk_ref[...]
        ss = _dot((kraw * kraw).astype(BF), ind_ref[...])
        yield
        inv = lax.rsqrt(jnp.maximum(ss, 1e-12))
        kk = kraw * _dot(inv.astype(BF), indt_ref[...])
        yield
        r_out[0, rs] = r.astype(BF)
        lw_out[0, rs] = lw
        k_out[0, rs] = (k * (1.0 + (a - 1.0) * ka_ref[...])).astype(BF)
        v_out[0, rs] = v.astype(BF)
        kk_out[0, rs] = kk.astype(BF)
        kka_out[0, rs] = (kk * a).astype(BF)
        g_out[0, rs] = g.astype(BF)

    _round_robin(sub_tile(slice(i * tm // n_sub, (i + 1) * tm // n_sub)) for i in range(n_sub))


def _rwkv_pre(x, pre_g, mu, w_r, w_k, w_v, w0, w_dec1, w_dec2, a0, w_a1, w_a2, w_g1, w_g2,
              k_k, k_a, ind, indt, *, tm=512, n_sub=1):
    b, s, d = x.shape
    row = pl.BlockSpec((1, tm, d), lambda bi, si: (bi, si, 0))
    vec = lambda a: a.reshape(1, d)
    ops = [x, vec(pre_g), mu, w_r.astype(BF), w_k.astype(BF), w_v.astype(BF), vec(w0),
           w_dec1.astype(BF), w_dec2.astype(BF), vec(a0), w_a1.astype(BF), w_a2.astype(BF),
           w_g1.astype(BF), w_g2.astype(BF), vec(k_k), vec(k_a), ind, indt]
    return pl.pallas_call(
        functools.partial(_rwkv_pre_kernel, n_sub=n_sub),
        grid=(b, s // tm),
        in_specs=[row] + [_const_spec(o.shape) for o in ops[1:]],
        out_specs=[row] * 7,
        out_shape=[jax.ShapeDtypeStruct((b, s, d), F32 if i == 1 else BF) for i in range(7)],
        scratch_shapes=[pltpu.VMEM((8, d), F32)],
        compiler_params=_params(2),
        name="rwkv_pre",
    )(*ops)


def _rwkv_scan_kernel(r_ref, lw_ref, k_ref, v_ref, kk_ref, kka_ref, y_ref, state_ref):
    c = RWKV_CHUNK
    slab = SLAB_HEADS * RWKV_HEAD
    n_slab = r_ref.shape[2] // slab
    rows = SLAB_HEADS * c
    subs = [slice(i * c, (i + 1) * c) for i in range(r_ref.shape[1] // c)]

    @pl.when(pl.program_id(1) == 0)
    def _():
        state_ref[...] = jnp.zeros(state_ref.shape, F32)

    ri = lax.broadcasted_iota(jnp.int32, (rows, 1), 0)
    ci = lax.broadcasted_iota(jnp.int32, (1, rows), 1)
    li = lax.broadcasted_iota(jnp.int32, (1, slab), 1)
    c_shift = c.bit_length() - 1
    h_shift = RWKV_HEAD.bit_length() - 1
    assert c == 1 << c_shift and RWKV_HEAD == 1 << h_shift
    head_match = (ri >> c_shift) == (li >> h_shift)
    t_r, t_c = ri & (c - 1), ci & (c - 1)
    strict = t_c < t_r
    incl = t_c <= t_r
    eye = jnp.where(ri == ci, 1.0, 0.0)
    tri = jnp.where(lax.broadcasted_iota(jnp.int32, (c, c), 1) <= lax.broadcasted_iota(jnp.int32, (c, c), 0),
                    1.0, 0.0).astype(BF)
    state_match = (lax.broadcasted_iota(jnp.int32, (slab, 1), 0) >> h_shift) == (li >> h_shift)

    incl2 = jnp.concatenate([incl, incl], axis=1)

    def block_diag(x):
        return jnp.where(head_match, jnp.concatenate([x] * SLAB_HEADS, axis=0), 0.0)

    def fold(x):
        out = x[0:c]
        for h in range(1, SLAB_HEADS):
            out = out + x[h * c:(h + 1) * c]
        return out

    bf = lambda x: x.astype(BF)
    slabs = [slice(hs * slab, (hs + 1) * slab) for hs in range(n_slab)]

    def prepare(rs, sl):
        lw = lw_ref[0, rs, sl]
        hi = bf(lw)
        rem = lw - hi.astype(F32)
        mid = bf(rem)
        lo = bf(rem - mid.astype(F32))
        cw = _dot(tri, hi) + _dot(tri, mid) + _dot(tri, lo)
        w_in = jnp.exp(cw)
        w_inv = jnp.exp(-cw)
        w_prev = jnp.exp(cw - lw)
        w_end = w_in[c - 1:c]

        a_t = -kk_ref[0, rs, sl].astype(F32) * w_prev
        b_t = kka_ref[0, rs, sl].astype(F32) * w_inv
        k_t = k_ref[0, rs, sl].astype(F32) * w_inv
        r_t = r_ref[0, rs, sl].astype(F32) * w_in
        return a_t, b_t, k_t, r_t, w_end

    def interactions(a_t, b_t, k_t, r_t):
        lhs = bf(jnp.concatenate([block_diag(a_t), block_diag(r_t)], axis=0))
        rhs = bf(jnp.concatenate([block_diag(b_t), block_diag(k_t)], axis=0))
        big = _dot_nt(lhs, rhs)
        a_ab = jnp.where(strict, big[:rows, :rows], 0.0)
        a_ak = jnp.where(strict, big[:rows, rows:], 0.0)
        a_r = jnp.where(incl2, big[rows:], 0.0)
        return a_ab, bf(a_ak), bf(a_r)

    every = range(n_slab)
    units = [(rs, sl) for rs in subs for sl in slabs]
    prep = [prepare(rs, sl) for rs, sl in units]
    inter = [interactions(*p[:4]) for p in prep]
    vs = [v_ref[0, rs, sl] for rs, sl in units]
    vbd = [bf(block_diag(v)) for v in vs]
    akv = [_dot(inter[u][1], vbd[u]) for u in range(len(units))]

    levels = c.bit_length() - 2
    assert levels >= 1
    invs = [eye + x[0] for x in inter]
    pws = [_dot(bf(x[0]), bf(x[0])) for x in inter]
    for _ in range(levels - 1):
        both = [_dot(bf(pw), bf(jnp.concatenate([pw, inv], axis=1))) for pw, inv in zip(pws, invs)]
        pws = [x[:, :rows] for x in both]
        invs = [inv + x[:, rows:] for inv, x in zip(invs, both)]
    invs = [bf(inv + _dot(bf(pw), bf(inv))) for pw, inv in zip(pws, invs)]

    states = [state_ref[hs] for hs in every]
    for si, rs in enumerate(subs):
        us = [si * n_slab + hs for hs in every]
        ar = [_dot_nt(bf(jnp.concatenate([prep[u][0], prep[u][3]], axis=0)), bf(states[hs]))
              for hs, u in zip(every, us)]
        u_bd = [_dot(invs[u], bf(block_diag(ar[hs][:c]) + akv[u])) for hs, u in zip(every, us)]
        y_bd = [block_diag(ar[hs][c:]) + _dot(inter[u][2], jnp.concatenate([bf(u_bd[hs]), vbd[u]], axis=0))
                for hs, u in zip(every, us)]
        for hs, sl in enumerate(slabs):
            y_ref[0, rs, sl] = fold(y_bd[hs])
        new_states = []
        for hs, u in zip(every, us):
            _, b_t, k_t, _, w_end = prep[u]
            uv = jnp.concatenate([bf(fold(u_bd[hs])), bf(vs[u])], axis=0)
            bk = jnp.concatenate([bf(b_t * w_end), bf(k_t * w_end)], axis=0)
            new_states.append(jnp.where(state_match, states[hs] * w_end + _dot_tn(uv, bk), 0.0))
        states = new_states
    for hs in every:
        state_ref[hs] = states[hs]


def _rwkv_scan(r, lw, k, v, kk, kka):
    b, s, d = r.shape
    c = RWKV_CHUNK * SCAN_SUB
    slab = SLAB_HEADS * RWKV_HEAD
    blk = pl.BlockSpec((1, c, d), lambda bi, ci: (bi, ci, 0))
    return pl.pallas_call(
        _rwkv_scan_kernel,
        grid=(b, s // c),
        in_specs=[blk] * 6,
        out_specs=blk,
        out_shape=jax.ShapeDtypeStruct((b, s, d), F32),
        scratch_shapes=[pltpu.VMEM((d // slab, slab, slab), F32)],
        compiler_params=_params(2),
        name="rwkv_scan",
    )(r, lw, k, v, kk, kka)


def _rwkv_post_kernel(x_ref, y_ref, r_ref, k_ref, v_ref, g_ref, wo_ref, lng_ref, lnb_ref, rk_ref,
                      post_ref, ind_ref, indt_ref, o_ref, *, n_sub):
    ind, indt = ind_ref[...], indt_ref[...]
    inv_n = 1.0 / RWKV_HEAD
    head_sum = lambda a: _dot(a.astype(BF), ind)
    spread = lambda a: _dot(a.astype(BF), indt)
    tm = x_ref.shape[0]

    def sub_tile(rs):
        f32 = lambda ref: ref[rs].astype(F32)
        y = y_ref[rs]
        sums = head_sum(y)
        rk_sum = head_sum(f32(r_ref) * f32(k_ref) * rk_ref[...])
        yield
        mean = _split_dot(sums * inv_n, indt)
        bonus = spread(rk_sum) * f32(v_ref)
        yield
        yc = y - mean
        var = head_sum(yc * yc) * inv_n
        yield
        rstd = spread(lax.rsqrt(var + LNX_EPS))
        yield
        yn = yc * rstd * lng_ref[...] + lnb_ref[...]
        out = _dot(((yn + bonus) * f32(g_ref)).astype(BF), wo_ref[...])
        yield
        o_ref[rs] = x_ref[rs] + _rmsnorm(out, post_ref[...])

    _round_robin(sub_tile(slice(i * tm // n_sub, (i + 1) * tm // n_sub)) for i in range(n_sub))


def _rwkv_post(x2, y, r, k, v, g, w_o, lnx_g, lnx_b, r_k, post_g, ind, indt, *, tm=512, n_sub=2):
    t, d = x2.shape
    row = pl.BlockSpec((tm, d), lambda i: (i, 0))
    vec = lambda a: a.reshape(1, d)
    ops = [x2, y, r, k, v, g, w_o.astype(BF), vec(lnx_g), vec(lnx_b), vec(r_k), vec(post_g), ind, indt]
    return pl.pallas_call(
        functools.partial(_rwkv_post_kernel, n_sub=n_sub),
        grid=(t // tm,),
        in_specs=[row] * 6 + [_const_spec(o.shape) for o in ops[6:]],
        out_specs=row,
        out_shape=jax.ShapeDtypeStruct((t, d), F32),
        compiler_params=_params(1),
        name="rwkv_post",
    )(*ops)


def _layer0_mixer(x, pre_g, w_in, pe_k, w1_k, w2_k, pe_v, w1_v, w2_v, conv_w, conv_b):
    b, s, _ = x.shape
    q, gates, kcr, vcr, ks, vs, kw, vw, o_conv = _l0_in(x, pre_g, _l0_in_weight(w_in), conv_w, conv_b)
    kc = _compress(kcr, _cmp_weights(pe_k, w1_k, w2_k), is_key=True)
    vc = _compress(vcr, _cmp_weights(pe_v, w1_v, w2_v), is_key=False)
    o_cmp, sel, flags = _select(q, kc, vc)
    o_nsa = _attend(flags, q, gates, o_cmp, sel, ks, vs, kw, vw)
    return o_nsa.reshape(b * s, -1), o_conv.reshape(b * s, -1)


def _layer1_mixer(x, pre_g, post_g, mu, w_r, w_k, w_v, w_o, w0, w_dec1, w_dec2, a0, w_a1, w_a2,
                  w_g1, w_g2, k_k, k_a, r_k, lnx_g, lnx_b):
    b, s, d = x.shape
    ind, indt = _head_indicators(d)
    r, lw, k, v, kk, kka, g = _rwkv_pre(x, pre_g, mu, w_r, w_k, w_v, w0, w_dec1, w_dec2, a0,
                                        w_a1, w_a2, w_g1, w_g2, k_k, k_a, ind, indt)
    y = _rwkv_scan(r, lw, k, v, kk, kka)
    f = lambda a: a.reshape(b * s, d)
    return _rwkv_post(f(x), f(y), f(r), f(k), f(v), f(g), w_o, lnx_g, lnx_b, r_k, post_g,
                      ind, indt).reshape(b, s, d)


def kernel(x, l0_ffn1_pre_g, l0_ffn1_post_g, l0_ffn1_w_gate, l0_ffn1_w_up, l0_ffn1_w_down, l0_mix_pre_g, l0_mix_post_g, l0_w_in, l0_cmp_pe_k, l0_cmp_w1_k, l0_cmp_w2_k, l0_cmp_pe_v, l0_cmp_w1_v, l0_cmp_w2_v, l0_conv_w, l0_conv_b, l0_w_out, l0_ffn2_pre_g, l0_ffn2_post_g, l0_ffn2_w_gate, l0_ffn2_w_up, l0_ffn2_w_down, l1_ffn1_pre_g, l1_ffn1_post_g, l1_ffn1_w_gate, l1_ffn1_w_up, l1_ffn1_w_down, l1_mix_pre_g, l1_mix_post_g, l1_mu, l1_w_r, l1_w_k, l1_w_v, l1_w_o, l1_w0, l1_w_dec1, l1_w_dec2, l1_a0, l1_w_a1, l1_w_a2, l1_w_g1, l1_w_g2, l1_k_k, l1_k_a, l1_r_k, l1_lnx_g, l1_lnx_b, l1_ffn2_pre_g, l1_ffn2_post_g, l1_ffn2_w_gate, l1_ffn2_w_up, l1_ffn2_w_down):
    b, s, d = x.shape
    ffn = lambda a, *w, **kw: _ffn(a.reshape(b * s, d), *w, **kw).reshape(b, s, d)
    x = ffn(x, l0_ffn1_pre_g, l0_ffn1_post_g, l0_ffn1_w_gate, l0_ffn1_w_up, l0_ffn1_w_down)
    o_nsa, o_conv = _layer0_mixer(x, l0_mix_pre_g, l0_w_in, l0_cmp_pe_k, l0_cmp_w1_k, l0_cmp_w2_k,
                                  l0_cmp_pe_v, l0_cmp_w1_v, l0_cmp_w2_v, l0_conv_w, l0_conv_b)
    x = ffn(x, l0_ffn2_pre_g, l0_ffn2_post_g, l0_ffn2_w_gate, l0_ffn2_w_up, l0_ffn2_w_down,
            mixer=(o_nsa, o_conv, l0_w_out, l0_mix_post_g))
    x = ffn(x, l1_ffn1_pre_g, l1_ffn1_post_g, l1_ffn1_w_gate, l1_ffn1_w_up, l1_ffn1_w_down)
    x = _layer1_mixer(x, l1_mix_pre_g, l1_mix_post_g, l1_mu, l1_w_r, l1_w_k, l1_w_v, l1_w_o, l1_w0,
                      l1_w_dec1, l1_w_dec2, l1_a0, l1_w_a1, l1_w_a2, l1_w_g1, l1_w_g2, l1_k_k, l1_k_a,
                      l1_r_k, l1_lnx_g, l1_lnx_b)
    x = ffn(x, l1_ffn2_pre_g, l1_ffn2_post_g, l1_ffn2_w_gate, l1_ffn2_w_up, l1_ffn2_w_down)
    return x
```

```python
import functools

import jax
import jax.numpy as jnp
from jax import lax
from jax.experimental import pallas as pl
from jax.experimental.pallas import tpu as pltpu

BF = jnp.bfloat16
F32 = jnp.float32

EPS = 1e-6
LNX_EPS = 64e-5
HEAD_DIM = 64
NSA_HEADS = 8
NSA_GROUPS = 2
HEADS_PER_GROUP = NSA_HEADS // NSA_GROUPS
CMP_STRIDE = 16
CMP_BLOCK = 32
CMP_HIDDEN = 128
SLC_BLOCK = 64
SLC_TOP_N = 16
WINDOW = 512
Q_BLOCK = 128
CONV_WIDTH = 512
RWKV_HEAD = 64
RWKV_CHUNK = 64
SLAB_HEADS = 2
SCAN_SUB = 4
DECAY_SCALE = 0.6065306597126334
LANES = 128
MXU_DIM = 256
POS_SHIFT = 7
POS_SPLIT = 1 << POS_SHIFT
NEG = -1e30
VMEM_LIMIT = 56 * 1024 * 1024

NT_DIMS = (((1,), (1,)), ((), ()))
TN_DIMS = (((0,), (0,)), ((), ()))


def _dot(a, b):
    return jnp.dot(a, b, preferred_element_type=F32)


def _dot_nt(a, b):
    return lax.dot_general(a, b, NT_DIMS, preferred_element_type=F32)


def _dot_tn(a, b):
    return lax.dot_general(a, b, TN_DIMS, preferred_element_type=F32)


def _split_dot(x, w):
    hi = x.astype(BF)
    lo = (x - hi.astype(F32)).astype(BF)
    return _dot(hi, w) + _dot(lo, w)


def _rmsnorm(x, g):
    ms = jnp.mean(x * x, axis=-1, keepdims=True)
    return x * lax.rsqrt(ms + EPS) * g


def _sigmoid(x):
    return 1.0 / (1.0 + jnp.exp(-x))


def _round_robin(stage_generators):
    pending = list(stage_generators)
    while pending:
        still = []
        for gen in pending:
            try:
                next(gen)
                still.append(gen)
            except StopIteration:
                pass
        pending = still


def _const_spec(shape):
    zeros = (0,) * len(shape)
    return pl.BlockSpec(shape, lambda *_: zeros)


def _params(n_grid):
    return pltpu.CompilerParams(dimension_semantics=("arbitrary",) * n_grid,
                                vmem_limit_bytes=VMEM_LIMIT)


def _ffn_kernel(*refs, ff_chunk, n_sub, after_mixer):
    if after_mixer:
        x_ref, a_ref, c_ref, wa_ref, wc_ref, mixpost_ref = refs[:6]
        refs = refs[:1] + refs[6:]
    x_ref, pre_ref, post_ref, wg_ref, wu_ref, wd_ref, o_ref = refs

    def residual_rows(rs):
        if not after_mixer:
            return x_ref[rs]
        y = _dot(a_ref[rs], wa_ref[...]) + _dot(c_ref[rs], wc_ref[...])
        return x_ref[rs] + _rmsnorm(y, mixpost_ref[...])

    sub = x_ref.shape[0] // n_sub
    dff = wg_ref.shape[1]
    chunks = [(c0, min(c0 + ff_chunk, dff)) for c0 in range(0, dff, ff_chunk)]
    rows = [slice(i * sub, (i + 1) * sub) for i in range(n_sub)]
    xs, hs, accs = [], [], []
    for ci, (c0, c1) in enumerate(chunks):
        gu = []
        for i, rs in enumerate(rows):
            if ci == 0:
                xs.append(residual_rows(rs))
                hs.append(_rmsnorm(xs[i], pre_ref[...]).astype(BF))
                accs.append(jnp.zeros(xs[i].shape, F32))
            gu.append((_dot(hs[i], wg_ref[:, c0:c1]), _dot(hs[i], wu_ref[:, c0:c1])))
        for i, (g, u) in enumerate(gu):
            a = (g * _sigmoid(g) * u).astype(BF)
            accs[i] = accs[i] + _dot(a, wd_ref[c0:c1, :])
    for i, rs in enumerate(rows):
        o_ref[rs] = xs[i] + 0.5 * _rmsnorm(accs[i], post_ref[...])


def _ffn(x2, pre_g, post_g, w_gate, w_up, w_down, *, mixer=None, tm=1024, n_sub=2):
    t, d = x2.shape
    dff = w_gate.shape[1]
    ff_chunk = MXU_DIM * pl.cdiv(pl.cdiv(dff, MXU_DIM), 2)
    row = lambda w: pl.BlockSpec((tm, w), lambda i: (i, 0))
    weight = lambda shape: pl.BlockSpec(shape, lambda i: (0, 0), pipeline_mode=pl.Buffered(1))
    ops, specs = [x2], [row(d)]
    if mixer is not None:
        o_nsa, o_conv, w_out, mix_post_g = mixer
        na, nc = o_nsa.shape[1], o_conv.shape[1]
        ops += [o_nsa, o_conv, w_out[:na].astype(BF), w_out[na:].astype(BF), mix_post_g.reshape(1, d)]
        specs += [row(na), row(nc), weight((na, d)), weight((nc, d)), _const_spec((1, d))]
    ops += [pre_g.reshape(1, d), post_g.reshape(1, d), w_gate.astype(BF), w_up.astype(BF), w_down.astype(BF)]
    specs += [_const_spec((1, d)), _const_spec((1, d)), weight((d, dff)), weight((d, dff)), weight((dff, d))]
    return pl.pallas_call(
        functools.partial(_ffn_kernel, ff_chunk=ff_chunk, n_sub=n_sub, after_mixer=mixer is not None),
        grid=(t // tm,),
        in_specs=specs,
        out_specs=row(d),
        out_shape=jax.ShapeDtypeStruct((t, d), F32),
        compiler_params=_params(1),
        name="ffn",
    )(*ops)


_C_Q = 0
_C_G = _C_Q + NSA_HEADS * LANES
_C_KC = _C_G + NSA_GROUPS * LANES
_C_VC = _C_KC + LANES
_C_KS = _C_VC + LANES
_C_CONV = _C_KS + 4 * NSA_GROUPS * LANES
_C_END = _C_CONV + 3 * CONV_WIDTH


def _l0_in_weight(w_in):
    d = w_in.shape[0]
    nsa_w = NSA_HEADS * HEAD_DIM
    kv_w = NSA_GROUPS * HEAD_DIM
    o = 0
    q = w_in[:, o:o + nsa_w]; o += nsa_w
    g = w_in[:, o:o + 3 * NSA_HEADS]; o += 3 * NSA_HEADS
    kvs = []
    for _ in range(6):
        kvs.append(w_in[:, o:o + kv_w]); o += kv_w
    conv = w_in[:, o:]
    zpad = lambda w, n: jnp.pad(w, ((0, 0), (0, n - w.shape[1])))
    cols = [zpad(q[:, h * HEAD_DIM:(h + 1) * HEAD_DIM], LANES) for h in range(NSA_HEADS)]
    gpg = 3 * HEADS_PER_GROUP
    cols += [zpad(g[:, i * gpg:(i + 1) * gpg], LANES) for i in range(NSA_GROUPS)]
    cols += [kvs[0], kvs[1]]
    for w in kvs[2:]:
        cols += [zpad(w[:, i * HEAD_DIM:(i + 1) * HEAD_DIM], LANES) for i in range(NSA_GROUPS)]
    cols.append(conv)
    w = jnp.concatenate(cols, axis=1)
    assert w.shape == (d, _C_END)
    return w.astype(BF)


def _l0_in_kernel(x_ref, pre_ref, w_ref, cw_ref, cb_ref,
                  q_ref, g_ref, kcr_ref, vcr_ref, ks_ref, vs_ref, kw_ref, vw_ref, oc_ref,
                  zprev_ref):
    si = pl.program_id(1)
    tm = x_ref.shape[1]
    h = _rmsnorm(x_ref[0], pre_ref[...]).astype(BF)
    lane = lax.broadcasted_iota(jnp.int32, (1, LANES), 1)
    pos = si * tm + lax.broadcasted_iota(jnp.int32, (tm, 1), 0)
    pos_hi = (pos >> POS_SHIFT).astype(F32)
    pos_lo = (pos & (POS_SPLIT - 1)).astype(F32)

    wide = 4 * LANES

    def project(c0):
        res = _dot(h, w_ref[:, c0:c0 + wide])
        return [res[:, j * LANES:(j + 1) * LANES] for j in range(4)]

    for hd0 in range(0, NSA_HEADS, 4):
        for j, qh in enumerate(project(_C_Q + hd0 * LANES)):
            hd = hd0 + j
            slope = 2.0 ** (-(hd + 1))
            qh = qh * (HEAD_DIM ** -0.5)
            qh = jnp.where(lane == HEAD_DIM, POS_SPLIT * slope, jnp.where(lane == HEAD_DIM + 1, slope, qh))
            q_ref[0, :, hd * LANES:(hd + 1) * LANES] = qh.astype(BF)

    assert _C_KC == _C_G + 2 * LANES and _C_VC == _C_KC + LANES and NSA_GROUPS == 2
    g0, g1, kcr, vcr = project(_C_G)
    g_ref[0] = _sigmoid(jnp.concatenate([g0, g1], axis=1))
    kcr_ref[0] = kcr
    vcr_ref[0] = vcr

    slabs = project(_C_KS) + project(_C_KS + wide)
    for n, (ref, is_key) in enumerate(((ks_ref, True), (vs_ref, False), (kw_ref, True), (vw_ref, False))):
        for gi in range(NSA_GROUPS):
            t = slabs[n * NSA_GROUPS + gi]
            if is_key:
                t = jnp.where(lane == HEAD_DIM, pos_hi, jnp.where(lane == HEAD_DIM + 1, pos_lo, t))
            else:
                t = jnp.where(lane >= HEAD_DIM, 1.0, t)
            ref[0, gi] = t.astype(BF)

    @pl.when(si == 0)
    def _():
        zprev_ref[...] = jnp.zeros(zprev_ref.shape, F32)

    cb = _dot(h, w_ref[:, _C_CONV:_C_CONV + CONV_WIDTH])
    cc = _dot(h, w_ref[:, _C_CONV + CONV_WIDTH:_C_CONV + 2 * CONV_WIDTH])
    cu = _dot(h, w_ref[:, _C_CONV + 2 * CONV_WIDTH:_C_CONV + 3 * CONV_WIDTH])
    z = cc * cu
    prev = zprev_ref[...]
    rows = lax.broadcasted_iota(jnp.int32, (tm, 1), 0)
    zm1 = jnp.where(rows == 0, prev[7:8], pltpu.roll(z, 1, axis=0))
    zm2 = jnp.where(rows == 0, prev[6:7], jnp.where(rows == 1, prev[7:8], pltpu.roll(z, 2, axis=0)))
    y = cw_ref[0:1] * zm2 + cw_ref[1:2] * zm1 + cw_ref[2:3] * z
    oc_ref[0] = (cb * (y + cb_ref[...])).astype(BF)
    zprev_ref[...] = z[tm - 8:tm]


def _l0_in(x, pre_g, w_all, conv_w, conv_b, *, tm=512):
    b, s, d = x.shape
    row = lambda w: pl.BlockSpec((1, tm, w), lambda bi, si: (bi, si, 0))
    grp = pl.BlockSpec((1, NSA_GROUPS, tm, LANES), lambda bi, si: (bi, 0, si, 0))
    sds = jax.ShapeDtypeStruct
    kv_shape = sds((b, NSA_GROUPS, s, LANES), BF)
    return pl.pallas_call(
        _l0_in_kernel,
        grid=(b, s // tm),
        in_specs=[row(d), _const_spec((1, d)), _const_spec(w_all.shape),
                  _const_spec(conv_w.shape), _const_spec((1, CONV_WIDTH))],
        out_specs=[row(NSA_HEADS * LANES), row(NSA_GROUPS * LANES), row(LANES), row(LANES),
                   grp, grp, grp, grp, row(CONV_WIDTH)],
        out_shape=[sds((b, s, NSA_HEADS * LANES), BF), sds((b, s, NSA_GROUPS * LANES), F32),
                   sds((b, s, LANES), F32), sds((b, s, LANES), F32),
                   kv_shape, kv_shape, kv_shape, kv_shape, sds((b, s, CONV_WIDTH), BF)],
        scratch_shapes=[pltpu.VMEM((8, CONV_WIDTH), F32)],
        compiler_params=_params(2),
        name="l0_in",
    )(x, pre_g.reshape(1, d), w_all, conv_w, conv_b.reshape(1, CONV_WIDTH))


def _cmp_weights(pe, w1, w2):
    half = CMP_STRIDE * HEAD_DIM
    def expand(w):
        w = w.reshape(CMP_STRIDE, 1, HEAD_DIM, 1, CMP_HIDDEN)
        same_group = jnp.eye(NSA_GROUPS, dtype=F32).reshape(1, NSA_GROUPS, 1, NSA_GROUPS, 1)
        return (w * same_group).reshape(CMP_STRIDE * NSA_GROUPS * HEAD_DIM, NSA_GROUPS * CMP_HIDDEN).astype(BF)
    pe_row = lambda p: jnp.tile(p, (1, NSA_GROUPS)).reshape(1, CMP_STRIDE * NSA_GROUPS * HEAD_DIM)
    w2p = jnp.pad(w2, ((0, 0), (0, LANES - HEAD_DIM))).astype(BF)
    return (pe_row(pe[:CMP_STRIDE]), pe_row(pe[CMP_STRIDE:]), expand(w1[:half]), expand(w1[half:]), w2p)


def _gelu_tanh(x):
    return 0.5 * x * (1.0 + jnp.tanh(0.7978845608028654 * (x + 0.044715 * (x * x * x))))


def _cmp_kernel(x_ref, pet_ref, peb_ref, wt_ref, wb_ref, w2_ref, o_ref, *, is_key):
    nb = x_ref.shape[1]
    hw = x_ref.shape[2] // 4
    top, bot = [], []
    for c in range(4):
        xc = x_ref[0, :, c * hw:(c + 1) * hw]
        top.append(_dot((xc + pet_ref[...]).astype(BF), wt_ref[...]))
        bot.append(_dot((xc + peb_ref[...]).astype(BF), wb_ref[...]))
    lane = lax.broadcasted_iota(jnp.int32, (1, LANES), 1)
    j = lax.broadcasted_iota(jnp.int32, (nb, 1), 0)
    for c in range(4):
        nxt = bot[c + 1] if c < 3 else pltpu.roll(bot[0], nb - 1, axis=0)
        hid = _gelu_tanh(top[c] + nxt)
        n = 4 * j + c
        end = n * CMP_STRIDE + (CMP_BLOCK - 1)
        exists = n < 4 * nb - 1
        for gi in range(NSA_GROUPS):
            t = _dot(hid[:, gi * CMP_HIDDEN:(gi + 1) * CMP_HIDDEN].astype(BF), w2_ref[...])
            t = jnp.where(exists, t, 0.0)
            if is_key:
                t = jnp.where(lane == HEAD_DIM, (end >> POS_SHIFT).astype(F32),
                              jnp.where(lane == HEAD_DIM + 1, (end & (POS_SPLIT - 1)).astype(F32), t))
            else:
                t = jnp.where(lane == HEAD_DIM, 1.0, t)
            o_ref[0, gi, c * nb:(c + 1) * nb, :] = t.astype(BF)


def _compress(raw, weights, *, is_key):
    b, s, _ = raw.shape
    nb = s // (4 * CMP_STRIDE)
    x = raw.reshape(b, nb, 4 * CMP_STRIDE * LANES)
    pet, peb, wt, wb, w2p = weights
    return pl.pallas_call(
        functools.partial(_cmp_kernel, is_key=is_key),
        grid=(b,),
        in_specs=[pl.BlockSpec((1, nb, x.shape[2]), lambda bi: (bi, 0, 0)),
                  _const_spec(pet.shape), _const_spec(peb.shape), _const_spec(wt.shape),
                  _const_spec(wb.shape), _const_spec(w2p.shape)],
        out_specs=pl.BlockSpec((1, NSA_GROUPS, 4 * nb, LANES), lambda bi: (bi, 0, 0, 0)),
        out_shape=jax.ShapeDtypeStruct((b, NSA_GROUPS, 4 * nb, LANES), BF),
        compiler_params=_params(1),
        name="compress",
    )(x, pet, peb, wt, wb, w2p)


def _softmax_update(carry, s, v):
    m, acc = carry
    m_new = jnp.maximum(m, jnp.max(s, axis=-1, keepdims=True))
    p = jnp.exp(s - m_new)
    acc = jnp.exp(m - m_new) * acc + _dot(p.astype(BF), v)
    return m_new, acc


SEL_SUB = 2
N_FORCED = 3
FLAG_BITS = 16
ATT_Q = 256
KEY_TILE = 128
TILE_BLOCKS = KEY_TILE // SLC_BLOCK
GROUP_TILES = 4


def _stack_heads(q_ref, g):
    base = g * HEADS_PER_GROUP
    return jnp.concatenate([q_ref[0, :, (base + r) * LANES:(base + r + 1) * LANES]
                            for r in range(HEADS_PER_GROUP)], axis=0)


def _flag_weights(nb):
    j = jnp.arange(nb)[:, None]
    w = jnp.arange(LANES)[None, :]
    return jnp.where(j // FLAG_BITS == w, 2.0 ** (j % FLAG_BITS), 0.0).astype(BF)


def _prefix_weights(nb):
    return (jnp.arange(nb)[:, None] <= jnp.arange(nb)[None, :]).astype(BF)


def _select_kernel(q_ref, kc_ref, vc_ref, pw_ref, tri_ref, oc_ref, sel_ref, flag_ref):
    qi = pl.program_id(1)
    q0 = qi * (SEL_SUB * Q_BLOCK)
    rq = HEADS_PER_GROUP * Q_BLOCK
    nb = kc_ref.shape[2] // 4
    subs = range(SEL_SUB)
    units = [(sb, g) for sb in subs for g in range(NSA_GROUPS)]
    rows = [slice(sb * Q_BLOCK, (sb + 1) * Q_BLOCK) for sb in subs]
    t1 = [q0 + sb * Q_BLOCK + lax.broadcasted_iota(jnp.int32, (Q_BLOCK, 1), 0) for sb in subs]
    t4 = [q0 + sb * Q_BLOCK + (lax.broadcasted_iota(jnp.int32, (rq, 1), 0) & (Q_BLOCK - 1))
          for sb in subs]
    blk_t = [t >> 6 for t in t1]

    def stacked_queries(sb, g):
        base = g * HEADS_PER_GROUP
        return jnp.concatenate([q_ref[0, rows[sb], (base + r) * LANES:(base + r + 1) * LANES]
                                for r in range(HEADS_PER_GROUP)], axis=0)

    def run(w):
        w_shift = w.bit_length() - 1
        assert w == 1 << w_shift and w % LANES == 0
        col = lax.broadcasted_iota(jnp.int32, (1, 4 * w), 1)
        cmp_end = (4 * (col & (w - 1)) + (col >> w_shift)) * CMP_STRIDE + (CMP_BLOCK - 1)
        jl = lax.broadcasted_iota(jnp.int32, (1, w), 1)
        valid = [cmp_end <= t4[sb] for sb in subs]
        forced = [(jl == 0) | (jl == blk_t[sb]) | (jl == blk_t[sb] - 1) for sb in subs]
        eligible = [jl <= blk_t[sb] for sb in subs]
        pickable = [eligible[sb] & ~forced[sb] for sb in subs]
        n_pick = min(SLC_TOP_N, nb) - N_FORCED

        def columns(ref, g):
            if w == nb:
                return ref[0, g]
            return jnp.concatenate([ref[0, g, c * nb:c * nb + w] for c in range(4)], axis=0)

        def importance(p):
            ps = p[0:Q_BLOCK]
            for r in range(1, HEADS_PER_GROUP):
                ps = ps + p[r * Q_BLOCK:(r + 1) * Q_BLOCK]
            parts = [ps[:, c * w:(c + 1) * w] for c in range(4)]
            prev = jnp.where(jl == 0, 0.0, pltpu.roll(parts[3], 1, axis=1))
            return (parts[0] + parts[1] + parts[2] + parts[3]) - 0.5 * parts[3] + 0.5 * prev

        s = [jnp.where(valid[sb], _dot_nt(stacked_queries(sb, g), columns(kc_ref, g)), NEG) for sb, g in units]
        m = [jnp.max(x, axis=-1, keepdims=True) for x in s]
        m = [jnp.where(x <= NEG, 0.0, x) for x in m]
        p = [jnp.exp(x - mx) for x, mx in zip(s, m)]
        p = [x / jnp.maximum(jnp.sum(x, axis=-1, keepdims=True), 1e-30) for x in p]
        o_cmp = [_dot(x.astype(BF), columns(vc_ref, g)) for x, (sb, g) in zip(p, units)]
        score = [jnp.where(pickable[sb], importance(x), NEG) for x, (sb, g) in zip(p, units)]
        work = score
        for _ in range(n_pick - 1):
            hit = [jl == jnp.argmax(x, axis=-1, keepdims=True).astype(jnp.int32) for x in work]
            work = [jnp.where(h, 3.0 * NEG, x) for h, x in zip(hit, work)]
        thr = [jnp.max(x, axis=-1, keepdims=True) for x in work]
        above = [x > t for x, t in zip(score, thr)]
        tie = [x == t for x, t in zip(score, thr)]
        n_above = [jnp.sum(jnp.where(x, 1.0, 0.0), axis=-1, keepdims=True) for x in above]
        tie_rank = [_dot(jnp.where(x, 1.0, 0.0).astype(BF), tri_ref[0:w, 0:w]) for x in tie]
        for u, (sb, g) in enumerate(units):
            sel = forced[sb] | above[u] | (tie[u] & (tie_rank[u] <= n_pick - n_above[u]))
            for r in range(HEADS_PER_GROUP):
                h0 = (g * HEADS_PER_GROUP + r) * HEAD_DIM
                oc_ref[0, rows[sb], h0:h0 + HEAD_DIM] = o_cmp[u][r * Q_BLOCK:(r + 1) * Q_BLOCK, :HEAD_DIM]
            sel_u = jnp.where(eligible[sb] & sel, 1.0, 0.0)
            sel_ref[0, g, rows[sb], 0:w] = sel_u.astype(BF)
            if w < nb:
                sel_ref[0, g, rows[sb], w:nb] = jnp.zeros((Q_BLOCK, nb - w), BF)
            any_sel = jnp.broadcast_to(jnp.max(sel_u, axis=0, keepdims=True), (8, w)).astype(BF)
            flag_ref[0, g, sb] = _dot(any_sel, pw_ref[0:w]).astype(jnp.int32)

    half = nb // 2
    if half % LANES == 0:
        in_first_half = q0 + SEL_SUB * Q_BLOCK <= half * SLC_BLOCK
        pl.when(in_first_half)(lambda: run(half))
        pl.when(jnp.logical_not(in_first_half))(lambda: run(nb))
    else:
        run(nb)


def _select(q, kc, vc):
    b, s, _ = q.shape
    ncp = kc.shape[2]
    nb = ncp // 4
    nq = s // Q_BLOCK
    step_q = SEL_SUB * Q_BLOCK
    pw = _flag_weights(nb)
    tri = _prefix_weights(nb)
    sds = jax.ShapeDtypeStruct
    cmp_spec = pl.BlockSpec((1, NSA_GROUPS, ncp, LANES), lambda bi, qi: (bi, 0, 0, 0))
    return pl.pallas_call(
        _select_kernel,
        grid=(b, s // step_q),
        in_specs=[pl.BlockSpec((1, step_q, NSA_HEADS * LANES), lambda bi, qi: (bi, qi, 0)),
                  cmp_spec, cmp_spec, _const_spec(pw.shape), _const_spec(tri.shape)],
        out_specs=[pl.BlockSpec((1, step_q, NSA_HEADS * HEAD_DIM), lambda bi, qi: (bi, qi, 0)),
                   pl.BlockSpec((1, NSA_GROUPS, step_q, nb), lambda bi, qi: (bi, 0, qi, 0)),
                   pl.BlockSpec((1, NSA_GROUPS, SEL_SUB, 8, LANES), lambda bi, qi: (bi, 0, qi, 0, 0))],
        out_shape=[sds((b, s, NSA_HEADS * HEAD_DIM), F32), sds((b, NSA_GROUPS, s, nb), BF),
                   sds((b, NSA_GROUPS, nq, 8, LANES), jnp.int32)],
        compiler_params=_params(2),
        name="nsa_select",
    )(q, kc, vc, pw, tri)


def _gate_spread():
    gw = HEADS_PER_GROUP * HEAD_DIM
    src = jnp.arange(LANES)[:, None]
    dst = jnp.arange(3 * gw)[None, :]
    c, r = dst // gw, (dst % gw) // HEAD_DIM
    return (src == 3 * r + c).astype(BF)


def _attend_kernel(flag_ref, q_ref, g_ref, oc_ref, sel_ref, spread_ref, ks_ref, vs_ref, *rest, n_words):
    nwin = (WINDOW + ATT_Q) // KEY_TILE
    kw_refs, vw_refs = rest[:nwin], rest[nwin:2 * nwin]
    o_ref, list_ref = rest[2 * nwin], rest[2 * nwin + 1]
    bi, gi, qi = pl.program_id(0), pl.program_id(1), pl.program_id(2)
    q0 = qi * ATT_Q
    diag0 = qi * (ATT_Q // KEY_TILE)
    rq = HEADS_PER_GROUP * ATT_Q
    nb = sel_ref.shape[3]
    tiles_per_word = FLAG_BITS // TILE_BLOCKS
    tile_bits = (1 << TILE_BLOCKS) - 1
    sel_per_step = ATT_Q // Q_BLOCK

    qs = _stack_heads(q_ref, 0)
    selb = sel_ref[0, 0]
    t1 = q0 + lax.broadcasted_iota(jnp.int32, (ATT_Q, 1), 0)

    base = ((bi * NSA_GROUPS + gi) * pl.num_programs(2) + qi) * sel_per_step * n_words

    def word_body(wi, n):
        word = flag_ref[base + wi]
        for extra in range(1, sel_per_step):
            word = word | flag_ref[base + extra * n_words + wi]

        def scan_word(n):
            def tile_body(u, n):
                kt = wi * tiles_per_word + u
                active = (((word >> (u * TILE_BLOCKS)) & tile_bits) != 0) & (kt < diag0)

                @pl.when(active)
                def _():
                    list_ref[n] = kt
                return n + active.astype(jnp.int32)
            return lax.fori_loop(0, tiles_per_word, tile_body, n)
        return lax.cond(word != 0, scan_word, lambda n: n, n)

    n_active = lax.fori_loop(0, (diag0 + tiles_per_word - 1) // tiles_per_word, word_body, 0)
    for u in range(GROUP_TILES):
        list_ref[n_active + u] = 0

    jrow = lax.broadcasted_iota(jnp.int32, (nb, KEY_TILE), 0)
    kblk = lax.broadcasted_iota(jnp.int32, (nb, KEY_TILE), 1) >> 6
    kcol = lax.broadcasted_iota(jnp.int32, (1, KEY_TILE), 1)

    def masked_scores(k_all, member):
        s_ = _dot_nt(qs, k_all)
        return jnp.concatenate([jnp.where(member, s_[r * ATT_Q:(r + 1) * ATT_Q], NEG)
                                for r in range(HEADS_PER_GROUP)], axis=0)

    def tile_rows(ref, kt, n=KEY_TILE):
        return ref[0, 0, pl.ds(pl.multiple_of(kt * KEY_TILE, KEY_TILE), n), :]

    def group_body(it, carry):
        ks_t, vs_t, ex_t = [], [], []
        for u in range(GROUP_TILES):
            pos = it * GROUP_TILES + u
            kt = list_ref[pos]
            first_blk = jnp.where(pos < n_active, kt * TILE_BLOCKS, -TILE_BLOCKS - nb)
            ks_t.append(tile_rows(ks_ref, kt))
            vs_t.append(tile_rows(vs_ref, kt))
            ex_t.append(jnp.where(jrow == kblk + first_blk, 1.0, 0.0).astype(BF))
        member = _dot(selb, jnp.concatenate(ex_t, axis=1)) > 0.5
        s_ = masked_scores(jnp.concatenate(ks_t, axis=0), member)
        return _softmax_update(carry, s_, jnp.concatenate(vs_t, axis=0))

    init = (jnp.full((rq, 1), NEG, F32), jnp.zeros((rq, LANES), F32))
    carry = lax.fori_loop(0, (n_active + GROUP_TILES - 1) // GROUP_TILES, group_body, init)
    drow = lax.broadcasted_iota(jnp.int32, (nb, ATT_Q), 0)
    dcol = lax.broadcasted_iota(jnp.int32, (1, ATT_Q), 1)
    expand = jnp.where(drow == (dcol >> 6) + diag0 * TILE_BLOCKS, 1.0, 0.0).astype(BF)
    member = (_dot(selb, expand) > 0.5) & (q0 + dcol <= t1)
    kpos = []
    for i in range(nwin):
        kti = diag0 - WINDOW // KEY_TILE + i
        kpos.append(jnp.where(kti >= 0, kti * KEY_TILE, 1 << 24) + kcol)
    d = t1 - jnp.concatenate(kpos, axis=1)
    in_window = (d >= 0) & (d < WINDOW)
    s_d = masked_scores(tile_rows(ks_ref, diag0, ATT_Q), member)
    s_w = masked_scores(jnp.concatenate([r[0, 0] for r in kw_refs], axis=0), in_window)
    m_d = jnp.maximum(carry[0], jnp.max(s_d, axis=-1, keepdims=True))
    m_w = jnp.max(s_w, axis=-1, keepdims=True)
    p_d = jnp.exp(s_d - m_d).astype(BF)
    p_w = jnp.exp(s_w - m_w).astype(BF)
    acc_d = jnp.exp(carry[0] - m_d) * carry[1] + _dot(p_d, tile_rows(vs_ref, diag0, ATT_Q))
    acc_w = _dot(p_w, jnp.concatenate([r[0, 0] for r in vw_refs], axis=0))

    assert LANES == 2 * HEAD_DIM and HEADS_PER_GROUP % 2 == 0
    low_half = lax.broadcasted_iota(jnp.int32, (1, LANES), 1) < HEAD_DIM

    def heads_on_lanes(acc):
        inv = 1.0 / jnp.maximum(acc, 1e-30)
        pairs = []
        for r in range(0, HEADS_PER_GROUP, 2):
            ev, od = slice(r * ATT_Q, (r + 1) * ATT_Q), slice((r + 1) * ATT_Q, (r + 2) * ATT_Q)
            even = acc[ev] * pltpu.roll(inv[ev], HEAD_DIM, axis=1)
            odd = pltpu.roll(acc[od], HEAD_DIM, axis=1) * inv[od]
            pairs.append(jnp.where(low_half, even, odd))
        return jnp.concatenate(pairs, axis=1)

    gw = HEADS_PER_GROUP * HEAD_DIM
    gates = _split_dot(g_ref[0], spread_ref[...])
    out = (gates[:, 0:gw] * oc_ref[0] + gates[:, gw:2 * gw] * heads_on_lanes(acc_d)
           + gates[:, 2 * gw:3 * gw] * heads_on_lanes(acc_w))
    o_ref[0] = out.astype(BF)


def _attend(flags, q, gates, o_cmp, sel, ks, vs, kw, vw):
    b, s, _ = q.shape
    nb = sel.shape[3]
    n_words = nb // FLAG_BITS
    nwin = (WINDOW + ATT_Q) // KEY_TILE
    per_q = ATT_Q // KEY_TILE
    gw = HEADS_PER_GROUP * LANES
    flat = flags[:, :, :, 0, :n_words].reshape(-1)
    spread = _gate_spread()
    full = pl.BlockSpec((1, 1, s, LANES), lambda bi, gi, qi, f: (bi, gi, 0, 0))
    win = lambda i: pl.BlockSpec(
        (1, 1, KEY_TILE, LANES),
        lambda bi, gi, qi, f: (bi, gi, jnp.maximum(qi * per_q - WINDOW // KEY_TILE + i, 0), 0))
    grid_spec = pltpu.PrefetchScalarGridSpec(
        num_scalar_prefetch=1,
        grid=(b, NSA_GROUPS, s // ATT_Q),
        in_specs=[pl.BlockSpec((1, ATT_Q, gw), lambda bi, gi, qi, f: (bi, qi, gi)),
                  pl.BlockSpec((1, ATT_Q, LANES), lambda bi, gi, qi, f: (bi, qi, gi)),
                  pl.BlockSpec((1, ATT_Q, HEADS_PER_GROUP * HEAD_DIM), lambda bi, gi, qi, f: (bi, qi, gi)),
                  pl.BlockSpec((1, 1, ATT_Q, nb), lambda bi, gi, qi, f: (bi, gi, qi, 0)),
                  pl.BlockSpec(spread.shape, lambda bi, gi, qi, f: (0, 0)),
                  full, full] + [win(i) for i in range(nwin)] * 2,
        out_specs=pl.BlockSpec((1, ATT_Q, HEADS_PER_GROUP * HEAD_DIM), lambda bi, gi, qi, f: (bi, qi, gi)),
        scratch_shapes=[pltpu.SMEM((s // KEY_TILE + GROUP_TILES,), jnp.int32)],
    )
    return pl.pallas_call(
        functools.partial(_attend_kernel, n_words=n_words),
        grid_spec=grid_spec,
        out_shape=jax.ShapeDtypeStruct((b, s, NSA_HEADS * HEAD_DIM), BF),
        compiler_params=_params(3),
        name="nsa_attend",
    )(flat, q, gates, o_cmp, sel, spread, ks, vs, *([kw] * nwin), *([vw] * nwin))


def _head_indicators(d):
    ch = jnp.arange(d)[:, None] // RWKV_HEAD
    ind = (ch == jnp.arange(LANES)[None, :]).astype(BF)
    return ind, ind.T


def _rwkv_pre_kernel(x_ref, pre_ref, mu_ref, wr_ref, wk_ref, wv_ref, w0_ref, wd1_ref, wd2_ref,
                     a0_ref, wa1_ref, wa2_ref, wg1_ref, wg2_ref, kk_ref, ka_ref, ind_ref, indt_ref,
                     r_out, lw_out, k_out, v_out, kk_out, kka_out, g_out, hprev_ref, *, n_sub):
    si = pl.program_id(1)
    tm = x_ref.shape[1]

    @pl.when(si == 0)
    def _():
        hprev_ref[...] = jnp.zeros(hprev_ref.shape, F32)

    h_all = _rmsnorm(x_ref[0], pre_ref[...])
    rows = lax.broadcasted_iota(jnp.int32, (tm, 1), 0)
    xx_all = jnp.where(rows == 0, hprev_ref[7:8], pltpu.roll(h_all, 1, axis=0)) - h_all
    hprev_ref[...] = h_all[tm - 8:tm]

    def sub_tile(rs):
        h, xx = h_all[rs], xx_all[rs]
        mix = lambda i: (h + xx * mu_ref[i:i + 1]).astype(BF)
        xr, xw, xk, xv, xa, xg = [mix(i) for i in range(6)]
        r = _dot(xr, wr_ref[...])
        k = _dot(xk, wk_ref[...])
        v = _dot(xv, wv_ref[...])
        yield
        z = w0_ref[...] + _dot(jnp.tanh(_dot(xw, wd1_ref[...])).astype(BF), wd2_ref[...])
        lw = -DECAY_SCALE * _sigmoid(z)
        a = _sigmoid(a0_ref[...] + _dot(_dot(xa, wa1_ref[...]).astype(BF), wa2_ref[...]))
        g = _dot(_sigmoid(_dot(xg, wg1_ref[...])).astype(BF), wg2_ref[...])
        yield
        kraw = k * kk_ref[...]
        ss = _dot((kraw * kraw).astype(BF), ind_ref[...])
        yield
        inv = lax.rsqrt(jnp.maximum(ss, 1e-12))
        kk = kraw * _dot(inv.astype(BF), indt_ref[...])
        yield
        r_out[0, rs] = r.astype(BF)
        lw_out[0, rs] = lw
        k_out[0, rs] = (k * (1.0 + (a - 1.0) * ka_ref[...])).astype(BF)
        v_out[0, rs] = v.astype(BF)
        kk_out[0, rs] = kk.astype(BF)
        kka_out[0, rs] = (kk * a).astype(BF)
        g_out[0, rs] = g.astype(BF)

    _round_robin(sub_tile(slice(i * tm // n_sub, (i + 1) * tm // n_sub)) for i in range(n_sub))


def _rwkv_pre(x, pre_g, mu, w_r, w_k, w_v, w0, w_dec1, w_dec2, a0, w_a1, w_a2, w_g1, w_g2,
              k_k, k_a, ind, indt, *, tm=512, n_sub=1):
    b, s, d = x.shape
    row = pl.BlockSpec((1, tm, d), lambda bi, si: (bi, si, 0))
    vec = lambda a: a.reshape(1, d)
    ops = [x, vec(pre_g), mu, w_r.astype(BF), w_k.astype(BF), w_v.astype(BF), vec(w0),
           w_dec1.astype(BF), w_dec2.astype(BF), vec(a0), w_a1.astype(BF), w_a2.astype(BF),
           w_g1.astype(BF), w_g2.astype(BF), vec(k_k), vec(k_a), ind, indt]
    return pl.pallas_call(
        functools.partial(_rwkv_pre_kernel, n_sub=n_sub),
        grid=(b, s // tm),
        in_specs=[row] + [_const_spec(o.shape) for o in ops[1:]],
        out_specs=[row] * 7,
        out_shape=[jax.ShapeDtypeStruct((b, s, d), F32 if i == 1 else BF) for i in range(7)],
        scratch_shapes=[pltpu.VMEM((8, d), F32)],
        compiler_params=_params(2),
        name="rwkv_pre",
    )(*ops)


def _rwkv_scan_kernel(r_ref, lw_ref, k_ref, v_ref, kk_ref, kka_ref, y_ref, state_ref):
    c = RWKV_CHUNK
    slab = SLAB_HEADS * RWKV_HEAD
    n_slab = r_ref.shape[2] // slab
    rows = SLAB_HEADS * c
    subs = [slice(i * c, (i + 1) * c) for i in range(r_ref.shape[1] // c)]

    @pl.when(pl.program_id(1) == 0)
    def _():
        state_ref[...] = jnp.zeros(state_ref.shape, F32)

    ri = lax.broadcasted_iota(jnp.int32, (rows, 1), 0)
    ci = lax.broadcasted_iota(jnp.int32, (1, rows), 1)
    li = lax.broadcasted_iota(jnp.int32, (1, slab), 1)
    c_shift = c.bit_length() - 1
    h_shift = RWKV_HEAD.bit_length() - 1
    assert c == 1 << c_shift and RWKV_HEAD == 1 << h_shift
    head_match = (ri >> c_shift) == (li >> h_shift)
    t_r, t_c = ri & (c - 1), ci & (c - 1)
    strict = t_c < t_r
    incl = t_c <= t_r
    eye = jnp.where(ri == ci, 1.0, 0.0)
    tri = jnp.where(lax.broadcasted_iota(jnp.int32, (c, c), 1) <= lax.broadcasted_iota(jnp.int32, (c, c), 0),
                    1.0, 0.0).astype(BF)
    state_match = (lax.broadcasted_iota(jnp.int32, (slab, 1), 0) >> h_shift) == (li >> h_shift)

    incl2 = jnp.concatenate([incl, incl], axis=1)

    def block_diag(x):
        return jnp.where(head_match, jnp.concatenate([x] * SLAB_HEADS, axis=0), 0.0)

    def fold(x):
        out = x[0:c]
        for h in range(1, SLAB_HEADS):
            out = out + x[h * c:(h + 1) * c]
        return out

    bf = lambda x: x.astype(BF)
    slabs = [slice(hs * slab, (hs + 1) * slab) for hs in range(n_slab)]

    def prepare(rs, sl):
        lw = lw_ref[0, rs, sl]
        hi = bf(lw)
        rem = lw - hi.astype(F32)
        mid = bf(rem)
        lo = bf(rem - mid.astype(F32))
        cw = _dot(tri, hi) + _dot(tri, mid) + _dot(tri, lo)
        w_in = jnp.exp(cw)
        w_inv = jnp.exp(-cw)
        w_prev = jnp.exp(cw - lw)
        w_end = w_in[c - 1:c]

        a_t = -kk_ref[0, rs, sl].astype(F32) * w_prev
        b_t = kka_ref[0, rs, sl].astype(F32) * w_inv
        k_t = k_ref[0, rs, sl].astype(F32) * w_inv
        r_t = r_ref[0, rs, sl].astype(F32) * w_in
        return a_t, b_t, k_t, r_t, w_end

    def interactions(a_t, b_t, k_t, r_t):
        lhs = bf(jnp.concatenate([block_diag(a_t), block_diag(r_t)], axis=0))
        rhs = bf(jnp.concatenate([block_diag(b_t), block_diag(k_t)], axis=0))
        big = _dot_nt(lhs, rhs)
        a_ab = jnp.where(strict, big[:rows, :rows], 0.0)
        a_ak = jnp.where(strict, big[:rows, rows:], 0.0)
        a_r = jnp.where(incl2, big[rows:], 0.0)
        return a_ab, bf(a_ak), bf(a_r)

    every = range(n_slab)
    units = [(rs, sl) for rs in subs for sl in slabs]
    prep = [prepare(rs, sl) for rs, sl in units]
    inter = [interactions(*p[:4]) for p in prep]
    vs = [v_ref[0, rs, sl] for rs, sl in units]
    vbd = [bf(block_diag(v)) for v in vs]
    akv = [_dot(inter[u][1], vbd[u]) for u in range(len(units))]

    levels = c.bit_length() - 2
    assert levels >= 1
    invs = [eye + x[0] for x in inter]
    pws = [_dot(bf(x[0]), bf(x[0])) for x in inter]
    for _ in range(levels - 1):
        both = [_dot(bf(pw), bf(jnp.concatenate([pw, inv], axis=1))) for pw, inv in zip(pws, invs)]
        pws = [x[:, :rows] for x in both]
        invs = [inv + x[:, rows:] for inv, x in zip(invs, both)]
    invs = [bf(inv + _dot(bf(pw), bf(inv))) for pw, inv in zip(pws, invs)]

    states = [state_ref[hs] for hs in every]
    for si, rs in enumerate(subs):
        us = [si * n_slab + hs for hs in every]
        ar = [_dot_nt(bf(jnp.concatenate([prep[u][0], prep[u][3]], axis=0)), bf(states[hs]))
              for hs, u in zip(every, us)]
        u_bd = [_dot(invs[u], bf(block_diag(ar[hs][:c]) + akv[u])) for hs, u in zip(every, us)]
        y_bd = [block_diag(ar[hs][c:]) + _dot(inter[u][2], jnp.concatenate([bf(u_bd[hs]), vbd[u]], axis=0))
                for hs, u in zip(every, us)]
        for hs, sl in enumerate(slabs):
            y_ref[0, rs, sl] = fold(y_bd[hs])
        new_states = []
        for hs, u in zip(every, us):
            _, b_t, k_t, _, w_end = prep[u]
            uv = jnp.concatenate([bf(fold(u_bd[hs])), bf(vs[u])], axis=0)
            bk = jnp.concatenate([bf(b_t * w_end), bf(k_t * w_end)], axis=0)
            new_states.append(jnp.where(state_match, states[hs] * w_end + _dot_tn(uv, bk), 0.0))
        states = new_states
    for hs in every:
        state_ref[hs] = states[hs]


def _rwkv_scan(r, lw, k, v, kk, kka):
    b, s, d = r.shape
    c = RWKV_CHUNK * SCAN_SUB
    slab = SLAB_HEADS * RWKV_HEAD
    blk = pl.BlockSpec((1, c, d), lambda bi, ci: (bi, ci, 0))
    return pl.pallas_call(
        _rwkv_scan_kernel,
        grid=(b, s // c),
        in_specs=[blk] * 6,
        out_specs=blk,
        out_shape=jax.ShapeDtypeStruct((b, s, d), F32),
        scratch_shapes=[pltpu.VMEM((d // slab, slab, slab), F32)],
        compiler_params=_params(2),
        name="rwkv_scan",
    )(r, lw, k, v, kk, kka)


def _rwkv_post_kernel(x_ref, y_ref, r_ref, k_ref, v_ref, g_ref, wo_ref, lng_ref, lnb_ref, rk_ref,
                      post_ref, ind_ref, indt_ref, o_ref, *, n_sub):
    ind, indt = ind_ref[...], indt_ref[...]
    inv_n = 1.0 / RWKV_HEAD
    head_sum = lambda a: _dot(a.astype(BF), ind)
    spread = lambda a: _dot(a.astype(BF), indt)
    tm = x_ref.shape[0]

    def sub_tile(rs):
        f32 = lambda ref: ref[rs].astype(F32)
        y = y_ref[rs]
        sums = head_sum(y)
        rk_sum = head_sum(f32(r_ref) * f32(k_ref) * rk_ref[...])
        yield
        mean = _split_dot(sums * inv_n, indt)
        bonus = spread(rk_sum) * f32(v_ref)
        yield
        yc = y - mean
        var = head_sum(yc * yc) * inv_n
        yield
        rstd = spread(lax.rsqrt(var + LNX_EPS))
        yield
        yn = yc * rstd * lng_ref[...] + lnb_ref[...]
        out = _dot(((yn + bonus) * f32(g_ref)).astype(BF), wo_ref[...])
        yield
        o_ref[rs] = x_ref[rs] + _rmsnorm(out, post_ref[...])

    _round_robin(sub_tile(slice(i * tm // n_sub, (i + 1) * tm // n_sub)) for i in range(n_sub))


def _rwkv_post(x2, y, r, k, v, g, w_o, lnx_g, lnx_b, r_k, post_g, ind, indt, *, tm=512, n_sub=2):
    t, d = x2.shape
    row = pl.BlockSpec((tm, d), lambda i: (i, 0))
    vec = lambda a: a.reshape(1, d)
    ops = [x2, y, r, k, v, g, w_o.astype(BF), vec(lnx_g), vec(lnx_b), vec(r_k), vec(post_g), ind, indt]
    return pl.pallas_call(
        functools.partial(_rwkv_post_kernel, n_sub=n_sub),
        grid=(t // tm,),
        in_specs=[row] * 6 + [_const_spec(o.shape) for o in ops[6:]],
        out_specs=row,
        out_shape=jax.ShapeDtypeStruct((t, d), F32),
        compiler_params=_params(1),
        name="rwkv_post",
    )(*ops)


def _layer0_mixer(x, pre_g, w_in, pe_k, w1_k, w2_k, pe_v, w1_v, w2_v, conv_w, conv_b):
    b, s, _ = x.shape
    q, gates, kcr, vcr, ks, vs, kw, vw, o_conv = _l0_in(x, pre_g, _l0_in_weight(w_in), conv_w, conv_b)
    kc = _compress(kcr, _cmp_weights(pe_k, w1_k, w2_k), is_key=True)
    vc = _compress(vcr, _cmp_weights(pe_v, w1_v, w2_v), is_key=False)
    o_cmp, sel, flags = _select(q, kc, vc)
    o_nsa = _attend(flags, q, gates, o_cmp, sel, ks, vs, kw, vw)
    return o_nsa.reshape(b * s, -1), o_conv.reshape(b * s, -1)


def _layer1_mixer(x, pre_g, post_g, mu, w_r, w_k, w_v, w_o, w0, w_dec1, w_dec2, a0, w_a1, w_a2,
                  w_g1, w_g2, k_k, k_a, r_k, lnx_g, lnx_b):
    b, s, d = x.shape
    ind, indt = _head_indicators(d)
    r, lw, k, v, kk, kka, g = _rwkv_pre(x, pre_g, mu, w_r, w_k, w_v, w0, w_dec1, w_dec2, a0,
                                        w_a1, w_a2, w_g1, w_g2, k_k, k_a, ind, indt)
    y = _rwkv_scan(r, lw, k, v, kk, kka)
    f = lambda a: a.reshape(b * s, d)
    return _rwkv_post(f(x), f(y), f(r), f(k), f(v), f(g), w_o, lnx_g, lnx_b, r_k, post_g,
                      ind, indt).reshape(b, s, d)


def kernel(x, l0_ffn1_pre_g, l0_ffn1_post_g, l0_ffn1_w_gate, l0_ffn1_w_up, l0_ffn1_w_down, l0_mix_pre_g, l0_mix_post_g, l0_w_in, l0_cmp_pe_k, l0_cmp_w1_k, l0_cmp_w2_k, l0_cmp_pe_v, l0_cmp_w1_v, l0_cmp_w2_v, l0_conv_w, l0_conv_b, l0_w_out, l0_ffn2_pre_g, l0_ffn2_post_g, l0_ffn2_w_gate, l0_ffn2_w_up, l0_ffn2_w_down, l1_ffn1_pre_g, l1_ffn1_post_g, l1_ffn1_w_gate, l1_ffn1_w_up, l1_ffn1_w_down, l1_mix_pre_g, l1_mix_post_g, l1_mu, l1_w_r, l1_w_k, l1_w_v, l1_w_o, l1_w0, l1_w_dec1, l1_w_dec2, l1_a0, l1_w_a1, l1_w_a2, l1_w_g1, l1_w_g2, l1_k_k, l1_k_a, l1_r_k, l1_lnx_g, l1_lnx_b, l1_ffn2_pre_g, l1_ffn2_post_g, l1_ffn2_w_gate, l1_ffn2_w_up, l1_ffn2_w_down):
    b, s, d = x.shape
    ffn = lambda a, *w, **kw: _ffn(a.reshape(b * s, d), *w, **kw).reshape(b, s, d)
    x = ffn(x, l0_ffn1_pre_g, l0_ffn1_post_g, l0_ffn1_w_gate, l0_ffn1_w_up, l0_ffn1_w_down)
    o_nsa, o_conv = _layer0_mixer(x, l0_mix_pre_g, l0_w_in, l0_cmp_pe_k, l0_cmp_w1_k, l0_cmp_w2_k,
                                  l0_cmp_pe_v, l0_cmp_w1_v, l0_cmp_w2_v, l0_conv_w, l0_conv_b)
    x = ffn(x, l0_ffn2_pre_g, l0_ffn2_post_g, l0_ffn2_w_gate, l0_ffn2_w_up, l0_ffn2_w_down,
            mixer=(o_nsa, o_conv, l0_w_out, l0_mix_post_g))
    x = ffn(x, l1_ffn1_pre_g, l1_ffn1_post_g, l1_ffn1_w_gate, l1_ffn1_w_up, l1_ffn1_w_down)
    x = _layer1_mixer(x, l1_mix_pre_g, l1_mix_post_g, l1_mu, l1_w_r, l1_w_k, l1_w_v, l1_w_o, l1_w0,
                      l1_w_dec1, l1_w_dec2, l1_a0, l1_w_a1, l1_w_a2, l1_w_g1, l1_w_g2, l1_k_k, l1_k_a,
                      l1_r_k, l1_lnx_g, l1_lnx_b)
    x = ffn(x, l1_ffn2_pre_g, l1_ffn2_post_g, l1_ffn2_w_gate, l1_ffn2_w_up, l1_ffn2_w_down)
    return x
```

```python
import functools

import jax
import jax.numpy as jnp
import numpy as np
from jax import lax
from jax.experimental import pallas as pl
from jax.experimental.pallas import tpu as pltpu

BF = jnp.bfloat16
F32 = jnp.float32

EPS = 1e-6
LNX_EPS = 64e-5
HEAD_DIM = 64
NSA_HEADS = 8
NSA_GROUPS = 2
HEADS_PER_GROUP = NSA_HEADS // NSA_GROUPS
CMP_STRIDE = 16
CMP_BLOCK = 32
CMP_HIDDEN = 128
SLC_BLOCK = 64
SLC_TOP_N = 16
WINDOW = 512
Q_BLOCK = 128
CONV_WIDTH = 512
RWKV_HEAD = 64
RWKV_CHUNK = 64
SLAB_HEADS = 2
SCAN_SUB = 4
DECAY_SCALE = 0.6065306597126334
LANES = 128
MXU_DIM = 256
POS_SHIFT = 7
POS_SPLIT = 1 << POS_SHIFT
LOG2E = 1.4426950408889634
BIAS_PIECES = 3
NEG = -1e30
VMEM_LIMIT = 56 * 1024 * 1024

NT_DIMS = (((1,), (1,)), ((), ()))
TN_DIMS = (((0,), (0,)), ((), ()))


def _dot(a, b):
    return jnp.dot(a, b, preferred_element_type=F32)


def _dot_nt(a, b):
    return lax.dot_general(a, b, NT_DIMS, preferred_element_type=F32)


def _dot_tn(a, b):
    return lax.dot_general(a, b, TN_DIMS, preferred_element_type=F32)


def _split_dot(x, w):
    hi = x.astype(BF)
    lo = (x - hi.astype(F32)).astype(BF)
    return _dot(hi, w) + _dot(lo, w)


def _rmsnorm(x, g):
    ms = jnp.mean(x * x, axis=-1, keepdims=True)
    return x * lax.rsqrt(ms + EPS) * g


def _sigmoid(x):
    return 1.0 / (1.0 + jnp.exp(-x))


def _round_robin(stage_generators):
    pending = list(stage_generators)
    while pending:
        still = []
        for gen in pending:
            try:
                next(gen)
                still.append(gen)
            except StopIteration:
                pass
        pending = still


def _bf16_pieces(value, n):
    pieces = []
    for _ in range(n):
        pieces.append(float(np.float32(value).astype(jnp.bfloat16)))
        value -= pieces[-1]
    return pieces


def _query_bias_lanes(qh, lane, slope):
    for i, c in enumerate(_bf16_pieces(slope * LOG2E, BIAS_PIECES)):
        qh = jnp.where(lane == HEAD_DIM + i, POS_SPLIT * c, qh)
        qh = jnp.where(lane == HEAD_DIM + BIAS_PIECES + i, c, qh)
    return qh


def _key_position_lanes(t, lane, hi, lo):
    in_hi = (lane >= HEAD_DIM) & (lane < HEAD_DIM + BIAS_PIECES)
    in_lo = (lane >= HEAD_DIM + BIAS_PIECES) & (lane < HEAD_DIM + 2 * BIAS_PIECES)
    return jnp.where(in_hi, hi, jnp.where(in_lo, lo, t))


def _const_spec(shape):
    zeros = (0,) * len(shape)
    return pl.BlockSpec(shape, lambda *_: zeros)


def _params(n_grid):
    return pltpu.CompilerParams(dimension_semantics=("arbitrary",) * n_grid,
                                vmem_limit_bytes=VMEM_LIMIT)


def _ffn_kernel(*refs, ff_chunk, n_sub, after_mixer):
    if after_mixer:
        x_ref, a_ref, c_ref, wa_ref, wc_ref, mixpost_ref = refs[:6]
        refs = refs[:1] + refs[6:]
    x_ref, pre_ref, post_ref, wg_ref, wu_ref, wd_ref, o_ref = refs

    def residual_rows(rs):
        if not after_mixer:
            return x_ref[rs]
        y = _dot(a_ref[rs], wa_ref[...]) + _dot(c_ref[rs], wc_ref[...])
        return x_ref[rs] + _rmsnorm(y, mixpost_ref[...])

    sub = x_ref.shape[0] // n_sub
    dff = wg_ref.shape[1]
    chunks = [(c0, min(c0 + ff_chunk, dff)) for c0 in range(0, dff, ff_chunk)]
    rows = [slice(i * sub, (i + 1) * sub) for i in range(n_sub)]
    xs, hs, accs = [], [], []
    for ci, (c0, c1) in enumerate(chunks):
        gu = []
        for i, rs in enumerate(rows):
            if ci == 0:
                xs.append(residual_rows(rs))
                hs.append(_rmsnorm(xs[i], pre_ref[...]).astype(BF))
                accs.append(jnp.zeros(xs[i].shape, F32))
            gu.append((_dot(hs[i], wg_ref[:, c0:c1]), _dot(hs[i], wu_ref[:, c0:c1])))
        for i, (g, u) in enumerate(gu):
            a = (g * _sigmoid(g) * u).astype(BF)
            accs[i] = accs[i] + _dot(a, wd_ref[c0:c1, :])
    for i, rs in enumerate(rows):
        o_ref[rs] = xs[i] + 0.5 * _rmsnorm(accs[i], post_ref[...])


def _ffn(x2, pre_g, post_g, w_gate, w_up, w_down, *, mixer=None, tm=1024, n_sub=2):
    t, d = x2.shape
    dff = w_gate.shape[1]
    ff_chunk = MXU_DIM * pl.cdiv(pl.cdiv(dff, MXU_DIM), 2)
    row = lambda w: pl.BlockSpec((tm, w), lambda i: (i, 0))
    weight = lambda shape: pl.BlockSpec(shape, lambda i: (0, 0), pipeline_mode=pl.Buffered(1))
    ops, specs = [x2], [row(d)]
    if mixer is not None:
        o_nsa, o_conv, w_out, mix_post_g = mixer
        na, nc = o_nsa.shape[1], o_conv.shape[1]
        ops += [o_nsa, o_conv, w_out[:na].astype(BF), w_out[na:].astype(BF), mix_post_g.reshape(1, d)]
        specs += [row(na), row(nc), weight((na, d)), weight((nc, d)), _const_spec((1, d))]
    ops += [pre_g.reshape(1, d), post_g.reshape(1, d), w_gate.astype(BF), w_up.astype(BF), w_down.astype(BF)]
    specs += [_const_spec((1, d)), _const_spec((1, d)), weight((d, dff)), weight((d, dff)), weight((dff, d))]
    return pl.pallas_call(
        functools.partial(_ffn_kernel, ff_chunk=ff_chunk, n_sub=n_sub, after_mixer=mixer is not None),
        grid=(t // tm,),
        in_specs=specs,
        out_specs=row(d),
        out_shape=jax.ShapeDtypeStruct((t, d), F32),
        compiler_params=_params(1),
        name="ffn",
    )(*ops)


_C_Q = 0
_C_G = _C_Q + NSA_HEADS * LANES
_C_KC = _C_G + NSA_GROUPS * LANES
_C_VC = _C_KC + LANES
_C_KS = _C_VC + LANES
_C_CONV = _C_KS + 4 * NSA_GROUPS * LANES
_C_END = _C_CONV + 3 * CONV_WIDTH


def _l0_in_weight(w_in):
    d = w_in.shape[0]
    nsa_w = NSA_HEADS * HEAD_DIM
    kv_w = NSA_GROUPS * HEAD_DIM
    o = 0
    q = w_in[:, o:o + nsa_w]; o += nsa_w
    g = w_in[:, o:o + 3 * NSA_HEADS]; o += 3 * NSA_HEADS
    kvs = []
    for _ in range(6):
        kvs.append(w_in[:, o:o + kv_w]); o += kv_w
    conv = w_in[:, o:]
    zpad = lambda w, n: jnp.pad(w, ((0, 0), (0, n - w.shape[1])))
    cols = [zpad(q[:, h * HEAD_DIM:(h + 1) * HEAD_DIM], LANES) for h in range(NSA_HEADS)]
    gpg = 3 * HEADS_PER_GROUP
    cols += [zpad(g[:, i * gpg:(i + 1) * gpg], LANES) for i in range(NSA_GROUPS)]
    cols += [kvs[0], kvs[1]]
    for w in kvs[2:]:
        cols += [zpad(w[:, i * HEAD_DIM:(i + 1) * HEAD_DIM], LANES) for i in range(NSA_GROUPS)]
    cols.append(conv)
    w = jnp.concatenate(cols, axis=1)
    assert w.shape == (d, _C_END)
    return w.astype(BF)


def _l0_in_kernel(x_ref, pre_ref, w_ref, cw_ref, cb_ref,
                  q_ref, g_ref, kcr_ref, vcr_ref, ks_ref, vs_ref, kw_ref, vw_ref, oc_ref,
                  zprev_ref):
    si = pl.program_id(1)
    tm = x_ref.shape[1]
    h = _rmsnorm(x_ref[0], pre_ref[...]).astype(BF)
    lane = lax.broadcasted_iota(jnp.int32, (1, LANES), 1)
    pos = si * tm + lax.broadcasted_iota(jnp.int32, (tm, 1), 0)
    pos_hi = (pos >> POS_SHIFT).astype(F32)
    pos_lo = (pos & (POS_SPLIT - 1)).astype(F32)

    wide = 4 * LANES

    def project(c0):
        res = _dot(h, w_ref[:, c0:c0 + wide])
        return [res[:, j * LANES:(j + 1) * LANES] for j in range(4)]

    for hd0 in range(0, NSA_HEADS, 4):
        for j, qh in enumerate(project(_C_Q + hd0 * LANES)):
            hd = hd0 + j
            slope = 2.0 ** (-(hd + 1))
            qh = _query_bias_lanes(qh * (HEAD_DIM ** -0.5 * LOG2E), lane, slope)
            q_ref[0, :, hd * LANES:(hd + 1) * LANES] = qh.astype(BF)

    assert _C_KC == _C_G + 2 * LANES and _C_VC == _C_KC + LANES and NSA_GROUPS == 2
    g0, g1, kcr, vcr = project(_C_G)
    g_ref[0] = _sigmoid(jnp.concatenate([g0, g1], axis=1))
    kcr_ref[0] = kcr
    vcr_ref[0] = vcr

    slabs = project(_C_KS) + project(_C_KS + wide)
    for n, (ref, is_key) in enumerate(((ks_ref, True), (vs_ref, False), (kw_ref, True), (vw_ref, False))):
        for gi in range(NSA_GROUPS):
            t = slabs[n * NSA_GROUPS + gi]
            if is_key:
                t = _key_position_lanes(t, lane, pos_hi, pos_lo)
            else:
                t = jnp.where(lane >= HEAD_DIM, 1.0, t)
            ref[0, gi] = t.astype(BF)

    @pl.when(si == 0)
    def _():
        zprev_ref[...] = jnp.zeros(zprev_ref.shape, F32)

    cb = _dot(h, w_ref[:, _C_CONV:_C_CONV + CONV_WIDTH])
    cc = _dot(h, w_ref[:, _C_CONV + CONV_WIDTH:_C_CONV + 2 * CONV_WIDTH])
    cu = _dot(h, w_ref[:, _C_CONV + 2 * CONV_WIDTH:_C_CONV + 3 * CONV_WIDTH])
    z = cc * cu
    prev = zprev_ref[...]
    rows = lax.broadcasted_iota(jnp.int32, (tm, 1), 0)
    zm1 = jnp.where(rows == 0, prev[7:8], pltpu.roll(z, 1, axis=0))
    zm2 = jnp.where(rows == 0, prev[6:7], jnp.where(rows == 1, prev[7:8], pltpu.roll(z, 2, axis=0)))
    y = cw_ref[0:1] * zm2 + cw_ref[1:2] * zm1 + cw_ref[2:3] * z
    oc_ref[0] = (cb * (y + cb_ref[...])).astype(BF)
    zprev_ref[...] = z[tm - 8:tm]


def _l0_in(x, pre_g, w_all, conv_w, conv_b, *, tm=512):
    b, s, d = x.shape
    row = lambda w: pl.BlockSpec((1, tm, w), lambda bi, si: (bi, si, 0))
    grp = pl.BlockSpec((1, NSA_GROUPS, tm, LANES), lambda bi, si: (bi, 0, si, 0))
    sds = jax.ShapeDtypeStruct
    kv_shape = sds((b, NSA_GROUPS, s, LANES), BF)
    return pl.pallas_call(
        _l0_in_kernel,
        grid=(b, s // tm),
        in_specs=[row(d), _const_spec((1, d)), _const_spec(w_all.shape),
                  _const_spec(conv_w.shape), _const_spec((1, CONV_WIDTH))],
        out_specs=[row(NSA_HEADS * LANES), row(NSA_GROUPS * LANES), row(LANES), row(LANES),
                   grp, grp, grp, grp, row(CONV_WIDTH)],
        out_shape=[sds((b, s, NSA_HEADS * LANES), BF), sds((b, s, NSA_GROUPS * LANES), F32),
                   sds((b, s, LANES), F32), sds((b, s, LANES), F32),
                   kv_shape, kv_shape, kv_shape, kv_shape, sds((b, s, CONV_WIDTH), BF)],
        scratch_shapes=[pltpu.VMEM((8, CONV_WIDTH), F32)],
        compiler_params=_params(2),
        name="l0_in",
    )(x, pre_g.reshape(1, d), w_all, conv_w, conv_b.reshape(1, CONV_WIDTH))


def _cmp_weights(pe, w1, w2):
    half = CMP_STRIDE * HEAD_DIM
    def expand(w):
        w = w.reshape(CMP_STRIDE, 1, HEAD_DIM, 1, CMP_HIDDEN)
        same_group = jnp.eye(NSA_GROUPS, dtype=F32).reshape(1, NSA_GROUPS, 1, NSA_GROUPS, 1)
        return (w * same_group).reshape(CMP_STRIDE * NSA_GROUPS * HEAD_DIM, NSA_GROUPS * CMP_HIDDEN).astype(BF)
    pe_row = lambda p: jnp.tile(p, (1, NSA_GROUPS)).reshape(1, CMP_STRIDE * NSA_GROUPS * HEAD_DIM)
    w2p = jnp.pad(w2, ((0, 0), (0, LANES - HEAD_DIM))).astype(BF)
    return (pe_row(pe[:CMP_STRIDE]), pe_row(pe[CMP_STRIDE:]), expand(w1[:half]), expand(w1[half:]), w2p)


def _gelu_tanh(x):
    return 0.5 * x * (1.0 + jnp.tanh(0.7978845608028654 * (x + 0.044715 * (x * x * x))))


def _cmp_kernel(x_ref, pet_ref, peb_ref, wt_ref, wb_ref, w2_ref, o_ref, *, is_key):
    nb = x_ref.shape[1]
    hw = x_ref.shape[2] // 4
    top, bot = [], []
    for c in range(4):
        xc = x_ref[0, :, c * hw:(c + 1) * hw]
        top.append(_dot((xc + pet_ref[...]).astype(BF), wt_ref[...]))
        bot.append(_dot((xc + peb_ref[...]).astype(BF), wb_ref[...]))
    lane = lax.broadcasted_iota(jnp.int32, (1, LANES), 1)
    j = lax.broadcasted_iota(jnp.int32, (nb, 1), 0)
    for c in range(4):
        nxt = bot[c + 1] if c < 3 else pltpu.roll(bot[0], nb - 1, axis=0)
        hid = _gelu_tanh(top[c] + nxt)
        n = 4 * j + c
        end = n * CMP_STRIDE + (CMP_BLOCK - 1)
        exists = n < 4 * nb - 1
        for gi in range(NSA_GROUPS):
            t = _dot(hid[:, gi * CMP_HIDDEN:(gi + 1) * CMP_HIDDEN].astype(BF), w2_ref[...])
            t = jnp.where(exists, t, 0.0)
            if is_key:
                t = _key_position_lanes(t, lane, (end >> POS_SHIFT).astype(F32),
                                        (end & (POS_SPLIT - 1)).astype(F32))
            else:
                t = jnp.where(lane == HEAD_DIM, 1.0, t)
            o_ref[0, gi, c * nb:(c + 1) * nb, :] = t.astype(BF)


def _compress(raw, weights, *, is_key):
    b, s, _ = raw.shape
    nb = s // (4 * CMP_STRIDE)
    x = raw.reshape(b, nb, 4 * CMP_STRIDE * LANES)
    pet, peb, wt, wb, w2p = weights
    return pl.pallas_call(
        functools.partial(_cmp_kernel, is_key=is_key),
        grid=(b,),
        in_specs=[pl.BlockSpec((1, nb, x.shape[2]), lambda bi: (bi, 0, 0)),
                  _const_spec(pet.shape), _const_spec(peb.shape), _const_spec(wt.shape),
                  _const_spec(wb.shape), _const_spec(w2p.shape)],
        out_specs=pl.BlockSpec((1, NSA_GROUPS, 4 * nb, LANES), lambda bi: (bi, 0, 0, 0)),
        out_shape=jax.ShapeDtypeStruct((b, NSA_GROUPS, 4 * nb, LANES), BF),
        compiler_params=_params(1),
        name="compress",
    )(x, pet, peb, wt, wb, w2p)


def _softmax_update(carry, s, v):
    m, acc = carry
    m_new = jnp.maximum(m, jnp.max(s, axis=-1, keepdims=True))
    p = jnp.exp2(s - m_new)
    acc = jnp.exp2(m - m_new) * acc + _dot(p.astype(BF), v)
    return m_new, acc


SEL_SUB = 2
N_FORCED = 3
FLAG_BITS = 16
ATT_Q = 256
KEY_TILE = 128
TILE_BLOCKS = KEY_TILE // SLC_BLOCK
GROUP_TILES = 4


def _stack_heads(q_ref, g):
    base = g * HEADS_PER_GROUP
    return jnp.concatenate([q_ref[0, :, (base + r) * LANES:(base + r + 1) * LANES]
                            for r in range(HEADS_PER_GROUP)], axis=0)


def _flag_weights(nb):
    j = jnp.arange(nb)[:, None]
    w = jnp.arange(LANES)[None, :]
    return jnp.where(j // FLAG_BITS == w, 2.0 ** (j % FLAG_BITS), 0.0).astype(BF)


def _prefix_weights(nb):
    return (jnp.arange(nb)[:, None] <= jnp.arange(nb)[None, :]).astype(BF)


def _select_kernel(q_ref, kc_ref, vc_ref, pw_ref, tri_ref, oc_ref, sel_ref, flag_ref):
    qi = pl.program_id(1)
    q0 = qi * (SEL_SUB * Q_BLOCK)
    rq = HEADS_PER_GROUP * Q_BLOCK
    nb = kc_ref.shape[2] // 4
    subs = range(SEL_SUB)
    units = [(sb, g) for sb in subs for g in range(NSA_GROUPS)]
    rows = [slice(sb * Q_BLOCK, (sb + 1) * Q_BLOCK) for sb in subs]
    t1 = [q0 + sb * Q_BLOCK + lax.broadcasted_iota(jnp.int32, (Q_BLOCK, 1), 0) for sb in subs]
    t4 = [q0 + sb * Q_BLOCK + (lax.broadcasted_iota(jnp.int32, (rq, 1), 0) & (Q_BLOCK - 1))
          for sb in subs]
    blk_t = [t >> 6 for t in t1]

    def stacked_queries(sb, g):
        base = g * HEADS_PER_GROUP
        return jnp.concatenate([q_ref[0, rows[sb], (base + r) * LANES:(base + r + 1) * LANES]
                                for r in range(HEADS_PER_GROUP)], axis=0)

    def run(w):
        w_shift = w.bit_length() - 1
        assert w == 1 << w_shift and w % LANES == 0
        col = lax.broadcasted_iota(jnp.int32, (1, 4 * w), 1)
        cmp_end = (4 * (col & (w - 1)) + (col >> w_shift)) * CMP_STRIDE + (CMP_BLOCK - 1)
        jl = lax.broadcasted_iota(jnp.int32, (1, w), 1)
        valid = [cmp_end <= t4[sb] for sb in subs]
        forced = [(jl == 0) | (jl == blk_t[sb]) | (jl == blk_t[sb] - 1) for sb in subs]
        eligible = [jl <= blk_t[sb] for sb in subs]
        pickable = [eligible[sb] & ~forced[sb] for sb in subs]
        n_pick = min(SLC_TOP_N, nb) - N_FORCED

        def columns(ref, g):
            if w == nb:
                return ref[0, g]
            return jnp.concatenate([ref[0, g, c * nb:c * nb + w] for c in range(4)], axis=0)

        def importance(p):
            ps = p[0:Q_BLOCK]
            for r in range(1, HEADS_PER_GROUP):
                ps = ps + p[r * Q_BLOCK:(r + 1) * Q_BLOCK]
            parts = [ps[:, c * w:(c + 1) * w] for c in range(4)]
            prev = jnp.where(jl == 0, 0.0, pltpu.roll(parts[3], 1, axis=1))
            return (parts[0] + parts[1] + parts[2] + parts[3]) - 0.5 * parts[3] + 0.5 * prev

        s = [jnp.where(valid[sb], _dot_nt(stacked_queries(sb, g), columns(kc_ref, g)), NEG) for sb, g in units]
        m = [jnp.max(x, axis=-1, keepdims=True) for x in s]
        m = [jnp.where(x <= NEG, 0.0, x) for x in m]
        p = [jnp.exp2(x - mx) for x, mx in zip(s, m)]
        p = [x / jnp.maximum(jnp.sum(x, axis=-1, keepdims=True), 1e-30) for x in p]
        o_cmp = [_dot(x.astype(BF), columns(vc_ref, g)) for x, (sb, g) in zip(p, units)]
        score = [jnp.where(pickable[sb], importance(x), NEG) for x, (sb, g) in zip(p, units)]
        work = score
        for _ in range(n_pick - 1):
            hit = [jl == jnp.argmax(x, axis=-1, keepdims=True).astype(jnp.int32) for x in work]
            work = [jnp.where(h, 3.0 * NEG, x) for h, x in zip(hit, work)]
        thr = [jnp.max(x, axis=-1, keepdims=True) for x in work]
        above = [x > t for x, t in zip(score, thr)]
        tie = [x == t for x, t in zip(score, thr)]
        n_above = [jnp.sum(jnp.where(x, 1.0, 0.0), axis=-1, keepdims=True) for x in above]
        tie_rank = [_dot(jnp.where(x, 1.0, 0.0).astype(BF), tri_ref[0:w, 0:w]) for x in tie]
        for u, (sb, g) in enumerate(units):
            sel = forced[sb] | above[u] | (tie[u] & (tie_rank[u] <= n_pick - n_above[u]))
            for r in range(HEADS_PER_GROUP):
                h0 = (g * HEADS_PER_GROUP + r) * HEAD_DIM
                oc_ref[0, rows[sb], h0:h0 + HEAD_DIM] = o_cmp[u][r * Q_BLOCK:(r + 1) * Q_BLOCK, :HEAD_DIM]
            sel_u = jnp.where(eligible[sb] & sel, 1.0, 0.0)
            sel_ref[0, g, rows[sb], 0:w] = sel_u.astype(BF)
            if w < nb:
                sel_ref[0, g, rows[sb], w:nb] = jnp.zeros((Q_BLOCK, nb - w), BF)
            any_sel = jnp.broadcast_to(jnp.max(sel_u, axis=0, keepdims=True), (8, w)).astype(BF)
            flag_ref[0, g, sb] = _dot(any_sel, pw_ref[0:w]).astype(jnp.int32)

    half = nb // 2
    if half % LANES == 0:
        in_first_half = q0 + SEL_SUB * Q_BLOCK <= half * SLC_BLOCK
        pl.when(in_first_half)(lambda: run(half))
        pl.when(jnp.logical_not(in_first_half))(lambda: run(nb))
    else:
        run(nb)


def _select(q, kc, vc):
    b, s, _ = q.shape
    ncp = kc.shape[2]
    nb = ncp // 4
    nq = s // Q_BLOCK
    step_q = SEL_SUB * Q_BLOCK
    pw = _flag_weights(nb)
    tri = _prefix_weights(nb)
    sds = jax.ShapeDtypeStruct
    cmp_spec = pl.BlockSpec((1, NSA_GROUPS, ncp, LANES), lambda bi, qi: (bi, 0, 0, 0))
    return pl.pallas_call(
        _select_kernel,
        grid=(b, s // step_q),
        in_specs=[pl.BlockSpec((1, step_q, NSA_HEADS * LANES), lambda bi, qi: (bi, qi, 0)),
                  cmp_spec, cmp_spec, _const_spec(pw.shape), _const_spec(tri.shape)],
        out_specs=[pl.BlockSpec((1, step_q, NSA_HEADS * HEAD_DIM), lambda bi, qi: (bi, qi, 0)),
                   pl.BlockSpec((1, NSA_GROUPS, step_q, nb), lambda bi, qi: (bi, 0, qi, 0)),
                   pl.BlockSpec((1, NSA_GROUPS, SEL_SUB, 8, LANES), lambda bi, qi: (bi, 0, qi, 0, 0))],
        out_shape=[sds((b, s, NSA_HEADS * HEAD_DIM), F32), sds((b, NSA_GROUPS, s, nb), BF),
                   sds((b, NSA_GROUPS, nq, 8, LANES), jnp.int32)],
        compiler_params=_params(2),
        name="nsa_select",
    )(q, kc, vc, pw, tri)


def _gate_spread():
    gw = HEADS_PER_GROUP * HEAD_DIM
    src = jnp.arange(LANES)[:, None]
    dst = jnp.arange(3 * gw)[None, :]
    c, r = dst // gw, (dst % gw) // HEAD_DIM
    return (src == 3 * r + c).astype(BF)


def _attend_kernel(flag_ref, q_ref, g_ref, oc_ref, sel_ref, spread_ref, ks_ref, vs_ref, *rest, n_words):
    nwin = (WINDOW + ATT_Q) // KEY_TILE
    kw_refs, vw_refs = rest[:nwin], rest[nwin:2 * nwin]
    o_ref, list_ref = rest[2 * nwin], rest[2 * nwin + 1]
    bi, gi, qi = pl.program_id(0), pl.program_id(1), pl.program_id(2)
    q0 = qi * ATT_Q
    diag0 = qi * (ATT_Q // KEY_TILE)
    rq = HEADS_PER_GROUP * ATT_Q
    nb = sel_ref.shape[3]
    tiles_per_word = FLAG_BITS // TILE_BLOCKS
    tile_bits = (1 << TILE_BLOCKS) - 1
    sel_per_step = ATT_Q // Q_BLOCK

    qs = _stack_heads(q_ref, 0)
    selb = sel_ref[0, 0]
    t1 = q0 + lax.broadcasted_iota(jnp.int32, (ATT_Q, 1), 0)

    base = ((bi * NSA_GROUPS + gi) * pl.num_programs(2) + qi) * sel_per_step * n_words

    def word_body(wi, n):
        word = flag_ref[base + wi]
        for extra in range(1, sel_per_step):
            word = word | flag_ref[base + extra * n_words + wi]

        def scan_word(n):
            def tile_body(u, n):
                kt = wi * tiles_per_word + u
                active = (((word >> (u * TILE_BLOCKS)) & tile_bits) != 0) & (kt < diag0)

                @pl.when(active)
                def _():
                    list_ref[n] = kt
                return n + active.astype(jnp.int32)
            return lax.fori_loop(0, tiles_per_word, tile_body, n)
        return lax.cond(word != 0, scan_word, lambda n: n, n)

    n_active = lax.fori_loop(0, (diag0 + tiles_per_word - 1) // tiles_per_word, word_body, 0)
    for u in range(GROUP_TILES):
        list_ref[n_active + u] = 0

    jrow = lax.broadcasted_iota(jnp.int32, (nb, KEY_TILE), 0)
    kblk = lax.broadcasted_iota(jnp.int32, (nb, KEY_TILE), 1) >> 6
    kcol = lax.broadcasted_iota(jnp.int32, (1, KEY_TILE), 1)

    def masked_scores(k_all, member):
        s_ = _dot_nt(qs, k_all)
        return jnp.concatenate([jnp.where(member, s_[r * ATT_Q:(r + 1) * ATT_Q], NEG)
                                for r in range(HEADS_PER_GROUP)], axis=0)

    def tile_rows(ref, kt, n=KEY_TILE):
        return ref[0, 0, pl.ds(pl.multiple_of(kt * KEY_TILE, KEY_TILE), n), :]

    def group_body(it, carry):
        ks_t, vs_t, ex_t = [], [], []
        for u in range(GROUP_TILES):
            pos = it * GROUP_TILES + u
            kt = list_ref[pos]
            first_blk = jnp.where(pos < n_active, kt * TILE_BLOCKS, -TILE_BLOCKS - nb)
            ks_t.append(tile_rows(ks_ref, kt))
            vs_t.append(tile_rows(vs_ref, kt))
            ex_t.append(jnp.where(jrow == kblk + first_blk, 1.0, 0.0).astype(BF))
        member = _dot(selb, jnp.concatenate(ex_t, axis=1)) > 0.5
        s_ = masked_scores(jnp.concatenate(ks_t, axis=0), member)
        return _softmax_update(carry, s_, jnp.concatenate(vs_t, axis=0))

    init = (jnp.full((rq, 1), NEG, F32), jnp.zeros((rq, LANES), F32))
    carry = lax.fori_loop(0, (n_active + GROUP_TILES - 1) // GROUP_TILES, group_body, init)
    drow = lax.broadcasted_iota(jnp.int32, (nb, ATT_Q), 0)
    dcol = lax.broadcasted_iota(jnp.int32, (1, ATT_Q), 1)
    expand = jnp.where(drow == (dcol >> 6) + diag0 * TILE_BLOCKS, 1.0, 0.0).astype(BF)
    member = (_dot(selb, expand) > 0.5) & (q0 + dcol <= t1)
    kpos = []
    for i in range(nwin):
        kti = diag0 - WINDOW // KEY_TILE + i
        kpos.append(jnp.where(kti >= 0, kti * KEY_TILE, 1 << 24) + kcol)
    d = t1 - jnp.concatenate(kpos, axis=1)
    in_window = (d >= 0) & (d < WINDOW)
    s_d = masked_scores(tile_rows(ks_ref, diag0, ATT_Q), member)
    s_w = masked_scores(jnp.concatenate([r[0, 0] for r in kw_refs], axis=0), in_window)
    m_d = jnp.maximum(carry[0], jnp.max(s_d, axis=-1, keepdims=True))
    m_w = jnp.max(s_w, axis=-1, keepdims=True)
    p_d = jnp.exp2(s_d - m_d).astype(BF)
    p_w = jnp.exp2(s_w - m_w).astype(BF)
    acc_d = jnp.exp2(carry[0] - m_d) * carry[1] + _dot(p_d, tile_rows(vs_ref, diag0, ATT_Q))
    acc_w = _dot(p_w, jnp.concatenate([r[0, 0] for r in vw_refs], axis=0))

    assert LANES == 2 * HEAD_DIM and HEADS_PER_GROUP % 2 == 0
    low_half = lax.broadcasted_iota(jnp.int32, (1, LANES), 1) < HEAD_DIM

    def heads_on_lanes(acc):
        inv = 1.0 / jnp.maximum(acc, 1e-30)
        pairs = []
        for r in range(0, HEADS_PER_GROUP, 2):
            ev, od = slice(r * ATT_Q, (r + 1) * ATT_Q), slice((r + 1) * ATT_Q, (r + 2) * ATT_Q)
            even = acc[ev] * pltpu.roll(inv[ev], HEAD_DIM, axis=1)
            odd = pltpu.roll(acc[od], HEAD_DIM, axis=1) * inv[od]
            pairs.append(jnp.where(low_half, even, odd))
        return jnp.concatenate(pairs, axis=1)

    gw = HEADS_PER_GROUP * HEAD_DIM
    gates = _split_dot(g_ref[0], spread_ref[...])
    out = (gates[:, 0:gw] * oc_ref[0] + gates[:, gw:2 * gw] * heads_on_lanes(acc_d)
           + gates[:, 2 * gw:3 * gw] * heads_on_lanes(acc_w))
    o_ref[0] = out.astype(BF)


def _attend(flags, q, gates, o_cmp, sel, ks, vs, kw, vw):
    b, s, _ = q.shape
    nb = sel.shape[3]
    n_words = nb // FLAG_BITS
    nwin = (WINDOW + ATT_Q) // KEY_TILE
    per_q = ATT_Q // KEY_TILE
    gw = HEADS_PER_GROUP * LANES
    flat = flags[:, :, :, 0, :n_words].reshape(-1)
    spread = _gate_spread()
    full = pl.BlockSpec((1, 1, s, LANES), lambda bi, gi, qi, f: (bi, gi, 0, 0))
    win = lambda i: pl.BlockSpec(
        (1, 1, KEY_TILE, LANES),
        lambda bi, gi, qi, f: (bi, gi, jnp.maximum(qi * per_q - WINDOW // KEY_TILE + i, 0), 0))
    grid_spec = pltpu.PrefetchScalarGridSpec(
        num_scalar_prefetch=1,
        grid=(b, NSA_GROUPS, s // ATT_Q),
        in_specs=[pl.BlockSpec((1, ATT_Q, gw), lambda bi, gi, qi, f: (bi, qi, gi)),
                  pl.BlockSpec((1, ATT_Q, LANES), lambda bi, gi, qi, f: (bi, qi, gi)),
                  pl.BlockSpec((1, ATT_Q, HEADS_PER_GROUP * HEAD_DIM), lambda bi, gi, qi, f: (bi, qi, gi)),
                  pl.BlockSpec((1, 1, ATT_Q, nb), lambda bi, gi, qi, f: (bi, gi, qi, 0)),
                  pl.BlockSpec(spread.shape, lambda bi, gi, qi, f: (0, 0)),
                  full, full] + [win(i) for i in range(nwin)] * 2,
        out_specs=pl.BlockSpec((1, ATT_Q, HEADS_PER_GROUP * HEAD_DIM), lambda bi, gi, qi, f: (bi, qi, gi)),
        scratch_shapes=[pltpu.SMEM((s // KEY_TILE + GROUP_TILES,), jnp.int32)],
    )
    return pl.pallas_call(
        functools.partial(_attend_kernel, n_words=n_words),
        grid_spec=grid_spec,
        out_shape=jax.ShapeDtypeStruct((b, s, NSA_HEADS * HEAD_DIM), BF),
        compiler_params=_params(3),
        name="nsa_attend",
    )(flat, q, gates, o_cmp, sel, spread, ks, vs, *([kw] * nwin), *([vw] * nwin))


def _head_indicators(d):
    ch = jnp.arange(d)[:, None] // RWKV_HEAD
    ind = (ch == jnp.arange(LANES)[None, :]).astype(BF)
    return ind, ind.T


def _rwkv_pre_kernel(x_ref, pre_ref, mu_ref, wr_ref, wk_ref, wv_ref, w0_ref, wd1_ref, wd2_ref,
                     a0_ref, wa1_ref, wa2_ref, wg1_ref, wg2_ref, kk_ref, ka_ref, ind_ref, indt_ref,
                     r_out, lw_out, k_out, v_out, kk_out, kka_out, g_out, hprev_ref, *, n_sub):
    si = pl.program_id(1)
    tm = x_ref.shape[1]

    @pl.when(si == 0)
    def _():
        hprev_ref[...] = jnp.zeros(hprev_ref.shape, F32)

    h_all = _rmsnorm(x_ref[0], pre_ref[...])
    rows = lax.broadcasted_iota(jnp.int32, (tm, 1), 0)
    xx_all = jnp.where(rows == 0, hprev_ref[7:8], pltpu.roll(h_all, 1, axis=0)) - h_all
    hprev_ref[...] = h_all[tm - 8:tm]

    def sub_tile(rs):
        h, xx = h_all[rs], xx_all[rs]
        mix = lambda i: (h + xx * mu_ref[i:i + 1]).astype(BF)
        xr, xw, xk, xv, xa, xg = [mix(i) for i in range(6)]
        r = _dot(xr, wr_ref[...])
        k = _dot(xk, wk_ref[...])
        v = _dot(xv, wv_ref[...])
        yield
        z = w0_ref[...] + _dot(jnp.tanh(_dot(xw, wd1_ref[...])).astype(BF), wd2_ref[...])
        lw = -DECAY_SCALE * _sigmoid(z)
        a = _sigmoid(a0_ref[...] + _dot(_dot(xa, wa1_ref[...]).astype(BF), wa2_ref[...]))
        g = _dot(_sigmoid(_dot(xg, wg1_ref[...])).astype(BF), wg2_ref[...])
        yield
        kraw = k * kk_ref[...]
        ss = _dot((kraw * kraw).astype(BF), ind_ref[...])
        yield
        inv = lax.rsqrt(jnp.maximum(ss, 1e-12))
        kk = kraw * _dot(inv.astype(BF), indt_ref[...])
        yield
        r_out[0, rs] = r.astype(BF)
        lw_out[0, rs] = lw
        k_out[0, rs] = (k * (1.0 + (a - 1.0) * ka_ref[...])).astype(BF)
        v_out[0, rs] = v.astype(BF)
        kk_out[0, rs] = kk.astype(BF)
        kka_out[0, rs] = (kk * a).astype(BF)
        g_out[0, rs] = g.astype(BF)

    _round_robin(sub_tile(slice(i * tm // n_sub, (i + 1) * tm // n_sub)) for i in range(n_sub))


def _rwkv_pre(x, pre_g, mu, w_r, w_k, w_v, w0, w_dec1, w_dec2, a0, w_a1, w_a2, w_g1, w_g2,
              k_k, k_a, ind, indt, *, tm=512, n_sub=1):
    b, s, d = x.shape
    row = pl.BlockSpec((1, tm, d), lambda bi, si: (bi, si, 0))
    vec = lambda a: a.reshape(1, d)
    ops = [x, vec(pre_g), mu, w_r.astype(BF), w_k.astype(BF), w_v.astype(BF), vec(w0),
           w_dec1.astype(BF), w_dec2.astype(BF), vec(a0), w_a1.astype(BF), w_a2.astype(BF),
           w_g1.astype(BF), w_g2.astype(BF), vec(k_k), vec(k_a), ind, indt]
    return pl.pallas_call(
        functools.partial(_rwkv_pre_kernel, n_sub=n_sub),
        grid=(b, s // tm),
        in_specs=[row] + [_const_spec(o.shape) for o in ops[1:]],
        out_specs=[row] * 7,
        out_shape=[jax.ShapeDtypeStruct((b, s, d), F32 if i == 1 else BF) for i in range(7)],
        scratch_shapes=[pltpu.VMEM((8, d), F32)],
        compiler_params=_params(2),
        name="rwkv_pre",
    )(*ops)


def _rwkv_scan_kernel(r_ref, lw_ref, k_ref, v_ref, kk_ref, kka_ref, y_ref, state_ref):
    c = RWKV_CHUNK
    slab = SLAB_HEADS * RWKV_HEAD
    n_slab = r_ref.shape[2] // slab
    rows = SLAB_HEADS * c
    subs = [slice(i * c, (i + 1) * c) for i in range(r_ref.shape[1] // c)]

    @pl.when(pl.program_id(1) == 0)
    def _():
        state_ref[...] = jnp.zeros(state_ref.shape, F32)

    ri = lax.broadcasted_iota(jnp.int32, (rows, 1), 0)
    ci = lax.broadcasted_iota(jnp.int32, (1, rows), 1)
    li = lax.broadcasted_iota(jnp.int32, (1, slab), 1)
    c_shift = c.bit_length() - 1
    h_shift = RWKV_HEAD.bit_length() - 1
    assert c == 1 << c_shift and RWKV_HEAD == 1 << h_shift
    head_match = (ri >> c_shift) == (li >> h_shift)
    t_r, t_c = ri & (c - 1), ci & (c - 1)
    strict = t_c < t_r
    incl = t_c <= t_r
    eye = jnp.where(ri == ci, 1.0, 0.0)
    tri = jnp.where(lax.broadcasted_iota(jnp.int32, (c, c), 1) <= lax.broadcasted_iota(jnp.int32, (c, c), 0),
                    1.0, 0.0).astype(BF)
    state_match = (lax.broadcasted_iota(jnp.int32, (slab, 1), 0) >> h_shift) == (li >> h_shift)

    incl2 = jnp.concatenate([incl, incl], axis=1)

    def block_diag(x):
        return jnp.where(head_match, jnp.concatenate([x] * SLAB_HEADS, axis=0), 0.0)

    def fold(x):
        out = x[0:c]
        for h in range(1, SLAB_HEADS):
            out = out + x[h * c:(h + 1) * c]
        return out

    bf = lambda x: x.astype(BF)
    slabs = [slice(hs * slab, (hs + 1) * slab) for hs in range(n_slab)]

    def prepare(rs, sl):
        lw = lw_ref[0, rs, sl]
        hi = bf(lw)
        rem = lw - hi.astype(F32)
        mid = bf(rem)
        lo = bf(rem - mid.astype(F32))
        cw = _dot(tri, hi) + _dot(tri, mid) + _dot(tri, lo)
        w_in = jnp.exp(cw)
        w_inv = jnp.exp(-cw)
        w_prev = jnp.exp(cw - lw)
        w_end = w_in[c - 1:c]

        a_t = -kk_ref[0, rs, sl].astype(F32) * w_prev
        b_t = kka_ref[0, rs, sl].astype(F32) * w_inv
        k_t = k_ref[0, rs, sl].astype(F32) * w_inv
        r_t = r_ref[0, rs, sl].astype(F32) * w_in
        return a_t, b_t, k_t, r_t, w_end

    def interactions(a_t, b_t, k_t, r_t):
        lhs = bf(jnp.concatenate([block_diag(a_t), block_diag(r_t)], axis=0))
        rhs = bf(jnp.concatenate([block_diag(b_t), block_diag(k_t)], axis=0))
        big = _dot_nt(lhs, rhs)
        a_ab = jnp.where(strict, big[:rows, :rows], 0.0)
        a_ak = jnp.where(strict, big[:rows, rows:], 0.0)
        a_r = jnp.where(incl2, big[rows:], 0.0)
        return a_ab, bf(a_ak), bf(a_r)

    every = range(n_slab)
    units = [(rs, sl) for rs in subs for sl in slabs]
    prep = [prepare(rs, sl) for rs, sl in units]
    inter = [interactions(*p[:4]) for p in prep]
    vs = [v_ref[0, rs, sl] for rs, sl in units]
    vbd = [bf(block_diag(v)) for v in vs]
    akv = [_dot(inter[u][1], vbd[u]) for u in range(len(units))]

    levels = c.bit_length() - 2
    assert levels >= 1
    invs = [eye + x[0] for x in inter]
    pws = [_dot(bf(x[0]), bf(x[0])) for x in inter]
    for _ in range(levels - 1):
        both = [_dot(bf(pw), bf(jnp.concatenate([pw, inv], axis=1))) for pw, inv in zip(pws, invs)]
        pws = [x[:, :rows] for x in both]
        invs = [inv + x[:, rows:] for inv, x in zip(invs, both)]
    invs = [bf(inv + _dot(bf(pw), bf(inv))) for pw, inv in zip(pws, invs)]

    states = [state_ref[hs] for hs in every]
    for si, rs in enumerate(subs):
        us = [si * n_slab + hs for hs in every]
        ar = [_dot_nt(bf(jnp.concatenate([prep[u][0], prep[u][3]], axis=0)), bf(states[hs]))
              for hs, u in zip(every, us)]
        u_bd = [_dot(invs[u], bf(block_diag(ar[hs][:c]) + akv[u])) for hs, u in zip(every, us)]
        y_bd = [block_diag(ar[hs][c:]) + _dot(inter[u][2], jnp.concatenate([bf(u_bd[hs]), vbd[u]], axis=0))
                for hs, u in zip(every, us)]
        for hs, sl in enumerate(slabs):
            y_ref[0, rs, sl] = fold(y_bd[hs])
        new_states = []
        for hs, u in zip(every, us):
            _, b_t, k_t, _, w_end = prep[u]
            uv = jnp.concatenate([bf(fold(u_bd[hs])), bf(vs[u])], axis=0)
            bk = jnp.concatenate([bf(b_t * w_end), bf(k_t * w_end)], axis=0)
            new_states.append(jnp.where(state_match, states[hs] * w_end + _dot_tn(uv, bk), 0.0))
        states = new_states
    for hs in every:
        state_ref[hs] = states[hs]


def _rwkv_scan(r, lw, k, v, kk, kka):
    b, s, d = r.shape
    c = RWKV_CHUNK * SCAN_SUB
    slab = SLAB_HEADS * RWKV_HEAD
    blk = pl.BlockSpec((1, c, d), lambda bi, ci: (bi, ci, 0))
    return pl.pallas_call(
        _rwkv_scan_kernel,
        grid=(b, s // c),
        in_specs=[blk] * 6,
        out_specs=blk,
        out_shape=jax.ShapeDtypeStruct((b, s, d), F32),
        scratch_shapes=[pltpu.VMEM((d // slab, slab, slab), F32)],
        compiler_params=_params(2),
        name="rwkv_scan",
    )(r, lw, k, v, kk, kka)


def _rwkv_post_kernel(x_ref, y_ref, r_ref, k_ref, v_ref, g_ref, wo_ref, lng_ref, lnb_ref, rk_ref,
                      post_ref, ind_ref, indt_ref, o_ref, *, n_sub):
    ind, indt = ind_ref[...], indt_ref[...]
    inv_n = 1.0 / RWKV_HEAD
    head_sum = lambda a: _dot(a.astype(BF), ind)
    spread = lambda a: _dot(a.astype(BF), indt)
    tm = x_ref.shape[0]

    def sub_tile(rs):
        f32 = lambda ref: ref[rs].astype(F32)
        y = y_ref[rs]
        sums = head_sum(y)
        rk_sum = head_sum(f32(r_ref) * f32(k_ref) * rk_ref[...])
        yield
        mean = _split_dot(sums * inv_n, indt)
        bonus = spread(rk_sum) * f32(v_ref)
        yield
        yc = y - mean
        var = head_sum(yc * yc) * inv_n
        yield
        rstd = spread(lax.rsqrt(var + LNX_EPS))
        yield
        yn = yc * rstd * lng_ref[...] + lnb_ref[...]
        out = _dot(((yn + bonus) * f32(g_ref)).astype(BF), wo_ref[...])
        yield
        o_ref[rs] = x_ref[rs] + _rmsnorm(out, post_ref[...])

    _round_robin(sub_tile(slice(i * tm // n_sub, (i + 1) * tm // n_sub)) for i in range(n_sub))


def _rwkv_post(x2, y, r, k, v, g, w_o, lnx_g, lnx_b, r_k, post_g, ind, indt, *, tm=512, n_sub=2):
    t, d = x2.shape
    row = pl.BlockSpec((tm, d), lambda i: (i, 0))
    vec = lambda a: a.reshape(1, d)
    ops = [x2, y, r, k, v, g, w_o.astype(BF), vec(lnx_g), vec(lnx_b), vec(r_k), vec(post_g), ind, indt]
    return pl.pallas_call(
        functools.partial(_rwkv_post_kernel, n_sub=n_sub),
        grid=(t // tm,),
        in_specs=[row] * 6 + [_const_spec(o.shape) for o in ops[6:]],
        out_specs=row,
        out_shape=jax.ShapeDtypeStruct((t, d), F32),
        compiler_params=_params(1),
        name="rwkv_post",
    )(*ops)


def _layer0_mixer(x, pre_g, w_in, pe_k, w1_k, w2_k, pe_v, w1_v, w2_v, conv_w, conv_b):
    b, s, _ = x.shape
    q, gates, kcr, vcr, ks, vs, kw, vw, o_conv = _l0_in(x, pre_g, _l0_in_weight(w_in), conv_w, conv_b)
    kc = _compress(kcr, _cmp_weights(pe_k, w1_k, w2_k), is_key=True)
    vc = _compress(vcr, _cmp_weights(pe_v, w1_v, w2_v), is_key=False)
    o_cmp, sel, flags = _select(q, kc, vc)
    o_nsa = _attend(flags, q, gates, o_cmp, sel, ks, vs, kw, vw)
    return o_nsa.reshape(b * s, -1), o_conv.reshape(b * s, -1)


def _layer1_mixer(x, pre_g, post_g, mu, w_r, w_k, w_v, w_o, w0, w_dec1, w_dec2, a0, w_a1, w_a2,
                  w_g1, w_g2, k_k, k_a, r_k, lnx_g, lnx_b):
    b, s, d = x.shape
    ind, indt = _head_indicators(d)
    r, lw, k, v, kk, kka, g = _rwkv_pre(x, pre_g, mu, w_r, w_k, w_v, w0, w_dec1, w_dec2, a0,
                                        w_a1, w_a2, w_g1, w_g2, k_k, k_a, ind, indt)
    y = _rwkv_scan(r, lw, k, v, kk, kka)
    f = lambda a: a.reshape(b * s, d)
    return _rwkv_post(f(x), f(y), f(r), f(k), f(v), f(g), w_o, lnx_g, lnx_b, r_k, post_g,
                      ind, indt).reshape(b, s, d)


def kernel(x, l0_ffn1_pre_g, l0_ffn1_post_g, l0_ffn1_w_gate, l0_ffn1_w_up, l0_ffn1_w_down, l0_mix_pre_g, l0_mix_post_g, l0_w_in, l0_cmp_pe_k, l0_cmp_w1_k, l0_cmp_w2_k, l0_cmp_pe_v, l0_cmp_w1_v, l0_cmp_w2_v, l0_conv_w, l0_conv_b, l0_w_out, l0_ffn2_pre_g, l0_ffn2_post_g, l0_ffn2_w_gate, l0_ffn2_w_up, l0_ffn2_w_down, l1_ffn1_pre_g, l1_ffn1_post_g, l1_ffn1_w_gate, l1_ffn1_w_up, l1_ffn1_w_down, l1_mix_pre_g, l1_mix_post_g, l1_mu, l1_w_r, l1_w_k, l1_w_v, l1_w_o, l1_w0, l1_w_dec1, l1_w_dec2, l1_a0, l1_w_a1, l1_w_a2, l1_w_g1, l1_w_g2, l1_k_k, l1_k_a, l1_r_k, l1_lnx_g, l1_lnx_b, l1_ffn2_pre_g, l1_ffn2_post_g, l1_ffn2_w_gate, l1_ffn2_w_up, l1_ffn2_w_down):
    b, s, d = x.shape
    ffn = lambda a, *w, **kw: _ffn(a.reshape(b * s, d), *w, **kw).reshape(b, s, d)
    x = ffn(x, l0_ffn1_pre_g, l0_ffn1_post_g, l0_ffn1_w_gate, l0_ffn1_w_up, l0_ffn1_w_down)
    o_nsa, o_conv = _layer0_mixer(x, l0_mix_pre_g, l0_w_in, l0_cmp_pe_k, l0_cmp_w1_k, l0_cmp_w2_k,
                                  l0_cmp_pe_v, l0_cmp_w1_v, l0_cmp_w2_v, l0_conv_w, l0_conv_b)
    x = ffn(x, l0_ffn2_pre_g, l0_ffn2_post_g, l0_ffn2_w_gate, l0_ffn2_w_up, l0_ffn2_w_down,
            mixer=(o_nsa, o_conv, l0_w_out, l0_mix_post_g))
    x = ffn(x, l1_ffn1_pre_g, l1_ffn1_post_g, l1_ffn1_w_gate, l1_ffn1_w_up, l1_ffn1_w_down)
    x = _layer1_mixer(x, l1_mix_pre_g, l1_mix_post_g, l1_mu, l1_w_r, l1_w_k, l1_w_v, l1_w_o, l1_w0,
                      l1_w_dec1, l1_w_dec2, l1_a0, l1_w_a1, l1_w_a2, l1_w_g1, l1_w_g2, l1_k_k, l1_k_a,
                      l1_r_k, l1_lnx_g, l1_lnx_b)
    x = ffn(x, l1_ffn2_pre_g, l1_ffn2_post_g, l1_ffn2_w_gate, l1_ffn2_w_up, l1_ffn2_w_down)
    return x
```

```python
import functools

import jax
import jax.numpy as jnp
import numpy as np
from jax import lax
from jax.experimental import pallas as pl
from jax.experimental.pallas import tpu as pltpu

BF = jnp.bfloat16
F32 = jnp.float32

EPS = 1e-6
LNX_EPS = 64e-5
HEAD_DIM = 64
NSA_HEADS = 8
NSA_GROUPS = 2
HEADS_PER_GROUP = NSA_HEADS // NSA_GROUPS
CMP_STRIDE = 16
CMP_BLOCK = 32
CMP_HIDDEN = 128
SLC_BLOCK = 64
SLC_TOP_N = 16
WINDOW = 512
Q_BLOCK = 128
CONV_WIDTH = 512
RWKV_HEAD = 64
RWKV_CHUNK = 64
SLAB_HEADS = 2
SCAN_SUB = 4
DECAY_SCALE = 0.6065306597126334
LANES = 128
MXU_DIM = 256
POS_SHIFT = 7
POS_SPLIT = 1 << POS_SHIFT
LOG2E = 1.4426950408889634
BIAS_PIECES = 3
NEG = -1e30
VMEM_LIMIT = 56 * 1024 * 1024

NT_DIMS = (((1,), (1,)), ((), ()))
TN_DIMS = (((0,), (0,)), ((), ()))


def _dot(a, b):
    return jnp.dot(a, b, preferred_element_type=F32)


def _dot_nt(a, b):
    return lax.dot_general(a, b, NT_DIMS, preferred_element_type=F32)


def _dot_tn(a, b):
    return lax.dot_general(a, b, TN_DIMS, preferred_element_type=F32)


def _split_dot(x, w):
    hi = x.astype(BF)
    lo = (x - hi.astype(F32)).astype(BF)
    return _dot(hi, w) + _dot(lo, w)


def _rmsnorm(x, g):
    ms = jnp.mean(x * x, axis=-1, keepdims=True)
    return x * lax.rsqrt(ms + EPS) * g


def _sigmoid(x):
    return 1.0 / (1.0 + jnp.exp(-x))


def _round_robin(stage_generators):
    pending = list(stage_generators)
    while pending:
        still = []
        for gen in pending:
            try:
                next(gen)
                still.append(gen)
            except StopIteration:
                pass
        pending = still


def _bf16_pieces(value, n):
    pieces = []
    for _ in range(n):
        pieces.append(float(np.float32(value).astype(jnp.bfloat16)))
        value -= pieces[-1]
    return pieces


def _query_bias_lanes(qh, lane, slope):
    bias = jnp.zeros(lane.shape, F32)
    for i, c in enumerate(_bf16_pieces(slope * LOG2E, BIAS_PIECES)):
        bias = jnp.where(lane == HEAD_DIM + i, POS_SPLIT * c, bias)
        bias = jnp.where(lane == HEAD_DIM + BIAS_PIECES + i, c, bias)
    return qh + bias


def _key_position_lanes(lane, hi, lo):
    in_hi = (lane >= HEAD_DIM) & (lane < HEAD_DIM + BIAS_PIECES)
    in_lo = (lane >= HEAD_DIM + BIAS_PIECES) & (lane < HEAD_DIM + 2 * BIAS_PIECES)
    return jnp.where(in_hi, hi, jnp.where(in_lo, lo, 0.0))


def _const_spec(shape):
    zeros = (0,) * len(shape)
    return pl.BlockSpec(shape, lambda *_: zeros)


def _params(n_grid):
    return pltpu.CompilerParams(dimension_semantics=("arbitrary",) * n_grid,
                                vmem_limit_bytes=VMEM_LIMIT)


def _ffn_kernel(*refs, ff_chunk, n_sub, after_mixer):
    if after_mixer:
        x_ref, a_ref, c_ref, wa_ref, wc_ref, mixpost_ref = refs[:6]
        refs = refs[:1] + refs[6:]
    x_ref, pre_ref, post_ref, wg_ref, wu_ref, wd_ref, o_ref = refs

    def residual_rows(rs):
        if not after_mixer:
            return x_ref[rs]
        y = _dot(a_ref[rs], wa_ref[...]) + _dot(c_ref[rs], wc_ref[...])
        return x_ref[rs] + _rmsnorm(y, mixpost_ref[...])

    sub = x_ref.shape[0] // n_sub
    dff = wg_ref.shape[1]
    chunks = [(c0, min(c0 + ff_chunk, dff)) for c0 in range(0, dff, ff_chunk)]
    rows = [slice(i * sub, (i + 1) * sub) for i in range(n_sub)]
    xs, hs, accs = [], [], []
    for ci, (c0, c1) in enumerate(chunks):
        gu = []
        for i, rs in enumerate(rows):
            if ci == 0:
                xs.append(residual_rows(rs))
                hs.append(_rmsnorm(xs[i], pre_ref[...]).astype(BF))
                accs.append(jnp.zeros(xs[i].shape, F32))
            gu.append((_dot(hs[i], wg_ref[:, c0:c1]), _dot(hs[i], wu_ref[:, c0:c1])))
        for i, (g, u) in enumerate(gu):
            a = (g * _sigmoid(g) * u).astype(BF)
            accs[i] = accs[i] + _dot(a, wd_ref[c0:c1, :])
    for i, rs in enumerate(rows):
        o_ref[rs] = xs[i] + 0.5 * _rmsnorm(accs[i], post_ref[...])


def _ffn(x2, pre_g, post_g, w_gate, w_up, w_down, *, mixer=None, tm=1024, n_sub=2):
    t, d = x2.shape
    dff = w_gate.shape[1]
    ff_chunk = MXU_DIM * pl.cdiv(pl.cdiv(dff, MXU_DIM), 2)
    row = lambda w: pl.BlockSpec((tm, w), lambda i: (i, 0))
    weight = lambda shape: pl.BlockSpec(shape, lambda i: (0, 0), pipeline_mode=pl.Buffered(1))
    ops, specs = [x2], [row(d)]
    if mixer is not None:
        o_nsa, o_conv, w_out, mix_post_g = mixer
        na, nc = o_nsa.shape[1], o_conv.shape[1]
        ops += [o_nsa, o_conv, w_out[:na].astype(BF), w_out[na:].astype(BF), mix_post_g.reshape(1, d)]
        specs += [row(na), row(nc), weight((na, d)), weight((nc, d)), _const_spec((1, d))]
    ops += [pre_g.reshape(1, d), post_g.reshape(1, d), w_gate.astype(BF), w_up.astype(BF), w_down.astype(BF)]
    specs += [_const_spec((1, d)), _const_spec((1, d)), weight((d, dff)), weight((d, dff)), weight((dff, d))]
    return pl.pallas_call(
        functools.partial(_ffn_kernel, ff_chunk=ff_chunk, n_sub=n_sub, after_mixer=mixer is not None),
        grid=(t // tm,),
        in_specs=specs,
        out_specs=row(d),
        out_shape=jax.ShapeDtypeStruct((t, d), F32),
        compiler_params=_params(1),
        name="ffn",
    )(*ops)


_C_Q = 0
_C_G = _C_Q + NSA_HEADS * LANES
_C_KC = _C_G + NSA_GROUPS * LANES
_C_VC = _C_KC + LANES
_C_KS = _C_VC + LANES
_C_CONV = _C_KS + 4 * NSA_GROUPS * LANES
_C_END = _C_CONV + 3 * CONV_WIDTH


def _l0_in_weight(w_in):
    d = w_in.shape[0]
    nsa_w = NSA_HEADS * HEAD_DIM
    kv_w = NSA_GROUPS * HEAD_DIM
    o = 0
    q = w_in[:, o:o + nsa_w]; o += nsa_w
    g = w_in[:, o:o + 3 * NSA_HEADS]; o += 3 * NSA_HEADS
    kvs = []
    for _ in range(6):
        kvs.append(w_in[:, o:o + kv_w]); o += kv_w
    conv = w_in[:, o:]
    zpad = lambda w, n: jnp.pad(w, ((0, 0), (0, n - w.shape[1])))
    cols = [zpad(q[:, h * HEAD_DIM:(h + 1) * HEAD_DIM], LANES) for h in range(NSA_HEADS)]
    gpg = 3 * HEADS_PER_GROUP
    cols += [zpad(g[:, i * gpg:(i + 1) * gpg], LANES) for i in range(NSA_GROUPS)]
    cols += [kvs[0], kvs[1]]
    for w in kvs[2:]:
        cols += [zpad(w[:, i * HEAD_DIM:(i + 1) * HEAD_DIM], LANES) for i in range(NSA_GROUPS)]
    cols.append(conv)
    w = jnp.concatenate(cols, axis=1)
    assert w.shape == (d, _C_END)
    return w.astype(BF)


def _l0_in_kernel(x_ref, pre_ref, w_ref, cw_ref, cb_ref,
                  q_ref, g_ref, kcr_ref, vcr_ref, ks_ref, vs_ref, kw_ref, vw_ref, oc_ref,
                  zprev_ref):
    si = pl.program_id(1)
    tm = x_ref.shape[1]
    h = _rmsnorm(x_ref[0], pre_ref[...]).astype(BF)
    lane = lax.broadcasted_iota(jnp.int32, (1, LANES), 1)
    pos = si * tm + lax.broadcasted_iota(jnp.int32, (tm, 1), 0)
    pos_hi = (pos >> POS_SHIFT).astype(F32)
    pos_lo = (pos & (POS_SPLIT - 1)).astype(F32)

    wide = 4 * LANES

    def project(c0):
        res = _dot(h, w_ref[:, c0:c0 + wide])
        return [res[:, j * LANES:(j + 1) * LANES] for j in range(4)]

    for hd0 in range(0, NSA_HEADS, 4):
        for j, qh in enumerate(project(_C_Q + hd0 * LANES)):
            hd = hd0 + j
            slope = 2.0 ** (-(hd + 1))
            qh = _query_bias_lanes(qh * (HEAD_DIM ** -0.5 * LOG2E), lane, slope)
            q_ref[0, :, hd * LANES:(hd + 1) * LANES] = qh.astype(BF)

    assert _C_KC == _C_G + 2 * LANES and _C_VC == _C_KC + LANES and NSA_GROUPS == 2
    g0, g1, kcr, vcr = project(_C_G)
    g_ref[0] = _sigmoid(jnp.concatenate([g0, g1], axis=1))
    kcr_ref[0] = kcr
    vcr_ref[0] = vcr

    slabs = project(_C_KS) + project(_C_KS + wide)
    key_pos = _key_position_lanes(lane, pos_hi, pos_lo)
    for n, (ref, is_key) in enumerate(((ks_ref, True), (vs_ref, False), (kw_ref, True), (vw_ref, False))):
        for gi in range(NSA_GROUPS):
            t = slabs[n * NSA_GROUPS + gi]
            if is_key:
                t = t + key_pos
            else:
                t = jnp.where(lane >= HEAD_DIM, 1.0, t)
            ref[0, gi] = t.astype(BF)

    @pl.when(si == 0)
    def _():
        zprev_ref[...] = jnp.zeros(zprev_ref.shape, F32)

    cb = _dot(h, w_ref[:, _C_CONV:_C_CONV + CONV_WIDTH])
    cc = _dot(h, w_ref[:, _C_CONV + CONV_WIDTH:_C_CONV + 2 * CONV_WIDTH])
    cu = _dot(h, w_ref[:, _C_CONV + 2 * CONV_WIDTH:_C_CONV + 3 * CONV_WIDTH])
    z = cc * cu
    prev = zprev_ref[...]
    rows = lax.broadcasted_iota(jnp.int32, (tm, 1), 0)
    zm1 = jnp.where(rows == 0, prev[7:8], pltpu.roll(z, 1, axis=0))
    zm2 = jnp.where(rows == 0, prev[6:7], jnp.where(rows == 1, prev[7:8], pltpu.roll(z, 2, axis=0)))
    y = cw_ref[0:1] * zm2 + cw_ref[1:2] * zm1 + cw_ref[2:3] * z
    oc_ref[0] = (cb * (y + cb_ref[...])).astype(BF)
    zprev_ref[...] = z[tm - 8:tm]


def _l0_in(x, pre_g, w_all, conv_w, conv_b, *, tm=512):
    b, s, d = x.shape
    row = lambda w: pl.BlockSpec((1, tm, w), lambda bi, si: (bi, si, 0))
    grp = pl.BlockSpec((1, NSA_GROUPS, tm, LANES), lambda bi, si: (bi, 0, si, 0))
    sds = jax.ShapeDtypeStruct
    kv_shape = sds((b, NSA_GROUPS, s, LANES), BF)
    return pl.pallas_call(
        _l0_in_kernel,
        grid=(b, s // tm),
        in_specs=[row(d), _const_spec((1, d)), _const_spec(w_all.shape),
                  _const_spec(conv_w.shape), _const_spec((1, CONV_WIDTH))],
        out_specs=[row(NSA_HEADS * LANES), row(NSA_GROUPS * LANES), row(LANES), row(LANES),
                   grp, grp, grp, grp, row(CONV_WIDTH)],
        out_shape=[sds((b, s, NSA_HEADS * LANES), BF), sds((b, s, NSA_GROUPS * LANES), F32),
                   sds((b, s, LANES), F32), sds((b, s, LANES), F32),
                   kv_shape, kv_shape, kv_shape, kv_shape, sds((b, s, CONV_WIDTH), BF)],
        scratch_shapes=[pltpu.VMEM((8, CONV_WIDTH), F32)],
        compiler_params=_params(2),
        name="l0_in",
    )(x, pre_g.reshape(1, d), w_all, conv_w, conv_b.reshape(1, CONV_WIDTH))


def _cmp_weights(pe, w1, w2):
    half = CMP_STRIDE * HEAD_DIM
    def expand(w):
        w = w.reshape(CMP_STRIDE, 1, HEAD_DIM, 1, CMP_HIDDEN)
        same_group = jnp.eye(NSA_GROUPS, dtype=F32).reshape(1, NSA_GROUPS, 1, NSA_GROUPS, 1)
        return (w * same_group).reshape(CMP_STRIDE * NSA_GROUPS * HEAD_DIM, NSA_GROUPS * CMP_HIDDEN).astype(BF)
    pe_row = lambda p: jnp.tile(p, (1, NSA_GROUPS)).reshape(1, CMP_STRIDE * NSA_GROUPS * HEAD_DIM)
    w2p = jnp.pad(w2, ((0, 0), (0, LANES - HEAD_DIM))).astype(BF)
    return (pe_row(pe[:CMP_STRIDE]), pe_row(pe[CMP_STRIDE:]), expand(w1[:half]), expand(w1[half:]), w2p)


def _gelu_tanh(x):
    return 0.5 * x * (1.0 + jnp.tanh(0.7978845608028654 * (x + 0.044715 * (x * x * x))))


def _cmp_kernel(x_ref, pet_ref, peb_ref, wt_ref, wb_ref, w2_ref, o_ref, *, is_key):
    nb = x_ref.shape[1]
    hw = x_ref.shape[2] // 4
    top, bot = [], []
    for c in range(4):
        xc = x_ref[0, :, c * hw:(c + 1) * hw]
        top.append(_dot((xc + pet_ref[...]).astype(BF), wt_ref[...]))
        bot.append(_dot((xc + peb_ref[...]).astype(BF), wb_ref[...]))
    lane = lax.broadcasted_iota(jnp.int32, (1, LANES), 1)
    j = lax.broadcasted_iota(jnp.int32, (nb, 1), 0)
    for c in range(4):
        nxt = bot[c + 1] if c < 3 else pltpu.roll(bot[0], nb - 1, axis=0)
        hid = _gelu_tanh(top[c] + nxt)
        n = 4 * j + c
        end = n * CMP_STRIDE + (CMP_BLOCK - 1)
        exists = n < 4 * nb - 1
        for gi in range(NSA_GROUPS):
            t = _dot(hid[:, gi * CMP_HIDDEN:(gi + 1) * CMP_HIDDEN].astype(BF), w2_ref[...])
            t = jnp.where(exists, t, 0.0)
            if is_key:
                t = t + _key_position_lanes(lane, (end >> POS_SHIFT).astype(F32),
                                            (end & (POS_SPLIT - 1)).astype(F32))
            else:
                t = jnp.where(lane == HEAD_DIM, 1.0, t)
            o_ref[0, gi, c * nb:(c + 1) * nb, :] = t.astype(BF)


def _compress(raw, weights, *, is_key):
    b, s, _ = raw.shape
    nb = s // (4 * CMP_STRIDE)
    x = raw.reshape(b, nb, 4 * CMP_STRIDE * LANES)
    pet, peb, wt, wb, w2p = weights
    return pl.pallas_call(
        functools.partial(_cmp_kernel, is_key=is_key),
        grid=(b,),
        in_specs=[pl.BlockSpec((1, nb, x.shape[2]), lambda bi: (bi, 0, 0)),
                  _const_spec(pet.shape), _const_spec(peb.shape), _const_spec(wt.shape),
                  _const_spec(wb.shape), _const_spec(w2p.shape)],
        out_specs=pl.BlockSpec((1, NSA_GROUPS, 4 * nb, LANES), lambda bi: (bi, 0, 0, 0)),
        out_shape=jax.ShapeDtypeStruct((b, NSA_GROUPS, 4 * nb, LANES), BF),
        compiler_params=_params(1),
        name="compress",
    )(x, pet, peb, wt, wb, w2p)


def _softmax_update(carry, s, v):
    m, acc = carry
    m_new = jnp.maximum(m, jnp.max(s, axis=-1, keepdims=True))
    p = jnp.exp2(s - m_new)
    acc = jnp.exp2(m - m_new) * acc + _dot(p.astype(BF), v)
    return m_new, acc


SEL_SUB = 2
N_FORCED = 3
FLAG_BITS = 16
ATT_Q = 256
KEY_TILE = 128
TILE_BLOCKS = KEY_TILE // SLC_BLOCK
GROUP_TILES = 4


def _stack_heads(q_ref, g):
    base = g * HEADS_PER_GROUP
    return jnp.concatenate([q_ref[0, :, (base + r) * LANES:(base + r + 1) * LANES]
                            for r in range(HEADS_PER_GROUP)], axis=0)


def _flag_weights(nb):
    j = jnp.arange(nb)[:, None]
    w = jnp.arange(LANES)[None, :]
    return jnp.where(j // FLAG_BITS == w, 2.0 ** (j % FLAG_BITS), 0.0).astype(BF)


def _prefix_weights(nb):
    return (jnp.arange(nb)[:, None] <= jnp.arange(nb)[None, :]).astype(BF)


def _select_kernel(q_ref, kc_ref, vc_ref, pw_ref, tri_ref, oc_ref, sel_ref, flag_ref):
    qi = pl.program_id(1)
    q0 = qi * (SEL_SUB * Q_BLOCK)
    rq = HEADS_PER_GROUP * Q_BLOCK
    nb = kc_ref.shape[2] // 4
    subs = range(SEL_SUB)
    units = [(sb, g) for sb in subs for g in range(NSA_GROUPS)]
    rows = [slice(sb * Q_BLOCK, (sb + 1) * Q_BLOCK) for sb in subs]
    t1 = [q0 + sb * Q_BLOCK + lax.broadcasted_iota(jnp.int32, (Q_BLOCK, 1), 0) for sb in subs]
    t4 = [q0 + sb * Q_BLOCK + (lax.broadcasted_iota(jnp.int32, (rq, 1), 0) & (Q_BLOCK - 1))
          for sb in subs]
    blk_t = [t >> 6 for t in t1]

    def stacked_queries(sb, g):
        base = g * HEADS_PER_GROUP
        return jnp.concatenate([q_ref[0, rows[sb], (base + r) * LANES:(base + r + 1) * LANES]
                                for r in range(HEADS_PER_GROUP)], axis=0)

    def run(w):
        w_shift = w.bit_length() - 1
        assert w == 1 << w_shift and w % LANES == 0
        col = lax.broadcasted_iota(jnp.int32, (1, 4 * w), 1)
        cmp_end = (4 * (col & (w - 1)) + (col >> w_shift)) * CMP_STRIDE + (CMP_BLOCK - 1)
        jl = lax.broadcasted_iota(jnp.int32, (1, w), 1)
        valid = [cmp_end <= t4[sb] for sb in subs]
        forced = [(jl == 0) | (jl == blk_t[sb]) | (jl == blk_t[sb] - 1) for sb in subs]
        eligible = [jl <= blk_t[sb] for sb in subs]
        pickable = [eligible[sb] & ~forced[sb] for sb in subs]
        n_pick = min(SLC_TOP_N, nb) - N_FORCED

        def columns(ref, g):
            if w == nb:
                return ref[0, g]
            return jnp.concatenate([ref[0, g, c * nb:c * nb + w] for c in range(4)], axis=0)

        def importance(p):
            ps = p[0:Q_BLOCK]
            for r in range(1, HEADS_PER_GROUP):
                ps = ps + p[r * Q_BLOCK:(r + 1) * Q_BLOCK]
            parts = [ps[:, c * w:(c + 1) * w] for c in range(4)]
            prev = jnp.where(jl == 0, 0.0, pltpu.roll(parts[3], 1, axis=1))
            return (parts[0] + parts[1] + parts[2] + parts[3]) - 0.5 * parts[3] + 0.5 * prev

        s = [jnp.where(valid[sb], _dot_nt(stacked_queries(sb, g), columns(kc_ref, g)), NEG) for sb, g in units]
        m = [jnp.max(x, axis=-1, keepdims=True) for x in s]
        m = [jnp.where(x <= NEG, 0.0, x) for x in m]
        p = [jnp.exp2(x - mx) for x, mx in zip(s, m)]
        p = [x / jnp.maximum(jnp.sum(x, axis=-1, keepdims=True), 1e-30) for x in p]
        o_cmp = [_dot(x.astype(BF), columns(vc_ref, g)) for x, (sb, g) in zip(p, units)]
        score = [jnp.where(pickable[sb], importance(x), NEG) for x, (sb, g) in zip(p, units)]
        work = score
        for _ in range(n_pick - 1):
            hit = [jl == jnp.argmax(x, axis=-1, keepdims=True).astype(jnp.int32) for x in work]
            work = [jnp.where(h, 3.0 * NEG, x) for h, x in zip(hit, work)]
        thr = [jnp.max(x, axis=-1, keepdims=True) for x in work]
        above = [x > t for x, t in zip(score, thr)]
        tie = [x == t for x, t in zip(score, thr)]
        n_above = [jnp.sum(jnp.where(x, 1.0, 0.0), axis=-1, keepdims=True) for x in above]
        tie_rank = [_dot(jnp.where(x, 1.0, 0.0).astype(BF), tri_ref[0:w, 0:w]) for x in tie]
        for u, (sb, g) in enumerate(units):
            sel = forced[sb] | above[u] | (tie[u] & (tie_rank[u] <= n_pick - n_above[u]))
            for r in range(HEADS_PER_GROUP):
                h0 = (g * HEADS_PER_GROUP + r) * HEAD_DIM
                oc_ref[0, rows[sb], h0:h0 + HEAD_DIM] = o_cmp[u][r * Q_BLOCK:(r + 1) * Q_BLOCK, :HEAD_DIM]
            sel_u = jnp.where(eligible[sb] & sel, 1.0, 0.0)
            sel_ref[0, g, rows[sb], 0:w] = sel_u.astype(BF)
            if w < nb:
                sel_ref[0, g, rows[sb], w:nb] = jnp.zeros((Q_BLOCK, nb - w), BF)
            any_sel = jnp.broadcast_to(jnp.max(sel_u, axis=0, keepdims=True), (8, w)).astype(BF)
            flag_ref[0, g, sb] = _dot(any_sel, pw_ref[0:w]).astype(jnp.int32)

    half = nb // 2
    if half % LANES == 0:
        in_first_half = q0 + SEL_SUB * Q_BLOCK <= half * SLC_BLOCK
        pl.when(in_first_half)(lambda: run(half))
        pl.when(jnp.logical_not(in_first_half))(lambda: run(nb))
    else:
        run(nb)


def _select(q, kc, vc):
    b, s, _ = q.shape
    ncp = kc.shape[2]
    nb = ncp // 4
    nq = s // Q_BLOCK
    step_q = SEL_SUB * Q_BLOCK
    pw = _flag_weights(nb)
    tri = _prefix_weights(nb)
    sds = jax.ShapeDtypeStruct
    cmp_spec = pl.BlockSpec((1, NSA_GROUPS, ncp, LANES), lambda bi, qi: (bi, 0, 0, 0))
    return pl.pallas_call(
        _select_kernel,
        grid=(b, s // step_q),
        in_specs=[pl.BlockSpec((1, step_q, NSA_HEADS * LANES), lambda bi, qi: (bi, qi, 0)),
                  cmp_spec, cmp_spec, _const_spec(pw.shape), _const_spec(tri.shape)],
        out_specs=[pl.BlockSpec((1, step_q, NSA_HEADS * HEAD_DIM), lambda bi, qi: (bi, qi, 0)),
                   pl.BlockSpec((1, NSA_GROUPS, step_q, nb), lambda bi, qi: (bi, 0, qi, 0)),
                   pl.BlockSpec((1, NSA_GROUPS, SEL_SUB, 8, LANES), lambda bi, qi: (bi, 0, qi, 0, 0))],
        out_shape=[sds((b, s, NSA_HEADS * HEAD_DIM), F32), sds((b, NSA_GROUPS, s, nb), BF),
                   sds((b, NSA_GROUPS, nq, 8, LANES), jnp.int32)],
        compiler_params=_params(2),
        name="nsa_select",
    )(q, kc, vc, pw, tri)


def _gate_spread():
    gw = HEADS_PER_GROUP * HEAD_DIM
    src = jnp.arange(LANES)[:, None]
    dst = jnp.arange(3 * gw)[None, :]
    c, r = dst // gw, (dst % gw) // HEAD_DIM
    return (src == 3 * r + c).astype(BF)


def _attend_kernel(flag_ref, q_ref, g_ref, oc_ref, sel_ref, spread_ref, ks_ref, vs_ref, *rest, n_words):
    nwin = (WINDOW + ATT_Q) // KEY_TILE
    kw_refs, vw_refs = rest[:nwin], rest[nwin:2 * nwin]
    o_ref, list_ref = rest[2 * nwin], rest[2 * nwin + 1]
    bi, gi, qi = pl.program_id(0), pl.program_id(1), pl.program_id(2)
    q0 = qi * ATT_Q
    diag0 = qi * (ATT_Q // KEY_TILE)
    rq = HEADS_PER_GROUP * ATT_Q
    nb = sel_ref.shape[3]
    tiles_per_word = FLAG_BITS // TILE_BLOCKS
    tile_bits = (1 << TILE_BLOCKS) - 1
    sel_per_step = ATT_Q // Q_BLOCK

    qs = _stack_heads(q_ref, 0)
    selb = sel_ref[0, 0]
    t1 = q0 + lax.broadcasted_iota(jnp.int32, (ATT_Q, 1), 0)

    base = ((bi * NSA_GROUPS + gi) * pl.num_programs(2) + qi) * sel_per_step * n_words

    def word_body(wi, n):
        word = flag_ref[base + wi]
        for extra in range(1, sel_per_step):
            word = word | flag_ref[base + extra * n_words + wi]

        def scan_word(n):
            def tile_body(u, n):
                kt = wi * tiles_per_word + u
                active = (((word >> (u * TILE_BLOCKS)) & tile_bits) != 0) & (kt < diag0)

                @pl.when(active)
                def _():
                    list_ref[n] = kt
                return n + active.astype(jnp.int32)
            return lax.fori_loop(0, tiles_per_word, tile_body, n)
        return lax.cond(word != 0, scan_word, lambda n: n, n)

    n_active = lax.fori_loop(0, (diag0 + tiles_per_word - 1) // tiles_per_word, word_body, 0)
    for u in range(GROUP_TILES):
        list_ref[n_active + u] = 0

    jrow = lax.broadcasted_iota(jnp.int32, (nb, KEY_TILE), 0)
    kblk = lax.broadcasted_iota(jnp.int32, (nb, KEY_TILE), 1) >> 6
    kcol = lax.broadcasted_iota(jnp.int32, (1, KEY_TILE), 1)

    def masked_scores(k_all, member):
        s_ = _dot_nt(qs, k_all)
        return jnp.concatenate([jnp.where(member, s_[r * ATT_Q:(r + 1) * ATT_Q], NEG)
                                for r in range(HEADS_PER_GROUP)], axis=0)

    def tile_rows(ref, kt, n=KEY_TILE):
        return ref[0, 0, pl.ds(pl.multiple_of(kt * KEY_TILE, KEY_TILE), n), :]

    def group_body(it, carry):
        ks_t, vs_t, ex_t = [], [], []
        for u in range(GROUP_TILES):
            pos = it * GROUP_TILES + u
            kt = list_ref[pos]
            first_blk = jnp.where(pos < n_active, kt * TILE_BLOCKS, -TILE_BLOCKS - nb)
            ks_t.append(tile_rows(ks_ref, kt))
            vs_t.append(tile_rows(vs_ref, kt))
            ex_t.append(jnp.where(jrow == kblk + first_blk, 1.0, 0.0).astype(BF))
        member = _dot(selb, jnp.concatenate(ex_t, axis=1)) > 0.5
        s_ = masked_scores(jnp.concatenate(ks_t, axis=0), member)
        return _softmax_update(carry, s_, jnp.concatenate(vs_t, axis=0))

    init = (jnp.full((rq, 1), NEG, F32), jnp.zeros((rq, LANES), F32))
    carry = lax.fori_loop(0, (n_active + GROUP_TILES - 1) // GROUP_TILES, group_body, init)
    drow = lax.broadcasted_iota(jnp.int32, (nb, ATT_Q), 0)
    dcol = lax.broadcasted_iota(jnp.int32, (1, ATT_Q), 1)
    expand = jnp.where(drow == (dcol >> 6) + diag0 * TILE_BLOCKS, 1.0, 0.0).astype(BF)
    member = (_dot(selb, expand) > 0.5) & (q0 + dcol <= t1)
    kpos = []
    for i in range(nwin):
        kti = diag0 - WINDOW // KEY_TILE + i
        kpos.append(jnp.where(kti >= 0, kti * KEY_TILE, 1 << 24) + kcol)
    d = t1 - jnp.concatenate(kpos, axis=1)
    in_window = (d >= 0) & (d < WINDOW)
    s_d = masked_scores(tile_rows(ks_ref, diag0, ATT_Q), member)
    s_w = masked_scores(jnp.concatenate([r[0, 0] for r in kw_refs], axis=0), in_window)
    m_d = jnp.maximum(carry[0], jnp.max(s_d, axis=-1, keepdims=True))
    m_w = jnp.max(s_w, axis=-1, keepdims=True)
    p_d = jnp.exp2(s_d - m_d).astype(BF)
    p_w = jnp.exp2(s_w - m_w).astype(BF)
    acc_d = jnp.exp2(carry[0] - m_d) * carry[1] + _dot(p_d, tile_rows(vs_ref, diag0, ATT_Q))
    acc_w = _dot(p_w, jnp.concatenate([r[0, 0] for r in vw_refs], axis=0))

    assert LANES == 2 * HEAD_DIM and HEADS_PER_GROUP % 2 == 0
    low_half = lax.broadcasted_iota(jnp.int32, (1, LANES), 1) < HEAD_DIM

    def heads_on_lanes(acc):
        inv = 1.0 / jnp.maximum(acc, 1e-30)
        pairs = []
        for r in range(0, HEADS_PER_GROUP, 2):
            ev, od = slice(r * ATT_Q, (r + 1) * ATT_Q), slice((r + 1) * ATT_Q, (r + 2) * ATT_Q)
            even = acc[ev] * pltpu.roll(inv[ev], HEAD_DIM, axis=1)
            odd = pltpu.roll(acc[od], HEAD_DIM, axis=1) * inv[od]
            pairs.append(jnp.where(low_half, even, odd))
        return jnp.concatenate(pairs, axis=1)

    gw = HEADS_PER_GROUP * HEAD_DIM
    gates = _split_dot(g_ref[0], spread_ref[...])
    out = (gates[:, 0:gw] * oc_ref[0] + gates[:, gw:2 * gw] * heads_on_lanes(acc_d)
           + gates[:, 2 * gw:3 * gw] * heads_on_lanes(acc_w))
    o_ref[0] = out.astype(BF)


def _attend(flags, q, gates, o_cmp, sel, ks, vs, kw, vw):
    b, s, _ = q.shape
    nb = sel.shape[3]
    n_words = nb // FLAG_BITS
    nwin = (WINDOW + ATT_Q) // KEY_TILE
    per_q = ATT_Q // KEY_TILE
    gw = HEADS_PER_GROUP * LANES
    flat = flags[:, :, :, 0, :n_words].reshape(-1)
    spread = _gate_spread()
    full = pl.BlockSpec((1, 1, s, LANES), lambda bi, gi, qi, f: (bi, gi, 0, 0))
    win = lambda i: pl.BlockSpec(
        (1, 1, KEY_TILE, LANES),
        lambda bi, gi, qi, f: (bi, gi, jnp.maximum(qi * per_q - WINDOW // KEY_TILE + i, 0), 0))
    grid_spec = pltpu.PrefetchScalarGridSpec(
        num_scalar_prefetch=1,
        grid=(b, NSA_GROUPS, s // ATT_Q),
        in_specs=[pl.BlockSpec((1, ATT_Q, gw), lambda bi, gi, qi, f: (bi, qi, gi)),
                  pl.BlockSpec((1, ATT_Q, LANES), lambda bi, gi, qi, f: (bi, qi, gi)),
                  pl.BlockSpec((1, ATT_Q, HEADS_PER_GROUP * HEAD_DIM), lambda bi, gi, qi, f: (bi, qi, gi)),
                  pl.BlockSpec((1, 1, ATT_Q, nb), lambda bi, gi, qi, f: (bi, gi, qi, 0)),
                  pl.BlockSpec(spread.shape, lambda bi, gi, qi, f: (0, 0)),
                  full, full] + [win(i) for i in range(nwin)] * 2,
        out_specs=pl.BlockSpec((1, ATT_Q, HEADS_PER_GROUP * HEAD_DIM), lambda bi, gi, qi, f: (bi, qi, gi)),
        scratch_shapes=[pltpu.SMEM((s // KEY_TILE + GROUP_TILES,), jnp.int32)],
    )
    return pl.pallas_call(
        functools.partial(_attend_kernel, n_words=n_words),
        grid_spec=grid_spec,
        out_shape=jax.ShapeDtypeStruct((b, s, NSA_HEADS * HEAD_DIM), BF),
        compiler_params=_params(3),
        name="nsa_attend",
    )(flat, q, gates, o_cmp, sel, spread, ks, vs, *([kw] * nwin), *([vw] * nwin))


def _head_indicators(d):
    ch = jnp.arange(d)[:, None] // RWKV_HEAD
    ind = (ch == jnp.arange(LANES)[None, :]).astype(BF)
    return ind, ind.T


def _rwkv_pre_kernel(x_ref, pre_ref, mu_ref, wr_ref, wk_ref, wv_ref, w0_ref, wd1_ref, wd2_ref,
                     a0_ref, wa1_ref, wa2_ref, wg1_ref, wg2_ref, kk_ref, ka_ref, ind_ref, indt_ref,
                     r_out, lw_out, k_out, v_out, kk_out, kka_out, g_out, hprev_ref, *, n_sub):
    si = pl.program_id(1)
    tm = x_ref.shape[1]

    @pl.when(si == 0)
    def _():
        hprev_ref[...] = jnp.zeros(hprev_ref.shape, F32)

    h_all = _rmsnorm(x_ref[0], pre_ref[...])
    rows = lax.broadcasted_iota(jnp.int32, (tm, 1), 0)
    xx_all = jnp.where(rows == 0, hprev_ref[7:8], pltpu.roll(h_all, 1, axis=0)) - h_all
    hprev_ref[...] = h_all[tm - 8:tm]

    def sub_tile(rs):
        h, xx = h_all[rs], xx_all[rs]
        mix = lambda i: (h + xx * mu_ref[i:i + 1]).astype(BF)
        xr, xw, xk, xv, xa, xg = [mix(i) for i in range(6)]
        r = _dot(xr, wr_ref[...])
        k = _dot(xk, wk_ref[...])
        v = _dot(xv, wv_ref[...])
        yield
        z = w0_ref[...] + _dot(jnp.tanh(_dot(xw, wd1_ref[...])).astype(BF), wd2_ref[...])
        lw = -DECAY_SCALE * _sigmoid(z)
        a = _sigmoid(a0_ref[...] + _dot(_dot(xa, wa1_ref[...]).astype(BF), wa2_ref[...]))
        g = _dot(_sigmoid(_dot(xg, wg1_ref[...])).astype(BF), wg2_ref[...])
        yield
        kraw = k * kk_ref[...]
        ss = _dot((kraw * kraw).astype(BF), ind_ref[...])
        yield
        inv = lax.rsqrt(jnp.maximum(ss, 1e-12))
        kk = kraw * _dot(inv.astype(BF), indt_ref[...])
        yield
        r_out[0, rs] = r.astype(BF)
        lw_out[0, rs] = lw
        k_out[0, rs] = (k * (1.0 + (a - 1.0) * ka_ref[...])).astype(BF)
        v_out[0, rs] = v.astype(BF)
        kk_out[0, rs] = kk.astype(BF)
        kka_out[0, rs] = (kk * a).astype(BF)
        g_out[0, rs] = g.astype(BF)

    _round_robin(sub_tile(slice(i * tm // n_sub, (i + 1) * tm // n_sub)) for i in range(n_sub))


def _rwkv_pre(x, pre_g, mu, w_r, w_k, w_v, w0, w_dec1, w_dec2, a0, w_a1, w_a2, w_g1, w_g2,
              k_k, k_a, ind, indt, *, tm=512, n_sub=1):
    b, s, d = x.shape
    row = pl.BlockSpec((1, tm, d), lambda bi, si: (bi, si, 0))
    vec = lambda a: a.reshape(1, d)
    ops = [x, vec(pre_g), mu, w_r.astype(BF), w_k.astype(BF), w_v.astype(BF), vec(w0),
           w_dec1.astype(BF), w_dec2.astype(BF), vec(a0), w_a1.astype(BF), w_a2.astype(BF),
           w_g1.astype(BF), w_g2.astype(BF), vec(k_k), vec(k_a), ind, indt]
    return pl.pallas_call(
        functools.partial(_rwkv_pre_kernel, n_sub=n_sub),
        grid=(b, s // tm),
        in_specs=[row] + [_const_spec(o.shape) for o in ops[1:]],
        out_specs=[row] * 7,
        out_shape=[jax.ShapeDtypeStruct((b, s, d), F32 if i == 1 else BF) for i in range(7)],
        scratch_shapes=[pltpu.VMEM((8, d), F32)],
        compiler_params=_params(2),
        name="rwkv_pre",
    )(*ops)


def _rwkv_scan_kernel(r_ref, lw_ref, k_ref, v_ref, kk_ref, kka_ref, y_ref, state_ref):
    c = RWKV_CHUNK
    slab = SLAB_HEADS * RWKV_HEAD
    n_slab = r_ref.shape[2] // slab
    rows = SLAB_HEADS * c
    subs = [slice(i * c, (i + 1) * c) for i in range(r_ref.shape[1] // c)]

    @pl.when(pl.program_id(1) == 0)
    def _():
        state_ref[...] = jnp.zeros(state_ref.shape, F32)

    ri = lax.broadcasted_iota(jnp.int32, (rows, 1), 0)
    ci = lax.broadcasted_iota(jnp.int32, (1, rows), 1)
    li = lax.broadcasted_iota(jnp.int32, (1, slab), 1)
    c_shift = c.bit_length() - 1
    h_shift = RWKV_HEAD.bit_length() - 1
    assert c == 1 << c_shift and RWKV_HEAD == 1 << h_shift
    head_match = (ri >> c_shift) == (li >> h_shift)
    t_r, t_c = ri & (c - 1), ci & (c - 1)
    strict = t_c < t_r
    incl = t_c <= t_r
    eye = jnp.where(ri == ci, 1.0, 0.0)
    tri = jnp.where(lax.broadcasted_iota(jnp.int32, (c, c), 1) <= lax.broadcasted_iota(jnp.int32, (c, c), 0),
                    1.0, 0.0).astype(BF)
    state_match = (lax.broadcasted_iota(jnp.int32, (slab, 1), 0) >> h_shift) == (li >> h_shift)

    incl2 = jnp.concatenate([incl, incl], axis=1)

    def block_diag(x):
        return jnp.where(head_match, jnp.concatenate([x] * SLAB_HEADS, axis=0), 0.0)

    def fold(x):
        out = x[0:c]
        for h in range(1, SLAB_HEADS):
            out = out + x[h * c:(h + 1) * c]
        return out

    bf = lambda x: x.astype(BF)
    slabs = [slice(hs * slab, (hs + 1) * slab) for hs in range(n_slab)]

    def prepare(rs, sl):
        lw = lw_ref[0, rs, sl]
        hi = bf(lw)
        rem = lw - hi.astype(F32)
        mid = bf(rem)
        lo = bf(rem - mid.astype(F32))
        cw = _dot(tri, hi) + _dot(tri, mid) + _dot(tri, lo)
        w_in = jnp.exp(cw)
        w_inv = jnp.exp(-cw)
        w_prev = jnp.exp(cw - lw)
        w_end = w_in[c - 1:c]

        a_t = -kk_ref[0, rs, sl].astype(F32) * w_prev
        b_t = kka_ref[0, rs, sl].astype(F32) * w_inv
        k_t = k_ref[0, rs, sl].astype(F32) * w_inv
        r_t = r_ref[0, rs, sl].astype(F32) * w_in
        return a_t, b_t, k_t, r_t, w_end

    def interactions(a_t, b_t, k_t, r_t):
        lhs = bf(jnp.concatenate([block_diag(a_t), block_diag(r_t)], axis=0))
        rhs = bf(jnp.concatenate([block_diag(b_t), block_diag(k_t)], axis=0))
        big = _dot_nt(lhs, rhs)
        a_ab = jnp.where(strict, big[:rows, :rows], 0.0)
        a_ak = jnp.where(strict, big[:rows, rows:], 0.0)
        a_r = jnp.where(incl2, big[rows:], 0.0)
        return a_ab, bf(a_ak), bf(a_r)

    every = range(n_slab)
    units = [(rs, sl) for rs in subs for sl in slabs]
    prep = [prepare(rs, sl) for rs, sl in units]
    inter = [interactions(*p[:4]) for p in prep]
    vs = [v_ref[0, rs, sl] for rs, sl in units]
    vbd = [bf(block_diag(v)) for v in vs]
    akv = [_dot(inter[u][1], vbd[u]) for u in range(len(units))]

    levels = c.bit_length() - 2
    assert levels >= 1
    invs = [eye + x[0] for x in inter]
    pws = [_dot(bf(x[0]), bf(x[0])) for x in inter]
    for _ in range(levels - 1):
        both = [_dot(bf(pw), bf(jnp.concatenate([pw, inv], axis=1))) for pw, inv in zip(pws, invs)]
        pws = [x[:, :rows] for x in both]
        invs = [inv + x[:, rows:] for inv, x in zip(invs, both)]
    invs = [bf(inv + _dot(bf(pw), bf(inv))) for pw, inv in zip(pws, invs)]

    states = [state_ref[hs] for hs in every]
    for si, rs in enumerate(subs):
        us = [si * n_slab + hs for hs in every]
        ar = [_dot_nt(bf(jnp.concatenate([prep[u][0], prep[u][3]], axis=0)), bf(states[hs]))
              for hs, u in zip(every, us)]
        u_bd = [_dot(invs[u], bf(block_diag(ar[hs][:c]) + akv[u])) for hs, u in zip(every, us)]
        y_bd = [block_diag(ar[hs][c:]) + _dot(inter[u][2], jnp.concatenate([bf(u_bd[hs]), vbd[u]], axis=0))
                for hs, u in zip(every, us)]
        for hs, sl in enumerate(slabs):
            y_ref[0, rs, sl] = fold(y_bd[hs])
        new_states = []
        for hs, u in zip(every, us):
            _, b_t, k_t, _, w_end = prep[u]
            uv = jnp.concatenate([bf(fold(u_bd[hs])), bf(vs[u])], axis=0)
            bk = jnp.concatenate([bf(b_t * w_end), bf(k_t * w_end)], axis=0)
            new_states.append(jnp.where(state_match, states[hs] * w_end + _dot_tn(uv, bk), 0.0))
        states = new_states
    for hs in every:
        state_ref[hs] = states[hs]


def _rwkv_scan(r, lw, k, v, kk, kka):
    b, s, d = r.shape
    c = RWKV_CHUNK * SCAN_SUB
    slab = SLAB_HEADS * RWKV_HEAD
    blk = pl.BlockSpec((1, c, d), lambda bi, ci: (bi, ci, 0))
    return pl.pallas_call(
        _rwkv_scan_kernel,
        grid=(b, s // c),
        in_specs=[blk] * 6,
        out_specs=blk,
        out_shape=jax.ShapeDtypeStruct((b, s, d), F32),
        scratch_shapes=[pltpu.VMEM((d // slab, slab, slab), F32)],
        compiler_params=_params(2),
        name="rwkv_scan",
    )(r, lw, k, v, kk, kka)


def _rwkv_post_kernel(x_ref, y_ref, r_ref, k_ref, v_ref, g_ref, wo_ref, lng_ref, lnb_ref, rk_ref,
                      post_ref, ind_ref, indt_ref, o_ref, *, n_sub):
    ind, indt = ind_ref[...], indt_ref[...]
    inv_n = 1.0 / RWKV_HEAD
    head_sum = lambda a: _dot(a.astype(BF), ind)
    spread = lambda a: _dot(a.astype(BF), indt)
    tm = x_ref.shape[0]

    def sub_tile(rs):
        f32 = lambda ref: ref[rs].astype(F32)
        y = y_ref[rs]
        sums = head_sum(y)
        rk_sum = head_sum(f32(r_ref) * f32(k_ref) * rk_ref[...])
        yield
        mean = _split_dot(sums * inv_n, indt)
        bonus = spread(rk_sum) * f32(v_ref)
        yield
        yc = y - mean
        var = head_sum(yc * yc) * inv_n
        yield
        rstd = spread(lax.rsqrt(var + LNX_EPS))
        yield
        yn = yc * rstd * lng_ref[...] + lnb_ref[...]
        out = _dot(((yn + bonus) * f32(g_ref)).astype(BF), wo_ref[...])
        yield
        o_ref[rs] = x_ref[rs] + _rmsnorm(out, post_ref[...])

    _round_robin(sub_tile(slice(i * tm // n_sub, (i + 1) * tm // n_sub)) for i in range(n_sub))


def _rwkv_post(x2, y, r, k, v, g, w_o, lnx_g, lnx_b, r_k, post_g, ind, indt, *, tm=512, n_sub=2):
    t, d = x2.shape
    row = pl.BlockSpec((tm, d), lambda i: (i, 0))
    vec = lambda a: a.reshape(1, d)
    ops = [x2, y, r, k, v, g, w_o.astype(BF), vec(lnx_g), vec(lnx_b), vec(r_k), vec(post_g), ind, indt]
    return pl.pallas_call(
        functools.partial(_rwkv_post_kernel, n_sub=n_sub),
        grid=(t // tm,),
        in_specs=[row] * 6 + [_const_spec(o.shape) for o in ops[6:]],
        out_specs=row,
        out_shape=jax.ShapeDtypeStruct((t, d), F32),
        compiler_params=_params(1),
        name="rwkv_post",
    )(*ops)


def _layer0_mixer(x, pre_g, w_in, pe_k, w1_k, w2_k, pe_v, w1_v, w2_v, conv_w, conv_b):
    b, s, _ = x.shape
    q, gates, kcr, vcr, ks, vs, kw, vw, o_conv = _l0_in(x, pre_g, _l0_in_weight(w_in), conv_w, conv_b)
    kc = _compress(kcr, _cmp_weights(pe_k, w1_k, w2_k), is_key=True)
    vc = _compress(vcr, _cmp_weights(pe_v, w1_v, w2_v), is_key=False)
    o_cmp, sel, flags = _select(q, kc, vc)
    o_nsa = _attend(flags, q, gates, o_cmp, sel, ks, vs, kw, vw)
    return o_nsa.reshape(b * s, -1), o_conv.reshape(b * s, -1)


def _layer1_mixer(x, pre_g, post_g, mu, w_r, w_k, w_v, w_o, w0, w_dec1, w_dec2, a0, w_a1, w_a2,
                  w_g1, w_g2, k_k, k_a, r_k, lnx_g, lnx_b):
    b, s, d = x.shape
    ind, indt = _head_indicators(d)
    r, lw, k, v, kk, kka, g = _rwkv_pre(x, pre_g, mu, w_r, w_k, w_v, w0, w_dec1, w_dec2, a0,
                                        w_a1, w_a2, w_g1, w_g2, k_k, k_a, ind, indt)
    y = _rwkv_scan(r, lw, k, v, kk, kka)
    f = lambda a: a.reshape(b * s, d)
    return _rwkv_post(f(x), f(y), f(r), f(k), f(v), f(g), w_o, lnx_g, lnx_b, r_k, post_g,
                      ind, indt).reshape(b, s, d)


def kernel(x, l0_ffn1_pre_g, l0_ffn1_post_g, l0_ffn1_w_gate, l0_ffn1_w_up, l0_ffn1_w_down, l0_mix_pre_g, l0_mix_post_g, l0_w_in, l0_cmp_pe_k, l0_cmp_w1_k, l0_cmp_w2_k, l0_cmp_pe_v, l0_cmp_w1_v, l0_cmp_w2_v, l0_conv_w, l0_conv_b, l0_w_out, l0_ffn2_pre_g, l0_ffn2_post_g, l0_ffn2_w_gate, l0_ffn2_w_up, l0_ffn2_w_down, l1_ffn1_pre_g, l1_ffn1_post_g, l1_ffn1_w_gate, l1_ffn1_w_up, l1_ffn1_w_down, l1_mix_pre_g, l1_mix_post_g, l1_mu, l1_w_r, l1_w_k, l1_w_v, l1_w_o, l1_w0, l1_w_dec1, l1_w_dec2, l1_a0, l1_w_a1, l1_w_a2, l1_w_g1, l1_w_g2, l1_k_k, l1_k_a, l1_r_k, l1_lnx_g, l1_lnx_b, l1_ffn2_pre_g, l1_ffn2_post_g, l1_ffn2_w_gate, l1_ffn2_w_up, l1_ffn2_w_down):
    b, s, d = x.shape
    ffn = lambda a, *w, **kw: _ffn(a.reshape(b * s, d), *w, **kw).reshape(b, s, d)
    x = ffn(x, l0_ffn1_pre_g, l0_ffn1_post_g, l0_ffn1_w_gate, l0_ffn1_w_up, l0_ffn1_w_down)
    o_nsa, o_conv = _layer0_mixer(x, l0_mix_pre_g, l0_w_in, l0_cmp_pe_k, l0_cmp_w1_k, l0_cmp_w2_k,
                                  l0_cmp_pe_v, l0_cmp_w1_v, l0_cmp_w2_v, l0_conv_w, l0_conv_b)
    x = ffn(x, l0_ffn2_pre_g, l0_ffn2_post_g, l0_ffn2_w_gate, l0_ffn2_w_up, l0_ffn2_w_down,
            mixer=(o_nsa, o_conv, l0_w_out, l0_mix_post_g))
    x = ffn(x, l1_ffn1_pre_g, l1_ffn1_post_g, l1_ffn1_w_gate, l1_ffn1_w_up, l1_ffn1_w_down)
    x = _layer1_mixer(x, l1_mix_pre_g, l1_mix_post_g, l1_mu, l1_w_r, l1_w_k, l1_w_v, l1_w_o, l1_w0,
                      l1_w_dec1, l1_w_dec2, l1_a0, l1_w_a1, l1_w_a2, l1_w_g1, l1_w_g2, l1_k_k, l1_k_a,
                      l1_r_k, l1_lnx_g, l1_lnx_b)
    x = ffn(x, l1_ffn2_pre_g, l1_ffn2_post_g, l1_ffn2_w_gate, l1_ffn2_w_up, l1_ffn2_w_down)
    return x
```
